```python
import jax
import jax.numpy as jnp
from jax import lax
import numpy as np

D_MODEL = 1024
BATCH = 8
SEQ = 2048
DEPTH = 4

GRID_W = 64
CTX_LEN = 256
Q_BLOCK = 128
NORM_EPS = 1e-6
NEG_INF = -1e30

A_HEAD_DIM = 64
A_DIM = D_MODEL // 2
A_HEADS = A_DIM // A_HEAD_DIM
LORA_W = 64
LORA_A = 64
LORA_G = 128
GN_EPS = 64e-5

QK_NOPE = 64
QK_ROPE = 32
V_HEAD = 64
B_DIM = D_MODEL - A_DIM
B_HEADS = B_DIM // V_HEAD
Q_RANK = 384
KV_RANK = 256
ROPE_BASE = 10000.0

A_COLS = 3 * A_DIM + 2 * LORA_W + 2 * LORA_A + LORA_G
AB_COLS = A_COLS + Q_RANK + KV_RANK + QK_ROPE
MIX_DIM = A_DIM + B_DIM

C_HEAD_DIM = 64
C_HEADS = D_MODEL // C_HEAD_DIM
WIN_R = 8
WIN_C = 16
Q_COL_BLOCK = 16
KEY_COL_SPAN = 32

N_EXPERTS = 16
EXPERT_FF = 2048
CAPACITY_FACTOR = 2

kernel_name = 'hybrid_diffusion_trunk'


def rms_norm(x, w):
    xf = x.astype(jnp.float32)
    y = xf * lax.rsqrt(jnp.mean(xf * xf, axis=-1, keepdims=True) + NORM_EPS)
    return (y * w).astype(x.dtype)


def modulate(xn, shift, scale):
    return xn * (1.0 + scale) + shift


def axial_rope(T, dtype):
    t = jnp.arange(T)
    row = (t // GRID_W).astype(jnp.float32)
    col = (t % GRID_W).astype(jnp.float32)
    n_freq = QK_ROPE // 4
    inv = ROPE_BASE ** (-jnp.arange(n_freq, dtype=jnp.float32) / n_freq)
    ang = jnp.concatenate([row[:, None] * inv, col[:, None] * inv], axis=-1)
    return jnp.cos(ang).astype(dtype), jnp.sin(ang).astype(dtype)


def apply_rope(x, cos, sin):
    x2 = x.reshape(x.shape[:-1] + (-1, 2))
    x0, x1 = x2[..., 0], x2[..., 1]
    return jnp.stack([x0 * cos - x1 * sin, x0 * sin + x1 * cos], axis=-1).reshape(x.shape)


def blocked_attention(q, k, v, scale):
    B, S, H, Dq = q.shape
    Dv = v.shape[-1]
    nb = S // Q_BLOCK
    qb = jnp.moveaxis(q.reshape(B, nb, Q_BLOCK, H, Dq), 1, 0)

    def one_block(q_blk):
        s = jnp.einsum('bqhd,bkhd->bhqk', q_blk, k).astype(jnp.float32) * scale
        p = jax.nn.softmax(s, axis=-1).astype(v.dtype)
        return jnp.einsum('bhqk,bkhd->bqhd', p, v)

    o = lax.map(one_block, qb)
    return jnp.moveaxis(o, 0, 1).reshape(B, S, H, Dv)


def token_shift(f, mu):
    prev = jnp.pad(f, ((0, 0), (1, 0), (0, 0)))[:, :-1]
    nxt = jnp.pad(f, ((0, 0), (0, 1), (0, 0)))[:, 1:]
    return f + mu[0] * (prev - f) + mu[1] * (nxt - f)


def rwkv_prepare(z, mu, w0, w2, a0, a2, g2, k_k, k_a):
    z = token_shift(z, mu).astype(jnp.float32)
    B, T, _ = z.shape
    cuts = [A_DIM, 2 * A_DIM, 3 * A_DIM, 3 * A_DIM + 2 * LORA_W, 3 * A_DIM + 2 * LORA_W + 2 * LORA_A]
    r, k, v, wd, ad, gd = jnp.split(z, cuts, axis=-1)
    wd = wd.reshape(B, T, 2, LORA_W)
    ad = ad.reshape(B, T, 2, LORA_A)
    w_log = -jax.nn.softplus(-(w0 + jnp.einsum('btdr,drc->btdc', jnp.tanh(wd), w2))) - 0.5
    decay = jnp.exp(-jnp.exp(w_log))
    a = jax.nn.sigmoid(a0 + jnp.einsum('btdr,drc->btdc', ad, a2))
    g = jax.nn.sigmoid(gd) @ g2
    kk = (k * k_k).reshape(B, T, A_HEADS, A_HEAD_DIM)
    kk = kk / jnp.maximum(jnp.sqrt(jnp.sum(kk * kk, axis=-1, keepdims=True)), 1e-12)
    kd = k[:, :, None, :] * (1.0 + (a - 1.0) * k_a)
    hd = (B, T, A_HEADS, A_HEAD_DIM)
    hd2 = (B, T, 2, A_HEADS, A_HEAD_DIM)
    return (r.reshape(hd), v.reshape(hd), kk, g, decay.reshape(hd2), a.reshape(hd2), kd.reshape(hd2))


def wkv_scan(state0, r, decay, k, v, kk, a, reverse):
    def step(state, inp):
        r_t, w_t, k_t, v_t, kk_t, a_t = inp
        sa = jnp.einsum('bhvk,bhk->bhv', state, -kk_t)
        state = (state * w_t[:, :, None, :]
                 + sa[..., None] * (kk_t * a_t)[:, :, None, :]
                 + v_t[..., None] * k_t[:, :, None, :])
        return state, jnp.einsum('bhvk,bhk->bhv', state, r_t)

    xs = tuple(jnp.swapaxes(t, 0, 1) for t in (r, decay, k, v, kk, a))
    state, ys = lax.scan(step, state0, xs, reverse=reverse)
    return state, jnp.swapaxes(ys, 0, 1)


def rwkv_scan_dir(state0, feats, d, reverse):
    r, v, kk, g, decay, a, kd = feats
    return wkv_scan(state0, r, decay[:, :, d], kd[:, :, d], v, kk, a[:, :, d], reverse)


def rwkv_output(y, feats, r_k, ln_w, ln_b):
    r, v, kk, g, decay, a, kd = feats
    B, T = y.shape[:2]
    mean = jnp.mean(y, axis=-1, keepdims=True)
    var = jnp.mean(jnp.square(y - mean), axis=-1, keepdims=True)
    yn = (y - mean) * lax.rsqrt(var + GN_EPS)
    yn = yn * ln_w.reshape(A_HEADS, A_HEAD_DIM) + ln_b.reshape(A_HEADS, A_HEAD_DIM)
    bonus = jnp.sum(jnp.sum(r[:, :, None] * kd * r_k, axis=-1, keepdims=True) * v[:, :, None], axis=2)
    return (yn + bonus).reshape(B, T, A_DIM) * g


def mla_project(zq, zkv, zr, cos, sin, qn_w, w_qup, kvn_w, w_kvup):
    B, T, _ = zq.shape
    q = (rms_norm(zq, qn_w) @ w_qup).reshape(B, T, B_HEADS, QK_NOPE + QK_ROPE)
    kv = (rms_norm(zkv, kvn_w) @ w_kvup).reshape(B, T, B_HEADS, QK_NOPE + V_HEAD)
    q_nope, q_rope = q[..., :QK_NOPE], q[..., QK_NOPE:]
    k_nope, v = kv[..., :QK_NOPE], kv[..., QK_NOPE:]
    k_rope = zr
    if cos is not None:
        q_rope = apply_rope(q_rope, cos[:, None], sin[:, None])
        k_rope = apply_rope(k_rope, cos, sin)
    k_full = jnp.concatenate([k_nope, jnp.broadcast_to(k_rope[:, :, None, :], (B, T, B_HEADS, QK_ROPE))], axis=-1)
    q_full = jnp.concatenate([q_nope, q_rope], axis=-1)
    return q_full, k_full, v


def rwkv_mla_mixer(h, hc, cos, sin, need_ctx, w_in, w_out, mu, w0, w2, a0, a2, g2, k_k, k_a, r_k,
                   ln_w, ln_b, qn_w, w_qup, kvn_w, w_kvup):
    B, S, _ = h.shape
    L = hc.shape[1]
    cuts = [A_COLS, A_COLS + Q_RANK, A_COLS + Q_RANK + KV_RANK]
    za, zq, zkv, zr = jnp.split(h @ w_in, cuts, axis=-1)
    zac, zqc, zkvc, zrc = jnp.split(hc @ w_in, cuts, axis=-1)
    lat = rwkv_prepare(za, mu, w0, w2, a0, a2, g2, k_k, k_a)
    cx = rwkv_prepare(zac, mu, w0, w2, a0, a2, g2, k_k, k_a)
    state0 = jnp.zeros((B, A_HEADS, A_HEAD_DIM, A_HEAD_DIM), jnp.float32)
    st_f, yc_f = rwkv_scan_dir(state0, cx, 0, False)
    st_b, yc_b = rwkv_scan_dir(state0, cx, 1, True)
    _, y_f = rwkv_scan_dir(st_f, lat, 0, False)
    _, y_b = rwkv_scan_dir(st_b, lat, 1, True)
    o_a = rwkv_output(y_f + y_b, lat, r_k, ln_w, ln_b).astype(h.dtype)
    q, k, v = mla_project(zq, zkv, zr, cos, sin, qn_w, w_qup, kvn_w, w_kvup)
    qc, kc, vc = mla_project(zqc, zkvc, zrc, None, None, qn_w, w_qup, kvn_w, w_kvup)
    scale = (QK_NOPE + QK_ROPE) ** -0.5
    o_b = blocked_attention(q, jnp.concatenate([kc, k], axis=1), jnp.concatenate([vc, v], axis=1), scale)
    out = jnp.concatenate([o_a, o_b.reshape(B, S, B_DIM)], axis=-1) @ w_out
    if not need_ctx:
        return out, None
    oc_a = rwkv_output(yc_f + yc_b, cx, r_k, ln_w, ln_b).astype(hc.dtype)
    oc_b = blocked_attention(qc, kc, vc, scale)
    out_c = jnp.concatenate([oc_a, oc_b.reshape(B, L, B_DIM)], axis=-1) @ w_out
    return out, out_c


def neighbourhood_attention(q, k, v, k_ctx, v_ctx, rpb):
    B, S, H, Dh = q.shape
    rows = S // GRID_W
    kr = min(WIN_R, rows)
    n_cb = GRID_W // Q_COL_BLOCK
    scale = Dh ** -0.5
    qrow = jnp.arange(rows)
    row_start = jnp.clip(qrow - kr // 2, 0, rows - kr)
    key_rows = row_start[:, None] + jnp.arange(kr)
    dr_idx = key_rows - qrow[:, None] + (WIN_R - 1)
    cb = jnp.arange(n_cb)
    qcol = cb[:, None] * Q_COL_BLOCK + jnp.arange(Q_COL_BLOCK)
    col_start = jnp.clip(qcol - WIN_C // 2, 0, GRID_W - WIN_C)
    span_start = jnp.clip(cb * Q_COL_BLOCK - WIN_C // 2, 0, GRID_W - KEY_COL_SPAN)
    key_cols = span_start[:, None] + jnp.arange(KEY_COL_SPAN)
    kcb = key_cols[:, None, :]
    col_valid = (kcb >= col_start[..., None]) & (kcb < col_start[..., None] + WIN_C)
    dc_idx = jnp.clip(kcb - qcol[..., None] + (WIN_C - 1), 0, 2 * WIN_C - 2)
    kg = k.reshape(B, rows, GRID_W, H, Dh)
    vg = v.reshape(B, rows, GRID_W, H, Dh)
    qg = jnp.moveaxis(q.reshape(B, rows, n_cb, Q_COL_BLOCK, H, Dh), 1, 0)
    n_nb = kr * KEY_COL_SPAN

    def one_row(args):
        q_r, rows_r, dr_r = args
        k_blk = kg[:, rows_r][:, :, key_cols]
        v_blk = vg[:, rows_r][:, :, key_cols]
        s_nb = jnp.einsum('bnqhd,brnkhd->bhnqrk', q_r, k_blk).astype(jnp.float32) * scale
        bias = jnp.transpose(rpb[:, dr_r][:, :, dc_idx], (0, 2, 3, 1, 4)).astype(jnp.float32)
        s_nb = jnp.where(col_valid[:, :, None, :], s_nb + bias, NEG_INF)
        s_ctx = jnp.einsum('bnqhd,blhd->bhnql', q_r, k_ctx).astype(jnp.float32) * scale
        s = jnp.concatenate([s_nb.reshape(s_nb.shape[:4] + (n_nb,)), s_ctx], axis=-1)
        p = jax.nn.softmax(s, axis=-1).astype(v.dtype)
        p_nb = p[..., :n_nb].reshape(s_nb.shape)
        p_ctx = p[..., n_nb:]
        return (jnp.einsum('bhnqrk,brnkhd->bnqhd', p_nb, v_blk)
                + jnp.einsum('bhnql,blhd->bnqhd', p_ctx, v_ctx))

    out = lax.map(one_row, (qg, key_rows, dr_idx))
    return jnp.moveaxis(out, 0, 1).reshape(B, S, H, Dh)


def na_mixer(h, hc, need_ctx, w_qkv, rpb, w_out):
    B, S, D = h.shape
    L = hc.shape[1]
    qkv = (h @ w_qkv).reshape(B, S, 3, C_HEADS, C_HEAD_DIM)
    kvc = (hc @ w_qkv[:, D:]).reshape(B, L, 2, C_HEADS, C_HEAD_DIM)
    kc, vc = kvc[:, :, 0], kvc[:, :, 1]
    o = neighbourhood_attention(qkv[:, :, 0], qkv[:, :, 1], qkv[:, :, 2], kc, vc, rpb)
    out = o.reshape(B, S, D) @ w_out
    if not need_ctx:
        return out, None
    qc = (hc @ w_qkv[:, :D]).reshape(B, L, C_HEADS, C_HEAD_DIM)
    oc = blocked_attention(qc, kc, vc, C_HEAD_DIM ** -0.5)
    return out, oc.reshape(B, L, D) @ w_out


def expert_choice_ffn(h, w_router, w1, w3, w2):
    B, T, D = h.shape
    cap = CAPACITY_FACTOR * T // N_EXPERTS
    aff = jax.nn.softmax((h @ w_router).astype(jnp.float32), axis=-1)
    gate, idx = lax.top_k(jnp.swapaxes(aff, 1, 2), cap)
    xin = jax.vmap(lambda hb, ib: hb[ib])(h, idx)
    hid = jax.nn.silu(jnp.einsum('becd,edf->becf', xin, w1)) * jnp.einsum('becd,edf->becf', xin, w3)
    y = jnp.einsum('becf,efd->becd', hid, w2) * gate[..., None].astype(h.dtype)
    return jax.vmap(lambda yb, ib: jnp.zeros((T, D), yb.dtype).at[ib.reshape(-1)].add(yb.reshape(-1, D)))(y, idx)


def setup_inputs(seed: int = 0) -> dict:
    key = jax.random.key(seed)
    ks = iter(jax.random.split(key, 64))
    n_even = (DEPTH + 1) // 2
    n_odd = DEPTH // 2
    D = D_MODEL

    def nrm(shape, scale):
        return jax.random.normal(next(ks), shape, jnp.float32) * scale

    def unif(shape, lo, hi):
        return jax.random.uniform(next(ks), shape, jnp.float32, lo, hi)

    return {
        'x': nrm((BATCH, SEQ, D), 1.0),
        'c': nrm((BATCH, D), 1.0),
        'ctx': nrm((BATCH, CTX_LEN, D), 1.0),
        'c_ctx': nrm((D,), 1.0),
        'mod_w': nrm((DEPTH, D, 6 * D), 0.5 * D ** -0.5),
        'mod_b': nrm((DEPTH, 6 * D), 0.02),
        'norm1_w': 1.0 + nrm((DEPTH, D), 0.02),
        'norm2_w': 1.0 + nrm((DEPTH, D), 0.02),
        'final_norm_w': 1.0 + nrm((D,), 0.02),
        'ab_w_in': nrm((n_even, D, AB_COLS), D ** -0.5),
        'ab_w_out': nrm((n_even, MIX_DIM, D), MIX_DIM ** -0.5),
        'rk_mu': unif((n_even, 2, A_COLS), 0.0, 0.5),
        'rk_w0': unif((n_even, 2, A_DIM), -6.5, -1.5),
        'rk_w2': nrm((n_even, 2, LORA_W, A_DIM), 0.5 * LORA_W ** -0.5),
        'rk_a0': nrm((n_even, 2, A_DIM), 0.1),
        'rk_a2': nrm((n_even, 2, LORA_A, A_DIM), 0.5 * LORA_A ** -0.5),
        'rk_g2': nrm((n_even, LORA_G, A_DIM), LORA_G ** -0.5),
        'rk_kk': 0.85 + nrm((n_even, A_DIM), 0.02),
        'rk_ka': 1.0 + nrm((n_even, A_DIM), 0.02),
        'rk_rk': nrm((n_even, A_HEADS, A_HEAD_DIM), 0.1),
        'rk_ln_w': 1.0 + nrm((n_even, A_DIM), 0.02),
        'rk_ln_b': nrm((n_even, A_DIM), 0.02),
        'mla_qn_w': 1.0 + nrm((n_even, Q_RANK), 0.02),
        'mla_w_qup': nrm((n_even, Q_RANK, B_HEADS * (QK_NOPE + QK_ROPE)), Q_RANK ** -0.5),
        'mla_kvn_w': 1.0 + nrm((n_even, KV_RANK), 0.02),
        'mla_w_kvup': nrm((n_even, KV_RANK, B_HEADS * (QK_NOPE + V_HEAD)), KV_RANK ** -0.5),
        'na_w_qkv': nrm((n_odd, D, 3 * D), D ** -0.5),
        'na_rpb': nrm((n_odd, C_HEADS, 2 * WIN_R - 1, 2 * WIN_C - 1), 0.1),
        'na_w_out': nrm((n_odd, D, D), D ** -0.5),
        'moe_router': nrm((DEPTH, D, N_EXPERTS), D ** -0.5),
        'moe_w1': nrm((DEPTH, N_EXPERTS, D, EXPERT_FF), D ** -0.5),
        'moe_w3': nrm((DEPTH, N_EXPERTS, D, EXPERT_FF), D ** -0.5),
        'moe_w2': nrm((DEPTH, N_EXPERTS, EXPERT_FF, D), EXPERT_FF ** -0.5),
    }


def reference(x, c, ctx, c_ctx, mod_w, mod_b, norm1_w, norm2_w, final_norm_w, ab_w_in, ab_w_out,
              rk_mu, rk_w0, rk_w2, rk_a0, rk_a2, rk_g2, rk_kk, rk_ka, rk_rk, rk_ln_w, rk_ln_b,
              mla_qn_w, mla_w_qup, mla_kvn_w, mla_w_kvup, na_w_qkv, na_rpb, na_w_out,
              moe_router, moe_w1, moe_w3, moe_w2):
    S = x.shape[1]
    cos, sin = axial_rope(S, x.dtype)
    silu_c = jax.nn.silu(c)
    silu_cc = jax.nn.silu(c_ctx)
    for layer in range(DEPTH):
        need_ctx = layer < DEPTH - 1
        i = layer // 2
        m = (silu_c @ mod_w[layer] + mod_b[layer])[:, None, :]
        sh1, sc1, g1, sh2, sc2, g2 = jnp.split(m, 6, axis=-1)
        mc = silu_cc @ mod_w[layer] + mod_b[layer]
        sh1c, sc1c, g1c, sh2c, sc2c, g2c = jnp.split(mc, 6, axis=-1)
        h = modulate(rms_norm(x, norm1_w[layer]), sh1, sc1)
        hc = modulate(rms_norm(ctx, norm1_w[layer]), sh1c, sc1c)
        if layer % 2 == 0:
            o, oc = rwkv_mla_mixer(h, hc, cos, sin, need_ctx, ab_w_in[i], ab_w_out[i], rk_mu[i], rk_w0[i],
                                   rk_w2[i], rk_a0[i], rk_a2[i], rk_g2[i], rk_kk[i], rk_ka[i], rk_rk[i],
                                   rk_ln_w[i], rk_ln_b[i], mla_qn_w[i], mla_w_qup[i], mla_kvn_w[i], mla_w_kvup[i])
        else:
            o, oc = na_mixer(h, hc, need_ctx, na_w_qkv[i], na_rpb[i], na_w_out[i])
        x = x + g1 * o
        h2 = modulate(rms_norm(x, norm2_w[layer]), sh2, sc2)
        x = x + g2 * expert_choice_ffn(h2, moe_router[layer], moe_w1[layer], moe_w3[layer], moe_w2[layer])
        if need_ctx:
            ctx = ctx + g1c * oc
            h2c = modulate(rms_norm(ctx, norm2_w[layer]), sh2c, sc2c)
            ctx = ctx + g2c * expert_choice_ffn(h2c, moe_router[layer], moe_w1[layer], moe_w3[layer], moe_w2[layer])
    return rms_norm(x, final_norm_w)
```

```python
import functools

import jax
import jax.numpy as jnp
import numpy as np
from jax import lax
from jax.experimental import pallas as pl
from jax.experimental.pallas import tpu as pltpu

F32 = jnp.float32
BF16 = jnp.bfloat16
HIGHEST = lax.Precision.HIGHEST

GRID_W = 64
NORM_EPS = 1e-6
NEG_INF = -1e30
GN_EPS = 64e-5
A_HEAD_DIM = 64
LORA_W = 64
LORA_A = 64
LORA_G = 128
QK_NOPE = 64
QK_ROPE = 32
V_HEAD = 64
Q_RANK = 384
KV_RANK = 256
ROPE_BASE = 10000.0
C_HEAD_DIM = 64
WIN_R = 8
WIN_C = 16
N_EXPERTS = 16
CAPACITY_FACTOR = 2
SCAN_CHUNK = 64
LANES = 128

VMEM_LIMIT = 56 * 1024 * 1024


def _params(*sem):
    return pltpu.CompilerParams(dimension_semantics=sem, vmem_limit_bytes=VMEM_LIMIT)


def _dot(a, b, precision=None):
    return jnp.dot(a, b, preferred_element_type=F32, precision=precision)


def _dot_nt(a, b, precision=None):
    return lax.dot_general(a, b, (((1,), (1,)), ((), ())), preferred_element_type=F32, precision=precision)


def _row_tile(L, S):
    tm = 256
    while L % tm or S % tm:
        tm //= 2
    return tm


def _mod_kernel(c_ref, w_ref, b_ref, o_ref):
    c = c_ref[...]
    sc = c * jax.nn.sigmoid(c)
    o_ref[0] = _dot(sc.astype(BF16), w_ref[0].astype(BF16)) + b_ref[0]


def _mod_vectors(cvec, mod_w, mod_b):
    depth, D, N = mod_w.shape
    R = cvec.shape[0]
    tn = 1024
    return pl.pallas_call(
        _mod_kernel,
        grid=(depth, N // tn),
        in_specs=[pl.BlockSpec((R, D), lambda l, j: (0, 0)),
                  pl.BlockSpec((1, D, tn), lambda l, j: (l, 0, j)),
                  pl.BlockSpec((1, 1, tn), lambda l, j: (l, 0, j))],
        out_specs=pl.BlockSpec((1, R, tn), lambda l, j: (l, 0, j)),
        out_shape=jax.ShapeDtypeStruct((depth, R, N), F32),
        compiler_params=_params("arbitrary", "arbitrary"),
        name="mod_vectors",
    )(cvec, mod_w, mod_b.reshape(depth, 1, N))


def _norm_mod(x, nw, ss):
    y = x * lax.rsqrt(jnp.mean(x * x, axis=-1, keepdims=True) + NORM_EPS)
    y = y * nw
    return y * (1.0 + ss[1:2]) + ss[0:1]


def _norm_linear_kernel(x_ref, nw_ref, ss_ref, w_ref, o_ref):
    h = _norm_mod(x_ref[0], nw_ref[...], ss_ref[0, 0])
    o_ref[0] = _dot(h.astype(BF16), w_ref[...]).astype(o_ref.dtype)


def _norm_linear(xa, nw, ss, w, L, out_dtype=F32):
    B, T, D = xa.shape
    N = w.shape[1]
    tm = _row_tile(L, T - L)
    nct = L // tm
    return pl.pallas_call(
        _norm_linear_kernel,
        grid=(B, T // tm),
        in_specs=[pl.BlockSpec((1, tm, D), lambda b, i: (b, i, 0)),
                  pl.BlockSpec((1, D), lambda b, i: (0, 0)),
                  pl.BlockSpec((1, 1, 2, D), lambda b, i: (b, jnp.where(i >= nct, 1, 0), 0, 0)),
                  pl.BlockSpec((D, N), lambda b, i: (0, 0))],
        out_specs=pl.BlockSpec((1, tm, N), lambda b, i: (b, i, 0)),
        out_shape=jax.ShapeDtypeStruct((B, T, N), out_dtype),
        compiler_params=_params("arbitrary", "arbitrary"),
        name="norm_linear",
    )(xa, nw.reshape(1, D), ss, w)


def _linear_resid_kernel(*refs, ks):
    n = len(ks)
    a_refs, (w_ref, x_ref, g_ref, o_ref) = refs[:n], refs[n:]
    acc = None
    off = 0
    for a_ref, k in zip(a_refs, ks):
        part = _dot(a_ref[0].astype(BF16), w_ref[off:off + k, :])
        acc = part if acc is None else acc + part
        off += k
    o_ref[0] = x_ref[0] + g_ref[0, 0] * acc


def _linear_resid(a_list, w, xa, gate, L):
    B, T, D = xa.shape
    tm = _row_tile(L, T - L)
    nct = L // tm
    ks = tuple(a.shape[-1] for a in a_list)
    in_specs = [pl.BlockSpec((1, tm, k), lambda b, i: (b, i, 0)) for k in ks]
    in_specs += [pl.BlockSpec(w.shape, lambda b, i: (0, 0)),
                 pl.BlockSpec((1, tm, D), lambda b, i: (b, i, 0)),
                 pl.BlockSpec((1, 1, 1, D), lambda b, i: (b, jnp.where(i >= nct, 1, 0), 0, 0))]
    return pl.pallas_call(
        functools.partial(_linear_resid_kernel, ks=ks),
        grid=(B, T // tm),
        in_specs=in_specs,
        out_specs=pl.BlockSpec((1, tm, D), lambda b, i: (b, i, 0)),
        out_shape=jax.ShapeDtypeStruct((B, T, D), F32),
        compiler_params=_params("arbitrary", "arbitrary"),
        name="linear_resid",
    )(*a_list, w, xa, gate)


def _rms_kernel(x_ref, w_ref, o_ref):
    x = x_ref[0]
    o_ref[0] = x * lax.rsqrt(jnp.mean(x * x, axis=-1, keepdims=True) + NORM_EPS) * w_ref[...]


def _final_norm(xa, w, L):
    B, T, D = xa.shape
    S = T - L
    tm = _row_tile(L, S)
    nct = L // tm
    return pl.pallas_call(
        _rms_kernel,
        grid=(B, S // tm),
        in_specs=[pl.BlockSpec((1, tm, D), lambda b, i: (b, i + nct, 0)),
                  pl.BlockSpec((1, D), lambda b, i: (0, 0))],
        out_specs=pl.BlockSpec((1, tm, D), lambda b, i: (b, i, 0)),
        out_shape=jax.ShapeDtypeStruct((B, S, D), F32),
        compiler_params=_params("arbitrary", "arbitrary"),
        name="final_norm",
    )(xa, w.reshape(1, D))


def _rms(x, w):
    return x * lax.rsqrt(jnp.mean(x * x, axis=-1, keepdims=True) + NORM_EPS) * w


def _softmax_pv(s_list, v_list):
    m = None
    for s in s_list:
        mi = jnp.max(s, axis=-1, keepdims=True)
        m = mi if m is None else jnp.maximum(m, mi)
    l = None
    o = None
    for s, v in zip(s_list, v_list):
        p = jnp.exp(s - m)
        li = jnp.sum(p, axis=-1, keepdims=True)
        oi = _dot(p.astype(BF16), v)
        l = li if l is None else l + li
        o = oi if o is None else o + oi
    return o / l


def _mla_kernel(zq_ref, zkv_ref, zr_ref, qn_ref, kvn_ref, wq_ref, wk_ref, wv_ref, cq_ref, ck_ref, sk_ref,
                o_ref, q_s, k_s, v_s, *, L, tq, scale):
    T = zq_ref.shape[1]
    zqn = _rms(zq_ref[0], qn_ref[...]).astype(BF16)
    zkvn = _rms(zkv_ref[0], kvn_ref[...]).astype(BF16)
    qh = _dot(zqn, wq_ref[0])
    kn = _dot(zkvn, wk_ref[0])
    v_s[...] = _dot(zkvn, wv_ref[0]).astype(BF16)
    zr = zr_ref[0]
    kr = zr[:, :LANES] * ck_ref[...] + zr[:, LANES:] * sk_ref[...]
    cq = cq_ref[...] * scale
    for h in range(2):
        q_s[h] = (qh[:, h * LANES:(h + 1) * LANES] * cq).astype(BF16)
        k_s[h] = (kn[:, h * LANES:(h + 1) * LANES] + kr).astype(BF16)
    first_head = lax.broadcasted_iota(jnp.int32, (tq, LANES), 1) < V_HEAD

    def tile(row0, nk):
        outs = []
        for h in range(2):
            s = _dot_nt(q_s[h, pl.ds(row0, tq), :], k_s[h, 0:nk, :])
            outs.append(_softmax_pv([s], [v_s[0:nk, :]]))
        o_ref[0, pl.ds(row0, tq), :] = jnp.where(first_head, outs[0], outs[1])

    for i in range(L // tq):
        tile(i * tq, L)

    def body(i, carry):
        tile(pl.multiple_of(i * tq, tq), T)
        return carry

    lax.fori_loop(L // tq, T // tq, body, 0)


def _mla_attention(z, qn_w, kvn_w, wq, wk, wv, cq, ck, sk, L, col_q, col_kv, col_r):
    B, T, _ = z.shape
    HP = wq.shape[0]
    tq = _row_tile(L, T - L)
    scale = float((QK_NOPE + QK_ROPE) ** -0.5)
    return pl.pallas_call(
        functools.partial(_mla_kernel, L=L, tq=tq, scale=scale),
        grid=(B, HP),
        in_specs=[pl.BlockSpec((1, T, Q_RANK), lambda b, p: (b, 0, col_q // Q_RANK)),
                  pl.BlockSpec((1, T, KV_RANK), lambda b, p: (b, 0, col_kv // KV_RANK)),
                  pl.BlockSpec((1, T, 2 * LANES), lambda b, p: (b, 0, col_r // (2 * LANES))),
                  pl.BlockSpec((1, Q_RANK), lambda b, p: (0, 0)),
                  pl.BlockSpec((1, KV_RANK), lambda b, p: (0, 0)),
                  pl.BlockSpec((1, Q_RANK, 2 * LANES), lambda b, p: (p, 0, 0)),
                  pl.BlockSpec((1, KV_RANK, 2 * LANES), lambda b, p: (p, 0, 0)),
                  pl.BlockSpec((1, KV_RANK, LANES), lambda b, p: (p, 0, 0)),
                  pl.BlockSpec((T, LANES), lambda b, p: (0, 0)),
                  pl.BlockSpec((T, LANES), lambda b, p: (0, 0)),
                  pl.BlockSpec((T, LANES), lambda b, p: (0, 0))],
        out_specs=pl.BlockSpec((1, T, LANES), lambda b, p: (b, 0, p)),
        out_shape=jax.ShapeDtypeStruct((B, T, HP * LANES), F32),
        scratch_shapes=[pltpu.VMEM((2, T, LANES), BF16), pltpu.VMEM((2, T, LANES), BF16),
                        pltpu.VMEM((T, LANES), BF16)],
        compiler_params=_params("arbitrary", "arbitrary"),
        name="mla_attention",
    )(z, z, z, qn_w.reshape(1, -1), kvn_w.reshape(1, -1), wq, wk, wv, cq, ck, sk)


def _na_kernel(q_ref, k_ref, v_ref, bt_ref, o_ref, k_s, v_s, *, L, rows, kr, need_ctx, scale):
    W = GRID_W
    k_s[...] = k_ref[0].astype(BF16)
    v_s[...] = v_ref[0].astype(BF16)
    nwin = kr * W
    lane = lax.broadcasted_iota(jnp.int32, (W, LANES), 1)
    head_mask = [(lane < C_HEAD_DIM).astype(F32), (lane >= C_HEAD_DIM).astype(F32)]
    first_head = lane < C_HEAD_DIM
    qcol = lax.broadcasted_iota(jnp.int32, (W, nwin), 0)
    kcol = lax.broadcasted_iota(jnp.int32, (W, nwin), 1) % W
    cstart = jnp.clip(qcol - WIN_C // 2, 0, W - WIN_C)
    col_valid = (kcol >= cstart) & (kcol < cstart + WIN_C)

    def row(r, carry):
        rs = jnp.clip(r - kr // 2, 0, rows - kr)
        q0 = pl.multiple_of(L + r * W, W)
        k0 = pl.multiple_of(L + rs * W, W)
        q = q_ref[0, pl.ds(q0, W), :] * scale
        kw = k_s[pl.ds(k0, nwin), :]
        vw = v_s[pl.ds(k0, nwin), :]
        dr0 = rs - r + (WIN_R - 1)
        outs = []
        for h in range(2):
            qm = (q * head_mask[h]).astype(BF16)
            s_nb = _dot_nt(qm, kw)
            bias = jnp.concatenate([bt_ref[0, h, dr0 + 2 * m] for m in range(kr // 2)], axis=-1)
            s_nb = jnp.where(col_valid, s_nb + bias, NEG_INF)
            s_ctx = _dot_nt(qm, k_s[0:L, :])
            outs.append(_softmax_pv([s_nb, s_ctx], [vw, v_s[0:L, :]]))
        o_ref[0, pl.ds(q0, W), :] = jnp.where(first_head, outs[0], outs[1])
        return carry

    lax.fori_loop(0, rows, row, 0)

    tq = min(L, 256)
    lane_c = lax.broadcasted_iota(jnp.int32, (tq, LANES), 1)
    for i in range(L // tq):
        if need_ctx:
            q = q_ref[0, i * tq:(i + 1) * tq, :] * scale
            outs = []
            for h in range(2):
                hm = (lane_c < C_HEAD_DIM) if h == 0 else (lane_c >= C_HEAD_DIM)
                qm = jnp.where(hm, q, 0.0).astype(BF16)
                s = _dot_nt(qm, k_s[0:L, :])
                outs.append(_softmax_pv([s], [v_s[0:L, :]]))
            o_ref[0, i * tq:(i + 1) * tq, :] = jnp.where(lane_c < C_HEAD_DIM, outs[0], outs[1])
        else:
            o_ref[0, i * tq:(i + 1) * tq, :] = jnp.zeros((tq, LANES), F32)


def _na_attention(qkv, bias_tab, L, need_ctx):
    B, T, D3 = qkv.shape
    D = D3 // 3
    HP = D // LANES
    rows = (T - L) // GRID_W
    kr = min(WIN_R, rows)
    assert kr % 2 == 0
    nd = bias_tab.shape[2]
    return pl.pallas_call(
        functools.partial(_na_kernel, L=L, rows=rows, kr=kr, need_ctx=need_ctx, scale=float(C_HEAD_DIM ** -0.5)),
        grid=(B, HP),
        in_specs=[pl.BlockSpec((1, T, LANES), lambda b, p: (b, 0, p)),
                  pl.BlockSpec((1, T, LANES), lambda b, p: (b, 0, HP + p)),
                  pl.BlockSpec((1, T, LANES), lambda b, p: (b, 0, 2 * HP + p)),
                  pl.BlockSpec((1, 2, nd, GRID_W, LANES), lambda b, p: (p, 0, 0, 0, 0))],
        out_specs=pl.BlockSpec((1, T, LANES), lambda b, p: (b, 0, p)),
        out_shape=jax.ShapeDtypeStruct((B, T, D), F32),
        scratch_shapes=[pltpu.VMEM((T, LANES), BF16), pltpu.VMEM((T, LANES), BF16)],
        compiler_params=_params("arbitrary", "arbitrary"),
        name="na_attention",
    )(qkv, qkv, qkv, bias_tab)


def _na_bias_table(rpb):
    H = rpb.shape[0]
    qc = np.arange(GRID_W)[:, None]
    kc = np.arange(GRID_W)[None, :]
    dc = np.clip(kc - qc + (WIN_C - 1), 0, 2 * WIN_C - 2)
    t = rpb[:, :, dc]
    t2 = jnp.concatenate([t[:, :-1], t[:, 1:]], axis=-1)
    return t2.reshape(H // 2, 2, 2 * WIN_R - 2, GRID_W, 2 * GRID_W)


def _tri_inverse(Lm, eye, m16, m32, prec):
    d0 = jnp.where(m16, Lm, 0.0)
    t = eye + d0
    s = _dot(d0, d0, prec)
    t = t + _dot(t, s, prec)
    s = _dot(s, s, prec)
    t = t + _dot(t, s, prec)
    s = _dot(s, s, prec)
    t = t + _dot(t, s, prec)
    l1 = jnp.where(m32 & (~m16), Lm, 0.0)
    t = t + _dot(t, _dot(l1, t, prec), prec)
    l2 = jnp.where(~m32, Lm, 0.0)
    t = t + _dot(t, _dot(l2, t, prec), prec)
    return t


def _wkv_kernel(r_ref, v_ref, kk_ref, lw_ref, a_ref, kd_ref, bT_ref, kdT_ref, lwT_ref, y_ref, h_s, *, B, NH, prec):
    C = SCAN_CHUNK
    g = pl.program_id(0)
    c = pl.program_id(1)
    fwd = g < B

    @pl.when(c == 0)
    def _():
        h_s[...] = jnp.zeros(h_s.shape, F32)

    ri = lax.broadcasted_iota(jnp.int32, (C, C), 0)
    ci = lax.broadcasted_iota(jnp.int32, (C, C), 1)
    eye_b = ri == ci
    eye = eye_b.astype(F32)
    order = (ri - ci) * jnp.where(fwd, 1, -1)
    before = order > 0
    before_eq = order >= 0
    tri = before_eq.astype(F32)
    triT = (order <= 0).astype(F32)
    m16 = (ri // 16) == (ci // 16)
    m32 = (ri // 32) == (ci // 32)

    for h in range(NH):
        r = r_ref[0, h]
        v = v_ref[0, h]
        kk = kk_ref[0, h]
        lw = lw_ref[0, 0, h]
        a = a_ref[0, 0, h]
        kd = kd_ref[0, 0, h]
        bT = bT_ref[0, 0, h, 0]
        kdT = kdT_ref[0, 0, h, 0]
        lwT = lwT_ref[0, 0, h, 0]
        cum = _dot(tri, lw, HIGHEST)
        cumT = _dot(lwT, triT, HIGHEST)
        totT = jnp.sum(lwT, axis=1, keepdims=True)
        e_pos = jnp.exp(cum)
        e_neg = jnp.exp(-cum)
        at = -kk * jnp.exp(cum - lw)
        rt = r * e_pos
        bt = kk * a * e_neg
        kt = kd * e_neg
        e_tail = jnp.exp(totT - cumT)
        bhT = bT * e_tail
        khT = kdT * e_tail
        ar = jnp.concatenate([at, rt], axis=0)
        pb = _dot_nt(ar, bt, prec)
        pk = _dot_nt(ar, kt, prec)
        lab = jnp.where(before, pb[:C], 0.0)
        lak = jnp.where(before, pk[:C], 0.0)
        lrb = jnp.where(before_eq, pb[C:], 0.0)
        lrk = jnp.where(before_eq, pk[C:], 0.0)
        tinv = _tri_inverse(lab, eye, m16, m32, prec)
        u = _dot(lak, v, prec)
        ta = _dot(tinv, at, prec)
        z0 = _dot(tinv, u, prec)
        m_mat = jnp.where(eye_b, jnp.exp(totT), 0.0) + _dot(bhT, ta, prec)
        n_mat = _dot(bhT, z0, prec) + _dot(khT, v, prec)
        q_mat = rt + _dot(lrb, ta, prec)
        y0 = _dot(lrb, z0, prec) + _dot(lrk, v, prec)
        hin = h_s[h]
        y_ref[0, h] = _dot(q_mat, hin, prec) + y0
        h_s[h] = _dot(m_mat, hin, prec) + n_mat


def _wkv_scan(r, v, kk, lw, a, kd, bT, kdT, lwT, L, prec=HIGHEST):
    B, NH, T, N = r.shape
    C = SCAN_CHUNK
    nC = T // C
    nct = L // C

    def chunk(g, c):
        rev = jnp.where(c < nct, nct - 1 - c, nC - 1 - (c - nct))
        return jnp.where(g < B, c, rev)

    shared = pl.BlockSpec((1, NH, C, N), lambda g, c: (g % B, 0, chunk(g, c), 0))
    per_dir = pl.BlockSpec((1, 1, NH, C, N), lambda g, c: (g // B, g % B, 0, chunk(g, c), 0))
    per_dir_t = pl.BlockSpec((1, 1, NH, 1, N, C), lambda g, c: (g // B, g % B, 0, chunk(g, c), 0, 0))
    return pl.pallas_call(
        functools.partial(_wkv_kernel, B=B, NH=NH, prec=prec),
        grid=(2 * B, nC),
        in_specs=[shared, shared, shared, per_dir, per_dir, per_dir, per_dir_t, per_dir_t, per_dir_t],
        out_specs=pl.BlockSpec((1, NH, C, N), lambda g, c: (g, 0, chunk(g, c), 0)),
        out_shape=jax.ShapeDtypeStruct((2 * B, NH, T, N), F32),
        scratch_shapes=[pltpu.VMEM((NH, N, N), F32)],
        compiler_params=_params("arbitrary", "arbitrary"),
        name="wkv_scan",
    )(r, v, kk, lw, a, kd, bT, kdT, lwT)


def _rwkv_mixer(za, L, mu, w0, w2, a0, a2, g2, k_k, k_a, r_k, ln_w, ln_b):
    B, T, _ = za.shape
    A = w0.shape[-1]
    NH = A // A_HEAD_DIM
    N = A_HEAD_DIM
    pos = jnp.arange(T)
    first = ((pos == 0) | (pos == L))[None, :, None]
    last = ((pos == L - 1) | (pos == T - 1))[None, :, None]
    prev = jnp.where(first, 0.0, jnp.roll(za, 1, axis=1))
    nxt = jnp.where(last, 0.0, jnp.roll(za, -1, axis=1))
    zs = za + mu[0] * (prev - za) + mu[1] * (nxt - za)
    cuts = [A, 2 * A, 3 * A, 3 * A + 2 * LORA_W, 3 * A + 2 * LORA_W + 2 * LORA_A]
    r, k, v, wd, ad, gd = jnp.split(zs, cuts, axis=-1)
    wd = wd.reshape(B, T, 2, LORA_W)
    ad = ad.reshape(B, T, 2, LORA_A)
    w_log = -jax.nn.softplus(-(w0 + jnp.einsum('btdr,drc->btdc', jnp.tanh(wd), w2))) - 0.5
    lw = -jnp.exp(w_log)
    a = jax.nn.sigmoid(a0 + jnp.einsum('btdr,drc->btdc', ad, a2))
    g = jax.nn.sigmoid(gd) @ g2
    kk = (k * k_k).reshape(B, T, NH, N)
    kk = kk / jnp.maximum(jnp.sqrt(jnp.sum(kk * kk, axis=-1, keepdims=True)), 1e-12)
    kd = k[:, :, None, :] * (1.0 + (a - 1.0) * k_a)

    def heads(x):
        return x.reshape(B, T, NH, N).transpose(0, 2, 1, 3)

    def heads_dir(x):
        return x.reshape(B, T, 2, NH, N).transpose(2, 0, 3, 1, 4)

    def heads_dir_t(x):
        return x.reshape(B, T // SCAN_CHUNK, SCAN_CHUNK, 2, NH, N).transpose(3, 0, 4, 1, 5, 2)

    kk_flat = kk.reshape(B, T, A)
    beta = kk_flat[:, :, None, :] * a
    y = _wkv_scan(heads(r), heads(v), heads(kk_flat), heads_dir(lw), heads_dir(a), heads_dir(kd),
                  heads_dir_t(beta), heads_dir_t(kd), heads_dir_t(lw), L)
    y = (y[:B] + y[B:]).transpose(0, 2, 1, 3)
    mean = jnp.mean(y, axis=-1, keepdims=True)
    var = jnp.mean(jnp.square(y - mean), axis=-1, keepdims=True)
    yn = (y - mean) * lax.rsqrt(var + GN_EPS)
    yn = yn * ln_w.reshape(NH, N) + ln_b.reshape(NH, N)
    rh = r.reshape(B, T, 1, NH, N)
    vh = v.reshape(B, T, 1, NH, N)
    kdh = kd.reshape(B, T, 2, NH, N)
    bonus = jnp.sum(jnp.sum(rh * kdh * r_k, axis=-1, keepdims=True) * vh, axis=2)
    return (yn + bonus).reshape(B, T, A) * g


def _router_kernel(x_ref, nw_ref, ss_ref, wr_ref, h_ref, aff_ref):
    h = _norm_mod(x_ref[0], nw_ref[...], ss_ref[0, 0])
    h_ref[0] = h.astype(BF16)
    logits = _dot_nt(wr_ref[...], h, HIGHEST)
    m = jnp.max(logits, axis=0, keepdims=True)
    p = jnp.exp(logits - m)
    aff_ref[0] = p / jnp.sum(p, axis=0, keepdims=True)


def _norm_router(xa, nw, ss, w_router, L):
    B, T, D = xa.shape
    E = w_router.shape[1]
    tm = _row_tile(L, T - L)
    nct = L // tm
    return pl.pallas_call(
        _router_kernel,
        grid=(B, T // tm),
        in_specs=[pl.BlockSpec((1, tm, D), lambda b, i: (b, i, 0)),
                  pl.BlockSpec((1, D), lambda b, i: (0, 0)),
                  pl.BlockSpec((1, 1, 2, D), lambda b, i: (b, jnp.where(i >= nct, 1, 0), 0, 0)),
                  pl.BlockSpec((E, D), lambda b, i: (0, 0))],
        out_specs=[pl.BlockSpec((1, tm, D), lambda b, i: (b, i, 0)),
                   pl.BlockSpec((1, E, tm), lambda b, i: (b, 0, i))],
        out_shape=[jax.ShapeDtypeStruct((B, T, D), BF16), jax.ShapeDtypeStruct((B, E, T), F32)],
        compiler_params=_params("arbitrary", "arbitrary"),
        name="norm_router",
    )(xa, nw.reshape(1, D), ss, w_router.T)


def _ffn_kernel(x_ref, w1_ref, w3_ref, w2_ref, o_ref, *, rm):
    j = pl.program_id(1)
    R = x_ref.shape[1]
    w1 = w1_ref[0].astype(BF16)
    w3 = w3_ref[0].astype(BF16)
    w2 = w2_ref[0].astype(BF16)

    def rows(i, carry):
        r0 = pl.multiple_of(i * rm, rm)
        x = x_ref[0, pl.ds(r0, rm), :]
        a = _dot(x, w1)
        b = _dot(x, w3)
        hid = (a * jax.nn.sigmoid(a) * b).astype(BF16)
        y = _dot(hid, w2)

        @pl.when(j == 0)
        def _():
            o_ref[0, pl.ds(r0, rm), :] = y

        @pl.when(j != 0)
        def _():
            o_ref[0, pl.ds(r0, rm), :] += y

        return carry

    lax.fori_loop(0, R // rm, rows, 0)


def _expert_ffn(xin, w1, w3, w2):
    E, R, D = xin.shape
    F = w1.shape[-1]
    tf = min(512, F)
    rm = 256
    while R % rm:
        rm //= 2
    return pl.pallas_call(
        functools.partial(_ffn_kernel, rm=rm),
        grid=(E, F // tf),
        in_specs=[pl.BlockSpec((1, R, D), lambda e, j: (e, 0, 0)),
                  pl.BlockSpec((1, D, tf), lambda e, j: (e, 0, j)),
                  pl.BlockSpec((1, D, tf), lambda e, j: (e, 0, j)),
                  pl.BlockSpec((1, tf, D), lambda e, j: (e, j, 0))],
        out_specs=pl.BlockSpec((1, R, D), lambda e, j: (e, 0, 0)),
        out_shape=jax.ShapeDtypeStruct((E, R, D), F32),
        compiler_params=_params("arbitrary", "arbitrary"),
        name="expert_ffn",
    )(xin, w1, w3, w2)


def _combine_kernel(y_ref, gate_ref, pos_ref, x_ref, g_ref, o_ref, yg_s, *, tq, nct):
    E, _, Ct, td = y_ref.shape
    T = x_ref.shape[1]
    for e in range(E):
        yg_s[e * Ct:(e + 1) * Ct, :] = (y_ref[e, 0] * gate_ref[0, e]).astype(BF16)
    pos = pos_ref[0]

    def tile(i, gate_row):
        r0 = pl.multiple_of(i * tq, tq)
        tok = lax.broadcasted_iota(jnp.int32, (tq, E * Ct), 0) + r0
        onehot = jnp.where(tok == pos, 1.0, 0.0).astype(BF16)
        o_ref[0, pl.ds(r0, tq), :] = x_ref[0, pl.ds(r0, tq), :] + gate_row * _dot(onehot, yg_s[...])

    for i in range(nct):
        tile(i, g_ref[0, 0])

    def body(i, carry):
        tile(i, g_ref[0, 1])
        return carry

    lax.fori_loop(nct, T // tq, body, 0)


def _moe_combine(y, gate, pos, xa, g2, L):
    E, B, Ct, D = y.shape
    T = xa.shape[1]
    td = min(512, D)
    tq = _row_tile(L, T - L)
    return pl.pallas_call(
        functools.partial(_combine_kernel, tq=tq, nct=L // tq),
        grid=(B, D // td),
        in_specs=[pl.BlockSpec((E, 1, Ct, td), lambda b, j: (0, b, 0, j)),
                  pl.BlockSpec((1, E, Ct, 1), lambda b, j: (b, 0, 0, 0)),
                  pl.BlockSpec((1, 1, E * Ct), lambda b, j: (b, 0, 0)),
                  pl.BlockSpec((1, T, td), lambda b, j: (b, 0, j)),
                  pl.BlockSpec((1, 2, 1, td), lambda b, j: (b, 0, 0, j))],
        out_specs=pl.BlockSpec((1, T, td), lambda b, j: (b, 0, j)),
        out_shape=jax.ShapeDtypeStruct((B, T, D), F32),
        scratch_shapes=[pltpu.VMEM((E * Ct, td), BF16)],
        compiler_params=_params("arbitrary", "arbitrary"),
        name="moe_combine",
    )(y, gate, pos, xa, g2)


def _moe(xa, nw, ss2, g2, w_router, w1, w3, w2, L, need_ctx):
    B, T, D = xa.shape
    S = T - L
    E = w_router.shape[1]
    h2, aff = _norm_router(xa, nw, ss2, w_router, L)
    cap_l = CAPACITY_FACTOR * S // E
    gate_l, idx_l = lax.top_k(aff[:, :, L:], cap_l)
    gates, poss = [gate_l], [idx_l + L]
    if need_ctx:
        cap_c = CAPACITY_FACTOR * L // E
        gate_c, idx_c = lax.top_k(aff[:, :, :L], cap_c)
        gates.append(gate_c)
        poss.append(idx_c)
    gate = jnp.concatenate(gates, axis=-1)
    pos = jnp.concatenate(poss, axis=-1).astype(jnp.int32)
    Ct = pos.shape[-1]
    flat = (pos + (jnp.arange(B, dtype=jnp.int32) * T)[:, None, None]).transpose(1, 0, 2).reshape(E, B * Ct)
    xin = jnp.take(h2.reshape(B * T, D), flat, axis=0)
    y = _expert_ffn(xin, w1, w3, w2).reshape(E, B, Ct, D)
    return _moe_combine(y, gate[..., None], pos.reshape(B, 1, E * Ct), xa, g2, L)


_PERM_EO = np.concatenate([np.arange(0, QK_ROPE, 2), np.arange(1, QK_ROPE, 2)])
_PERM_OE = np.concatenate([np.arange(1, QK_ROPE, 2), np.arange(0, QK_ROPE, 2)])


def _ab_input_weight(w_in, a_cols):
    D = w_in.shape[0]
    zr = w_in[:, a_cols + Q_RANK + KV_RANK:]
    zero = jnp.zeros((D, LANES - 2 * QK_ROPE), w_in.dtype)
    g1 = jnp.concatenate([zero, zr[:, _PERM_EO], zr[:, _PERM_EO]], axis=1)
    g2 = jnp.concatenate([zero, zr[:, _PERM_OE], zr[:, _PERM_OE]], axis=1)
    return jnp.concatenate([w_in[:, :a_cols + Q_RANK + KV_RANK], g1, g2], axis=1).astype(BF16)


def _mla_weights(w_qup, w_kvup):
    NH = w_qup.shape[1] // (QK_NOPE + QK_ROPE)
    wq = w_qup.reshape(Q_RANK, NH, QK_NOPE + QK_ROPE)
    rope = wq[:, :, QK_NOPE:]
    wq = jnp.concatenate([wq[:, :, :QK_NOPE], rope[:, :, _PERM_EO], rope[:, :, _PERM_OE]], axis=-1)
    wq = wq.reshape(Q_RANK, NH // 2, 2 * LANES).transpose(1, 0, 2)
    wkv = w_kvup.reshape(KV_RANK, NH, QK_NOPE + V_HEAD)
    wk = jnp.concatenate([wkv[:, :, :QK_NOPE], jnp.zeros((KV_RANK, NH, LANES - QK_NOPE), w_kvup.dtype)], axis=-1)
    wk = wk.reshape(KV_RANK, NH // 2, 2 * LANES).transpose(1, 0, 2)
    wv = wkv[:, :, QK_NOPE:].reshape(KV_RANK, NH // 2, 2 * V_HEAD).transpose(1, 0, 2)
    return wq.astype(BF16), wk.astype(BF16), wv.astype(BF16)


def _rope_tables(L, S):
    t = np.arange(S)
    row = (t // GRID_W).astype(np.float32)
    col = (t % GRID_W).astype(np.float32)
    n_freq = QK_ROPE // 4
    inv = (ROPE_BASE ** (-np.arange(n_freq, dtype=np.float32) / n_freq)).astype(np.float32)
    ang = jnp.concatenate([jnp.asarray(row[:, None] * inv), jnp.asarray(col[:, None] * inv)], axis=-1)
    cos = jnp.concatenate([jnp.ones((L, QK_ROPE // 2), F32), jnp.cos(ang)], axis=0)
    sin = jnp.concatenate([jnp.zeros((L, QK_ROPE // 2), F32), jnp.sin(ang)], axis=0)
    T = L + S
    cc = jnp.concatenate([cos, cos], axis=1)
    ss = jnp.concatenate([-sin, sin], axis=1)
    one = jnp.ones((T, LANES - 2 * QK_ROPE), F32)
    zero = jnp.zeros((T, LANES - 2 * QK_ROPE), F32)
    cq = jnp.concatenate([one, cc, ss], axis=1)
    ck = jnp.concatenate([zero, cc, cc], axis=1)
    sk = jnp.concatenate([zero, ss, ss], axis=1)
    return cq, ck, sk


def kernel(x, c, ctx, c_ctx, mod_w, mod_b, norm1_w, norm2_w, final_norm_w, ab_w_in, ab_w_out, rk_mu, rk_w0, rk_w2, rk_a0, rk_a2, rk_g2, rk_kk, rk_ka, rk_rk, rk_ln_w, rk_ln_b, mla_qn_w, mla_w_qup, mla_kvn_w, mla_w_kvup, na_w_qkv, na_rpb, na_w_out, moe_router, moe_w1, moe_w3, moe_w2):
    B, S, D = x.shape
    L = ctx.shape[1]
    depth = mod_w.shape[0]
    A = rk_w0.shape[-1]
    a_cols = rk_mu.shape[-1]

    rows_pad = -(B + 1) % 8
    cvec = jnp.concatenate([c, c_ctx[None], jnp.zeros((rows_pad, D), F32)], axis=0)
    mods = _mod_vectors(cvec, mod_w, mod_b)
    m_lat = mods[:, :B].reshape(depth, B, 6, D)
    m_ctx = jnp.broadcast_to(mods[:, B].reshape(depth, 1, 6, D), (depth, B, 6, D))
    mm = jnp.stack([m_ctx, m_lat], axis=2)

    cq, ck, sk = _rope_tables(L, S)
    xa = jnp.concatenate([ctx, x], axis=1)

    for layer in range(depth):
        need_ctx = layer < depth - 1
        i = layer // 2
        m = mm[layer]
        ss1, g1 = m[:, :, 0:2], m[:, :, 2:3]
        ss2, g2 = m[:, :, 3:5], m[:, :, 5:6]
        if layer % 2 == 0:
            w_in = _ab_input_weight(ab_w_in[i], a_cols)
            z = _norm_linear(xa, norm1_w[layer], ss1, w_in, L)
            o_a = _rwkv_mixer(z[:, :, :a_cols], L, rk_mu[i], rk_w0[i], rk_w2[i], rk_a0[i], rk_a2[i], rk_g2[i],
                              rk_kk[i], rk_ka[i], rk_rk[i], rk_ln_w[i], rk_ln_b[i])
            wq, wk, wv = _mla_weights(mla_w_qup[i], mla_w_kvup[i])
            o_b = _mla_attention(z, mla_qn_w[i], mla_kvn_w[i], wq, wk, wv, cq, ck, sk, L,
                                 a_cols, a_cols + Q_RANK, a_cols + Q_RANK + KV_RANK)
            xa = _linear_resid([o_a, o_b], ab_w_out[i].astype(BF16), xa, g1, L)
        else:
            qkv = _norm_linear(xa, norm1_w[layer], ss1, na_w_qkv[i].astype(BF16), L)
            o = _na_attention(qkv, _na_bias_table(na_rpb[i]), L, need_ctx)
            xa = _linear_resid([o], na_w_out[i].astype(BF16), xa, g1, L)
        xa = _moe(xa, norm2_w[layer], ss2, g2, moe_router[layer], moe_w1[layer], moe_w3[layer], moe_w2[layer],
                  L, need_ctx)
    return _final_norm(xa, final_norm_w, L)
```

```python
import functools

import jax
import jax.numpy as jnp
import numpy as np
from jax import lax
from jax.experimental import pallas as pl
from jax.experimental.pallas import tpu as pltpu

F32 = jnp.float32
BF16 = jnp.bfloat16
HIGHEST = lax.Precision.HIGHEST

GRID_W = 64
NORM_EPS = 1e-6
NEG_INF = -1e30
GN_EPS = 64e-5
A_HEAD_DIM = 64
LORA_W = 64
LORA_A = 64
LORA_G = 128
QK_NOPE = 64
QK_ROPE = 32
V_HEAD = 64
Q_RANK = 384
KV_RANK = 256
ROPE_BASE = 10000.0
C_HEAD_DIM = 64
WIN_R = 8
WIN_C = 16
N_EXPERTS = 16
CAPACITY_FACTOR = 2
SCAN_CHUNK = 64
LANES = 128

VMEM_LIMIT = 56 * 1024 * 1024


def _params(*sem):
    return pltpu.CompilerParams(dimension_semantics=sem, vmem_limit_bytes=VMEM_LIMIT)


def _dot(a, b, precision=None):
    return jnp.dot(a, b, preferred_element_type=F32, precision=precision)


def _dot_nt(a, b, precision=None):
    return lax.dot_general(a, b, (((1,), (1,)), ((), ())), preferred_element_type=F32, precision=precision)


def _row_tile(L, S):
    tm = 256
    while L % tm or S % tm:
        tm //= 2
    return tm


def _mod_kernel(c_ref, w_ref, b_ref, o_ref):
    c = c_ref[...]
    sc = c * jax.nn.sigmoid(c)
    o_ref[0] = _dot(sc.astype(BF16), w_ref[0].astype(BF16)) + b_ref[0]


def _mod_vectors(cvec, mod_w, mod_b):
    depth, D, N = mod_w.shape
    R = cvec.shape[0]
    tn = 1024
    return pl.pallas_call(
        _mod_kernel,
        grid=(depth, N // tn),
        in_specs=[pl.BlockSpec((R, D), lambda l, j: (0, 0)),
                  pl.BlockSpec((1, D, tn), lambda l, j: (l, 0, j)),
                  pl.BlockSpec((1, 1, tn), lambda l, j: (l, 0, j))],
        out_specs=pl.BlockSpec((1, R, tn), lambda l, j: (l, 0, j)),
        out_shape=jax.ShapeDtypeStruct((depth, R, N), F32),
        compiler_params=_params("arbitrary", "arbitrary"),
        name="mod_vectors",
    )(cvec, mod_w, mod_b.reshape(depth, 1, N))


def _norm_mod(x, nw, ss):
    y = x * lax.rsqrt(jnp.mean(x * x, axis=-1, keepdims=True) + NORM_EPS)
    y = y * nw
    return y * (1.0 + ss[1:2]) + ss[0:1]


def _norm_linear_kernel(x_ref, nw_ref, ss_ref, w_ref, o_ref):
    h = _norm_mod(x_ref[0], nw_ref[...], ss_ref[0, 0])
    o_ref[0] = _dot(h.astype(BF16), w_ref[...]).astype(o_ref.dtype)


def _norm_linear(xa, nw, ss, w, L, out_dtype=F32):
    B, T, D = xa.shape
    N = w.shape[1]
    tm = _row_tile(L, T - L)
    nct = L // tm
    return pl.pallas_call(
        _norm_linear_kernel,
        grid=(B, T // tm),
        in_specs=[pl.BlockSpec((1, tm, D), lambda b, i: (b, i, 0)),
                  pl.BlockSpec((1, D), lambda b, i: (0, 0)),
                  pl.BlockSpec((1, 1, 2, D), lambda b, i: (b, jnp.where(i >= nct, 1, 0), 0, 0)),
                  pl.BlockSpec((D, N), lambda b, i: (0, 0))],
        out_specs=pl.BlockSpec((1, tm, N), lambda b, i: (b, i, 0)),
        out_shape=jax.ShapeDtypeStruct((B, T, N), out_dtype),
        compiler_params=_params("arbitrary", "arbitrary"),
        name="norm_linear",
    )(xa, nw.reshape(1, D), ss, w)


def _linear_resid_kernel(*refs, ks):
    n = len(ks)
    a_refs, (w_ref, x_ref, g_ref, o_ref) = refs[:n], refs[n:]
    acc = None
    off = 0
    for a_ref, k in zip(a_refs, ks):
        part = _dot(a_ref[0].astype(BF16), w_ref[off:off + k, :])
        acc = part if acc is None else acc + part
        off += k
    o_ref[0] = x_ref[0] + g_ref[0, 0] * acc


def _linear_resid(a_list, w, xa, gate, L):
    B, T, D = xa.shape
    tm = _row_tile(L, T - L)
    nct = L // tm
    ks = tuple(a.shape[-1] for a in a_list)
    in_specs = [pl.BlockSpec((1, tm, k), lambda b, i: (b, i, 0)) for k in ks]
    in_specs += [pl.BlockSpec(w.shape, lambda b, i: (0, 0)),
                 pl.BlockSpec((1, tm, D), lambda b, i: (b, i, 0)),
                 pl.BlockSpec((1, 1, 1, D), lambda b, i: (b, jnp.where(i >= nct, 1, 0), 0, 0))]
    return pl.pallas_call(
        functools.partial(_linear_resid_kernel, ks=ks),
        grid=(B, T // tm),
        in_specs=in_specs,
        out_specs=pl.BlockSpec((1, tm, D), lambda b, i: (b, i, 0)),
        out_shape=jax.ShapeDtypeStruct((B, T, D), F32),
        compiler_params=_params("arbitrary", "arbitrary"),
        name="linear_resid",
    )(*a_list, w, xa, gate)


def _rms_kernel(x_ref, w_ref, o_ref):
    x = x_ref[0]
    o_ref[0] = x * lax.rsqrt(jnp.mean(x * x, axis=-1, keepdims=True) + NORM_EPS) * w_ref[...]


def _final_norm(xa, w, L):
    B, T, D = xa.shape
    S = T - L
    tm = _row_tile(L, S)
    nct = L // tm
    return pl.pallas_call(
        _rms_kernel,
        grid=(B, S // tm),
        in_specs=[pl.BlockSpec((1, tm, D), lambda b, i: (b, i + nct, 0)),
                  pl.BlockSpec((1, D), lambda b, i: (0, 0))],
        out_specs=pl.BlockSpec((1, tm, D), lambda b, i: (b, i, 0)),
        out_shape=jax.ShapeDtypeStruct((B, S, D), F32),
        compiler_params=_params("arbitrary", "arbitrary"),
        name="final_norm",
    )(xa, w.reshape(1, D))


def _rms(x, w):
    return x * lax.rsqrt(jnp.mean(x * x, axis=-1, keepdims=True) + NORM_EPS) * w


def _softmax_pv(chains):
    m = [functools.reduce(jnp.maximum, [jnp.max(s, axis=-1, keepdims=True) for s, _ in ch]) for ch in chains]
    p = [[jnp.exp(s - mi) for s, _ in ch] for ch, mi in zip(chains, m)]
    l = [functools.reduce(jnp.add, [jnp.sum(x, axis=-1, keepdims=True) for x in pc]) for pc in p]
    o = [functools.reduce(jnp.add, [_dot(x.astype(BF16), v) for x, (_, v) in zip(pc, ch)]) for pc, ch in zip(p, chains)]
    return [oi / li for oi, li in zip(o, l)]


def _mla_kernel(zq_ref, zkv_ref, zr_ref, qn_ref, kvn_ref, wq_ref, wk_ref, wv_ref, cq_ref, ck_ref, sk_ref,
                o_ref, q_s, k_s, v_s, *, L, tq, scale):
    T = zq_ref.shape[1]
    zqn = _rms(zq_ref[0], qn_ref[...]).astype(BF16)
    zkvn = _rms(zkv_ref[0], kvn_ref[...]).astype(BF16)
    qh = _dot(zqn, wq_ref[0])
    kn = _dot(zkvn, wk_ref[0])
    v_s[...] = _dot(zkvn, wv_ref[0]).astype(BF16)
    zr = zr_ref[0]
    kr = zr[:, :LANES] * ck_ref[...] + zr[:, LANES:] * sk_ref[...]
    cq = cq_ref[...] * scale
    for h in range(2):
        q_s[h] = (qh[:, h * LANES:(h + 1) * LANES] * cq).astype(BF16)
        k_s[h] = (kn[:, h * LANES:(h + 1) * LANES] + kr).astype(BF16)
    first_head = lax.broadcasted_iota(jnp.int32, (tq, LANES), 1) < V_HEAD

    def tile(row0, nk):
        s = [_dot_nt(q_s[h, pl.ds(row0, tq), :], k_s[h, 0:nk, :]) for h in range(2)]
        outs = _softmax_pv([[(si, v_s[0:nk, :])] for si in s])
        o_ref[0, pl.ds(row0, tq), :] = jnp.where(first_head, outs[0], outs[1])

    for i in range(L // tq):
        tile(i * tq, L)

    def body(i, carry):
        tile(pl.multiple_of(i * tq, tq), T)
        return carry

    lax.fori_loop(L // tq, T // tq, body, 0)


def _mla_attention(z, qn_w, kvn_w, wq, wk, wv, cq, ck, sk, L, col_q, col_kv, col_r):
    B, T, _ = z.shape
    HP = wq.shape[0]
    tq = _row_tile(L, T - L)
    scale = float((QK_NOPE + QK_ROPE) ** -0.5)
    return pl.pallas_call(
        functools.partial(_mla_kernel, L=L, tq=tq, scale=scale),
        grid=(B, HP),
        in_specs=[pl.BlockSpec((1, T, Q_RANK), lambda b, p: (b, 0, col_q // Q_RANK)),
                  pl.BlockSpec((1, T, KV_RANK), lambda b, p: (b, 0, col_kv // KV_RANK)),
                  pl.BlockSpec((1, T, 2 * LANES), lambda b, p: (b, 0, col_r // (2 * LANES))),
                  pl.BlockSpec((1, Q_RANK), lambda b, p: (0, 0)),
                  pl.BlockSpec((1, KV_RANK), lambda b, p: (0, 0)),
                  pl.BlockSpec((1, Q_RANK, 2 * LANES), lambda b, p: (p, 0, 0)),
                  pl.BlockSpec((1, KV_RANK, 2 * LANES), lambda b, p: (p, 0, 0)),
                  pl.BlockSpec((1, KV_RANK, LANES), lambda b, p: (p, 0, 0)),
                  pl.BlockSpec((T, LANES), lambda b, p: (0, 0)),
                  pl.BlockSpec((T, LANES), lambda b, p: (0, 0)),
                  pl.BlockSpec((T, LANES), lambda b, p: (0, 0))],
        out_specs=pl.BlockSpec((1, T, LANES), lambda b, p: (b, 0, p)),
        out_shape=jax.ShapeDtypeStruct((B, T, HP * LANES), F32),
        scratch_shapes=[pltpu.VMEM((2, T, LANES), BF16), pltpu.VMEM((2, T, LANES), BF16),
                        pltpu.VMEM((T, LANES), BF16)],
        compiler_params=_params("arbitrary", "arbitrary"),
        name="mla_attention",
    )(z, z, z, qn_w.reshape(1, -1), kvn_w.reshape(1, -1), wq, wk, wv, cq, ck, sk)


def _na_kernel(q_ref, k_ref, v_ref, bt_ref, o_ref, k_s, v_s, *, L, rows, kr, need_ctx, scale):
    W = GRID_W
    rpb = 4 if rows % 4 == 0 else 1
    k_s[...] = k_ref[0].astype(BF16)
    v_s[...] = v_ref[0].astype(BF16)
    nwin = kr * W
    lane = lax.broadcasted_iota(jnp.int32, (W, LANES), 1)
    head_mask = [(lane < C_HEAD_DIM).astype(F32), (lane >= C_HEAD_DIM).astype(F32)]
    first_head = lane < C_HEAD_DIM
    qcol = lax.broadcasted_iota(jnp.int32, (W, nwin), 0)
    kcol = lax.broadcasted_iota(jnp.int32, (W, nwin), 1) % W
    cstart = jnp.clip(qcol - WIN_C // 2, 0, W - WIN_C)
    col_valid = (kcol >= cstart) & (kcol < cstart + WIN_C)

    def row_block(rb, carry):
        q_blk = q_ref[0, pl.ds(pl.multiple_of(L + rb * (rpb * W), W), rpb * W), :] * scale
        s_ctx = [_dot_nt((q_blk * jnp.concatenate([head_mask[h]] * rpb, axis=0)).astype(BF16), k_s[0:L, :])
                 for h in range(2)]
        chains, q0s = [], []
        for j in range(rpb):
            r = rb * rpb + j
            rs = jnp.clip(r - kr // 2, 0, rows - kr)
            k0 = pl.multiple_of(L + rs * W, W)
            q0s.append(pl.multiple_of(L + r * W, W))
            q = q_blk[j * W:(j + 1) * W]
            kw = k_s[pl.ds(k0, nwin), :]
            vw = v_s[pl.ds(k0, nwin), :]
            dr0 = rs - r + (WIN_R - 1)
            for h in range(2):
                s_nb = _dot_nt((q * head_mask[h]).astype(BF16), kw)
                bias = jnp.concatenate([bt_ref[0, h, dr0 + 2 * m] for m in range(kr // 2)], axis=-1)
                s_nb = jnp.where(col_valid, s_nb + bias, NEG_INF)
                chains.append([(s_nb, vw), (s_ctx[h][j * W:(j + 1) * W], v_s[0:L, :])])
        outs = _softmax_pv(chains)
        for j in range(rpb):
            o_ref[0, pl.ds(q0s[j], W), :] = jnp.where(first_head, outs[2 * j], outs[2 * j + 1])
        return carry

    lax.fori_loop(0, rows // rpb, row_block, 0)

    tq = min(L, 256)
    lane_c = lax.broadcasted_iota(jnp.int32, (tq, LANES), 1)
    for i in range(L // tq):
        if need_ctx:
            q = q_ref[0, i * tq:(i + 1) * tq, :] * scale
            hm = [lane_c < C_HEAD_DIM, lane_c >= C_HEAD_DIM]
            s = [_dot_nt(jnp.where(hm[h], q, 0.0).astype(BF16), k_s[0:L, :]) for h in range(2)]
            outs = _softmax_pv([[(si, v_s[0:L, :])] for si in s])
            o_ref[0, i * tq:(i + 1) * tq, :] = jnp.where(lane_c < C_HEAD_DIM, outs[0], outs[1])
        else:
            o_ref[0, i * tq:(i + 1) * tq, :] = jnp.zeros((tq, LANES), F32)


def _na_attention(qkv, bias_tab, L, need_ctx):
    B, T, D3 = qkv.shape
    D = D3 // 3
    HP = D // LANES
    rows = (T - L) // GRID_W
    kr = min(WIN_R, rows)
    assert kr % 2 == 0
    nd = bias_tab.shape[2]
    return pl.pallas_call(
        functools.partial(_na_kernel, L=L, rows=rows, kr=kr, need_ctx=need_ctx, scale=float(C_HEAD_DIM ** -0.5)),
        grid=(B, HP),
        in_specs=[pl.BlockSpec((1, T, LANES), lambda b, p: (b, 0, p)),
                  pl.BlockSpec((1, T, LANES), lambda b, p: (b, 0, HP + p)),
                  pl.BlockSpec((1, T, LANES), lambda b, p: (b, 0, 2 * HP + p)),
                  pl.BlockSpec((1, 2, nd, GRID_W, LANES), lambda b, p: (p, 0, 0, 0, 0))],
        out_specs=pl.BlockSpec((1, T, LANES), lambda b, p: (b, 0, p)),
        out_shape=jax.ShapeDtypeStruct((B, T, D), F32),
        scratch_shapes=[pltpu.VMEM((T, LANES), BF16), pltpu.VMEM((T, LANES), BF16)],
        compiler_params=_params("arbitrary", "arbitrary"),
        name="na_attention",
    )(qkv, qkv, qkv, bias_tab)


def _na_bias_table(rpb):
    H = rpb.shape[0]
    qc = np.arange(GRID_W)[:, None]
    kc = np.arange(GRID_W)[None, :]
    dc = np.clip(kc - qc + (WIN_C - 1), 0, 2 * WIN_C - 2)
    t = rpb[:, :, dc]
    t2 = jnp.concatenate([t[:, :-1], t[:, 1:]], axis=-1)
    return t2.reshape(H // 2, 2, 2 * WIN_R - 2, GRID_W, 2 * GRID_W)


def _bf(x):
    return x.astype(BF16)


def _seg_sum(x, ones_bd):
    hi = _bf(x)
    lo = _bf(x - hi.astype(F32))
    return _dot(hi, ones_bd) + _dot(lo, ones_bd)


def _head_ones(A):
    seg = np.arange(A) // A_HEAD_DIM
    return jnp.asarray(seg[:, None] == seg[None, :], dtype=BF16)


def _rwkv_prep_kernel(z_ref, zp_ref, zn_ref, mu_ref, w0_ref, w2_ref, a0_ref, a2_ref, g2_ref, kk_ref, ka_ref, rk_ref,
                      ones_ref, r_o, v_o, kkn_o, g_o, bonus_o, lw_o, beta_o, kd_o, *, nct, nt, A):
    i = pl.program_id(1)
    za = z_ref[0]
    tm = za.shape[0]
    row = lax.broadcasted_iota(jnp.int32, za.shape, 0)
    seg_first = (i == 0) | (i == nct)
    seg_last = (i == nct - 1) | (i == nt - 1)
    prev_row = jnp.where(seg_first, 0.0, zp_ref[0, 7:8, :])
    next_row = jnp.where(seg_last, 0.0, zn_ref[0, 0:1, :])
    prev = jnp.where(row == 0, prev_row, pltpu.roll(za, 1, 0))
    nxt = jnp.where(row == tm - 1, next_row, pltpu.roll(za, tm - 1, 0))
    zs = za + mu_ref[0:1, :] * (prev - za) + mu_ref[1:2, :] * (nxt - za)
    r = zs[:, 0:A]
    k = zs[:, A:2 * A]
    v = zs[:, 2 * A:3 * A]
    wd = _bf(jnp.tanh(zs[:, 3 * A:3 * A + LANES]))
    ad = _bf(zs[:, 3 * A + LANES:3 * A + 2 * LANES])
    gd = _bf(jax.nn.sigmoid(zs[:, 3 * A + 2 * LANES:3 * A + 3 * LANES]))
    ones = ones_ref[...]
    kk = k * kk_ref[...]
    kkn = kk / jnp.maximum(jnp.sqrt(_seg_sum(kk * kk, ones)), 1e-12)
    kd_sum = None
    for d in range(2):
        w_log = -jax.nn.softplus(-(w0_ref[d:d + 1, :] + _dot(wd, w2_ref[d]))) - 0.5
        lw_o[d, 0] = -jnp.exp(w_log)
        a = jax.nn.sigmoid(a0_ref[d:d + 1, :] + _dot(ad, a2_ref[d]))
        beta_o[d, 0] = kkn * a
        kd = k * (1.0 + (a - 1.0) * ka_ref[...])
        kd_o[d, 0] = kd
        kd_sum = kd if kd_sum is None else kd_sum + kd
    bonus_o[0] = _seg_sum(r * kd_sum * rk_ref[...], ones) * v
    r_o[0] = r
    v_o[0] = v
    kkn_o[0] = kkn
    g_o[0] = _dot(gd, g2_ref[...])


def _rwkv_prep(z, L, a_cols, mu, w0, w2, a0, a2, g2, k_k, k_a, r_k):
    B, T, _ = z.shape
    A = w0.shape[-1]
    assert 2 * LORA_W == LANES and 2 * LORA_A == LANES and LORA_G == LANES and a_cols == 3 * A + 3 * LANES
    tm = _row_tile(L, T - L)
    nt = T // tm
    hb = tm // 8

    def pad_lora(w):
        zero = jnp.zeros_like(w[0])
        return _bf(jnp.stack([jnp.concatenate([w[0], zero], 0), jnp.concatenate([zero, w[1]], 0)]))

    def const(shape):
        return pl.BlockSpec(shape, lambda b, i: (0,) * len(shape))

    tile = pl.BlockSpec((1, tm, A), lambda b, i: (b, i, 0))
    tile_d = pl.BlockSpec((2, 1, tm, A), lambda b, i: (0, b, i, 0))
    sd = jax.ShapeDtypeStruct((B, T, A), F32)
    sd_d = jax.ShapeDtypeStruct((2, B, T, A), F32)
    return pl.pallas_call(
        functools.partial(_rwkv_prep_kernel, nct=L // tm, nt=nt, A=A),
        grid=(B, nt),
        in_specs=[pl.BlockSpec((1, tm, a_cols), lambda b, i: (b, i, 0)),
                  pl.BlockSpec((1, 8, a_cols), lambda b, i: (b, jnp.maximum(i * hb - 1, 0), 0)),
                  pl.BlockSpec((1, 8, a_cols), lambda b, i: (b, jnp.minimum((i + 1) * hb, T // 8 - 1), 0)),
                  const((2, a_cols)), const((2, A)), const((2, 2 * LORA_W, A)), const((2, A)),
                  const((2, 2 * LORA_A, A)), const((LORA_G, A)), const((1, A)), const((1, A)), const((1, A)),
                  const((A, A))],
        out_specs=[tile, tile, tile, tile, tile, tile_d, tile_d, tile_d],
        out_shape=[sd, sd, sd, sd, sd, sd_d, sd_d, sd_d],
        compiler_params=_params("arbitrary", "arbitrary"),
        name="rwkv_prep",
    )(z, z, z, mu, w0, pad_lora(w2), a0, pad_lora(a2), _bf(g2), k_k.reshape(1, A), k_a.reshape(1, A),
      r_k.reshape(1, A), _head_ones(A))


def _tri_inverse(lms, eye, m16, m32):
    d0 = [_bf(jnp.where(m16, lm, 0.0)) for lm in lms]
    t = [eye + d.astype(F32) for d in d0]
    s = [_dot(d, d) for d in d0]
    for step in range(3):
        sb = [_bf(x) for x in s]
        t = [x + _dot(_bf(x), y) for x, y in zip(t, sb)]
        if step < 2:
            s = [_dot(y, y) for y in sb]
    for lvl in (m32 & (~m16), ~m32):
        tb = [_bf(x) for x in t]
        w = [_bf(_dot(_bf(jnp.where(lvl, lm, 0.0)), y)) for lm, y in zip(lms, tb)]
        t = [x + _dot(y, z) for x, y, z in zip(t, tb, w)]
    return t


def _wkv_kernel(r_ref, v_ref, kk_ref, lw_ref, beta_ref, kd_ref, y_ref, h_s, *, B, NP):
    C = SCAN_CHUNK
    P = 2 * C
    g = pl.program_id(0)
    c = pl.program_id(1)
    sgn = jnp.where(g < B, 1, -1)

    @pl.when(c == 0)
    def _():
        h_s[...] = jnp.zeros(h_s.shape, F32)

    ri = lax.broadcasted_iota(jnp.int32, (P, P), 0)
    ci = lax.broadcasted_iota(jnp.int32, (P, P), 1)
    same = (ri // C) == (ci // C)
    order = ((ri % C) - (ci % C)) * sgn
    before = same & (order > 0)
    before_eq = same & (order >= 0)
    eye_b = ri == ci
    eye = eye_b.astype(F32)
    m16 = (ri // 16) == (ci // 16)
    m32 = (ri // 32) == (ci // 32)
    r64 = lax.broadcasted_iota(jnp.int32, (C, C), 0)
    c64 = lax.broadcasted_iota(jnp.int32, (C, C), 1)
    tri = _bf((((r64 - c64) * sgn) >= 0).astype(F32))

    lw = lw_ref[0, 0]
    lw_hi = _bf(lw)
    lw_md = _bf(lw - lw_hi.astype(F32))
    lw_lo = _bf(lw - lw_hi.astype(F32) - lw_md.astype(F32))
    cum = _dot(tri, lw_hi) + _dot(tri, lw_md) + _dot(tri, lw_lo)
    tot = jnp.sum(lw, axis=0, keepdims=True)
    kk = kk_ref[0]
    beta = beta_ref[0, 0]
    kd = kd_ref[0, 0]
    e_neg = jnp.exp(-cum)
    e_tail = jnp.exp(tot - cum)
    at_all = -kk * jnp.exp(cum - lw)
    rt_all = r_ref[0] * jnp.exp(cum)
    bt_all = beta * e_neg
    kt_all = kd * e_neg
    bh_all = beta * e_tail
    kh_all = kd * e_tail
    etot = jnp.exp(tot)
    v_all = v_ref[0]
    top = lax.broadcasted_iota(jnp.int32, (C, P), 1) < C
    zero_blk = jnp.zeros((P, P), BF16)

    pairs = range(NP)
    sls = [slice(p * P, (p + 1) * P) for p in pairs]

    def bd(x):
        return [_bf(jnp.concatenate([jnp.where(top, x[:, sl], 0.0), jnp.where(top, 0.0, x[:, sl])], axis=0))
                for sl in sls]

    at, rt, bt, kt = bd(at_all), bd(rt_all), bd(bt_all), bd(kt_all)
    bh, kh, vv = bd(bh_all), bd(kh_all), bd(v_all)
    ar = [jnp.concatenate([a, r], axis=0) for a, r in zip(at, rt)]
    arb = [_dot_nt(x, b) for x, b in zip(ar, bt)]
    ark = [_dot_nt(x, k) for x, k in zip(ar, kt)]
    lab = [jnp.where(before, x[:P], 0.0) for x in arb]
    tinv = _tri_inverse(lab, eye, m16, m32)
    u = [_dot(_bf(jnp.where(before, x[:P], 0.0)), v) for x, v in zip(ark, vv)]
    x = [_bf(_dot(_bf(t), jnp.concatenate([a, _bf(w)], axis=1))) for t, a, w in zip(tinv, at, u)]
    rhs = [jnp.concatenate([xi, jnp.concatenate([zero_blk, v], axis=1)], axis=0) for xi, v in zip(x, vv)]
    mn = [lax.dot_general(jnp.concatenate([b, k], axis=0), w, (((0,), (0,)), ((), ())), preferred_element_type=F32)
          for b, k, w in zip(bh, kh, rhs)]
    lr = [_bf(jnp.concatenate([jnp.where(before_eq, xb[P:], 0.0), jnp.where(before_eq, xk[P:], 0.0)], axis=1))
          for xb, xk in zip(arb, ark)]
    qy = [_dot(l, w) for l, w in zip(lr, rhs)]
    qm = [_bf(jnp.concatenate([r[:P].astype(F32) + q[:, :P], jnp.where(eye_b, etot[:, sl], 0.0) + m[:, :P]], axis=0))
          for r, q, m, sl in zip(rt, qy, mn, sls)]
    hin = [h_s[p] for p in pairs]
    h_hi = [_bf(h) for h in hin]
    h_lo = [_bf(h - hh.astype(F32)) for h, hh in zip(hin, h_hi)]
    res = [_dot(w, hh) + _dot(w, hl) for w, hh, hl in zip(qm, h_hi, h_lo)]
    for p in pairs:
        ybd = res[p][:P] + qy[p][:, P:]
        y_ref[0, :, sls[p]] = ybd[:C] + ybd[C:]
        h_s[p] = res[p][P:] + mn[p][:, P:]


def _wkv_scan(r, v, kk, lw, beta, kd, L):
    B, T, A = r.shape
    C = SCAN_CHUNK
    nC = T // C
    nct = L // C

    def chunk(g, c):
        rev = jnp.where(c < nct, nct - 1 - c, nC - 1 - (c - nct))
        return jnp.where(g < B, c, rev)

    shared = pl.BlockSpec((1, C, A), lambda g, c: (g % B, chunk(g, c), 0))
    per_dir = pl.BlockSpec((1, 1, C, A), lambda g, c: (g // B, g % B, chunk(g, c), 0))
    return pl.pallas_call(
        functools.partial(_wkv_kernel, B=B, NP=A // (2 * C)),
        grid=(2 * B, nC),
        in_specs=[shared, shared, shared, per_dir, per_dir, per_dir],
        out_specs=pl.BlockSpec((1, C, A), lambda g, c: (g, chunk(g, c), 0)),
        out_shape=jax.ShapeDtypeStruct((2 * B, T, A), F32),
        scratch_shapes=[pltpu.VMEM((A // (2 * C), 2 * C, 2 * C), F32)],
        compiler_params=_params("arbitrary", "arbitrary"),
        name="wkv_scan",
    )(r, v, kk, lw, beta, kd)


def _rwkv_post_kernel(yf_ref, yb_ref, bonus_ref, g_ref, lnw_ref, lnb_ref, ones_ref, o_ref):
    y = yf_ref[0] + yb_ref[0]
    ones = ones_ref[...]
    inv_n = 1.0 / A_HEAD_DIM
    d = y - _seg_sum(y, ones) * inv_n
    var = _seg_sum(d * d, ones) * inv_n
    yn = d * lax.rsqrt(var + GN_EPS) * lnw_ref[...] + lnb_ref[...]
    o_ref[0] = (yn + bonus_ref[0]) * g_ref[0]


def _rwkv_post(y, bonus, g, ln_w, ln_b, L):
    B, T, A = bonus.shape
    tm = _row_tile(L, T - L)
    tile = pl.BlockSpec((1, tm, A), lambda b, i: (b, i, 0))
    vec = pl.BlockSpec((1, A), lambda b, i: (0, 0))
    return pl.pallas_call(
        _rwkv_post_kernel,
        grid=(B, T // tm),
        in_specs=[tile, pl.BlockSpec((1, tm, A), lambda b, i: (b + B, i, 0)), tile, tile, vec, vec,
                  pl.BlockSpec((A, A), lambda b, i: (0, 0))],
        out_specs=tile,
        out_shape=jax.ShapeDtypeStruct((B, T, A), F32),
        compiler_params=_params("arbitrary", "arbitrary"),
        name="rwkv_post",
    )(y, y, bonus, g, ln_w.reshape(1, A), ln_b.reshape(1, A), _head_ones(A))


def _rwkv_mixer(z, L, a_cols, mu, w0, w2, a0, a2, g2, k_k, k_a, r_k, ln_w, ln_b):
    r, v, kk, g, bonus, lw, beta, kd = _rwkv_prep(z, L, a_cols, mu, w0, w2, a0, a2, g2, k_k, k_a, r_k)
    y = _wkv_scan(r, v, kk, lw, beta, kd, L)
    return _rwkv_post(y, bonus, g, ln_w, ln_b, L)


def _router_kernel(x_ref, nw_ref, ss_ref, wr_ref, h_ref, aff_ref):
    h = _norm_mod(x_ref[0], nw_ref[...], ss_ref[0, 0])
    h_ref[0] = h.astype(BF16)
    logits = _dot_nt(wr_ref[...], h, HIGHEST)
    m = jnp.max(logits, axis=0, keepdims=True)
    p = jnp.exp(logits - m)
    aff_ref[0] = p / jnp.sum(p, axis=0, keepdims=True)


def _norm_router(xa, nw, ss, w_router, L):
    B, T, D = xa.shape
    E = w_router.shape[1]
    tm = _row_tile(L, T - L)
    nct = L // tm
    return pl.pallas_call(
        _router_kernel,
        grid=(B, T // tm),
        in_specs=[pl.BlockSpec((1, tm, D), lambda b, i: (b, i, 0)),
                  pl.BlockSpec((1, D), lambda b, i: (0, 0)),
                  pl.BlockSpec((1, 1, 2, D), lambda b, i: (b, jnp.where(i >= nct, 1, 0), 0, 0)),
                  pl.BlockSpec((E, D), lambda b, i: (0, 0))],
        out_specs=[pl.BlockSpec((1, tm, D), lambda b, i: (b, i, 0)),
                   pl.BlockSpec((1, E, tm), lambda b, i: (b, 0, i))],
        out_shape=[jax.ShapeDtypeStruct((B, T, D), BF16), jax.ShapeDtypeStruct((B, E, T), F32)],
        compiler_params=_params("arbitrary", "arbitrary"),
        name="norm_router",
    )(xa, nw.reshape(1, D), ss, w_router.T)


def _gather_kernel(h_ref, pos_ref, o_ref):
    T = h_ref.shape[1]
    Ct = pos_ref.shape[2]
    tok = lax.broadcasted_iota(jnp.int32, (Ct, T), 1)
    onehot = jnp.where(tok == pos_ref[0, 0], 1.0, 0.0).astype(BF16)
    o_ref[0, 0] = _dot(onehot, h_ref[0]).astype(BF16)


def _moe_gather(h2, pos):
    B, T, D = h2.shape
    E, Ct = pos.shape[1], pos.shape[2]
    return pl.pallas_call(
        _gather_kernel,
        grid=(B, E),
        in_specs=[pl.BlockSpec((1, T, D), lambda b, e: (b, 0, 0)),
                  pl.BlockSpec((1, 1, Ct, 1), lambda b, e: (b, e, 0, 0))],
        out_specs=pl.BlockSpec((1, 1, Ct, D), lambda b, e: (e, b, 0, 0)),
        out_shape=jax.ShapeDtypeStruct((E, B, Ct, D), BF16),
        compiler_params=_params("arbitrary", "arbitrary"),
        name="moe_gather",
    )(h2, pos)


def _ffn_kernel(x_ref, w1_ref, w3_ref, w2_ref, o_ref, *, rm):
    j = pl.program_id(1)
    R = x_ref.shape[1]
    w1 = w1_ref[0].astype(BF16)
    w3 = w3_ref[0].astype(BF16)
    w2 = w2_ref[0].astype(BF16)

    def rows(i, carry):
        r0 = pl.multiple_of(i * rm, rm)
        x = x_ref[0, pl.ds(r0, rm), :]
        a = _dot(x, w1)
        b = _dot(x, w3)
        hid = (a * jax.nn.sigmoid(a) * b).astype(BF16)
        y = _dot(hid, w2)

        @pl.when(j == 0)
        def _():
            o_ref[0, pl.ds(r0, rm), :] = y

        @pl.when(j != 0)
        def _():
            o_ref[0, pl.ds(r0, rm), :] += y

        return carry

    lax.fori_loop(0, R // rm, rows, 0)


def _expert_ffn(xin, w1, w3, w2):
    E, R, D = xin.shape
    F = w1.shape[-1]
    tf = min(512, F)
    rm = 256
    while R % rm:
        rm //= 2
    return pl.pallas_call(
        functools.partial(_ffn_kernel, rm=rm),
        grid=(E, F // tf),
        in_specs=[pl.BlockSpec((1, R, D), lambda e, j: (e, 0, 0)),
                  pl.BlockSpec((1, D, tf), lambda e, j: (e, 0, j)),
                  pl.BlockSpec((1, D, tf), lambda e, j: (e, 0, j)),
                  pl.BlockSpec((1, tf, D), lambda e, j: (e, j, 0))],
        out_specs=pl.BlockSpec((1, R, D), lambda e, j: (e, 0, 0)),
        out_shape=jax.ShapeDtypeStruct((E, R, D), F32),
        compiler_params=_params("arbitrary", "arbitrary"),
        name="expert_ffn",
    )(xin, w1, w3, w2)


def _combine_kernel(y_ref, gate_ref, pos_ref, x_ref, g_ref, o_ref, yg_s, *, tq, nct):
    E, _, Ct, td = y_ref.shape
    T = x_ref.shape[1]
    for e in range(E):
        yg_s[e * Ct:(e + 1) * Ct, :] = (y_ref[e, 0] * gate_ref[0, e]).astype(BF16)
    pos = pos_ref[0]

    def tile(i, gate_row):
        r0 = pl.multiple_of(i * tq, tq)
        tok = lax.broadcasted_iota(jnp.int32, (tq, E * Ct), 0) + r0
        onehot = jnp.where(tok == pos, 1.0, 0.0).astype(BF16)
        o_ref[0, pl.ds(r0, tq), :] = x_ref[0, pl.ds(r0, tq), :] + gate_row * _dot(onehot, yg_s[...])

    for i in range(nct):
        tile(i, g_ref[0, 0])

    def body(i, carry):
        tile(i, g_ref[0, 1])
        return carry

    lax.fori_loop(nct, T // tq, body, 0)


def _moe_combine(y, gate, pos, xa, g2, L):
    E, B, Ct, D = y.shape
    T = xa.shape[1]
    td = min(512, D)
    tq = _row_tile(L, T - L)
    return pl.pallas_call(
        functools.partial(_combine_kernel, tq=tq, nct=L // tq),
        grid=(B, D // td),
        in_specs=[pl.BlockSpec((E, 1, Ct, td), lambda b, j: (0, b, 0, j)),
                  pl.BlockSpec((1, E, Ct, 1), lambda b, j: (b, 0, 0, 0)),
                  pl.BlockSpec((1, 1, E * Ct), lambda b, j: (b, 0, 0)),
                  pl.BlockSpec((1, T, td), lambda b, j: (b, 0, j)),
                  pl.BlockSpec((1, 2, 1, td), lambda b, j: (b, 0, 0, j))],
        out_specs=pl.BlockSpec((1, T, td), lambda b, j: (b, 0, j)),
        out_shape=jax.ShapeDtypeStruct((B, T, D), F32),
        scratch_shapes=[pltpu.VMEM((E * Ct, td), BF16)],
        compiler_params=_params("arbitrary", "arbitrary"),
        name="moe_combine",
    )(y, gate, pos, xa, g2)


def _moe(xa, nw, ss2, g2, w_router, w1, w3, w2, L, need_ctx):
    B, T, D = xa.shape
    S = T - L
    E = w_router.shape[1]
    h2, aff = _norm_router(xa, nw, ss2, w_router, L)
    cap_l = CAPACITY_FACTOR * S // E
    gate_l, idx_l = lax.top_k(aff[:, :, L:], cap_l)
    gates, poss = [gate_l], [idx_l + L]
    if need_ctx:
        cap_c = CAPACITY_FACTOR * L // E
        gate_c, idx_c = lax.top_k(aff[:, :, :L], cap_c)
        gates.append(gate_c)
        poss.append(idx_c)
    gate = jnp.concatenate(gates, axis=-1)
    pos = jnp.concatenate(poss, axis=-1).astype(jnp.int32)
    Ct = pos.shape[-1]
    xin = _moe_gather(h2, pos[..., None]).reshape(E, B * Ct, D)
    y = _expert_ffn(xin, w1, w3, w2).reshape(E, B, Ct, D)
    return _moe_combine(y, gate[..., None], pos.reshape(B, 1, E * Ct), xa, g2, L)


_PERM_EO = np.concatenate([np.arange(0, QK_ROPE, 2), np.arange(1, QK_ROPE, 2)])
_PERM_OE = np.concatenate([np.arange(1, QK_ROPE, 2), np.arange(0, QK_ROPE, 2)])


def _ab_input_weight(w_in, a_cols):
    D = w_in.shape[0]
    zr = w_in[:, a_cols + Q_RANK + KV_RANK:]
    zero = jnp.zeros((D, LANES - 2 * QK_ROPE), w_in.dtype)
    g1 = jnp.concatenate([zero, zr[:, _PERM_EO], zr[:, _PERM_EO]], axis=1)
    g2 = jnp.concatenate([zero, zr[:, _PERM_OE], zr[:, _PERM_OE]], axis=1)
    return jnp.concatenate([w_in[:, :a_cols + Q_RANK + KV_RANK], g1, g2], axis=1).astype(BF16)


def _mla_weights(w_qup, w_kvup):
    NH = w_qup.shape[1] // (QK_NOPE + QK_ROPE)
    wq = w_qup.reshape(Q_RANK, NH, QK_NOPE + QK_ROPE)
    rope = wq[:, :, QK_NOPE:]
    wq = jnp.concatenate([wq[:, :, :QK_NOPE], rope[:, :, _PERM_EO], rope[:, :, _PERM_OE]], axis=-1)
    wq = wq.reshape(Q_RANK, NH // 2, 2 * LANES).transpose(1, 0, 2)
    wkv = w_kvup.reshape(KV_RANK, NH, QK_NOPE + V_HEAD)
    wk = jnp.concatenate([wkv[:, :, :QK_NOPE], jnp.zeros((KV_RANK, NH, LANES - QK_NOPE), w_kvup.dtype)], axis=-1)
    wk = wk.reshape(KV_RANK, NH // 2, 2 * LANES).transpose(1, 0, 2)
    wv = wkv[:, :, QK_NOPE:].reshape(KV_RANK, NH // 2, 2 * V_HEAD).transpose(1, 0, 2)
    return wq.astype(BF16), wk.astype(BF16), wv.astype(BF16)


def _rope_tables(L, S):
    t = np.arange(S)
    row = (t // GRID_W).astype(np.float32)
    col = (t % GRID_W).astype(np.float32)
    n_freq = QK_ROPE // 4
    inv = (ROPE_BASE ** (-np.arange(n_freq, dtype=np.float32) / n_freq)).astype(np.float32)
    ang = jnp.concatenate([jnp.asarray(row[:, None] * inv), jnp.asarray(col[:, None] * inv)], axis=-1)
    cos = jnp.concatenate([jnp.ones((L, QK_ROPE // 2), F32), jnp.cos(ang)], axis=0)
    sin = jnp.concatenate([jnp.zeros((L, QK_ROPE // 2), F32), jnp.sin(ang)], axis=0)
    T = L + S
    cc = jnp.concatenate([cos, cos], axis=1)
    ss = jnp.concatenate([-sin, sin], axis=1)
    one = jnp.ones((T, LANES - 2 * QK_ROPE), F32)
    zero = jnp.zeros((T, LANES - 2 * QK_ROPE), F32)
    cq = jnp.concatenate([one, cc, ss], axis=1)
    ck = jnp.concatenate([zero, cc, cc], axis=1)
    sk = jnp.concatenate([zero, ss, ss], axis=1)
    return cq, ck, sk


def kernel(x, c, ctx, c_ctx, mod_w, mod_b, norm1_w, norm2_w, final_norm_w, ab_w_in, ab_w_out, rk_mu, rk_w0, rk_w2, rk_a0, rk_a2, rk_g2, rk_kk, rk_ka, rk_rk, rk_ln_w, rk_ln_b, mla_qn_w, mla_w_qup, mla_kvn_w, mla_w_kvup, na_w_qkv, na_rpb, na_w_out, moe_router, moe_w1, moe_w3, moe_w2):
    B, S, D = x.shape
    L = ctx.shape[1]
    depth = mod_w.shape[0]
    A = rk_w0.shape[-1]
    a_cols = rk_mu.shape[-1]

    rows_pad = -(B + 1) % 8
    cvec = jnp.concatenate([c, c_ctx[None], jnp.zeros((rows_pad, D), F32)], axis=0)
    mods = _mod_vectors(cvec, mod_w, mod_b)
    m_lat = mods[:, :B].reshape(depth, B, 6, D)
    m_ctx = jnp.broadcast_to(mods[:, B].reshape(depth, 1, 6, D), (depth, B, 6, D))
    mm = jnp.stack([m_ctx, m_lat], axis=2)

    cq, ck, sk = _rope_tables(L, S)
    xa = jnp.concatenate([ctx, x], axis=1)

    for layer in range(depth):
        need_ctx = layer < depth - 1
        i = layer // 2
        m = mm[layer]
        ss1, g1 = m[:, :, 0:2], m[:, :, 2:3]
        ss2, g2 = m[:, :, 3:5], m[:, :, 5:6]
        if layer % 2 == 0:
            w_in = _ab_input_weight(ab_w_in[i], a_cols)
            z = _norm_linear(xa, norm1_w[layer], ss1, w_in, L)
            o_a = _rwkv_mixer(z, L, a_cols, rk_mu[i], rk_w0[i], rk_w2[i], rk_a0[i], rk_a2[i], rk_g2[i],
                              rk_kk[i], rk_ka[i], rk_rk[i], rk_ln_w[i], rk_ln_b[i])
            wq, wk, wv = _mla_weights(mla_w_qup[i], mla_w_kvup[i])
            o_b = _mla_attention(z, mla_qn_w[i], mla_kvn_w[i], wq, wk, wv, cq, ck, sk, L,
                                 a_cols, a_cols + Q_RANK, a_cols + Q_RANK + KV_RANK)
            xa = _linear_resid([o_a, o_b], ab_w_out[i].astype(BF16), xa, g1, L)
        else:
            qkv = _norm_linear(xa, norm1_w[layer], ss1, na_w_qkv[i].astype(BF16), L)
            o = _na_attention(qkv, _na_bias_table(na_rpb[i]), L, need_ctx)
            xa = _linear_resid([o], na_w_out[i].astype(BF16), xa, g1, L)
        xa = _moe(xa, norm2_w[layer], ss2, g2, moe_router[layer], moe_w1[layer], moe_w3[layer], moe_w2[layer],
                  L, need_ctx)
    return _final_norm(xa, final_norm_w, L)
```

```python
import functools

import jax
import jax.numpy as jnp
import numpy as np
from jax import lax
from jax.experimental import pallas as pl
from jax.experimental.pallas import tpu as pltpu

F32 = jnp.float32
BF16 = jnp.bfloat16
HIGHEST = lax.Precision.HIGHEST

GRID_W = 64
NORM_EPS = 1e-6
NEG_INF = -1e30
GN_EPS = 64e-5
A_HEAD_DIM = 64
LORA_W = 64
LORA_A = 64
LORA_G = 128
QK_NOPE = 64
QK_ROPE = 32
V_HEAD = 64
Q_RANK = 384
KV_RANK = 256
ROPE_BASE = 10000.0
C_HEAD_DIM = 64
WIN_R = 8
WIN_C = 16
N_EXPERTS = 16
CAPACITY_FACTOR = 2
SCAN_CHUNK = 64
LANES = 128

VMEM_LIMIT = 56 * 1024 * 1024


def _params(*sem):
    return pltpu.CompilerParams(dimension_semantics=sem, vmem_limit_bytes=VMEM_LIMIT)


def _dot(a, b, precision=None):
    return jnp.dot(a, b, preferred_element_type=F32, precision=precision)


def _dot_nt(a, b, precision=None):
    return lax.dot_general(a, b, (((1,), (1,)), ((), ())), preferred_element_type=F32, precision=precision)


def _row_tile(L, S):
    tm = 256
    while L % tm or S % tm:
        tm //= 2
    return tm


def _mod_kernel(c_ref, w_ref, b_ref, o_ref):
    c = c_ref[...]
    sc = c * jax.nn.sigmoid(c)
    o_ref[0] = _dot(sc.astype(BF16), w_ref[0].astype(BF16)) + b_ref[0]


def _mod_vectors(cvec, mod_w, mod_b):
    depth, D, N = mod_w.shape
    R = cvec.shape[0]
    tn = 1024
    return pl.pallas_call(
        _mod_kernel,
        grid=(depth, N // tn),
        in_specs=[pl.BlockSpec((R, D), lambda l, j: (0, 0)),
                  pl.BlockSpec((1, D, tn), lambda l, j: (l, 0, j)),
                  pl.BlockSpec((1, 1, tn), lambda l, j: (l, 0, j))],
        out_specs=pl.BlockSpec((1, R, tn), lambda l, j: (l, 0, j)),
        out_shape=jax.ShapeDtypeStruct((depth, R, N), F32),
        compiler_params=_params("arbitrary", "arbitrary"),
        name="mod_vectors",
    )(cvec, mod_w, mod_b.reshape(depth, 1, N))


def _norm_mod(x, nw, ss):
    y = x * lax.rsqrt(jnp.mean(x * x, axis=-1, keepdims=True) + NORM_EPS)
    y = y * nw
    return y * (1.0 + ss[1:2]) + ss[0:1]


def _norm_linear_kernel(x_ref, nw_ref, ss_ref, w_ref, o_ref):
    h = _norm_mod(x_ref[0], nw_ref[...], ss_ref[0, 0])
    o_ref[0] = _dot(h.astype(BF16), w_ref[...]).astype(o_ref.dtype)


def _norm_linear(xa, nw, ss, w, L, out_dtype=F32):
    B, T, D = xa.shape
    N = w.shape[1]
    tm = _row_tile(L, T - L)
    nct = L // tm
    return pl.pallas_call(
        _norm_linear_kernel,
        grid=(B, T // tm),
        in_specs=[pl.BlockSpec((1, tm, D), lambda b, i: (b, i, 0)),
                  pl.BlockSpec((1, D), lambda b, i: (0, 0)),
                  pl.BlockSpec((1, 1, 2, D), lambda b, i: (b, jnp.where(i >= nct, 1, 0), 0, 0)),
                  pl.BlockSpec((D, N), lambda b, i: (0, 0))],
        out_specs=pl.BlockSpec((1, tm, N), lambda b, i: (b, i, 0)),
        out_shape=jax.ShapeDtypeStruct((B, T, N), out_dtype),
        compiler_params=_params("arbitrary", "arbitrary"),
        name="norm_linear",
    )(xa, nw.reshape(1, D), ss, w)


def _linear_resid_kernel(*refs, ks):
    n = len(ks)
    a_refs, (w_ref, x_ref, g_ref, o_ref) = refs[:n], refs[n:]
    acc = None
    off = 0
    for a_ref, k in zip(a_refs, ks):
        part = _dot(a_ref[0].astype(BF16), w_ref[off:off + k, :])
        acc = part if acc is None else acc + part
        off += k
    o_ref[0] = x_ref[0] + g_ref[0, 0] * acc


def _linear_resid(a_list, w, xa, gate, L):
    B, T, D = xa.shape
    tm = _row_tile(L, T - L)
    nct = L // tm
    ks = tuple(a.shape[-1] for a in a_list)
    in_specs = [pl.BlockSpec((1, tm, k), lambda b, i: (b, i, 0)) for k in ks]
    in_specs += [pl.BlockSpec(w.shape, lambda b, i: (0, 0)),
                 pl.BlockSpec((1, tm, D), lambda b, i: (b, i, 0)),
                 pl.BlockSpec((1, 1, 1, D), lambda b, i: (b, jnp.where(i >= nct, 1, 0), 0, 0))]
    return pl.pallas_call(
        functools.partial(_linear_resid_kernel, ks=ks),
        grid=(B, T // tm),
        in_specs=in_specs,
        out_specs=pl.BlockSpec((1, tm, D), lambda b, i: (b, i, 0)),
        out_shape=jax.ShapeDtypeStruct((B, T, D), F32),
        compiler_params=_params("arbitrary", "arbitrary"),
        name="linear_resid",
    )(*a_list, w, xa, gate)


def _rms_kernel(x_ref, w_ref, o_ref):
    x = x_ref[0]
    o_ref[0] = x * lax.rsqrt(jnp.mean(x * x, axis=-1, keepdims=True) + NORM_EPS) * w_ref[...]


def _final_norm(xa, w, L):
    B, T, D = xa.shape
    S = T - L
    tm = _row_tile(L, S)
    nct = L // tm
    return pl.pallas_call(
        _rms_kernel,
        grid=(B, S // tm),
        in_specs=[pl.BlockSpec((1, tm, D), lambda b, i: (b, i + nct, 0)),
                  pl.BlockSpec((1, D), lambda b, i: (0, 0))],
        out_specs=pl.BlockSpec((1, tm, D), lambda b, i: (b, i, 0)),
        out_shape=jax.ShapeDtypeStruct((B, S, D), F32),
        compiler_params=_params("arbitrary", "arbitrary"),
        name="final_norm",
    )(xa, w.reshape(1, D))


def _rms(x, w):
    return x * lax.rsqrt(jnp.mean(x * x, axis=-1, keepdims=True) + NORM_EPS) * w


def _softmax_pv(chains):
    m = [functools.reduce(jnp.maximum, [jnp.max(s, axis=-1, keepdims=True) for s, _ in ch]) for ch in chains]
    p = [[jnp.exp(s - mi) for s, _ in ch] for ch, mi in zip(chains, m)]
    l = [functools.reduce(jnp.add, [jnp.sum(x, axis=-1, keepdims=True) for x in pc]) for pc in p]
    o = [functools.reduce(jnp.add, [_dot(x.astype(BF16), v) for x, (_, v) in zip(pc, ch)]) for pc, ch in zip(p, chains)]
    return [oi / li for oi, li in zip(o, l)]


def _mla_kernel(zq_ref, zkv_ref, zr_ref, qn_ref, kvn_ref, wq_ref, wk_ref, wv_ref, cq_ref, ck_ref, sk_ref,
                o_ref, q_s, k_s, v_s, *, L, tq, scale):
    T = zq_ref.shape[1]
    zqn = _rms(zq_ref[0], qn_ref[...]).astype(BF16)
    zkvn = _rms(zkv_ref[0], kvn_ref[...]).astype(BF16)
    qh = _dot(zqn, wq_ref[0])
    kn = _dot(zkvn, wk_ref[0])
    v_s[...] = _dot(zkvn, wv_ref[0]).astype(BF16)
    zr = zr_ref[0]
    kr = zr[:, :LANES] * ck_ref[...] + zr[:, LANES:] * sk_ref[...]
    cq = cq_ref[...] * scale
    for h in range(2):
        q_s[h] = (qh[:, h * LANES:(h + 1) * LANES] * cq).astype(BF16)
        k_s[h] = (kn[:, h * LANES:(h + 1) * LANES] + kr).astype(BF16)
    first_head = lax.broadcasted_iota(jnp.int32, (tq, LANES), 1) < V_HEAD

    def tile(row0, nk):
        s = [_dot_nt(q_s[h, pl.ds(row0, tq), :], k_s[h, 0:nk, :]) for h in range(2)]
        outs = _softmax_pv([[(si, v_s[0:nk, :])] for si in s])
        o_ref[0, pl.ds(row0, tq), :] = jnp.where(first_head, outs[0], outs[1])

    for i in range(L // tq):
        tile(i * tq, L)

    def body(i, carry):
        tile(pl.multiple_of(i * tq, tq), T)
        return carry

    lax.fori_loop(L // tq, T // tq, body, 0)


def _mla_attention(z, qn_w, kvn_w, wq, wk, wv, cq, ck, sk, L, col_q, col_kv, col_r):
    B, T, _ = z.shape
    HP = wq.shape[0]
    tq = _row_tile(L, T - L)
    scale = float((QK_NOPE + QK_ROPE) ** -0.5)
    return pl.pallas_call(
        functools.partial(_mla_kernel, L=L, tq=tq, scale=scale),
        grid=(B, HP),
        in_specs=[pl.BlockSpec((1, T, Q_RANK), lambda b, p: (b, 0, col_q // Q_RANK)),
                  pl.BlockSpec((1, T, KV_RANK), lambda b, p: (b, 0, col_kv // KV_RANK)),
                  pl.BlockSpec((1, T, 2 * LANES), lambda b, p: (b, 0, col_r // (2 * LANES))),
                  pl.BlockSpec((1, Q_RANK), lambda b, p: (0, 0)),
                  pl.BlockSpec((1, KV_RANK), lambda b, p: (0, 0)),
                  pl.BlockSpec((1, Q_RANK, 2 * LANES), lambda b, p: (p, 0, 0)),
                  pl.BlockSpec((1, KV_RANK, 2 * LANES), lambda b, p: (p, 0, 0)),
                  pl.BlockSpec((1, KV_RANK, LANES), lambda b, p: (p, 0, 0)),
                  pl.BlockSpec((T, LANES), lambda b, p: (0, 0)),
                  pl.BlockSpec((T, LANES), lambda b, p: (0, 0)),
                  pl.BlockSpec((T, LANES), lambda b, p: (0, 0))],
        out_specs=pl.BlockSpec((1, T, LANES), lambda b, p: (b, 0, p)),
        out_shape=jax.ShapeDtypeStruct((B, T, HP * LANES), F32),
        scratch_shapes=[pltpu.VMEM((2, T, LANES), BF16), pltpu.VMEM((2, T, LANES), BF16),
                        pltpu.VMEM((T, LANES), BF16)],
        compiler_params=_params("arbitrary", "arbitrary"),
        name="mla_attention",
    )(z, z, z, qn_w.reshape(1, -1), kvn_w.reshape(1, -1), wq, wk, wv, cq, ck, sk)


def _na_kernel(q_ref, k_ref, v_ref, bt_ref, o_ref, k_s, v_s, *, L, rows, kr, need_ctx, scale):
    W = GRID_W
    rpb = 4 if rows % 4 == 0 else 1
    k_s[...] = k_ref[0].astype(BF16)
    v_s[...] = v_ref[0].astype(BF16)
    nwin = kr * W
    lane = lax.broadcasted_iota(jnp.int32, (W, LANES), 1)
    head_mask = [(lane < C_HEAD_DIM).astype(F32), (lane >= C_HEAD_DIM).astype(F32)]
    first_head = lane < C_HEAD_DIM
    qcol = lax.broadcasted_iota(jnp.int32, (W, nwin), 0)
    kcol = lax.broadcasted_iota(jnp.int32, (W, nwin), 1) % W
    cstart = jnp.clip(qcol - WIN_C // 2, 0, W - WIN_C)
    col_valid = (kcol >= cstart) & (kcol < cstart + WIN_C)

    def row_block(rb, carry):
        q_blk = q_ref[0, pl.ds(pl.multiple_of(L + rb * (rpb * W), W), rpb * W), :] * scale
        s_ctx = [_dot_nt((q_blk * jnp.concatenate([head_mask[h]] * rpb, axis=0)).astype(BF16), k_s[0:L, :])
                 for h in range(2)]
        chains, q0s = [], []
        for j in range(rpb):
            r = rb * rpb + j
            rs = jnp.clip(r - kr // 2, 0, rows - kr)
            k0 = pl.multiple_of(L + rs * W, W)
            q0s.append(pl.multiple_of(L + r * W, W))
            q = q_blk[j * W:(j + 1) * W]
            kw = k_s[pl.ds(k0, nwin), :]
            vw = v_s[pl.ds(k0, nwin), :]
            dr0 = rs - r + (WIN_R - 1)
            for h in range(2):
                s_nb = _dot_nt((q * head_mask[h]).astype(BF16), kw)
                bias = jnp.concatenate([bt_ref[0, h, dr0 + 2 * m] for m in range(kr // 2)], axis=-1)
                s_nb = jnp.where(col_valid, s_nb + bias, NEG_INF)
                chains.append([(s_nb, vw), (s_ctx[h][j * W:(j + 1) * W], v_s[0:L, :])])
        outs = _softmax_pv(chains)
        for j in range(rpb):
            o_ref[0, pl.ds(q0s[j], W), :] = jnp.where(first_head, outs[2 * j], outs[2 * j + 1])
        return carry

    lax.fori_loop(0, rows // rpb, row_block, 0)

    tq = min(L, 256)
    lane_c = lax.broadcasted_iota(jnp.int32, (tq, LANES), 1)
    for i in range(L // tq):
        if need_ctx:
            q = q_ref[0, i * tq:(i + 1) * tq, :] * scale
            hm = [lane_c < C_HEAD_DIM, lane_c >= C_HEAD_DIM]
            s = [_dot_nt(jnp.where(hm[h], q, 0.0).astype(BF16), k_s[0:L, :]) for h in range(2)]
            outs = _softmax_pv([[(si, v_s[0:L, :])] for si in s])
            o_ref[0, i * tq:(i + 1) * tq, :] = jnp.where(lane_c < C_HEAD_DIM, outs[0], outs[1])
        else:
            o_ref[0, i * tq:(i + 1) * tq, :] = jnp.zeros((tq, LANES), F32)


def _na_attention(qkv, bias_tab, L, need_ctx):
    B, T, D3 = qkv.shape
    D = D3 // 3
    HP = D // LANES
    rows = (T - L) // GRID_W
    kr = min(WIN_R, rows)
    assert kr % 2 == 0
    nd = bias_tab.shape[2]
    return pl.pallas_call(
        functools.partial(_na_kernel, L=L, rows=rows, kr=kr, need_ctx=need_ctx, scale=float(C_HEAD_DIM ** -0.5)),
        grid=(B, HP),
        in_specs=[pl.BlockSpec((1, T, LANES), lambda b, p: (b, 0, p)),
                  pl.BlockSpec((1, T, LANES), lambda b, p: (b, 0, HP + p)),
                  pl.BlockSpec((1, T, LANES), lambda b, p: (b, 0, 2 * HP + p)),
                  pl.BlockSpec((1, 2, nd, GRID_W, LANES), lambda b, p: (p, 0, 0, 0, 0))],
        out_specs=pl.BlockSpec((1, T, LANES), lambda b, p: (b, 0, p)),
        out_shape=jax.ShapeDtypeStruct((B, T, D), F32),
        scratch_shapes=[pltpu.VMEM((T, LANES), BF16), pltpu.VMEM((T, LANES), BF16)],
        compiler_params=_params("arbitrary", "arbitrary"),
        name="na_attention",
    )(qkv, qkv, qkv, bias_tab)


def _na_bias_table(rpb):
    H = rpb.shape[0]
    qc = np.arange(GRID_W)[:, None]
    kc = np.arange(GRID_W)[None, :]
    dc = np.clip(kc - qc + (WIN_C - 1), 0, 2 * WIN_C - 2)
    t = rpb[:, :, dc]
    t2 = jnp.concatenate([t[:, :-1], t[:, 1:]], axis=-1)
    return t2.reshape(H // 2, 2, 2 * WIN_R - 2, GRID_W, 2 * GRID_W)


def _bf(x):
    return x.astype(BF16)


def _seg_sum(x, ones_bd):
    hi = _bf(x)
    lo = _bf(x - hi.astype(F32))
    return _dot(hi, ones_bd) + _dot(lo, ones_bd)


def _head_ones(A):
    seg = np.arange(A) // A_HEAD_DIM
    return jnp.asarray(seg[:, None] == seg[None, :], dtype=BF16)


def _rwkv_prep_kernel(z_ref, zp_ref, zn_ref, mu_ref, w0_ref, w2_ref, a0_ref, a2_ref, g2_ref, kk_ref, ka_ref, rk_ref,
                      ones_ref, r_o, v_o, kkn_o, g_o, bonus_o, lw_o, beta_o, kd_o, *, nct, nt, A):
    i = pl.program_id(1)
    za = z_ref[0]
    tm = za.shape[0]
    row = lax.broadcasted_iota(jnp.int32, za.shape, 0)
    seg_first = (i == 0) | (i == nct)
    seg_last = (i == nct - 1) | (i == nt - 1)
    prev_row = jnp.where(seg_first, 0.0, zp_ref[0, 7:8, :])
    next_row = jnp.where(seg_last, 0.0, zn_ref[0, 0:1, :])
    prev = jnp.where(row == 0, prev_row, pltpu.roll(za, 1, 0))
    nxt = jnp.where(row == tm - 1, next_row, pltpu.roll(za, tm - 1, 0))
    zs = za + mu_ref[0:1, :] * (prev - za) + mu_ref[1:2, :] * (nxt - za)
    r = zs[:, 0:A]
    k = zs[:, A:2 * A]
    v = zs[:, 2 * A:3 * A]
    wd = _bf(jnp.tanh(zs[:, 3 * A:3 * A + LANES]))
    ad = _bf(zs[:, 3 * A + LANES:3 * A + 2 * LANES])
    gd = _bf(jax.nn.sigmoid(zs[:, 3 * A + 2 * LANES:3 * A + 3 * LANES]))
    ones = ones_ref[...]
    kk = k * kk_ref[...]
    kkn = kk / jnp.maximum(jnp.sqrt(_seg_sum(kk * kk, ones)), 1e-12)
    kd_sum = None
    for d in range(2):
        w_log = -jax.nn.softplus(-(w0_ref[d:d + 1, :] + _dot(wd, w2_ref[d]))) - 0.5
        lw_o[d, 0] = -jnp.exp(w_log)
        a = jax.nn.sigmoid(a0_ref[d:d + 1, :] + _dot(ad, a2_ref[d]))
        beta_o[d, 0] = kkn * a
        kd = k * (1.0 + (a - 1.0) * ka_ref[...])
        kd_o[d, 0] = kd
        kd_sum = kd if kd_sum is None else kd_sum + kd
    bonus_o[0] = _seg_sum(r * kd_sum * rk_ref[...], ones) * v
    r_o[0] = r
    v_o[0] = v
    kkn_o[0] = kkn
    g_o[0] = _dot(gd, g2_ref[...])


def _rwkv_prep(z, L, a_cols, mu, w0, w2, a0, a2, g2, k_k, k_a, r_k):
    B, T, _ = z.shape
    A = w0.shape[-1]
    assert 2 * LORA_W == LANES and 2 * LORA_A == LANES and LORA_G == LANES and a_cols == 3 * A + 3 * LANES
    tm = _row_tile(L, T - L)
    nt = T // tm
    hb = tm // 8

    def pad_lora(w):
        zero = jnp.zeros_like(w[0])
        return _bf(jnp.stack([jnp.concatenate([w[0], zero], 0), jnp.concatenate([zero, w[1]], 0)]))

    def const(shape):
        return pl.BlockSpec(shape, lambda b, i: (0,) * len(shape))

    tile = pl.BlockSpec((1, tm, A), lambda b, i: (b, i, 0))
    tile_d = pl.BlockSpec((2, 1, tm, A), lambda b, i: (0, b, i, 0))
    sd = jax.ShapeDtypeStruct((B, T, A), F32)
    sd_d = jax.ShapeDtypeStruct((2, B, T, A), F32)
    return pl.pallas_call(
        functools.partial(_rwkv_prep_kernel, nct=L // tm, nt=nt, A=A),
        grid=(B, nt),
        in_specs=[pl.BlockSpec((1, tm, a_cols), lambda b, i: (b, i, 0)),
                  pl.BlockSpec((1, 8, a_cols), lambda b, i: (b, jnp.maximum(i * hb - 1, 0), 0)),
                  pl.BlockSpec((1, 8, a_cols), lambda b, i: (b, jnp.minimum((i + 1) * hb, T // 8 - 1), 0)),
                  const((2, a_cols)), const((2, A)), const((2, 2 * LORA_W, A)), const((2, A)),
                  const((2, 2 * LORA_A, A)), const((LORA_G, A)), const((1, A)), const((1, A)), const((1, A)),
                  const((A, A))],
        out_specs=[tile, tile, tile, tile, tile, tile_d, tile_d, tile_d],
        out_shape=[sd, sd, sd, sd, sd, sd_d, sd_d, sd_d],
        compiler_params=_params("arbitrary", "arbitrary"),
        name="rwkv_prep",
    )(z, z, z, mu, w0, pad_lora(w2), a0, pad_lora(a2), _bf(g2), k_k.reshape(1, A), k_a.reshape(1, A),
      r_k.reshape(1, A), _head_ones(A))


def _tri_inverse(lms, eye, m16, m32):
    d0 = [_bf(jnp.where(m16, lm, 0.0)) for lm in lms]
    t = [eye + d.astype(F32) for d in d0]
    s = [_dot(d, d) for d in d0]
    for step in range(3):
        sb = [_bf(x) for x in s]
        t = [x + _dot(_bf(x), y) for x, y in zip(t, sb)]
        if step < 2:
            s = [_dot(y, y) for y in sb]
    for lvl in (m32 & (~m16), ~m32):
        tb = [_bf(x) for x in t]
        w = [_bf(_dot(_bf(jnp.where(lvl, lm, 0.0)), y)) for lm, y in zip(lms, tb)]
        t = [x + _dot(y, z) for x, y, z in zip(t, tb, w)]
    return t


def _wkv_kernel(*refs, NP):
    C = SCAN_CHUNK
    P = 2 * C
    fwd_refs, bwd_refs, (yf_ref, yb_ref, h_s) = refs[0:6], refs[6:12], refs[12:]

    @pl.when(pl.program_id(1) == 0)
    def _():
        h_s[...] = jnp.zeros(h_s.shape, F32)

    ri = lax.broadcasted_iota(jnp.int32, (P, P), 0)
    ci = lax.broadcasted_iota(jnp.int32, (P, P), 1)
    same = (ri // C) == (ci // C)
    diff = (ri % C) - (ci % C)
    eye_b = ri == ci
    eye = eye_b.astype(F32)
    m16 = (ri // 16) == (ci // 16)
    m32 = (ri // 32) == (ci // 32)
    diff64 = lax.broadcasted_iota(jnp.int32, (C, C), 0) - lax.broadcasted_iota(jnp.int32, (C, C), 1)
    top = lax.broadcasted_iota(jnp.int32, (C, P), 1) < C
    zero_blk = jnp.zeros((P, P), BF16)
    sls = [slice(p * P, (p + 1) * P) for p in range(NP)]

    def bd(x):
        return [_bf(jnp.concatenate([jnp.where(top, x[:, sl], 0.0), jnp.where(top, 0.0, x[:, sl])], axis=0))
                for sl in sls]

    at, rt, bt, kt, bh, kh, vv, etots, before, before_eq = [], [], [], [], [], [], [], [], [], []
    for (r_ref, v_ref, kk_ref, lw_ref, beta_ref, kd_ref), sgn in ((fwd_refs, 1), (bwd_refs, -1)):
        order = diff * sgn
        before += [same & (order > 0)] * NP
        before_eq += [same & (order >= 0)] * NP
        tri = _bf(((diff64 * sgn) >= 0).astype(F32))
        lw = lw_ref[0, 0]
        lw_hi = _bf(lw)
        lw_md = _bf(lw - lw_hi.astype(F32))
        lw_lo = _bf(lw - lw_hi.astype(F32) - lw_md.astype(F32))
        cum = _dot(tri, lw_hi) + _dot(tri, lw_md) + _dot(tri, lw_lo)
        tot = jnp.sum(lw, axis=0, keepdims=True)
        beta = beta_ref[0, 0]
        kd = kd_ref[0, 0]
        e_neg = jnp.exp(-cum)
        e_tail = jnp.exp(tot - cum)
        at += bd(-kk_ref[0] * jnp.exp(cum - lw))
        rt += bd(r_ref[0] * jnp.exp(cum))
        bt += bd(beta * e_neg)
        kt += bd(kd * e_neg)
        bh += bd(beta * e_tail)
        kh += bd(kd * e_tail)
        vv += bd(v_ref[0])
        etot = jnp.exp(tot)
        etots += [etot[:, sl] for sl in sls]

    ar = [jnp.concatenate([a, r], axis=0) for a, r in zip(at, rt)]
    arb = [_dot_nt(x, b) for x, b in zip(ar, bt)]
    ark = [_dot_nt(x, k) for x, k in zip(ar, kt)]
    lab = [jnp.where(m, x[:P], 0.0) for x, m in zip(arb, before)]
    tinv = _tri_inverse(lab, eye, m16, m32)
    u = [_dot(_bf(jnp.where(m, x[:P], 0.0)), v) for x, v, m in zip(ark, vv, before)]
    x = [_bf(_dot(_bf(t), jnp.concatenate([a, _bf(w)], axis=1))) for t, a, w in zip(tinv, at, u)]
    rhs = [jnp.concatenate([xi, jnp.concatenate([zero_blk, v], axis=1)], axis=0) for xi, v in zip(x, vv)]
    mn = [lax.dot_general(jnp.concatenate([b, k], axis=0), w, (((0,), (0,)), ((), ())), preferred_element_type=F32)
          for b, k, w in zip(bh, kh, rhs)]
    lr = [_bf(jnp.concatenate([jnp.where(m, xb[P:], 0.0), jnp.where(m, xk[P:], 0.0)], axis=1))
          for xb, xk, m in zip(arb, ark, before_eq)]
    qy = [_dot(l, w) for l, w in zip(lr, rhs)]
    qm = [_bf(jnp.concatenate([r.astype(F32) + q[:, :P], jnp.where(eye_b, e, 0.0) + m[:, :P]], axis=0))
          for r, q, m, e in zip(rt, qy, mn, etots)]
    hin = [h_s[i] for i in range(2 * NP)]
    h_hi = [_bf(h) for h in hin]
    h_lo = [_bf(h - hh.astype(F32)) for h, hh in zip(hin, h_hi)]
    res = [_dot(w, hh) + _dot(w, hl) for w, hh, hl in zip(qm, h_hi, h_lo)]
    for i in range(2 * NP):
        ybd = res[i][:P] + qy[i][:, P:]
        y_ref = yf_ref if i < NP else yb_ref
        y_ref[0, :, sls[i % NP]] = ybd[:C] + ybd[C:]
        h_s[i] = res[i][P:] + mn[i][:, P:]


def _wkv_scan(r, v, kk, lw, beta, kd, L):
    B, T, A = r.shape
    C = SCAN_CHUNK
    nC = T // C
    nct = L // C
    NP = A // (2 * C)

    def rev(c):
        return jnp.where(c < nct, nct - 1 - c, nC - 1 - (c - nct))

    fwd = pl.BlockSpec((1, C, A), lambda b, c: (b, c, 0))
    bwd = pl.BlockSpec((1, C, A), lambda b, c: (b, rev(c), 0))
    fwd_d = pl.BlockSpec((1, 1, C, A), lambda b, c: (0, b, c, 0))
    bwd_d = pl.BlockSpec((1, 1, C, A), lambda b, c: (1, b, rev(c), 0))
    sd = jax.ShapeDtypeStruct((B, T, A), F32)
    return pl.pallas_call(
        functools.partial(_wkv_kernel, NP=NP),
        grid=(B, nC),
        in_specs=[fwd, fwd, fwd, fwd_d, fwd_d, fwd_d, bwd, bwd, bwd, bwd_d, bwd_d, bwd_d],
        out_specs=[fwd, bwd],
        out_shape=[sd, sd],
        scratch_shapes=[pltpu.VMEM((2 * NP, 2 * C, 2 * C), F32)],
        compiler_params=_params("arbitrary", "arbitrary"),
        name="wkv_scan",
    )(r, v, kk, lw, beta, kd, r, v, kk, lw, beta, kd)


def _rwkv_post_kernel(yf_ref, yb_ref, bonus_ref, g_ref, lnw_ref, lnb_ref, ones_ref, o_ref):
    y = yf_ref[0] + yb_ref[0]
    ones = ones_ref[...]
    inv_n = 1.0 / A_HEAD_DIM
    d = y - _seg_sum(y, ones) * inv_n
    var = _seg_sum(d * d, ones) * inv_n
    yn = d * lax.rsqrt(var + GN_EPS) * lnw_ref[...] + lnb_ref[...]
    o_ref[0] = (yn + bonus_ref[0]) * g_ref[0]


def _rwkv_post(yf, yb, bonus, g, ln_w, ln_b, L):
    B, T, A = bonus.shape
    tm = _row_tile(L, T - L)
    tile = pl.BlockSpec((1, tm, A), lambda b, i: (b, i, 0))
    vec = pl.BlockSpec((1, A), lambda b, i: (0, 0))
    return pl.pallas_call(
        _rwkv_post_kernel,
        grid=(B, T // tm),
        in_specs=[tile, tile, tile, tile, vec, vec, pl.BlockSpec((A, A), lambda b, i: (0, 0))],
        out_specs=tile,
        out_shape=jax.ShapeDtypeStruct((B, T, A), F32),
        compiler_params=_params("arbitrary", "arbitrary"),
        name="rwkv_post",
    )(yf, yb, bonus, g, ln_w.reshape(1, A), ln_b.reshape(1, A), _head_ones(A))


def _rwkv_mixer(z, L, a_cols, mu, w0, w2, a0, a2, g2, k_k, k_a, r_k, ln_w, ln_b):
    r, v, kk, g, bonus, lw, beta, kd = _rwkv_prep(z, L, a_cols, mu, w0, w2, a0, a2, g2, k_k, k_a, r_k)
    yf, yb = _wkv_scan(r, v, kk, lw, beta, kd, L)
    return _rwkv_post(yf, yb, bonus, g, ln_w, ln_b, L)


def _router_kernel(x_ref, nw_ref, ss_ref, wr_ref, h_ref, aff_ref):
    h = _norm_mod(x_ref[0], nw_ref[...], ss_ref[0, 0])
    h_ref[0] = h.astype(BF16)
    logits = _dot_nt(wr_ref[...], h, HIGHEST)
    m = jnp.max(logits, axis=0, keepdims=True)
    p = jnp.exp(logits - m)
    aff_ref[0] = p / jnp.sum(p, axis=0, keepdims=True)


def _norm_router(xa, nw, ss, w_router, L):
    B, T, D = xa.shape
    E = w_router.shape[1]
    tm = _row_tile(L, T - L)
    nct = L // tm
    return pl.pallas_call(
        _router_kernel,
        grid=(B, T // tm),
        in_specs=[pl.BlockSpec((1, tm, D), lambda b, i: (b, i, 0)),
                  pl.BlockSpec((1, D), lambda b, i: (0, 0)),
                  pl.BlockSpec((1, 1, 2, D), lambda b, i: (b, jnp.where(i >= nct, 1, 0), 0, 0)),
                  pl.BlockSpec((E, D), lambda b, i: (0, 0))],
        out_specs=[pl.BlockSpec((1, tm, D), lambda b, i: (b, i, 0)),
                   pl.BlockSpec((1, E, tm), lambda b, i: (b, 0, i))],
        out_shape=[jax.ShapeDtypeStruct((B, T, D), BF16), jax.ShapeDtypeStruct((B, E, T), F32)],
        compiler_params=_params("arbitrary", "arbitrary"),
        name="norm_router",
    )(xa, nw.reshape(1, D), ss, w_router.T)


def _gather_kernel(h_ref, pos_ref, o_ref, *, L, cap_l):
    T = h_ref.shape[1]
    Ct = pos_ref.shape[2]
    pos = pos_ref[0, 0]
    tok = lax.broadcasted_iota(jnp.int32, (cap_l, T - L), 1) + L
    onehot = jnp.where(tok == pos[:cap_l], 1.0, 0.0).astype(BF16)
    o_ref[0, 0, :cap_l, :] = _dot(onehot, h_ref[0, L:, :]).astype(BF16)
    if Ct > cap_l:
        tok = lax.broadcasted_iota(jnp.int32, (Ct - cap_l, L), 1)
        onehot = jnp.where(tok == pos[cap_l:], 1.0, 0.0).astype(BF16)
        o_ref[0, 0, cap_l:, :] = _dot(onehot, h_ref[0, :L, :]).astype(BF16)


def _moe_gather(h2, pos, L, cap_l):
    B, T, D = h2.shape
    E, Ct = pos.shape[1], pos.shape[2]
    return pl.pallas_call(
        functools.partial(_gather_kernel, L=L, cap_l=cap_l),
        grid=(B, E),
        in_specs=[pl.BlockSpec((1, T, D), lambda b, e: (b, 0, 0)),
                  pl.BlockSpec((1, 1, Ct, 1), lambda b, e: (b, e, 0, 0))],
        out_specs=pl.BlockSpec((1, 1, Ct, D), lambda b, e: (e, b, 0, 0)),
        out_shape=jax.ShapeDtypeStruct((E, B, Ct, D), BF16),
        compiler_params=_params("arbitrary", "arbitrary"),
        name="moe_gather",
    )(h2, pos)


def _ffn_kernel(x_ref, w1_ref, w3_ref, w2_ref, o_ref, *, rm):
    j = pl.program_id(1)
    R = x_ref.shape[1]
    w1 = w1_ref[0, 0].astype(BF16)
    w3 = w3_ref[0, 0].astype(BF16)
    w2 = w2_ref[0, 0].astype(BF16)

    def rows(i, carry):
        r0 = pl.multiple_of(i * rm, rm)
        x = x_ref[0, pl.ds(r0, rm), :]
        a = _dot(x, w1)
        b = _dot(x, w3)
        hid = (a * jax.nn.sigmoid(a) * b).astype(BF16)
        y = _dot(hid, w2)

        @pl.when(j == 0)
        def _():
            o_ref[0, pl.ds(r0, rm), :] = y

        @pl.when(j != 0)
        def _():
            o_ref[0, pl.ds(r0, rm), :] += y

        return carry

    lax.fori_loop(0, R // rm, rows, 0)


def _expert_ffn(xin, w1, w3, w2, layer):
    E, R, D = xin.shape
    F = w1.shape[-1]
    tf = min(512, F)
    rm = max(m for m in (768, 512, 256, 128, 64, 32, 16) if R % m == 0)
    return pl.pallas_call(
        functools.partial(_ffn_kernel, rm=rm),
        grid=(E, F // tf),
        in_specs=[pl.BlockSpec((1, R, D), lambda e, j: (e, 0, 0)),
                  pl.BlockSpec((1, 1, D, tf), lambda e, j: (layer, e, 0, j)),
                  pl.BlockSpec((1, 1, D, tf), lambda e, j: (layer, e, 0, j)),
                  pl.BlockSpec((1, 1, tf, D), lambda e, j: (layer, e, j, 0))],
        out_specs=pl.BlockSpec((1, R, D), lambda e, j: (e, 0, 0)),
        out_shape=jax.ShapeDtypeStruct((E, R, D), F32),
        compiler_params=_params("arbitrary", "arbitrary"),
        name="expert_ffn",
    )(xin, w1, w3, w2)


def _combine_kernel(y_ref, gate_ref, posl_ref, posc_ref, x_ref, g_ref, o_ref, yg_s, *, tq, nct, cap_l):
    E, _, Ct, td = y_ref.shape
    T = x_ref.shape[1]
    cap_c = Ct - cap_l
    nl = E * cap_l
    for e in range(E):
        yg_s[e * cap_l:(e + 1) * cap_l, :] = (y_ref[e, 0, :cap_l, :] * gate_ref[0, e, :cap_l, :]).astype(BF16)
        if cap_c:
            yg_s[nl + e * cap_c:nl + (e + 1) * cap_c, :] = (y_ref[e, 0, cap_l:, :] * gate_ref[0, e, cap_l:, :]).astype(BF16)

    def tile(i, gate_row, pos, lo, n):
        r0 = pl.multiple_of(i * tq, tq)
        tok = lax.broadcasted_iota(jnp.int32, (tq, n), 0) + r0
        onehot = jnp.where(tok == pos, 1.0, 0.0).astype(BF16)
        o_ref[0, pl.ds(r0, tq), :] = x_ref[0, pl.ds(r0, tq), :] + gate_row * _dot(onehot, yg_s[lo:lo + n, :])

    for i in range(nct):
        if cap_c:
            tile(i, g_ref[0, 0], posc_ref[0], nl, E * cap_c)
        else:
            o_ref[0, i * tq:(i + 1) * tq, :] = x_ref[0, i * tq:(i + 1) * tq, :]

    def body(i, carry):
        tile(i, g_ref[0, 1], posl_ref[0], 0, nl)
        return carry

    lax.fori_loop(nct, T // tq, body, 0)


def _moe_combine(y, gate, pos_l, pos_c, xa, g2, L, cap_l):
    E, B, Ct, D = y.shape
    T = xa.shape[1]
    td = min(512, D)
    tq = _row_tile(L, T - L)
    return pl.pallas_call(
        functools.partial(_combine_kernel, tq=tq, nct=L // tq, cap_l=cap_l),
        grid=(B, D // td),
        in_specs=[pl.BlockSpec((E, 1, Ct, td), lambda b, j: (0, b, 0, j)),
                  pl.BlockSpec((1, E, Ct, 1), lambda b, j: (b, 0, 0, 0)),
                  pl.BlockSpec((1, 1, pos_l.shape[-1]), lambda b, j: (b, 0, 0)),
                  pl.BlockSpec((1, 1, pos_c.shape[-1]), lambda b, j: (b, 0, 0)),
                  pl.BlockSpec((1, T, td), lambda b, j: (b, 0, j)),
                  pl.BlockSpec((1, 2, 1, td), lambda b, j: (b, 0, 0, j))],
        out_specs=pl.BlockSpec((1, T, td), lambda b, j: (b, 0, j)),
        out_shape=jax.ShapeDtypeStruct((B, T, D), F32),
        scratch_shapes=[pltpu.VMEM((E * Ct, td), BF16)],
        compiler_params=_params("arbitrary", "arbitrary"),
        name="moe_combine",
    )(y, gate, pos_l, pos_c, xa, g2)


def _moe(xa, nw, ss2, g2, w_router, w1, w3, w2, layer, L, need_ctx):
    B, T, D = xa.shape
    S = T - L
    E = w_router.shape[1]
    h2, aff = _norm_router(xa, nw, ss2, w_router, L)
    cap_l = CAPACITY_FACTOR * S // E
    gate_l, idx_l = lax.top_k(aff[:, :, L:], cap_l)
    gates, poss = [gate_l], [(idx_l + L).astype(jnp.int32)]
    pos_c = jnp.zeros((B, 1, LANES), jnp.int32)
    if need_ctx:
        cap_c = CAPACITY_FACTOR * L // E
        gate_c, idx_c = lax.top_k(aff[:, :, :L], cap_c)
        gates.append(gate_c)
        poss.append(idx_c.astype(jnp.int32))
        pos_c = poss[1].reshape(B, 1, E * cap_c)
    gate = jnp.concatenate(gates, axis=-1)
    pos = jnp.concatenate(poss, axis=-1)
    Ct = pos.shape[-1]
    xin = _moe_gather(h2, pos[..., None], L, cap_l).reshape(E, B * Ct, D)
    y = _expert_ffn(xin, w1, w3, w2, layer).reshape(E, B, Ct, D)
    return _moe_combine(y, gate[..., None], poss[0].reshape(B, 1, E * cap_l), pos_c, xa, g2, L, cap_l)


_PERM_EO = np.concatenate([np.arange(0, QK_ROPE, 2), np.arange(1, QK_ROPE, 2)])
_PERM_OE = np.concatenate([np.arange(1, QK_ROPE, 2), np.arange(0, QK_ROPE, 2)])


def _ab_input_weight(w_in, a_cols):
    D = w_in.shape[0]
    zr = w_in[:, a_cols + Q_RANK + KV_RANK:]
    zero = jnp.zeros((D, LANES - 2 * QK_ROPE), w_in.dtype)
    g1 = jnp.concatenate([zero, zr[:, _PERM_EO], zr[:, _PERM_EO]], axis=1)
    g2 = jnp.concatenate([zero, zr[:, _PERM_OE], zr[:, _PERM_OE]], axis=1)
    return jnp.concatenate([w_in[:, :a_cols + Q_RANK + KV_RANK], g1, g2], axis=1).astype(BF16)


def _mla_weights(w_qup, w_kvup):
    NH = w_qup.shape[1] // (QK_NOPE + QK_ROPE)
    wq = w_qup.reshape(Q_RANK, NH, QK_NOPE + QK_ROPE)
    rope = wq[:, :, QK_NOPE:]
    wq = jnp.concatenate([wq[:, :, :QK_NOPE], rope[:, :, _PERM_EO], rope[:, :, _PERM_OE]], axis=-1)
    wq = wq.reshape(Q_RANK, NH // 2, 2 * LANES).transpose(1, 0, 2)
    wkv = w_kvup.reshape(KV_RANK, NH, QK_NOPE + V_HEAD)
    wk = jnp.concatenate([wkv[:, :, :QK_NOPE], jnp.zeros((KV_RANK, NH, LANES - QK_NOPE), w_kvup.dtype)], axis=-1)
    wk = wk.reshape(KV_RANK, NH // 2, 2 * LANES).transpose(1, 0, 2)
    wv = wkv[:, :, QK_NOPE:].reshape(KV_RANK, NH // 2, 2 * V_HEAD).transpose(1, 0, 2)
    return wq.astype(BF16), wk.astype(BF16), wv.astype(BF16)


def _rope_tables(L, S):
    t = np.arange(S)
    row = (t // GRID_W).astype(np.float32)
    col = (t % GRID_W).astype(np.float32)
    n_freq = QK_ROPE // 4
    inv = (ROPE_BASE ** (-np.arange(n_freq, dtype=np.float32) / n_freq)).astype(np.float32)
    ang = jnp.concatenate([jnp.asarray(row[:, None] * inv), jnp.asarray(col[:, None] * inv)], axis=-1)
    cos = jnp.concatenate([jnp.ones((L, QK_ROPE // 2), F32), jnp.cos(ang)], axis=0)
    sin = jnp.concatenate([jnp.zeros((L, QK_ROPE // 2), F32), jnp.sin(ang)], axis=0)
    T = L + S
    cc = jnp.concatenate([cos, cos], axis=1)
    ss = jnp.concatenate([-sin, sin], axis=1)
    one = jnp.ones((T, LANES - 2 * QK_ROPE), F32)
    zero = jnp.zeros((T, LANES - 2 * QK_ROPE), F32)
    cq = jnp.concatenate([one, cc, ss], axis=1)
    ck = jnp.concatenate([zero, cc, cc], axis=1)
    sk = jnp.concatenate([zero, ss, ss], axis=1)
    return cq, ck, sk


def kernel(x, c, ctx, c_ctx, mod_w, mod_b, norm1_w, norm2_w, final_norm_w, ab_w_in, ab_w_out, rk_mu, rk_w0, rk_w2, rk_a0, rk_a2, rk_g2, rk_kk, rk_ka, rk_rk, rk_ln_w, rk_ln_b, mla_qn_w, mla_w_qup, mla_kvn_w, mla_w_kvup, na_w_qkv, na_rpb, na_w_out, moe_router, moe_w1, moe_w3, moe_w2):
    B, S, D = x.shape
    L = ctx.shape[1]
    depth = mod_w.shape[0]
    A = rk_w0.shape[-1]
    a_cols = rk_mu.shape[-1]

    rows_pad = -(B + 1) % 8
    cvec = jnp.concatenate([c, c_ctx[None], jnp.zeros((rows_pad, D), F32)], axis=0)
    mods = _mod_vectors(cvec, mod_w, mod_b)
    m_lat = mods[:, :B].reshape(depth, B, 6, D)
    m_ctx = jnp.broadcast_to(mods[:, B].reshape(depth, 1, 6, D), (depth, B, 6, D))
    mm = jnp.stack([m_ctx, m_lat], axis=2)

    cq, ck, sk = _rope_tables(L, S)
    xa = jnp.concatenate([ctx, x], axis=1)

    for layer in range(depth):
        need_ctx = layer < depth - 1
        i = layer // 2
        m = mm[layer]
        ss1, g1 = m[:, :, 0:2], m[:, :, 2:3]
        ss2, g2 = m[:, :, 3:5], m[:, :, 5:6]
        if layer % 2 == 0:
            w_in = _ab_input_weight(ab_w_in[i], a_cols)
            z = _norm_linear(xa, norm1_w[layer], ss1, w_in, L)
            o_a = _rwkv_mixer(z, L, a_cols, rk_mu[i], rk_w0[i], rk_w2[i], rk_a0[i], rk_a2[i], rk_g2[i],
                              rk_kk[i], rk_ka[i], rk_rk[i], rk_ln_w[i], rk_ln_b[i])
            wq, wk, wv = _mla_weights(mla_w_qup[i], mla_w_kvup[i])
            o_b = _mla_attention(z, mla_qn_w[i], mla_kvn_w[i], wq, wk, wv, cq, ck, sk, L,
                                 a_cols, a_cols + Q_RANK, a_cols + Q_RANK + KV_RANK)
            xa = _linear_resid([o_a, o_b], ab_w_out[i].astype(BF16), xa, g1, L)
        else:
            qkv = _norm_linear(xa, norm1_w[layer], ss1, na_w_qkv[i].astype(BF16), L)
            o = _na_attention(qkv, _na_bias_table(na_rpb[i]), L, need_ctx)
            xa = _linear_resid([o], na_w_out[i].astype(BF16), xa, g1, L)
        xa = _moe(xa, norm2_w[layer], ss2, g2, moe_router[layer], moe_w1, moe_w3, moe_w2, layer, L, need_ctx)
    return _final_norm(xa, final_norm_w, L)
```

```python
import functools

import jax
import jax.numpy as jnp
import numpy as np
from jax import lax
from jax.experimental import pallas as pl
from jax.experimental.pallas import tpu as pltpu

F32 = jnp.float32
BF16 = jnp.bfloat16
HIGHEST = lax.Precision.HIGHEST

GRID_W = 64
NORM_EPS = 1e-6
NEG_INF = -1e30
GN_EPS = 64e-5
A_HEAD_DIM = 64
LORA_W = 64
LORA_A = 64
LORA_G = 128
QK_NOPE = 64
QK_ROPE = 32
V_HEAD = 64
Q_RANK = 384
KV_RANK = 256
ROPE_BASE = 10000.0
C_HEAD_DIM = 64
WIN_R = 8
WIN_C = 16
N_EXPERTS = 16
CAPACITY_FACTOR = 2
SCAN_CHUNK = 64
LANES = 128

VMEM_LIMIT = 56 * 1024 * 1024


def _params(*sem):
    return pltpu.CompilerParams(dimension_semantics=sem, vmem_limit_bytes=VMEM_LIMIT)


def _dot(a, b, precision=None):
    return jnp.dot(a, b, preferred_element_type=F32, precision=precision)


def _dot_nt(a, b, precision=None):
    return lax.dot_general(a, b, (((1,), (1,)), ((), ())), preferred_element_type=F32, precision=precision)


def _row_tile(L, S):
    tm = 256
    while L % tm or S % tm:
        tm //= 2
    return tm


def _mod_kernel(c_ref, w_ref, b_ref, o_ref):
    c = c_ref[...]
    sc = c * jax.nn.sigmoid(c)
    o_ref[0] = _dot(sc.astype(BF16), w_ref[0].astype(BF16)) + b_ref[0]


def _mod_vectors(cvec, mod_w, mod_b):
    depth, D, N = mod_w.shape
    R = cvec.shape[0]
    tn = 1024
    return pl.pallas_call(
        _mod_kernel,
        grid=(depth, N // tn),
        in_specs=[pl.BlockSpec((R, D), lambda l, j: (0, 0)),
                  pl.BlockSpec((1, D, tn), lambda l, j: (l, 0, j)),
                  pl.BlockSpec((1, 1, tn), lambda l, j: (l, 0, j))],
        out_specs=pl.BlockSpec((1, R, tn), lambda l, j: (l, 0, j)),
        out_shape=jax.ShapeDtypeStruct((depth, R, N), F32),
        compiler_params=_params("arbitrary", "arbitrary"),
        name="mod_vectors",
    )(cvec, mod_w, mod_b.reshape(depth, 1, N))


def _norm_mod(x, nw, ss):
    y = x * lax.rsqrt(jnp.mean(x * x, axis=-1, keepdims=True) + NORM_EPS)
    y = y * nw
    return y * (1.0 + ss[1:2]) + ss[0:1]


def _norm_linear_kernel(x_ref, nw_ref, ss_ref, w_ref, o_ref):
    h = _norm_mod(x_ref[0], nw_ref[...], ss_ref[0, 0])
    o_ref[0] = _dot(h.astype(BF16), w_ref[...]).astype(o_ref.dtype)


def _norm_linear(xa, nw, ss, w, L, out_dtype=F32):
    B, T, D = xa.shape
    N = w.shape[1]
    tm = _row_tile(L, T - L)
    nct = L // tm
    return pl.pallas_call(
        _norm_linear_kernel,
        grid=(B, T // tm),
        in_specs=[pl.BlockSpec((1, tm, D), lambda b, i: (b, i, 0)),
                  pl.BlockSpec((1, D), lambda b, i: (0, 0)),
                  pl.BlockSpec((1, 1, 2, D), lambda b, i: (b, jnp.where(i >= nct, 1, 0), 0, 0)),
                  pl.BlockSpec((D, N), lambda b, i: (0, 0))],
        out_specs=pl.BlockSpec((1, tm, N), lambda b, i: (b, i, 0)),
        out_shape=jax.ShapeDtypeStruct((B, T, N), out_dtype),
        compiler_params=_params("arbitrary", "arbitrary"),
        name="norm_linear",
    )(xa, nw.reshape(1, D), ss, w)


def _linear_resid_kernel(*refs, ks):
    n = len(ks)
    a_refs, (w_ref, x_ref, g_ref, o_ref) = refs[:n], refs[n:]
    acc = None
    off = 0
    for a_ref, k in zip(a_refs, ks):
        part = _dot(a_ref[0].astype(BF16), w_ref[off:off + k, :])
        acc = part if acc is None else acc + part
        off += k
    o_ref[0] = x_ref[0] + g_ref[0, 0] * acc


def _linear_resid(a_list, w, xa, gate, L):
    B, T, D = xa.shape
    tm = _row_tile(L, T - L)
    nct = L // tm
    ks = tuple(a.shape[-1] for a in a_list)
    in_specs = [pl.BlockSpec((1, tm, k), lambda b, i: (b, i, 0)) for k in ks]
    in_specs += [pl.BlockSpec(w.shape, lambda b, i: (0, 0)),
                 pl.BlockSpec((1, tm, D), lambda b, i: (b, i, 0)),
                 pl.BlockSpec((1, 1, 1, D), lambda b, i: (b, jnp.where(i >= nct, 1, 0), 0, 0))]
    return pl.pallas_call(
        functools.partial(_linear_resid_kernel, ks=ks),
        grid=(B, T // tm),
        in_specs=in_specs,
        out_specs=pl.BlockSpec((1, tm, D), lambda b, i: (b, i, 0)),
        out_shape=jax.ShapeDtypeStruct((B, T, D), F32),
        compiler_params=_params("arbitrary", "arbitrary"),
        name="linear_resid",
    )(*a_list, w, xa, gate)


def _rms_kernel(x_ref, w_ref, o_ref):
    x = x_ref[0]
    o_ref[0] = x * lax.rsqrt(jnp.mean(x * x, axis=-1, keepdims=True) + NORM_EPS) * w_ref[...]


def _final_norm(xa, w, L):
    B, T, D = xa.shape
    S = T - L
    tm = _row_tile(L, S)
    nct = L // tm
    return pl.pallas_call(
        _rms_kernel,
        grid=(B, S // tm),
        in_specs=[pl.BlockSpec((1, tm, D), lambda b, i: (b, i + nct, 0)),
                  pl.BlockSpec((1, D), lambda b, i: (0, 0))],
        out_specs=pl.BlockSpec((1, tm, D), lambda b, i: (b, i, 0)),
        out_shape=jax.ShapeDtypeStruct((B, S, D), F32),
        compiler_params=_params("arbitrary", "arbitrary"),
        name="final_norm",
    )(xa, w.reshape(1, D))


def _rms(x, w):
    return x * lax.rsqrt(jnp.mean(x * x, axis=-1, keepdims=True) + NORM_EPS) * w


def _softmax_pv(chains):
    m = [functools.reduce(jnp.maximum, [jnp.max(s, axis=-1, keepdims=True) for s, _ in ch]) for ch in chains]
    p = [[jnp.exp(s - mi) for s, _ in ch] for ch, mi in zip(chains, m)]
    l = [functools.reduce(jnp.add, [jnp.sum(x, axis=-1, keepdims=True) for x in pc]) for pc in p]
    o = [functools.reduce(jnp.add, [_dot(x.astype(BF16), v) for x, (_, v) in zip(pc, ch)]) for pc, ch in zip(p, chains)]
    return [oi / li for oi, li in zip(o, l)]


def _mla_kernel(zq_ref, zkv_ref, zr_ref, qn_ref, kvn_ref, wq_ref, wk_ref, wv_ref, cq_ref, ck_ref, sk_ref,
                o_ref, q_s, k_s, v_s, *, L, tq, scale):
    T = zq_ref.shape[1]
    zqn = _rms(zq_ref[0], qn_ref[...]).astype(BF16)
    zkvn = _rms(zkv_ref[0], kvn_ref[...]).astype(BF16)
    qh = _dot(zqn, wq_ref[0])
    kn = _dot(zkvn, wk_ref[0])
    v_s[...] = _dot(zkvn, wv_ref[0]).astype(BF16)
    zr = zr_ref[0]
    kr = zr[:, :LANES] * ck_ref[...] + zr[:, LANES:] * sk_ref[...]
    cq = cq_ref[...] * scale
    for h in range(2):
        q_s[h] = (qh[:, h * LANES:(h + 1) * LANES] * cq).astype(BF16)
        k_s[h] = (kn[:, h * LANES:(h + 1) * LANES] + kr).astype(BF16)
    first_head = lax.broadcasted_iota(jnp.int32, (tq, LANES), 1) < V_HEAD

    def tile(row0, nk):
        s = [_dot_nt(q_s[h, pl.ds(row0, tq), :], k_s[h, 0:nk, :]) for h in range(2)]
        outs = _softmax_pv([[(si, v_s[0:nk, :])] for si in s])
        o_ref[0, pl.ds(row0, tq), :] = jnp.where(first_head, outs[0], outs[1])

    for i in range(L // tq):
        tile(i * tq, L)

    def body(i, carry):
        tile(pl.multiple_of(i * tq, tq), T)
        return carry

    lax.fori_loop(L // tq, T // tq, body, 0, unroll=2)


def _mla_attention(z, qn_w, kvn_w, wq, wk, wv, cq, ck, sk, L, col_q, col_kv, col_r):
    B, T, _ = z.shape
    HP = wq.shape[0]
    tq = _row_tile(L, T - L)
    scale = float((QK_NOPE + QK_ROPE) ** -0.5)
    return pl.pallas_call(
        functools.partial(_mla_kernel, L=L, tq=tq, scale=scale),
        grid=(B, HP),
        in_specs=[pl.BlockSpec((1, T, Q_RANK), lambda b, p: (b, 0, col_q // Q_RANK)),
                  pl.BlockSpec((1, T, KV_RANK), lambda b, p: (b, 0, col_kv // KV_RANK)),
                  pl.BlockSpec((1, T, 2 * LANES), lambda b, p: (b, 0, col_r // (2 * LANES))),
                  pl.BlockSpec((1, Q_RANK), lambda b, p: (0, 0)),
                  pl.BlockSpec((1, KV_RANK), lambda b, p: (0, 0)),
                  pl.BlockSpec((1, Q_RANK, 2 * LANES), lambda b, p: (p, 0, 0)),
                  pl.BlockSpec((1, KV_RANK, 2 * LANES), lambda b, p: (p, 0, 0)),
                  pl.BlockSpec((1, KV_RANK, LANES), lambda b, p: (p, 0, 0)),
                  pl.BlockSpec((T, LANES), lambda b, p: (0, 0)),
                  pl.BlockSpec((T, LANES), lambda b, p: (0, 0)),
                  pl.BlockSpec((T, LANES), lambda b, p: (0, 0))],
        out_specs=pl.BlockSpec((1, T, LANES), lambda b, p: (b, 0, p)),
        out_shape=jax.ShapeDtypeStruct((B, T, HP * LANES), F32),
        scratch_shapes=[pltpu.VMEM((2, T, LANES), BF16), pltpu.VMEM((2, T, LANES), BF16),
                        pltpu.VMEM((T, LANES), BF16)],
        compiler_params=_params("arbitrary", "arbitrary"),
        name="mla_attention",
    )(z, z, z, qn_w.reshape(1, -1), kvn_w.reshape(1, -1), wq, wk, wv, cq, ck, sk)


def _na_kernel(q_ref, k_ref, v_ref, bt_ref, o_ref, k_s, v_s, *, L, rows, kr, need_ctx, scale):
    W = GRID_W
    rpb = 4 if rows % 4 == 0 else 1
    k_s[...] = k_ref[0].astype(BF16)
    v_s[...] = v_ref[0].astype(BF16)
    nwin = kr * W
    lane = lax.broadcasted_iota(jnp.int32, (W, LANES), 1)
    head_mask = [(lane < C_HEAD_DIM).astype(F32), (lane >= C_HEAD_DIM).astype(F32)]
    first_head = lane < C_HEAD_DIM
    qcol = lax.broadcasted_iota(jnp.int32, (W, nwin), 0)
    kcol = lax.broadcasted_iota(jnp.int32, (W, nwin), 1) % W
    cstart = jnp.clip(qcol - WIN_C // 2, 0, W - WIN_C)
    col_valid = (kcol >= cstart) & (kcol < cstart + WIN_C)

    def row_block(rb, carry):
        q_blk = q_ref[0, pl.ds(pl.multiple_of(L + rb * (rpb * W), W), rpb * W), :] * scale
        s_ctx = [_dot_nt((q_blk * jnp.concatenate([head_mask[h]] * rpb, axis=0)).astype(BF16), k_s[0:L, :])
                 for h in range(2)]
        chains, q0s = [], []
        for j in range(rpb):
            r = rb * rpb + j
            rs = jnp.clip(r - kr // 2, 0, rows - kr)
            k0 = pl.multiple_of(L + rs * W, W)
            q0s.append(pl.multiple_of(L + r * W, W))
            q = q_blk[j * W:(j + 1) * W]
            kw = k_s[pl.ds(k0, nwin), :]
            vw = v_s[pl.ds(k0, nwin), :]
            dr0 = rs - r + (WIN_R - 1)
            for h in range(2):
                s_nb = _dot_nt((q * head_mask[h]).astype(BF16), kw)
                bias = jnp.concatenate([bt_ref[0, h, dr0 + 2 * m] for m in range(kr // 2)], axis=-1)
                s_nb = jnp.where(col_valid, s_nb + bias, NEG_INF)
                chains.append([(s_nb, vw), (s_ctx[h][j * W:(j + 1) * W], v_s[0:L, :])])
        outs = _softmax_pv(chains)
        for j in range(rpb):
            o_ref[0, pl.ds(q0s[j], W), :] = jnp.where(first_head, outs[2 * j], outs[2 * j + 1])
        return carry

    lax.fori_loop(0, rows // rpb, row_block, 0, unroll=2)

    tq = min(L, 256)
    lane_c = lax.broadcasted_iota(jnp.int32, (tq, LANES), 1)
    for i in range(L // tq):
        if need_ctx:
            q = q_ref[0, i * tq:(i + 1) * tq, :] * scale
            hm = [lane_c < C_HEAD_DIM, lane_c >= C_HEAD_DIM]
            s = [_dot_nt(jnp.where(hm[h], q, 0.0).astype(BF16), k_s[0:L, :]) for h in range(2)]
            outs = _softmax_pv([[(si, v_s[0:L, :])] for si in s])
            o_ref[0, i * tq:(i + 1) * tq, :] = jnp.where(lane_c < C_HEAD_DIM, outs[0], outs[1])
        else:
            o_ref[0, i * tq:(i + 1) * tq, :] = jnp.zeros((tq, LANES), F32)


def _na_attention(qkv, bias_tab, L, need_ctx):
    B, T, D3 = qkv.shape
    D = D3 // 3
    HP = D // LANES
    rows = (T - L) // GRID_W
    kr = min(WIN_R, rows)
    assert kr % 2 == 0
    nd = bias_tab.shape[2]
    return pl.pallas_call(
        functools.partial(_na_kernel, L=L, rows=rows, kr=kr, need_ctx=need_ctx, scale=float(C_HEAD_DIM ** -0.5)),
        grid=(B, HP),
        in_specs=[pl.BlockSpec((1, T, LANES), lambda b, p: (b, 0, p)),
                  pl.BlockSpec((1, T, LANES), lambda b, p: (b, 0, HP + p)),
                  pl.BlockSpec((1, T, LANES), lambda b, p: (b, 0, 2 * HP + p)),
                  pl.BlockSpec((1, 2, nd, GRID_W, LANES), lambda b, p: (p, 0, 0, 0, 0))],
        out_specs=pl.BlockSpec((1, T, LANES), lambda b, p: (b, 0, p)),
        out_shape=jax.ShapeDtypeStruct((B, T, D), F32),
        scratch_shapes=[pltpu.VMEM((T, LANES), BF16), pltpu.VMEM((T, LANES), BF16)],
        compiler_params=_params("arbitrary", "arbitrary"),
        name="na_attention",
    )(qkv, qkv, qkv, bias_tab)


def _na_bias_table(rpb):
    H = rpb.shape[0]
    qc = np.arange(GRID_W)[:, None]
    kc = np.arange(GRID_W)[None, :]
    dc = np.clip(kc - qc + (WIN_C - 1), 0, 2 * WIN_C - 2)
    t = rpb[:, :, dc]
    t2 = jnp.concatenate([t[:, :-1], t[:, 1:]], axis=-1)
    return t2.reshape(H // 2, 2, 2 * WIN_R - 2, GRID_W, 2 * GRID_W)


def _bf(x):
    return x.astype(BF16)


def _seg_sum(x, ones_bd):
    hi = _bf(x)
    lo = _bf(x - hi.astype(F32))
    return _dot(hi, ones_bd) + _dot(lo, ones_bd)


def _head_ones(A):
    seg = np.arange(A) // A_HEAD_DIM
    return jnp.asarray(seg[:, None] == seg[None, :], dtype=BF16)


def _rwkv_prep_kernel(z_ref, zp_ref, zn_ref, mu_ref, w0_ref, w2_ref, a0_ref, a2_ref, g2_ref, kk_ref, ka_ref, rk_ref,
                      ones_ref, r_o, v_o, kkn_o, g_o, bonus_o, lw_o, beta_o, kd_o, *, nct, nt, A):
    i = pl.program_id(1)
    za = z_ref[0]
    tm = za.shape[0]
    row = lax.broadcasted_iota(jnp.int32, za.shape, 0)
    seg_first = (i == 0) | (i == nct)
    seg_last = (i == nct - 1) | (i == nt - 1)
    prev_row = jnp.where(seg_first, 0.0, zp_ref[0, 7:8, :])
    next_row = jnp.where(seg_last, 0.0, zn_ref[0, 0:1, :])
    prev = jnp.where(row == 0, prev_row, pltpu.roll(za, 1, 0))
    nxt = jnp.where(row == tm - 1, next_row, pltpu.roll(za, tm - 1, 0))
    zs = za + mu_ref[0:1, :] * (prev - za) + mu_ref[1:2, :] * (nxt - za)
    r = zs[:, 0:A]
    k = zs[:, A:2 * A]
    v = zs[:, 2 * A:3 * A]
    wd = _bf(jnp.tanh(zs[:, 3 * A:3 * A + LANES]))
    ad = _bf(zs[:, 3 * A + LANES:3 * A + 2 * LANES])
    gd = _bf(jax.nn.sigmoid(zs[:, 3 * A + 2 * LANES:3 * A + 3 * LANES]))
    ones = ones_ref[...]
    kk = k * kk_ref[...]
    kkn = kk / jnp.maximum(jnp.sqrt(_seg_sum(kk * kk, ones)), 1e-12)
    kd_sum = None
    for d in range(2):
        w_log = -jax.nn.softplus(-(w0_ref[d:d + 1, :] + _dot(wd, w2_ref[d]))) - 0.5
        lw_o[d, 0] = -jnp.exp(w_log)
        a = jax.nn.sigmoid(a0_ref[d:d + 1, :] + _dot(ad, a2_ref[d]))
        beta_o[d, 0] = kkn * a
        kd = k * (1.0 + (a - 1.0) * ka_ref[...])
        kd_o[d, 0] = kd
        kd_sum = kd if kd_sum is None else kd_sum + kd
    bonus_o[0] = _seg_sum(r * kd_sum * rk_ref[...], ones) * v
    r_o[0] = r
    v_o[0] = v
    kkn_o[0] = kkn
    g_o[0] = _dot(gd, g2_ref[...])


def _rwkv_prep(z, L, a_cols, mu, w0, w2, a0, a2, g2, k_k, k_a, r_k):
    B, T, _ = z.shape
    A = w0.shape[-1]
    assert 2 * LORA_W == LANES and 2 * LORA_A == LANES and LORA_G == LANES and a_cols == 3 * A + 3 * LANES
    tm = _row_tile(L, T - L)
    nt = T // tm
    hb = tm // 8

    def pad_lora(w):
        zero = jnp.zeros_like(w[0])
        return _bf(jnp.stack([jnp.concatenate([w[0], zero], 0), jnp.concatenate([zero, w[1]], 0)]))

    def const(shape):
        return pl.BlockSpec(shape, lambda b, i: (0,) * len(shape))

    tile = pl.BlockSpec((1, tm, A), lambda b, i: (b, i, 0))
    tile_d = pl.BlockSpec((2, 1, tm, A), lambda b, i: (0, b, i, 0))
    sd = jax.ShapeDtypeStruct((B, T, A), F32)
    sd_d = jax.ShapeDtypeStruct((2, B, T, A), F32)
    return pl.pallas_call(
        functools.partial(_rwkv_prep_kernel, nct=L // tm, nt=nt, A=A),
        grid=(B, nt),
        in_specs=[pl.BlockSpec((1, tm, a_cols), lambda b, i: (b, i, 0)),
                  pl.BlockSpec((1, 8, a_cols), lambda b, i: (b, jnp.maximum(i * hb - 1, 0), 0)),
                  pl.BlockSpec((1, 8, a_cols), lambda b, i: (b, jnp.minimum((i + 1) * hb, T // 8 - 1), 0)),
                  const((2, a_cols)), const((2, A)), const((2, 2 * LORA_W, A)), const((2, A)),
                  const((2, 2 * LORA_A, A)), const((LORA_G, A)), const((1, A)), const((1, A)), const((1, A)),
                  const((A, A))],
        out_specs=[tile, tile, tile, tile, tile, tile_d, tile_d, tile_d],
        out_shape=[sd, sd, sd, sd, sd, sd_d, sd_d, sd_d],
        compiler_params=_params("arbitrary", "arbitrary"),
        name="rwkv_prep",
    )(z, z, z, mu, w0, pad_lora(w2), a0, pad_lora(a2), _bf(g2), k_k.reshape(1, A), k_a.reshape(1, A),
      r_k.reshape(1, A), _head_ones(A))


def _tri_inverse(lms, eye, m16, m32):
    d0 = [_bf(jnp.where(m16, lm, 0.0)) for lm in lms]
    t = [eye + d.astype(F32) for d in d0]
    s = [_dot(d, d) for d in d0]
    for step in range(3):
        sb = [_bf(x) for x in s]
        t = [x + _dot(_bf(x), y) for x, y in zip(t, sb)]
        if step < 2:
            s = [_dot(y, y) for y in sb]
    for lvl in (m32 & (~m16), ~m32):
        tb = [_bf(x) for x in t]
        w = [_bf(_dot(_bf(jnp.where(lvl, lm, 0.0)), y)) for lm, y in zip(lms, tb)]
        t = [x + _dot(y, z) for x, y, z in zip(t, tb, w)]
    return t


def _wkv_kernel(*refs, NP):
    C = SCAN_CHUNK
    P = 2 * C
    fwd_refs, bwd_refs, (yf_ref, yb_ref, h_s) = refs[0:6], refs[6:12], refs[12:]

    @pl.when(pl.program_id(1) == 0)
    def _():
        h_s[...] = jnp.zeros(h_s.shape, F32)

    ri = lax.broadcasted_iota(jnp.int32, (P, P), 0)
    ci = lax.broadcasted_iota(jnp.int32, (P, P), 1)
    same = (ri // C) == (ci // C)
    diff = (ri % C) - (ci % C)
    eye_b = ri == ci
    eye = eye_b.astype(F32)
    m16 = (ri // 16) == (ci // 16)
    m32 = (ri // 32) == (ci // 32)
    diff64 = lax.broadcasted_iota(jnp.int32, (C, C), 0) - lax.broadcasted_iota(jnp.int32, (C, C), 1)
    top = lax.broadcasted_iota(jnp.int32, (C, P), 1) < C
    zero_blk = jnp.zeros((P, P), BF16)
    sls = [slice(p * P, (p + 1) * P) for p in range(NP)]

    def bd(x):
        return [_bf(jnp.concatenate([jnp.where(top, x[:, sl], 0.0), jnp.where(top, 0.0, x[:, sl])], axis=0))
                for sl in sls]

    at, rt, bt, kt, bh, kh, vv, etots, before, before_eq = [], [], [], [], [], [], [], [], [], []
    for (r_ref, v_ref, kk_ref, lw_ref, beta_ref, kd_ref), sgn in ((fwd_refs, 1), (bwd_refs, -1)):
        order = diff * sgn
        before += [same & (order > 0)] * NP
        before_eq += [same & (order >= 0)] * NP
        tri = _bf(((diff64 * sgn) >= 0).astype(F32))
        lw = lw_ref[0, 0]
        lw_hi = _bf(lw)
        lw_md = _bf(lw - lw_hi.astype(F32))
        lw_lo = _bf(lw - lw_hi.astype(F32) - lw_md.astype(F32))
        cum = _dot(tri, lw_hi) + _dot(tri, lw_md) + _dot(tri, lw_lo)
        tot = jnp.sum(lw, axis=0, keepdims=True)
        beta = beta_ref[0, 0]
        kd = kd_ref[0, 0]
        e_neg = jnp.exp(-cum)
        e_tail = jnp.exp(tot - cum)
        at += bd(-kk_ref[0] * jnp.exp(cum - lw))
        rt += bd(r_ref[0] * jnp.exp(cum))
        bt += bd(beta * e_neg)
        kt += bd(kd * e_neg)
        bh += bd(beta * e_tail)
        kh += bd(kd * e_tail)
        vv += bd(v_ref[0])
        etot = jnp.exp(tot)
        etots += [etot[:, sl] for sl in sls]

    ar = [jnp.concatenate([a, r], axis=0) for a, r in zip(at, rt)]
    arb = [_dot_nt(x, b) for x, b in zip(ar, bt)]
    ark = [_dot_nt(x, k) for x, k in zip(ar, kt)]
    lab = [jnp.where(m, x[:P], 0.0) for x, m in zip(arb, before)]
    tinv = _tri_inverse(lab, eye, m16, m32)
    u = [_dot(_bf(jnp.where(m, x[:P], 0.0)), v) for x, v, m in zip(ark, vv, before)]
    x = [_bf(_dot(_bf(t), jnp.concatenate([a, _bf(w)], axis=1))) for t, a, w in zip(tinv, at, u)]
    rhs = [jnp.concatenate([xi, jnp.concatenate([zero_blk, v], axis=1)], axis=0) for xi, v in zip(x, vv)]
    mn = [lax.dot_general(jnp.concatenate([b, k], axis=0), w, (((0,), (0,)), ((), ())), preferred_element_type=F32)
          for b, k, w in zip(bh, kh, rhs)]
    lr = [_bf(jnp.concatenate([jnp.where(m, xb[P:], 0.0), jnp.where(m, xk[P:], 0.0)], axis=1))
          for xb, xk, m in zip(arb, ark, before_eq)]
    qy = [_dot(l, w) for l, w in zip(lr, rhs)]
    qm = [_bf(jnp.concatenate([r.astype(F32) + q[:, :P], jnp.where(eye_b, e, 0.0) + m[:, :P]], axis=0))
          for r, q, m, e in zip(rt, qy, mn, etots)]
    hin = [h_s[i] for i in range(2 * NP)]
    h_hi = [_bf(h) for h in hin]
    h_lo = [_bf(h - hh.astype(F32)) for h, hh in zip(hin, h_hi)]
    res = [_dot(w, hh) + _dot(w, hl) for w, hh, hl in zip(qm, h_hi, h_lo)]
    for i in range(2 * NP):
        ybd = res[i][:P] + qy[i][:, P:]
        y_ref = yf_ref if i < NP else yb_ref
        y_ref[0, :, sls[i % NP]] = ybd[:C] + ybd[C:]
        h_s[i] = res[i][P:] + mn[i][:, P:]


def _wkv_scan(r, v, kk, lw, beta, kd, L):
    B, T, A = r.shape
    C = SCAN_CHUNK
    nC = T // C
    nct = L // C
    NP = A // (2 * C)

    def rev(c):
        return jnp.where(c < nct, nct - 1 - c, nC - 1 - (c - nct))

    fwd = pl.BlockSpec((1, C, A), lambda b, c: (b, c, 0))
    bwd = pl.BlockSpec((1, C, A), lambda b, c: (b, rev(c), 0))
    fwd_d = pl.BlockSpec((1, 1, C, A), lambda b, c: (0, b, c, 0))
    bwd_d = pl.BlockSpec((1, 1, C, A), lambda b, c: (1, b, rev(c), 0))
    sd = jax.ShapeDtypeStruct((B, T, A), F32)
    return pl.pallas_call(
        functools.partial(_wkv_kernel, NP=NP),
        grid=(B, nC),
        in_specs=[fwd, fwd, fwd, fwd_d, fwd_d, fwd_d, bwd, bwd, bwd, bwd_d, bwd_d, bwd_d],
        out_specs=[fwd, bwd],
        out_shape=[sd, sd],
        scratch_shapes=[pltpu.VMEM((2 * NP, 2 * C, 2 * C), F32)],
        compiler_params=_params("arbitrary", "arbitrary"),
        name="wkv_scan",
    )(r, v, kk, lw, beta, kd, r, v, kk, lw, beta, kd)


def _rwkv_post_kernel(yf_ref, yb_ref, bonus_ref, g_ref, lnw_ref, lnb_ref, ones_ref, o_ref):
    y = yf_ref[0] + yb_ref[0]
    ones = ones_ref[...]
    inv_n = 1.0 / A_HEAD_DIM
    d = y - _seg_sum(y, ones) * inv_n
    var = _seg_sum(d * d, ones) * inv_n
    yn = d * lax.rsqrt(var + GN_EPS) * lnw_ref[...] + lnb_ref[...]
    o_ref[0] = (yn + bonus_ref[0]) * g_ref[0]


def _rwkv_post(yf, yb, bonus, g, ln_w, ln_b, L):
    B, T, A = bonus.shape
    tm = _row_tile(L, T - L)
    tile = pl.BlockSpec((1, tm, A), lambda b, i: (b, i, 0))
    vec = pl.BlockSpec((1, A), lambda b, i: (0, 0))
    return pl.pallas_call(
        _rwkv_post_kernel,
        grid=(B, T // tm),
        in_specs=[tile, tile, tile, tile, vec, vec, pl.BlockSpec((A, A), lambda b, i: (0, 0))],
        out_specs=tile,
        out_shape=jax.ShapeDtypeStruct((B, T, A), F32),
        compiler_params=_params("arbitrary", "arbitrary"),
        name="rwkv_post",
    )(yf, yb, bonus, g, ln_w.reshape(1, A), ln_b.reshape(1, A), _head_ones(A))


def _rwkv_mixer(z, L, a_cols, mu, w0, w2, a0, a2, g2, k_k, k_a, r_k, ln_w, ln_b):
    r, v, kk, g, bonus, lw, beta, kd = _rwkv_prep(z, L, a_cols, mu, w0, w2, a0, a2, g2, k_k, k_a, r_k)
    yf, yb = _wkv_scan(r, v, kk, lw, beta, kd, L)
    return _rwkv_post(yf, yb, bonus, g, ln_w, ln_b, L)


def _router_kernel(x_ref, nw_ref, ss_ref, wr_ref, h_ref, aff_ref):
    h = _norm_mod(x_ref[0], nw_ref[...], ss_ref[0, 0])
    h_ref[0] = h.astype(BF16)
    logits = _dot_nt(wr_ref[...], h, HIGHEST)
    m = jnp.max(logits, axis=0, keepdims=True)
    p = jnp.exp(logits - m)
    aff_ref[0] = p / jnp.sum(p, axis=0, keepdims=True)


def _norm_router(xa, nw, ss, w_router, L):
    B, T, D = xa.shape
    E = w_router.shape[1]
    tm = _row_tile(L, T - L)
    nct = L // tm
    return pl.pallas_call(
        _router_kernel,
        grid=(B, T // tm),
        in_specs=[pl.BlockSpec((1, tm, D), lambda b, i: (b, i, 0)),
                  pl.BlockSpec((1, D), lambda b, i: (0, 0)),
                  pl.BlockSpec((1, 1, 2, D), lambda b, i: (b, jnp.where(i >= nct, 1, 0), 0, 0)),
                  pl.BlockSpec((E, D), lambda b, i: (0, 0))],
        out_specs=[pl.BlockSpec((1, tm, D), lambda b, i: (b, i, 0)),
                   pl.BlockSpec((1, E, tm), lambda b, i: (b, 0, i))],
        out_shape=[jax.ShapeDtypeStruct((B, T, D), BF16), jax.ShapeDtypeStruct((B, E, T), F32)],
        compiler_params=_params("arbitrary", "arbitrary"),
        name="norm_router",
    )(xa, nw.reshape(1, D), ss, w_router.T)


def _gather_kernel(h_ref, pos_ref, o_ref, *, L, cap_l):
    T = h_ref.shape[1]
    Ct = pos_ref.shape[2]
    pos = pos_ref[0, 0]
    tok = lax.broadcasted_iota(jnp.int32, (cap_l, T - L), 1) + L
    onehot = jnp.where(tok == pos[:cap_l], 1.0, 0.0).astype(BF16)
    o_ref[0, 0, :cap_l, :] = _dot(onehot, h_ref[0, L:, :]).astype(BF16)
    if Ct > cap_l:
        tok = lax.broadcasted_iota(jnp.int32, (Ct - cap_l, L), 1)
        onehot = jnp.where(tok == pos[cap_l:], 1.0, 0.0).astype(BF16)
        o_ref[0, 0, cap_l:, :] = _dot(onehot, h_ref[0, :L, :]).astype(BF16)


def _moe_gather(h2, pos, L, cap_l):
    B, T, D = h2.shape
    E, Ct = pos.shape[1], pos.shape[2]
    return pl.pallas_call(
        functools.partial(_gather_kernel, L=L, cap_l=cap_l),
        grid=(B, E),
        in_specs=[pl.BlockSpec((1, T, D), lambda b, e: (b, 0, 0)),
                  pl.BlockSpec((1, 1, Ct, 1), lambda b, e: (b, e, 0, 0))],
        out_specs=pl.BlockSpec((1, 1, Ct, D), lambda b, e: (e, b, 0, 0)),
        out_shape=jax.ShapeDtypeStruct((E, B, Ct, D), BF16),
        compiler_params=_params("arbitrary", "arbitrary"),
        name="moe_gather",
    )(h2, pos)


def _ffn_kernel(x_ref, w1_ref, w3_ref, w2_ref, o_ref, *, rm):
    j = pl.program_id(1)
    R = x_ref.shape[1]
    w1 = w1_ref[0, 0].astype(BF16)
    w3 = w3_ref[0, 0].astype(BF16)
    w2 = w2_ref[0, 0].astype(BF16)

    @pl.when(j == 0)
    def _():
        o_ref[...] = jnp.zeros(o_ref.shape, F32)

    def rows(i, carry):
        r0 = pl.multiple_of(i * rm, rm)
        x = x_ref[0, pl.ds(r0, rm), :]
        a = _dot(x, w1)
        b = _dot(x, w3)
        hid = (a * jax.nn.sigmoid(a) * b).astype(BF16)
        o_ref[0, pl.ds(r0, rm), :] += _dot(hid, w2)
        return carry

    lax.fori_loop(0, R // rm, rows, 0, unroll=True)


def _expert_ffn(xin, w1, w3, w2, layer):
    E, R, D = xin.shape
    F = w1.shape[-1]
    tf = min(512, F)
    rm = max(m for m in (768, 512, 256, 128, 64, 32, 16) if R % m == 0)
    return pl.pallas_call(
        functools.partial(_ffn_kernel, rm=rm),
        grid=(E, F // tf),
        in_specs=[pl.BlockSpec((1, R, D), lambda e, j: (e, 0, 0)),
                  pl.BlockSpec((1, 1, D, tf), lambda e, j: (layer, e, 0, j)),
                  pl.BlockSpec((1, 1, D, tf), lambda e, j: (layer, e, 0, j)),
                  pl.BlockSpec((1, 1, tf, D), lambda e, j: (layer, e, j, 0))],
        out_specs=pl.BlockSpec((1, R, D), lambda e, j: (e, 0, 0)),
        out_shape=jax.ShapeDtypeStruct((E, R, D), F32),
        compiler_params=_params("arbitrary", "arbitrary"),
        name="expert_ffn",
    )(xin, w1, w3, w2)


def _combine_kernel(y_ref, gate_ref, posl_ref, posc_ref, x_ref, g_ref, o_ref, yg_s, *, tq, nct, cap_l):
    E, _, Ct, td = y_ref.shape
    T = x_ref.shape[1]
    cap_c = Ct - cap_l
    nl = E * cap_l
    for e in range(E):
        yg_s[e * cap_l:(e + 1) * cap_l, :] = (y_ref[e, 0, :cap_l, :] * gate_ref[0, e, :cap_l, :]).astype(BF16)
        if cap_c:
            yg_s[nl + e * cap_c:nl + (e + 1) * cap_c, :] = (y_ref[e, 0, cap_l:, :] * gate_ref[0, e, cap_l:, :]).astype(BF16)

    def tile(i, gate_row, pos, lo, n):
        r0 = pl.multiple_of(i * tq, tq)
        tok = lax.broadcasted_iota(jnp.int32, (tq, n), 0) + r0
        onehot = jnp.where(tok == pos, 1.0, 0.0).astype(BF16)
        o_ref[0, pl.ds(r0, tq), :] = x_ref[0, pl.ds(r0, tq), :] + gate_row * _dot(onehot, yg_s[lo:lo + n, :])

    for i in range(nct):
        if cap_c:
            tile(i, g_ref[0, 0], posc_ref[0], nl, E * cap_c)
        else:
            o_ref[0, i * tq:(i + 1) * tq, :] = x_ref[0, i * tq:(i + 1) * tq, :]

    def body(i, carry):
        tile(i, g_ref[0, 1], posl_ref[0], 0, nl)
        return carry

    lax.fori_loop(nct, T // tq, body, 0, unroll=2)


def _moe_combine(y, gate, pos_l, pos_c, xa, g2, L, cap_l):
    E, B, Ct, D = y.shape
    T = xa.shape[1]
    td = min(512, D)
    tq = _row_tile(L, T - L)
    return pl.pallas_call(
        functools.partial(_combine_kernel, tq=tq, nct=L // tq, cap_l=cap_l),
        grid=(B, D // td),
        in_specs=[pl.BlockSpec((E, 1, Ct, td), lambda b, j: (0, b, 0, j)),
                  pl.BlockSpec((1, E, Ct, 1), lambda b, j: (b, 0, 0, 0)),
                  pl.BlockSpec((1, 1, pos_l.shape[-1]), lambda b, j: (b, 0, 0)),
                  pl.BlockSpec((1, 1, pos_c.shape[-1]), lambda b, j: (b, 0, 0)),
                  pl.BlockSpec((1, T, td), lambda b, j: (b, 0, j)),
                  pl.BlockSpec((1, 2, 1, td), lambda b, j: (b, 0, 0, j))],
        out_specs=pl.BlockSpec((1, T, td), lambda b, j: (b, 0, j)),
        out_shape=jax.ShapeDtypeStruct((B, T, D), F32),
        scratch_shapes=[pltpu.VMEM((E * Ct, td), BF16)],
        compiler_params=_params("arbitrary", "arbitrary"),
        name="moe_combine",
    )(y, gate, pos_l, pos_c, xa, g2)


def _moe(xa, nw, ss2, g2, w_router, w1, w3, w2, layer, L, need_ctx):
    B, T, D = xa.shape
    S = T - L
    E = w_router.shape[1]
    h2, aff = _norm_router(xa, nw, ss2, w_router, L)
    cap_l = CAPACITY_FACTOR * S // E
    gate_l, idx_l = lax.top_k(aff[:, :, L:], cap_l)
    gates, poss = [gate_l], [(idx_l + L).astype(jnp.int32)]
    pos_c = jnp.zeros((B, 1, LANES), jnp.int32)
    if need_ctx:
        cap_c = CAPACITY_FACTOR * L // E
        gate_c, idx_c = lax.top_k(aff[:, :, :L], cap_c)
        gates.append(gate_c)
        poss.append(idx_c.astype(jnp.int32))
        pos_c = poss[1].reshape(B, 1, E * cap_c)
    gate = jnp.concatenate(gates, axis=-1)
    pos = jnp.concatenate(poss, axis=-1)
    Ct = pos.shape[-1]
    xin = _moe_gather(h2, pos[..., None], L, cap_l).reshape(E, B * Ct, D)
    y = _expert_ffn(xin, w1, w3, w2, layer).reshape(E, B, Ct, D)
    return _moe_combine(y, gate[..., None], poss[0].reshape(B, 1, E * cap_l), pos_c, xa, g2, L, cap_l)


_PERM_EO = np.concatenate([np.arange(0, QK_ROPE, 2), np.arange(1, QK_ROPE, 2)])
_PERM_OE = np.concatenate([np.arange(1, QK_ROPE, 2), np.arange(0, QK_ROPE, 2)])


def _ab_input_weight(w_in, a_cols):
    D = w_in.shape[0]
    zr = w_in[:, a_cols + Q_RANK + KV_RANK:]
    zero = jnp.zeros((D, LANES - 2 * QK_ROPE), w_in.dtype)
    g1 = jnp.concatenate([zero, zr[:, _PERM_EO], zr[:, _PERM_EO]], axis=1)
    g2 = jnp.concatenate([zero, zr[:, _PERM_OE], zr[:, _PERM_OE]], axis=1)
    return jnp.concatenate([w_in[:, :a_cols + Q_RANK + KV_RANK], g1, g2], axis=1).astype(BF16)


def _mla_weights(w_qup, w_kvup):
    NH = w_qup.shape[1] // (QK_NOPE + QK_ROPE)
    wq = w_qup.reshape(Q_RANK, NH, QK_NOPE + QK_ROPE)
    rope = wq[:, :, QK_NOPE:]
    wq = jnp.concatenate([wq[:, :, :QK_NOPE], rope[:, :, _PERM_EO], rope[:, :, _PERM_OE]], axis=-1)
    wq = wq.reshape(Q_RANK, NH // 2, 2 * LANES).transpose(1, 0, 2)
    wkv = w_kvup.reshape(KV_RANK, NH, QK_NOPE + V_HEAD)
    wk = jnp.concatenate([wkv[:, :, :QK_NOPE], jnp.zeros((KV_RANK, NH, LANES - QK_NOPE), w_kvup.dtype)], axis=-1)
    wk = wk.reshape(KV_RANK, NH // 2, 2 * LANES).transpose(1, 0, 2)
    wv = wkv[:, :, QK_NOPE:].reshape(KV_RANK, NH // 2, 2 * V_HEAD).transpose(1, 0, 2)
    return wq.astype(BF16), wk.astype(BF16), wv.astype(BF16)


def _rope_tables(L, S):
    t = np.arange(S)
    row = (t // GRID_W).astype(np.float32)
    col = (t % GRID_W).astype(np.float32)
    n_freq = QK_ROPE // 4
    inv = (ROPE_BASE ** (-np.arange(n_freq, dtype=np.float32) / n_freq)).astype(np.float32)
    ang = jnp.concatenate([jnp.asarray(row[:, None] * inv), jnp.asarray(col[:, None] * inv)], axis=-1)
    cos = jnp.concatenate([jnp.ones((L, QK_ROPE // 2), F32), jnp.cos(ang)], axis=0)
    sin = jnp.concatenate([jnp.zeros((L, QK_ROPE // 2), F32), jnp.sin(ang)], axis=0)
    T = L + S
    cc = jnp.concatenate([cos, cos], axis=1)
    ss = jnp.concatenate([-sin, sin], axis=1)
    one = jnp.ones((T, LANES - 2 * QK_ROPE), F32)
    zero = jnp.zeros((T, LANES - 2 * QK_ROPE), F32)
    cq = jnp.concatenate([one, cc, ss], axis=1)
    ck = jnp.concatenate([zero, cc, cc], axis=1)
    sk = jnp.concatenate([zero, ss, ss], axis=1)
    return cq, ck, sk


def kernel(x, c, ctx, c_ctx, mod_w, mod_b, norm1_w, norm2_w, final_norm_w, ab_w_in, ab_w_out, rk_mu, rk_w0, rk_w2, rk_a0, rk_a2, rk_g2, rk_kk, rk_ka, rk_rk, rk_ln_w, rk_ln_b, mla_qn_w, mla_w_qup, mla_kvn_w, mla_w_kvup, na_w_qkv, na_rpb, na_w_out, moe_router, moe_w1, moe_w3, moe_w2):
    B, S, D = x.shape
    L = ctx.shape[1]
    depth = mod_w.shape[0]
    A = rk_w0.shape[-1]
    a_cols = rk_mu.shape[-1]

    rows_pad = -(B + 1) % 8
    cvec = jnp.concatenate([c, c_ctx[None], jnp.zeros((rows_pad, D), F32)], axis=0)
    mods = _mod_vectors(cvec, mod_w, mod_b)
    m_lat = mods[:, :B].reshape(depth, B, 6, D)
    m_ctx = jnp.broadcast_to(mods[:, B].reshape(depth, 1, 6, D), (depth, B, 6, D))
    mm = jnp.stack([m_ctx, m_lat], axis=2)

    cq, ck, sk = _rope_tables(L, S)
    xa = jnp.concatenate([ctx, x], axis=1)

    for layer in range(depth):
        need_ctx = layer < depth - 1
        i = layer // 2
        m = mm[layer]
        ss1, g1 = m[:, :, 0:2], m[:, :, 2:3]
        ss2, g2 = m[:, :, 3:5], m[:, :, 5:6]
        if layer % 2 == 0:
            w_in = _ab_input_weight(ab_w_in[i], a_cols)
            z = _norm_linear(xa, norm1_w[layer], ss1, w_in, L)
            o_a = _rwkv_mixer(z, L, a_cols, rk_mu[i], rk_w0[i], rk_w2[i], rk_a0[i], rk_a2[i], rk_g2[i],
                              rk_kk[i], rk_ka[i], rk_rk[i], rk_ln_w[i], rk_ln_b[i])
            wq, wk, wv = _mla_weights(mla_w_qup[i], mla_w_kvup[i])
            o_b = _mla_attention(z, mla_qn_w[i], mla_kvn_w[i], wq, wk, wv, cq, ck, sk, L,
                                 a_cols, a_cols + Q_RANK, a_cols + Q_RANK + KV_RANK)
            xa = _linear_resid([o_a, o_b], ab_w_out[i].astype(BF16), xa, g1, L)
        else:
            qkv = _norm_linear(xa, norm1_w[layer], ss1, na_w_qkv[i].astype(BF16), L)
            o = _na_attention(qkv, _na_bias_table(na_rpb[i]), L, need_ctx)
            xa = _linear_resid([o], na_w_out[i].astype(BF16), xa, g1, L)
        xa = _moe(xa, norm2_w[layer], ss2, g2, moe_router[layer], moe_w1, moe_w3, moe_w2, layer, L, need_ctx)
    return _final_norm(xa, final_norm_w, L)
```

```python
import functools

import jax
import jax.numpy as jnp
import numpy as np
from jax import lax
from jax.experimental import pallas as pl
from jax.experimental.pallas import tpu as pltpu

F32 = jnp.float32
BF16 = jnp.bfloat16
HIGHEST = lax.Precision.HIGHEST

GRID_W = 64
NORM_EPS = 1e-6
NEG_INF = -1e30
GN_EPS = 64e-5
A_HEAD_DIM = 64
LORA_W = 64
LORA_A = 64
LORA_G = 128
QK_NOPE = 64
QK_ROPE = 32
V_HEAD = 64
Q_RANK = 384
KV_RANK = 256
ROPE_BASE = 10000.0
C_HEAD_DIM = 64
WIN_R = 8
WIN_C = 16
N_EXPERTS = 16
CAPACITY_FACTOR = 2
SCAN_CHUNK = 64
LANES = 128

VMEM_LIMIT = 56 * 1024 * 1024


def _params(*sem):
    return pltpu.CompilerParams(dimension_semantics=sem, vmem_limit_bytes=VMEM_LIMIT)


def _dot(a, b, precision=None):
    return jnp.dot(a, b, preferred_element_type=F32, precision=precision)


def _dot_nt(a, b, precision=None):
    return lax.dot_general(a, b, (((1,), (1,)), ((), ())), preferred_element_type=F32, precision=precision)


def _row_tile(L, S):
    tm = 256
    while L % tm or S % tm:
        tm //= 2
    return tm


def _mod_kernel(c_ref, w_ref, b_ref, o_ref):
    c = c_ref[...]
    sc = c * jax.nn.sigmoid(c)
    o_ref[0] = _dot(sc.astype(BF16), w_ref[0].astype(BF16)) + b_ref[0]


def _mod_vectors(cvec, mod_w, mod_b):
    depth, D, N = mod_w.shape
    R = cvec.shape[0]
    tn = 1024
    return pl.pallas_call(
        _mod_kernel,
        grid=(depth, N // tn),
        in_specs=[pl.BlockSpec((R, D), lambda l, j: (0, 0)),
                  pl.BlockSpec((1, D, tn), lambda l, j: (l, 0, j)),
                  pl.BlockSpec((1, 1, tn), lambda l, j: (l, 0, j))],
        out_specs=pl.BlockSpec((1, R, tn), lambda l, j: (l, 0, j)),
        out_shape=jax.ShapeDtypeStruct((depth, R, N), F32),
        compiler_params=_params("arbitrary", "arbitrary"),
        name="mod_vectors",
    )(cvec, mod_w, mod_b.reshape(depth, 1, N))


def _wide_tile(T):
    return max(m for m in range(8, 769, 8) if T % m == 0)


def _per_row(mod_ref, i, tm, L, k):
    row = lax.broadcasted_iota(jnp.int32, (tm, 1), 0) + i * tm
    return jnp.where(row < L, mod_ref[0, 0, k:k + 1, :], mod_ref[0, 1, k:k + 1, :])


def _norm_mod(x, nw, shift, scale):
    y = x * lax.rsqrt(jnp.mean(x * x, axis=-1, keepdims=True) + NORM_EPS)
    y = y * nw
    return y * (1.0 + scale) + shift


def _norm_linear_kernel(x_ref, nw_ref, ss_ref, w_ref, o_ref, *, L):
    i = pl.program_id(1)
    tm = x_ref.shape[1]
    h = _norm_mod(x_ref[0], nw_ref[...], _per_row(ss_ref, i, tm, L, 0), _per_row(ss_ref, i, tm, L, 1))
    o_ref[0] = _dot(h.astype(BF16), w_ref[...]).astype(o_ref.dtype)


def _norm_linear(xa, nw, ss, w, L, out_dtype=F32):
    B, T, D = xa.shape
    N = w.shape[1]
    tm = _wide_tile(T)
    return pl.pallas_call(
        functools.partial(_norm_linear_kernel, L=L),
        grid=(B, T // tm),
        in_specs=[pl.BlockSpec((1, tm, D), lambda b, i: (b, i, 0)),
                  pl.BlockSpec((1, D), lambda b, i: (0, 0)),
                  pl.BlockSpec((1, 2, 2, D), lambda b, i: (b, 0, 0, 0)),
                  pl.BlockSpec((D, N), lambda b, i: (0, 0))],
        out_specs=pl.BlockSpec((1, tm, N), lambda b, i: (b, i, 0)),
        out_shape=jax.ShapeDtypeStruct((B, T, N), out_dtype),
        compiler_params=_params("arbitrary", "arbitrary"),
        name="norm_linear",
    )(xa, nw.reshape(1, D), ss, w)


def _linear_resid_kernel(*refs, ks, L):
    n = len(ks)
    a_refs, (w_ref, x_ref, g_ref, o_ref) = refs[:n], refs[n:]
    acc = None
    off = 0
    for a_ref, k in zip(a_refs, ks):
        part = _dot(a_ref[0].astype(BF16), w_ref[off:off + k, :])
        acc = part if acc is None else acc + part
        off += k
    o_ref[0] = x_ref[0] + _per_row(g_ref, pl.program_id(1), x_ref.shape[1], L, 0) * acc


def _linear_resid(a_list, w, xa, gate, L):
    B, T, D = xa.shape
    tm = _wide_tile(T)
    ks = tuple(a.shape[-1] for a in a_list)
    in_specs = [pl.BlockSpec((1, tm, k), lambda b, i: (b, i, 0)) for k in ks]
    in_specs += [pl.BlockSpec(w.shape, lambda b, i: (0, 0)),
                 pl.BlockSpec((1, tm, D), lambda b, i: (b, i, 0)),
                 pl.BlockSpec((1, 2, 1, D), lambda b, i: (b, 0, 0, 0))]
    return pl.pallas_call(
        functools.partial(_linear_resid_kernel, ks=ks, L=L),
        grid=(B, T // tm),
        in_specs=in_specs,
        out_specs=pl.BlockSpec((1, tm, D), lambda b, i: (b, i, 0)),
        out_shape=jax.ShapeDtypeStruct((B, T, D), F32),
        compiler_params=_params("arbitrary", "arbitrary"),
        name="linear_resid",
    )(*a_list, w, xa, gate)


def _rms_kernel(x_ref, w_ref, o_ref):
    x = x_ref[0]
    o_ref[0] = x * lax.rsqrt(jnp.mean(x * x, axis=-1, keepdims=True) + NORM_EPS) * w_ref[...]


def _final_norm(xa, w, L):
    B, T, D = xa.shape
    S = T - L
    tm = _row_tile(L, S)
    nct = L // tm
    return pl.pallas_call(
        _rms_kernel,
        grid=(B, S // tm),
        in_specs=[pl.BlockSpec((1, tm, D), lambda b, i: (b, i + nct, 0)),
                  pl.BlockSpec((1, D), lambda b, i: (0, 0))],
        out_specs=pl.BlockSpec((1, tm, D), lambda b, i: (b, i, 0)),
        out_shape=jax.ShapeDtypeStruct((B, S, D), F32),
        compiler_params=_params("arbitrary", "arbitrary"),
        name="final_norm",
    )(xa, w.reshape(1, D))


def _rms(x, w):
    return x * lax.rsqrt(jnp.mean(x * x, axis=-1, keepdims=True) + NORM_EPS) * w


def _softmax_pv(chains):
    m = [functools.reduce(jnp.maximum, [jnp.max(s, axis=-1, keepdims=True) for s, _ in ch]) for ch in chains]
    p = [[jnp.exp(s - mi) for s, _ in ch] for ch, mi in zip(chains, m)]
    l = [functools.reduce(jnp.add, [jnp.sum(x, axis=-1, keepdims=True) for x in pc]) for pc in p]
    o = [functools.reduce(jnp.add, [_dot(x.astype(BF16), v) for x, (_, v) in zip(pc, ch)]) for pc, ch in zip(p, chains)]
    return [oi / li for oi, li in zip(o, l)]


def _mla_kernel(zq_ref, zkv_ref, zr_ref, qn_ref, kvn_ref, wq_ref, wk_ref, wv_ref, cq_ref, ck_ref, sk_ref,
                o_ref, q_s, k_s, v_s, *, L, tq, scale):
    T = zq_ref.shape[1]
    zqn = _rms(zq_ref[0], qn_ref[...]).astype(BF16)
    zkvn = _rms(zkv_ref[0], kvn_ref[...]).astype(BF16)
    qh = _dot(zqn, wq_ref[0])
    kn = _dot(zkvn, wk_ref[0])
    v_s[...] = _dot(zkvn, wv_ref[0]).astype(BF16)
    zr = zr_ref[0]
    kr = zr[:, :LANES] * ck_ref[...] + zr[:, LANES:] * sk_ref[...]
    cq = cq_ref[...] * scale
    for h in range(2):
        q_s[h] = (qh[:, h * LANES:(h + 1) * LANES] * cq).astype(BF16)
        k_s[h] = (kn[:, h * LANES:(h + 1) * LANES] + kr).astype(BF16)
    first_head = lax.broadcasted_iota(jnp.int32, (tq, LANES), 1) < V_HEAD

    def tile(row0, nk):
        s = [_dot_nt(q_s[h, pl.ds(row0, tq), :], k_s[h, 0:nk, :]) for h in range(2)]
        outs = _softmax_pv([[(si, v_s[0:nk, :])] for si in s])
        o_ref[0, pl.ds(row0, tq), :] = jnp.where(first_head, outs[0], outs[1])

    for i in range(L // tq):
        tile(i * tq, L)

    def body(i, carry):
        tile(pl.multiple_of(i * tq, tq), T)
        return carry

    lax.fori_loop(L // tq, T // tq, body, 0, unroll=2)


def _mla_attention(z, qn_w, kvn_w, wq, wk, wv, cq, ck, sk, L, col_q, col_kv, col_r):
    B, T, _ = z.shape
    HP = wq.shape[0]
    tq = _row_tile(L, T - L)
    scale = float((QK_NOPE + QK_ROPE) ** -0.5)
    return pl.pallas_call(
        functools.partial(_mla_kernel, L=L, tq=tq, scale=scale),
        grid=(B, HP),
        in_specs=[pl.BlockSpec((1, T, Q_RANK), lambda b, p: (b, 0, col_q // Q_RANK)),
                  pl.BlockSpec((1, T, KV_RANK), lambda b, p: (b, 0, col_kv // KV_RANK)),
                  pl.BlockSpec((1, T, 2 * LANES), lambda b, p: (b, 0, col_r // (2 * LANES))),
                  pl.BlockSpec((1, Q_RANK), lambda b, p: (0, 0)),
                  pl.BlockSpec((1, KV_RANK), lambda b, p: (0, 0)),
                  pl.BlockSpec((1, Q_RANK, 2 * LANES), lambda b, p: (p, 0, 0)),
                  pl.BlockSpec((1, KV_RANK, 2 * LANES), lambda b, p: (p, 0, 0)),
                  pl.BlockSpec((1, KV_RANK, LANES), lambda b, p: (p, 0, 0)),
                  pl.BlockSpec((T, LANES), lambda b, p: (0, 0)),
                  pl.BlockSpec((T, LANES), lambda b, p: (0, 0)),
                  pl.BlockSpec((T, LANES), lambda b, p: (0, 0))],
        out_specs=pl.BlockSpec((1, T, LANES), lambda b, p: (b, 0, p)),
        out_shape=jax.ShapeDtypeStruct((B, T, HP * LANES), F32),
        scratch_shapes=[pltpu.VMEM((2, T, LANES), BF16), pltpu.VMEM((2, T, LANES), BF16),
                        pltpu.VMEM((T, LANES), BF16)],
        compiler_params=_params("arbitrary", "arbitrary"),
        name="mla_attention",
    )(z, z, z, qn_w.reshape(1, -1), kvn_w.reshape(1, -1), wq, wk, wv, cq, ck, sk)


def _na_kernel(q_ref, k_ref, v_ref, bt_ref, o_ref, k_s, v_s, *, L, rows, kr, need_ctx, scale):
    W = GRID_W
    rpb = 4 if rows % 4 == 0 else 1
    k_s[...] = k_ref[0].astype(BF16)
    v_s[...] = v_ref[0].astype(BF16)
    nwin = kr * W
    lane = lax.broadcasted_iota(jnp.int32, (W, LANES), 1)
    head_mask = [(lane < C_HEAD_DIM).astype(F32), (lane >= C_HEAD_DIM).astype(F32)]
    first_head = lane < C_HEAD_DIM
    qcol = lax.broadcasted_iota(jnp.int32, (W, nwin), 0)
    kcol = lax.broadcasted_iota(jnp.int32, (W, nwin), 1) % W
    cstart = jnp.clip(qcol - WIN_C // 2, 0, W - WIN_C)
    col_valid = (kcol >= cstart) & (kcol < cstart + WIN_C)

    def row_block(rb, carry):
        q_blk = q_ref[0, pl.ds(pl.multiple_of(L + rb * (rpb * W), W), rpb * W), :] * scale
        s_ctx = [_dot_nt((q_blk * jnp.concatenate([head_mask[h]] * rpb, axis=0)).astype(BF16), k_s[0:L, :])
                 for h in range(2)]
        chains, q0s = [], []
        for j in range(rpb):
            r = rb * rpb + j
            rs = jnp.clip(r - kr // 2, 0, rows - kr)
            k0 = pl.multiple_of(L + rs * W, W)
            q0s.append(pl.multiple_of(L + r * W, W))
            q = q_blk[j * W:(j + 1) * W]
            kw = k_s[pl.ds(k0, nwin), :]
            vw = v_s[pl.ds(k0, nwin), :]
            dr0 = rs - r + (WIN_R - 1)
            for h in range(2):
                s_nb = _dot_nt((q * head_mask[h]).astype(BF16), kw)
                bias = jnp.concatenate([bt_ref[0, h, dr0 + 2 * m] for m in range(kr // 2)], axis=-1)
                s_nb = jnp.where(col_valid, s_nb + bias, NEG_INF)
                chains.append([(s_nb, vw), (s_ctx[h][j * W:(j + 1) * W], v_s[0:L, :])])
        outs = _softmax_pv(chains)
        for j in range(rpb):
            o_ref[0, pl.ds(q0s[j], W), :] = jnp.where(first_head, outs[2 * j], outs[2 * j + 1])
        return carry

    lax.fori_loop(0, rows // rpb, row_block, 0, unroll=2)

    tq = min(L, 256)
    lane_c = lax.broadcasted_iota(jnp.int32, (tq, LANES), 1)
    for i in range(L // tq):
        if need_ctx:
            q = q_ref[0, i * tq:(i + 1) * tq, :] * scale
            hm = [lane_c < C_HEAD_DIM, lane_c >= C_HEAD_DIM]
            s = [_dot_nt(jnp.where(hm[h], q, 0.0).astype(BF16), k_s[0:L, :]) for h in range(2)]
            outs = _softmax_pv([[(si, v_s[0:L, :])] for si in s])
            o_ref[0, i * tq:(i + 1) * tq, :] = jnp.where(lane_c < C_HEAD_DIM, outs[0], outs[1])
        else:
            o_ref[0, i * tq:(i + 1) * tq, :] = jnp.zeros((tq, LANES), F32)


def _na_attention(qkv, bias_tab, L, need_ctx):
    B, T, D3 = qkv.shape
    D = D3 // 3
    HP = D // LANES
    rows = (T - L) // GRID_W
    kr = min(WIN_R, rows)
    assert kr % 2 == 0
    nd = bias_tab.shape[2]
    return pl.pallas_call(
        functools.partial(_na_kernel, L=L, rows=rows, kr=kr, need_ctx=need_ctx, scale=float(C_HEAD_DIM ** -0.5)),
        grid=(B, HP),
        in_specs=[pl.BlockSpec((1, T, LANES), lambda b, p: (b, 0, p)),
                  pl.BlockSpec((1, T, LANES), lambda b, p: (b, 0, HP + p)),
                  pl.BlockSpec((1, T, LANES), lambda b, p: (b, 0, 2 * HP + p)),
                  pl.BlockSpec((1, 2, nd, GRID_W, LANES), lambda b, p: (p, 0, 0, 0, 0))],
        out_specs=pl.BlockSpec((1, T, LANES), lambda b, p: (b, 0, p)),
        out_shape=jax.ShapeDtypeStruct((B, T, D), F32),
        scratch_shapes=[pltpu.VMEM((T, LANES), BF16), pltpu.VMEM((T, LANES), BF16)],
        compiler_params=_params("arbitrary", "arbitrary"),
        name="na_attention",
    )(qkv, qkv, qkv, bias_tab)


def _na_bias_table(rpb):
    H = rpb.shape[0]
    qc = np.arange(GRID_W)[:, None]
    kc = np.arange(GRID_W)[None, :]
    dc = np.clip(kc - qc + (WIN_C - 1), 0, 2 * WIN_C - 2)
    t = rpb[:, :, dc]
    t2 = jnp.concatenate([t[:, :-1], t[:, 1:]], axis=-1)
    return t2.reshape(H // 2, 2, 2 * WIN_R - 2, GRID_W, 2 * GRID_W)


def _bf(x):
    return x.astype(BF16)


def _seg_sum(x, ones_bd):
    hi = _bf(x)
    lo = _bf(x - hi.astype(F32))
    return _dot(hi, ones_bd) + _dot(lo, ones_bd)


def _head_ones(A):
    seg = np.arange(A) // A_HEAD_DIM
    return jnp.asarray(seg[:, None] == seg[None, :], dtype=BF16)


def _rwkv_prep_kernel(z_ref, zp_ref, zn_ref, mu_ref, w0_ref, w2_ref, a0_ref, a2_ref, g2_ref, kk_ref, ka_ref, rk_ref,
                      ones_ref, r_o, v_o, kkn_o, g_o, bonus_o, lw_o, beta_o, kd_o, *, nct, nt, A):
    i = pl.program_id(1)
    za = z_ref[0]
    tm = za.shape[0]
    row = lax.broadcasted_iota(jnp.int32, za.shape, 0)
    seg_first = (i == 0) | (i == nct)
    seg_last = (i == nct - 1) | (i == nt - 1)
    prev_row = jnp.where(seg_first, 0.0, zp_ref[0, 7:8, :])
    next_row = jnp.where(seg_last, 0.0, zn_ref[0, 0:1, :])
    prev = jnp.where(row == 0, prev_row, pltpu.roll(za, 1, 0))
    nxt = jnp.where(row == tm - 1, next_row, pltpu.roll(za, tm - 1, 0))
    zs = za + mu_ref[0:1, :] * (prev - za) + mu_ref[1:2, :] * (nxt - za)
    r = zs[:, 0:A]
    k = zs[:, A:2 * A]
    v = zs[:, 2 * A:3 * A]
    wd = _bf(jnp.tanh(zs[:, 3 * A:3 * A + LANES]))
    ad = _bf(zs[:, 3 * A + LANES:3 * A + 2 * LANES])
    gd = _bf(jax.nn.sigmoid(zs[:, 3 * A + 2 * LANES:3 * A + 3 * LANES]))
    ones = ones_ref[...]
    kk = k * kk_ref[...]
    kkn = kk / jnp.maximum(jnp.sqrt(_seg_sum(kk * kk, ones)), 1e-12)
    kd_sum = None
    for d in range(2):
        w_log = -jax.nn.softplus(-(w0_ref[d:d + 1, :] + _dot(wd, w2_ref[d]))) - 0.5
        lw_o[d, 0] = -jnp.exp(w_log)
        a = jax.nn.sigmoid(a0_ref[d:d + 1, :] + _dot(ad, a2_ref[d]))
        beta_o[d, 0] = kkn * a
        kd = k * (1.0 + (a - 1.0) * ka_ref[...])
        kd_o[d, 0] = kd
        kd_sum = kd if kd_sum is None else kd_sum + kd
    bonus_o[0] = _seg_sum(r * kd_sum * rk_ref[...], ones) * v
    r_o[0] = r
    v_o[0] = v
    kkn_o[0] = kkn
    g_o[0] = _dot(gd, g2_ref[...])


def _rwkv_prep(z, L, a_cols, mu, w0, w2, a0, a2, g2, k_k, k_a, r_k):
    B, T, _ = z.shape
    A = w0.shape[-1]
    assert 2 * LORA_W == LANES and 2 * LORA_A == LANES and LORA_G == LANES and a_cols == 3 * A + 3 * LANES
    tm = _row_tile(L, T - L)
    nt = T // tm
    hb = tm // 8

    def pad_lora(w):
        zero = jnp.zeros_like(w[0])
        return _bf(jnp.stack([jnp.concatenate([w[0], zero], 0), jnp.concatenate([zero, w[1]], 0)]))

    def const(shape):
        return pl.BlockSpec(shape, lambda b, i: (0,) * len(shape))

    tile = pl.BlockSpec((1, tm, A), lambda b, i: (b, i, 0))
    tile_d = pl.BlockSpec((2, 1, tm, A), lambda b, i: (0, b, i, 0))
    sd = jax.ShapeDtypeStruct((B, T, A), F32)
    sd_d = jax.ShapeDtypeStruct((2, B, T, A), F32)
    return pl.pallas_call(
        functools.partial(_rwkv_prep_kernel, nct=L // tm, nt=nt, A=A),
        grid=(B, nt),
        in_specs=[pl.BlockSpec((1, tm, a_cols), lambda b, i: (b, i, 0)),
                  pl.BlockSpec((1, 8, a_cols), lambda b, i: (b, jnp.maximum(i * hb - 1, 0), 0)),
                  pl.BlockSpec((1, 8, a_cols), lambda b, i: (b, jnp.minimum((i + 1) * hb, T // 8 - 1), 0)),
                  const((2, a_cols)), const((2, A)), const((2, 2 * LORA_W, A)), const((2, A)),
                  const((2, 2 * LORA_A, A)), const((LORA_G, A)), const((1, A)), const((1, A)), const((1, A)),
                  const((A, A))],
        out_specs=[tile, tile, tile, tile, tile, tile_d, tile_d, tile_d],
        out_shape=[sd, sd, sd, sd, sd, sd_d, sd_d, sd_d],
        compiler_params=_params("arbitrary", "arbitrary"),
        name="rwkv_prep",
    )(z, z, z, mu, w0, pad_lora(w2), a0, pad_lora(a2), _bf(g2), k_k.reshape(1, A), k_a.reshape(1, A),
      r_k.reshape(1, A), _head_ones(A))


def _tri_inverse(lms, eye, m16, m32):
    d0 = [_bf(jnp.where(m16, lm, 0.0)) for lm in lms]
    t = [eye + d.astype(F32) for d in d0]
    s = [_dot(d, d) for d in d0]
    for step in range(3):
        sb = [_bf(x) for x in s]
        t = [x + _dot(_bf(x), y) for x, y in zip(t, sb)]
        if step < 2:
            s = [_dot(y, y) for y in sb]
    for lvl in (m32 & (~m16), ~m32):
        tb = [_bf(x) for x in t]
        w = [_bf(_dot(_bf(jnp.where(lvl, lm, 0.0)), y)) for lm, y in zip(lms, tb)]
        t = [x + _dot(y, z) for x, y, z in zip(t, tb, w)]
    return t


def _wkv_kernel(*refs, NP):
    C = SCAN_CHUNK
    P = 2 * C
    fwd_refs, bwd_refs, (yf_ref, yb_ref, h_s) = refs[0:6], refs[6:12], refs[12:]

    @pl.when(pl.program_id(1) == 0)
    def _():
        h_s[...] = jnp.zeros(h_s.shape, F32)

    ri = lax.broadcasted_iota(jnp.int32, (P, P), 0)
    ci = lax.broadcasted_iota(jnp.int32, (P, P), 1)
    same = (ri // C) == (ci // C)
    diff = (ri % C) - (ci % C)
    eye_b = ri == ci
    eye = eye_b.astype(F32)
    m16 = (ri // 16) == (ci // 16)
    m32 = (ri // 32) == (ci // 32)
    diff64 = lax.broadcasted_iota(jnp.int32, (C, C), 0) - lax.broadcasted_iota(jnp.int32, (C, C), 1)
    top = lax.broadcasted_iota(jnp.int32, (C, P), 1) < C
    zero_blk = jnp.zeros((P, P), BF16)
    sls = [slice(p * P, (p + 1) * P) for p in range(NP)]

    def bd(x):
        return [_bf(jnp.concatenate([jnp.where(top, x[:, sl], 0.0), jnp.where(top, 0.0, x[:, sl])], axis=0))
                for sl in sls]

    at, rt, bt, kt, bh, kh, vv, etots, before, before_eq = [], [], [], [], [], [], [], [], [], []
    for (r_ref, v_ref, kk_ref, lw_ref, beta_ref, kd_ref), sgn in ((fwd_refs, 1), (bwd_refs, -1)):
        order = diff * sgn
        before += [same & (order > 0)] * NP
        before_eq += [same & (order >= 0)] * NP
        tri = _bf(((diff64 * sgn) >= 0).astype(F32))
        lw = lw_ref[0, 0]
        lw_hi = _bf(lw)
        lw_md = _bf(lw - lw_hi.astype(F32))
        lw_lo = _bf(lw - lw_hi.astype(F32) - lw_md.astype(F32))
        cum = _dot(tri, lw_hi) + _dot(tri, lw_md) + _dot(tri, lw_lo)
        tot = jnp.sum(lw, axis=0, keepdims=True)
        beta = beta_ref[0, 0]
        kd = kd_ref[0, 0]
        e_neg = jnp.exp(-cum)
        e_tail = jnp.exp(tot - cum)
        at += bd(-kk_ref[0] * jnp.exp(cum - lw))
        rt += bd(r_ref[0] * jnp.exp(cum))
        bt += bd(beta * e_neg)
        kt += bd(kd * e_neg)
        bh += bd(beta * e_tail)
        kh += bd(kd * e_tail)
        vv += bd(v_ref[0])
        etot = jnp.exp(tot)
        etots += [etot[:, sl] for sl in sls]

    ar = [jnp.concatenate([a, r], axis=0) for a, r in zip(at, rt)]
    arb = [_dot_nt(x, b) for x, b in zip(ar, bt)]
    ark = [_dot_nt(x, k) for x, k in zip(ar, kt)]
    lab = [jnp.where(m, x[:P], 0.0) for x, m in zip(arb, before)]
    tinv = _tri_inverse(lab, eye, m16, m32)
    u = [_dot(_bf(jnp.where(m, x[:P], 0.0)), v) for x, v, m in zip(ark, vv, before)]
    x = [_bf(_dot(_bf(t), jnp.concatenate([a, _bf(w)], axis=1))) for t, a, w in zip(tinv, at, u)]
    rhs = [jnp.concatenate([xi, jnp.concatenate([zero_blk, v], axis=1)], axis=0) for xi, v in zip(x, vv)]
    mn = [lax.dot_general(jnp.concatenate([b, k], axis=0), w, (((0,), (0,)), ((), ())), preferred_element_type=F32)
          for b, k, w in zip(bh, kh, rhs)]
    lr = [_bf(jnp.concatenate([jnp.where(m, xb[P:], 0.0), jnp.where(m, xk[P:], 0.0)], axis=1))
          for xb, xk, m in zip(arb, ark, before_eq)]
    qy = [_dot(l, w) for l, w in zip(lr, rhs)]
    qm = [_bf(jnp.concatenate([r.astype(F32) + q[:, :P], jnp.where(eye_b, e, 0.0) + m[:, :P]], axis=0))
          for r, q, m, e in zip(rt, qy, mn, etots)]
    hin = [h_s[i] for i in range(2 * NP)]
    h_hi = [_bf(h) for h in hin]
    h_lo = [_bf(h - hh.astype(F32)) for h, hh in zip(hin, h_hi)]
    res = [_dot(w, hh) + _dot(w, hl) for w, hh, hl in zip(qm, h_hi, h_lo)]
    for i in range(2 * NP):
        ybd = res[i][:P] + qy[i][:, P:]
        y_ref = yf_ref if i < NP else yb_ref
        y_ref[0, :, sls[i % NP]] = ybd[:C] + ybd[C:]
        h_s[i] = res[i][P:] + mn[i][:, P:]


def _wkv_scan(r, v, kk, lw, beta, kd, L):
    B, T, A = r.shape
    C = SCAN_CHUNK
    nC = T // C
    nct = L // C
    NP = A // (2 * C)

    def rev(c):
        return jnp.where(c < nct, nct - 1 - c, nC - 1 - (c - nct))

    fwd = pl.BlockSpec((1, C, A), lambda b, c: (b, c, 0))
    bwd = pl.BlockSpec((1, C, A), lambda b, c: (b, rev(c), 0))
    fwd_d = pl.BlockSpec((1, 1, C, A), lambda b, c: (0, b, c, 0))
    bwd_d = pl.BlockSpec((1, 1, C, A), lambda b, c: (1, b, rev(c), 0))
    sd = jax.ShapeDtypeStruct((B, T, A), F32)
    return pl.pallas_call(
        functools.partial(_wkv_kernel, NP=NP),
        grid=(B, nC),
        in_specs=[fwd, fwd, fwd, fwd_d, fwd_d, fwd_d, bwd, bwd, bwd, bwd_d, bwd_d, bwd_d],
        out_specs=[fwd, bwd],
        out_shape=[sd, sd],
        scratch_shapes=[pltpu.VMEM((2 * NP, 2 * C, 2 * C), F32)],
        compiler_params=_params("arbitrary", "arbitrary"),
        name="wkv_scan",
    )(r, v, kk, lw, beta, kd, r, v, kk, lw, beta, kd)


def _rwkv_post_kernel(yf_ref, yb_ref, bonus_ref, g_ref, lnw_ref, lnb_ref, ones_ref, o_ref):
    y = yf_ref[0] + yb_ref[0]
    ones = ones_ref[...]
    inv_n = 1.0 / A_HEAD_DIM
    d = y - _seg_sum(y, ones) * inv_n
    var = _seg_sum(d * d, ones) * inv_n
    yn = d * lax.rsqrt(var + GN_EPS) * lnw_ref[...] + lnb_ref[...]
    o_ref[0] = (yn + bonus_ref[0]) * g_ref[0]


def _rwkv_post(yf, yb, bonus, g, ln_w, ln_b, L):
    B, T, A = bonus.shape
    tm = _row_tile(L, T - L)
    tile = pl.BlockSpec((1, tm, A), lambda b, i: (b, i, 0))
    vec = pl.BlockSpec((1, A), lambda b, i: (0, 0))
    return pl.pallas_call(
        _rwkv_post_kernel,
        grid=(B, T // tm),
        in_specs=[tile, tile, tile, tile, vec, vec, pl.BlockSpec((A, A), lambda b, i: (0, 0))],
        out_specs=tile,
        out_shape=jax.ShapeDtypeStruct((B, T, A), F32),
        compiler_params=_params("arbitrary", "arbitrary"),
        name="rwkv_post",
    )(yf, yb, bonus, g, ln_w.reshape(1, A), ln_b.reshape(1, A), _head_ones(A))


def _rwkv_mixer(z, L, a_cols, mu, w0, w2, a0, a2, g2, k_k, k_a, r_k, ln_w, ln_b):
    r, v, kk, g, bonus, lw, beta, kd = _rwkv_prep(z, L, a_cols, mu, w0, w2, a0, a2, g2, k_k, k_a, r_k)
    yf, yb = _wkv_scan(r, v, kk, lw, beta, kd, L)
    return _rwkv_post(yf, yb, bonus, g, ln_w, ln_b, L)


def _router_kernel(x_ref, nw_ref, ss_ref, wr_ref, h_ref, aff_ref, *, L):
    i = pl.program_id(1)
    tm = x_ref.shape[1]
    h = _norm_mod(x_ref[0], nw_ref[...], _per_row(ss_ref, i, tm, L, 0), _per_row(ss_ref, i, tm, L, 1))
    h_ref[0] = h.astype(BF16)
    logits = _dot_nt(wr_ref[...], h, HIGHEST)
    m = jnp.max(logits, axis=0, keepdims=True)
    p = jnp.exp(logits - m)
    aff_ref[0] = p / jnp.sum(p, axis=0, keepdims=True)


def _norm_router(xa, nw, ss, w_router, L):
    B, T, D = xa.shape
    E = w_router.shape[1]
    tm = max(m for m in range(LANES, 769, LANES) if T % m == 0)
    return pl.pallas_call(
        functools.partial(_router_kernel, L=L),
        grid=(B, T // tm),
        in_specs=[pl.BlockSpec((1, tm, D), lambda b, i: (b, i, 0)),
                  pl.BlockSpec((1, D), lambda b, i: (0, 0)),
                  pl.BlockSpec((1, 2, 2, D), lambda b, i: (b, 0, 0, 0)),
                  pl.BlockSpec((E, D), lambda b, i: (0, 0))],
        out_specs=[pl.BlockSpec((1, tm, D), lambda b, i: (b, i, 0)),
                   pl.BlockSpec((1, E, tm), lambda b, i: (b, 0, i))],
        out_shape=[jax.ShapeDtypeStruct((B, T, D), BF16), jax.ShapeDtypeStruct((B, E, T), F32)],
        compiler_params=_params("arbitrary", "arbitrary"),
        name="norm_router",
    )(xa, nw.reshape(1, D), ss, w_router.T)


def _gather_kernel(h_ref, pos_ref, o_ref, *, L, cap_l):
    T = h_ref.shape[1]
    Ct = pos_ref.shape[2]
    pos = pos_ref[0, 0]
    tok = lax.broadcasted_iota(jnp.int32, (cap_l, T - L), 1) + L
    onehot = jnp.where(tok == pos[:cap_l], 1.0, 0.0).astype(BF16)
    o_ref[0, 0, :cap_l, :] = _dot(onehot, h_ref[0, L:, :]).astype(BF16)
    if Ct > cap_l:
        tok = lax.broadcasted_iota(jnp.int32, (Ct - cap_l, L), 1)
        onehot = jnp.where(tok == pos[cap_l:], 1.0, 0.0).astype(BF16)
        o_ref[0, 0, cap_l:, :] = _dot(onehot, h_ref[0, :L, :]).astype(BF16)


def _moe_gather(h2, pos, L, cap_l):
    B, T, D = h2.shape
    E, Ct = pos.shape[1], pos.shape[2]
    return pl.pallas_call(
        functools.partial(_gather_kernel, L=L, cap_l=cap_l),
        grid=(B, E),
        in_specs=[pl.BlockSpec((1, T, D), lambda b, e: (b, 0, 0)),
                  pl.BlockSpec((1, 1, Ct, 1), lambda b, e: (b, e, 0, 0))],
        out_specs=pl.BlockSpec((1, 1, Ct, D), lambda b, e: (e, b, 0, 0)),
        out_shape=jax.ShapeDtypeStruct((E, B, Ct, D), BF16),
        compiler_params=_params("arbitrary", "arbitrary"),
        name="moe_gather",
    )(h2, pos)


def _ffn_kernel(x_ref, w1_ref, w3_ref, w2_ref, o_ref, *, rm):
    j = pl.program_id(1)
    R = x_ref.shape[1]
    w1 = w1_ref[0, 0].astype(BF16)
    w3 = w3_ref[0, 0].astype(BF16)
    w2 = w2_ref[0, 0].astype(BF16)

    @pl.when(j == 0)
    def _():
        o_ref[...] = jnp.zeros(o_ref.shape, F32)

    def rows(i, carry):
        r0 = pl.multiple_of(i * rm, rm)
        x = x_ref[0, pl.ds(r0, rm), :]
        a = _dot(x, w1)
        b = _dot(x, w3)
        hid = (a * jax.nn.sigmoid(a) * b).astype(BF16)
        o_ref[0, pl.ds(r0, rm), :] += _dot(hid, w2)
        return carry

    lax.fori_loop(0, R // rm, rows, 0, unroll=True)


def _expert_ffn(xin, w1, w3, w2, layer):
    E, R, D = xin.shape
    F = w1.shape[-1]
    tf = min(512, F)
    rm = max(m for m in (768, 512, 256, 128, 64, 32, 16) if R % m == 0)
    return pl.pallas_call(
        functools.partial(_ffn_kernel, rm=rm),
        grid=(E, F // tf),
        in_specs=[pl.BlockSpec((1, R, D), lambda e, j: (e, 0, 0)),
                  pl.BlockSpec((1, 1, D, tf), lambda e, j: (layer, e, 0, j)),
                  pl.BlockSpec((1, 1, D, tf), lambda e, j: (layer, e, 0, j)),
                  pl.BlockSpec((1, 1, tf, D), lambda e, j: (layer, e, j, 0))],
        out_specs=pl.BlockSpec((1, R, D), lambda e, j: (e, 0, 0)),
        out_shape=jax.ShapeDtypeStruct((E, R, D), F32),
        compiler_params=_params("arbitrary", "arbitrary"),
        name="expert_ffn",
    )(xin, w1, w3, w2)


def _combine_kernel(y_ref, gate_ref, posl_ref, posc_ref, x_ref, g_ref, o_ref, yg_s, *, tq, nct, cap_l):
    E, _, Ct, td = y_ref.shape
    T = x_ref.shape[1]
    cap_c = Ct - cap_l
    nl = E * cap_l
    for e in range(E):
        yg_s[e * cap_l:(e + 1) * cap_l, :] = (y_ref[e, 0, :cap_l, :] * gate_ref[0, e, :cap_l, :]).astype(BF16)
        if cap_c:
            yg_s[nl + e * cap_c:nl + (e + 1) * cap_c, :] = (y_ref[e, 0, cap_l:, :] * gate_ref[0, e, cap_l:, :]).astype(BF16)

    def tile(i, gate_row, pos, lo, n):
        r0 = pl.multiple_of(i * tq, tq)
        tok = lax.broadcasted_iota(jnp.int32, (tq, n), 0) + r0
        onehot = jnp.where(tok == pos, 1.0, 0.0).astype(BF16)
        o_ref[0, pl.ds(r0, tq), :] = x_ref[0, pl.ds(r0, tq), :] + gate_row * _dot(onehot, yg_s[lo:lo + n, :])

    for i in range(nct):
        if cap_c:
            tile(i, g_ref[0, 0], posc_ref[0], nl, E * cap_c)
        else:
            o_ref[0, i * tq:(i + 1) * tq, :] = x_ref[0, i * tq:(i + 1) * tq, :]

    def body(i, carry):
        tile(i, g_ref[0, 1], posl_ref[0], 0, nl)
        return carry

    lax.fori_loop(nct, T // tq, body, 0, unroll=2)


def _moe_combine(y, gate, pos_l, pos_c, xa, g2, L, cap_l):
    E, B, Ct, D = y.shape
    T = xa.shape[1]
    td = min(512, D)
    tq = _row_tile(L, T - L)
    return pl.pallas_call(
        functools.partial(_combine_kernel, tq=tq, nct=L // tq, cap_l=cap_l),
        grid=(B, D // td),
        in_specs=[pl.BlockSpec((E, 1, Ct, td), lambda b, j: (0, b, 0, j)),
                  pl.BlockSpec((1, E, Ct, 1), lambda b, j: (b, 0, 0, 0)),
                  pl.BlockSpec((1, 1, pos_l.shape[-1]), lambda b, j: (b, 0, 0)),
                  pl.BlockSpec((1, 1, pos_c.shape[-1]), lambda b, j: (b, 0, 0)),
                  pl.BlockSpec((1, T, td), lambda b, j: (b, 0, j)),
                  pl.BlockSpec((1, 2, 1, td), lambda b, j: (b, 0, 0, j))],
        out_specs=pl.BlockSpec((1, T, td), lambda b, j: (b, 0, j)),
        out_shape=jax.ShapeDtypeStruct((B, T, D), F32),
        scratch_shapes=[pltpu.VMEM((E * Ct, td), BF16)],
        compiler_params=_params("arbitrary", "arbitrary"),
        name="moe_combine",
    )(y, gate, pos_l, pos_c, xa, g2)


def _moe(xa, nw, ss2, g2, w_router, w1, w3, w2, layer, L, need_ctx):
    B, T, D = xa.shape
    S = T - L
    E = w_router.shape[1]
    h2, aff = _norm_router(xa, nw, ss2, w_router, L)
    cap_l = CAPACITY_FACTOR * S // E
    gate_l, idx_l = lax.top_k(aff[:, :, L:], cap_l)
    gates, poss = [gate_l], [(idx_l + L).astype(jnp.int32)]
    pos_c = jnp.zeros((B, 1, LANES), jnp.int32)
    if need_ctx:
        cap_c = CAPACITY_FACTOR * L // E
        gate_c, idx_c = lax.top_k(aff[:, :, :L], cap_c)
        gates.append(gate_c)
        poss.append(idx_c.astype(jnp.int32))
        pos_c = poss[1].reshape(B, 1, E * cap_c)
    gate = jnp.concatenate(gates, axis=-1)
    pos = jnp.concatenate(poss, axis=-1)
    Ct = pos.shape[-1]
    xin = _moe_gather(h2, pos[..., None], L, cap_l).reshape(E, B * Ct, D)
    y = _expert_ffn(xin, w1, w3, w2, layer).reshape(E, B, Ct, D)
    return _moe_combine(y, gate[..., None], poss[0].reshape(B, 1, E * cap_l), pos_c, xa, g2, L, cap_l)


_PERM_EO = np.concatenate([np.arange(0, QK_ROPE, 2), np.arange(1, QK_ROPE, 2)])
_PERM_OE = np.concatenate([np.arange(1, QK_ROPE, 2), np.arange(0, QK_ROPE, 2)])


def _ab_input_weight(w_in, a_cols):
    D = w_in.shape[0]
    zr = w_in[:, a_cols + Q_RANK + KV_RANK:]
    zero = jnp.zeros((D, LANES - 2 * QK_ROPE), w_in.dtype)
    g1 = jnp.concatenate([zero, zr[:, _PERM_EO], zr[:, _PERM_EO]], axis=1)
    g2 = jnp.concatenate([zero, zr[:, _PERM_OE], zr[:, _PERM_OE]], axis=1)
    return jnp.concatenate([w_in[:, :a_cols + Q_RANK + KV_RANK], g1, g2], axis=1).astype(BF16)


def _mla_weights(w_qup, w_kvup):
    NH = w_qup.shape[1] // (QK_NOPE + QK_ROPE)
    wq = w_qup.reshape(Q_RANK, NH, QK_NOPE + QK_ROPE)
    rope = wq[:, :, QK_NOPE:]
    wq = jnp.concatenate([wq[:, :, :QK_NOPE], rope[:, :, _PERM_EO], rope[:, :, _PERM_OE]], axis=-1)
    wq = wq.reshape(Q_RANK, NH // 2, 2 * LANES).transpose(1, 0, 2)
    wkv = w_kvup.reshape(KV_RANK, NH, QK_NOPE + V_HEAD)
    wk = jnp.concatenate([wkv[:, :, :QK_NOPE], jnp.zeros((KV_RANK, NH, LANES - QK_NOPE), w_kvup.dtype)], axis=-1)
    wk = wk.reshape(KV_RANK, NH // 2, 2 * LANES).transpose(1, 0, 2)
    wv = wkv[:, :, QK_NOPE:].reshape(KV_RANK, NH // 2, 2 * V_HEAD).transpose(1, 0, 2)
    return wq.astype(BF16), wk.astype(BF16), wv.astype(BF16)


def _rope_tables(L, S):
    t = np.arange(S)
    row = (t // GRID_W).astype(np.float32)
    col = (t % GRID_W).astype(np.float32)
    n_freq = QK_ROPE // 4
    inv = (ROPE_BASE ** (-np.arange(n_freq, dtype=np.float32) / n_freq)).astype(np.float32)
    ang = jnp.concatenate([jnp.asarray(row[:, None] * inv), jnp.asarray(col[:, None] * inv)], axis=-1)
    cos = jnp.concatenate([jnp.ones((L, QK_ROPE // 2), F32), jnp.cos(ang)], axis=0)
    sin = jnp.concatenate([jnp.zeros((L, QK_ROPE // 2), F32), jnp.sin(ang)], axis=0)
    T = L + S
    cc = jnp.concatenate([cos, cos], axis=1)
    ss = jnp.concatenate([-sin, sin], axis=1)
    one = jnp.ones((T, LANES - 2 * QK_ROPE), F32)
    zero = jnp.zeros((T, LANES - 2 * QK_ROPE), F32)
    cq = jnp.concatenate([one, cc, ss], axis=1)
    ck = jnp.concatenate([zero, cc, cc], axis=1)
    sk = jnp.concatenate([zero, ss, ss], axis=1)
    return cq, ck, sk


def kernel(x, c, ctx, c_ctx, mod_w, mod_b, norm1_w, norm2_w, final_norm_w, ab_w_in, ab_w_out, rk_mu, rk_w0, rk_w2, rk_a0, rk_a2, rk_g2, rk_kk, rk_ka, rk_rk, rk_ln_w, rk_ln_b, mla_qn_w, mla_w_qup, mla_kvn_w, mla_w_kvup, na_w_qkv, na_rpb, na_w_out, moe_router, moe_w1, moe_w3, moe_w2):
    B, S, D = x.shape
    L = ctx.shape[1]
    depth = mod_w.shape[0]
    A = rk_w0.shape[-1]
    a_cols = rk_mu.shape[-1]

    rows_pad = -(B + 1) % 8
    cvec = jnp.concatenate([c, c_ctx[None], jnp.zeros((rows_pad, D), F32)], axis=0)
    mods = _mod_vectors(cvec, mod_w, mod_b)
    m_lat = mods[:, :B].reshape(depth, B, 6, D)
    m_ctx = jnp.broadcast_to(mods[:, B].reshape(depth, 1, 6, D), (depth, B, 6, D))
    mm = jnp.stack([m_ctx, m_lat], axis=2)

    cq, ck, sk = _rope_tables(L, S)
    xa = jnp.concatenate([ctx, x], axis=1)

    for layer in range(depth):
        need_ctx = layer < depth - 1
        i = layer // 2
        m = mm[layer]
        ss1, g1 = m[:, :, 0:2], m[:, :, 2:3]
        ss2, g2 = m[:, :, 3:5], m[:, :, 5:6]
        if layer % 2 == 0:
            w_in = _ab_input_weight(ab_w_in[i], a_cols)
            z = _norm_linear(xa, norm1_w[layer], ss1, w_in, L)
            o_a = _rwkv_mixer(z, L, a_cols, rk_mu[i], rk_w0[i], rk_w2[i], rk_a0[i], rk_a2[i], rk_g2[i],
                              rk_kk[i], rk_ka[i], rk_rk[i], rk_ln_w[i], rk_ln_b[i])
            wq, wk, wv = _mla_weights(mla_w_qup[i], mla_w_kvup[i])
            o_b = _mla_attention(z, mla_qn_w[i], mla_kvn_w[i], wq, wk, wv, cq, ck, sk, L,
                                 a_cols, a_cols + Q_RANK, a_cols + Q_RANK + KV_RANK)
            xa = _linear_resid([o_a, o_b], ab_w_out[i].astype(BF16), xa, g1, L)
        else:
            qkv = _norm_linear(xa, norm1_w[layer], ss1, na_w_qkv[i].astype(BF16), L, out_dtype=BF16)
            o = _na_attention(qkv, _na_bias_table(na_rpb[i]), L, need_ctx)
            xa = _linear_resid([o], na_w_out[i].astype(BF16), xa, g1, L)
        xa = _moe(xa, norm2_w[layer], ss2, g2, moe_router[layer], moe_w1, moe_w3, moe_w2, layer, L, need_ctx)
    return _final_norm(xa, final_norm_w, L)
```

```python
import functools

import jax
import jax.numpy as jnp
import numpy as np
from jax import lax
from jax.experimental import pallas as pl
from jax.experimental.pallas import tpu as pltpu

F32 = jnp.float32
BF16 = jnp.bfloat16
HIGHEST = lax.Precision.HIGHEST

GRID_W = 64
NORM_EPS = 1e-6
NEG_INF = -1e30
GN_EPS = 64e-5
A_HEAD_DIM = 64
LORA_W = 64
LORA_A = 64
LORA_G = 128
QK_NOPE = 64
QK_ROPE = 32
V_HEAD = 64
Q_RANK = 384
KV_RANK = 256
ROPE_BASE = 10000.0
C_HEAD_DIM = 64
WIN_R = 8
WIN_C = 16
N_EXPERTS = 16
CAPACITY_FACTOR = 2
SCAN_CHUNK = 64
LANES = 128

VMEM_LIMIT = 56 * 1024 * 1024


def _params(*sem):
    return pltpu.CompilerParams(dimension_semantics=sem, vmem_limit_bytes=VMEM_LIMIT)


def _dot(a, b, precision=None):
    return jnp.dot(a, b, preferred_element_type=F32, precision=precision)


def _dot_nt(a, b, precision=None):
    return lax.dot_general(a, b, (((1,), (1,)), ((), ())), preferred_element_type=F32, precision=precision)


def _row_tile(L, S):
    tm = 256
    while L % tm or S % tm:
        tm //= 2
    return tm


def _mod_kernel(c_ref, w_ref, b_ref, o_ref):
    c = c_ref[...]
    sc = c * jax.nn.sigmoid(c)
    o_ref[0] = _dot(sc.astype(BF16), w_ref[0].astype(BF16)) + b_ref[0]


def _mod_vectors(cvec, mod_w, mod_b):
    depth, D, N = mod_w.shape
    R = cvec.shape[0]
    tn = 1024
    return pl.pallas_call(
        _mod_kernel,
        grid=(depth, N // tn),
        in_specs=[pl.BlockSpec((R, D), lambda l, j: (0, 0)),
                  pl.BlockSpec((1, D, tn), lambda l, j: (l, 0, j)),
                  pl.BlockSpec((1, 1, tn), lambda l, j: (l, 0, j))],
        out_specs=pl.BlockSpec((1, R, tn), lambda l, j: (l, 0, j)),
        out_shape=jax.ShapeDtypeStruct((depth, R, N), F32),
        compiler_params=_params("arbitrary", "arbitrary"),
        name="mod_vectors",
    )(cvec, mod_w, mod_b.reshape(depth, 1, N))


def _wide_tile(T):
    return max(m for m in range(8, 769, 8) if T % m == 0)


def _per_row(mod_ref, i, tm, L, k):
    row = lax.broadcasted_iota(jnp.int32, (tm, 1), 0) + i * tm
    return jnp.where(row < L, mod_ref[0, 0, k:k + 1, :], mod_ref[0, 1, k:k + 1, :])


def _norm_mod(x, nw, shift, scale):
    y = x * lax.rsqrt(jnp.mean(x * x, axis=-1, keepdims=True) + NORM_EPS)
    y = y * nw
    return y * (1.0 + scale) + shift


def _norm_linear_kernel(x_ref, nw_ref, ss_ref, w_ref, o_ref, *, L):
    i = pl.program_id(1)
    tm = x_ref.shape[1]
    h = _norm_mod(x_ref[0], nw_ref[...], _per_row(ss_ref, i, tm, L, 0), _per_row(ss_ref, i, tm, L, 1))
    o_ref[0] = _dot(h.astype(BF16), w_ref[...]).astype(o_ref.dtype)


def _norm_linear(xa, nw, ss, w, L, out_dtype=F32):
    B, T, D = xa.shape
    N = w.shape[1]
    tm = _wide_tile(T)
    return pl.pallas_call(
        functools.partial(_norm_linear_kernel, L=L),
        grid=(B, T // tm),
        in_specs=[pl.BlockSpec((1, tm, D), lambda b, i: (b, i, 0)),
                  pl.BlockSpec((1, D), lambda b, i: (0, 0)),
                  pl.BlockSpec((1, 2, 2, D), lambda b, i: (b, 0, 0, 0)),
                  pl.BlockSpec((D, N), lambda b, i: (0, 0))],
        out_specs=pl.BlockSpec((1, tm, N), lambda b, i: (b, i, 0)),
        out_shape=jax.ShapeDtypeStruct((B, T, N), out_dtype),
        compiler_params=_params("arbitrary", "arbitrary"),
        name="norm_linear",
    )(xa, nw.reshape(1, D), ss, w)


def _linear_resid_kernel(*refs, ks, L):
    n = len(ks)
    a_refs, (w_ref, x_ref, g_ref, o_ref) = refs[:n], refs[n:]
    acc = None
    off = 0
    for a_ref, k in zip(a_refs, ks):
        part = _dot(a_ref[0].astype(BF16), w_ref[off:off + k, :])
        acc = part if acc is None else acc + part
        off += k
    o_ref[0] = x_ref[0] + _per_row(g_ref, pl.program_id(1), x_ref.shape[1], L, 0) * acc


def _linear_resid(a_list, w, xa, gate, L):
    B, T, D = xa.shape
    tm = _wide_tile(T)
    ks = tuple(a.shape[-1] for a in a_list)
    in_specs = [pl.BlockSpec((1, tm, k), lambda b, i: (b, i, 0)) for k in ks]
    in_specs += [pl.BlockSpec(w.shape, lambda b, i: (0, 0)),
                 pl.BlockSpec((1, tm, D), lambda b, i: (b, i, 0)),
                 pl.BlockSpec((1, 2, 1, D), lambda b, i: (b, 0, 0, 0))]
    return pl.pallas_call(
        functools.partial(_linear_resid_kernel, ks=ks, L=L),
        grid=(B, T // tm),
        in_specs=in_specs,
        out_specs=pl.BlockSpec((1, tm, D), lambda b, i: (b, i, 0)),
        out_shape=jax.ShapeDtypeStruct((B, T, D), F32),
        compiler_params=_params("arbitrary", "arbitrary"),
        name="linear_resid",
    )(*a_list, w, xa, gate)


def _rms_kernel(x_ref, w_ref, o_ref):
    x = x_ref[0]
    o_ref[0] = x * lax.rsqrt(jnp.mean(x * x, axis=-1, keepdims=True) + NORM_EPS) * w_ref[...]


def _final_norm(xa, w, L):
    B, T, D = xa.shape
    S = T - L
    tm = _row_tile(L, S)
    nct = L // tm
    return pl.pallas_call(
        _rms_kernel,
        grid=(B, S // tm),
        in_specs=[pl.BlockSpec((1, tm, D), lambda b, i: (b, i + nct, 0)),
                  pl.BlockSpec((1, D), lambda b, i: (0, 0))],
        out_specs=pl.BlockSpec((1, tm, D), lambda b, i: (b, i, 0)),
        out_shape=jax.ShapeDtypeStruct((B, S, D), F32),
        compiler_params=_params("arbitrary", "arbitrary"),
        name="final_norm",
    )(xa, w.reshape(1, D))


def _rms(x, w):
    return x * lax.rsqrt(jnp.mean(x * x, axis=-1, keepdims=True) + NORM_EPS) * w


def _softmax_pv(chains):
    m = [functools.reduce(jnp.maximum, [jnp.max(s, axis=-1, keepdims=True) for s, _ in ch]) for ch in chains]
    p = [[jnp.exp(s - mi) for s, _ in ch] for ch, mi in zip(chains, m)]
    l = [functools.reduce(jnp.add, [jnp.sum(x, axis=-1, keepdims=True) for x in pc]) for pc in p]
    o = [functools.reduce(jnp.add, [_dot(x.astype(BF16), v) for x, (_, v) in zip(pc, ch)]) for pc, ch in zip(p, chains)]
    return [oi / li for oi, li in zip(o, l)]


def _mla_kernel(zq_ref, zkv_ref, zr_ref, qn_ref, kvn_ref, wq_ref, wk_ref, wv_ref, cq_ref, ck_ref, sk_ref,
                o_ref, q_s, k_s, v_s, *, L, tq, scale):
    T = zq_ref.shape[1]
    zqn = _rms(zq_ref[0], qn_ref[...]).astype(BF16)
    zkvn = _rms(zkv_ref[0], kvn_ref[...]).astype(BF16)
    qh = _dot(zqn, wq_ref[0])
    kn = _dot(zkvn, wk_ref[0])
    v_s[...] = _dot(zkvn, wv_ref[0]).astype(BF16)
    zr = zr_ref[0]
    kr = zr[:, :LANES] * ck_ref[...] + zr[:, LANES:] * sk_ref[...]
    cq = cq_ref[...] * scale
    for h in range(2):
        q_s[h] = (qh[:, h * LANES:(h + 1) * LANES] * cq).astype(BF16)
        k_s[h] = (kn[:, h * LANES:(h + 1) * LANES] + kr).astype(BF16)
    first_head = lax.broadcasted_iota(jnp.int32, (tq, LANES), 1) < V_HEAD

    def tile(row0, nk):
        s = [_dot_nt(q_s[h, pl.ds(row0, tq), :], k_s[h, 0:nk, :]) for h in range(2)]
        outs = _softmax_pv([[(si, v_s[0:nk, :])] for si in s])
        o_ref[0, pl.ds(row0, tq), :] = jnp.where(first_head, outs[0], outs[1])

    for i in range(L // tq):
        tile(i * tq, L)

    def body(i, carry):
        tile(pl.multiple_of(i * tq, tq), T)
        return carry

    lax.fori_loop(L // tq, T // tq, body, 0, unroll=2)


def _mla_attention(z, qn_w, kvn_w, wq, wk, wv, cq, ck, sk, L, col_q, col_kv, col_r):
    B, T, _ = z.shape
    HP = wq.shape[0]
    tq = _row_tile(L, T - L)
    scale = float((QK_NOPE + QK_ROPE) ** -0.5)
    return pl.pallas_call(
        functools.partial(_mla_kernel, L=L, tq=tq, scale=scale),
        grid=(B, HP),
        in_specs=[pl.BlockSpec((1, T, Q_RANK), lambda b, p: (b, 0, col_q // Q_RANK)),
                  pl.BlockSpec((1, T, KV_RANK), lambda b, p: (b, 0, col_kv // KV_RANK)),
                  pl.BlockSpec((1, T, 2 * LANES), lambda b, p: (b, 0, col_r // (2 * LANES))),
                  pl.BlockSpec((1, Q_RANK), lambda b, p: (0, 0)),
                  pl.BlockSpec((1, KV_RANK), lambda b, p: (0, 0)),
                  pl.BlockSpec((1, Q_RANK, 2 * LANES), lambda b, p: (p, 0, 0)),
                  pl.BlockSpec((1, KV_RANK, 2 * LANES), lambda b, p: (p, 0, 0)),
                  pl.BlockSpec((1, KV_RANK, LANES), lambda b, p: (p, 0, 0)),
                  pl.BlockSpec((T, LANES), lambda b, p: (0, 0)),
                  pl.BlockSpec((T, LANES), lambda b, p: (0, 0)),
                  pl.BlockSpec((T, LANES), lambda b, p: (0, 0))],
        out_specs=pl.BlockSpec((1, T, LANES), lambda b, p: (b, 0, p)),
        out_shape=jax.ShapeDtypeStruct((B, T, HP * LANES), F32),
        scratch_shapes=[pltpu.VMEM((2, T, LANES), BF16), pltpu.VMEM((2, T, LANES), BF16),
                        pltpu.VMEM((T, LANES), BF16)],
        compiler_params=_params("arbitrary", "arbitrary"),
        name="mla_attention",
    )(z, z, z, qn_w.reshape(1, -1), kvn_w.reshape(1, -1), wq, wk, wv, cq, ck, sk)


def _na_kernel(q_ref, k_ref, v_ref, bt_ref, o_ref, k_s, v_s, *, L, rows, kr, need_ctx, scale):
    W = GRID_W
    rpb = 4 if rows % 4 == 0 else 1
    k_s[...] = k_ref[0].astype(BF16)
    v_s[...] = v_ref[0].astype(BF16)
    nwin = kr * W
    lane = lax.broadcasted_iota(jnp.int32, (W, LANES), 1)
    head_mask = [(lane < C_HEAD_DIM).astype(F32), (lane >= C_HEAD_DIM).astype(F32)]
    first_head = lane < C_HEAD_DIM
    qcol = lax.broadcasted_iota(jnp.int32, (W, nwin), 0)
    kcol = lax.broadcasted_iota(jnp.int32, (W, nwin), 1) % W
    cstart = jnp.clip(qcol - WIN_C // 2, 0, W - WIN_C)
    col_valid = (kcol >= cstart) & (kcol < cstart + WIN_C)

    def row_block(rb, carry):
        q_blk = q_ref[0, pl.ds(pl.multiple_of(L + rb * (rpb * W), W), rpb * W), :] * scale
        s_ctx = [_dot_nt((q_blk * jnp.concatenate([head_mask[h]] * rpb, axis=0)).astype(BF16), k_s[0:L, :])
                 for h in range(2)]
        chains, q0s = [], []
        for j in range(rpb):
            r = rb * rpb + j
            rs = jnp.clip(r - kr // 2, 0, rows - kr)
            k0 = pl.multiple_of(L + rs * W, W)
            q0s.append(pl.multiple_of(L + r * W, W))
            q = q_blk[j * W:(j + 1) * W]
            kw = k_s[pl.ds(k0, nwin), :]
            vw = v_s[pl.ds(k0, nwin), :]
            dr0 = rs - r + (WIN_R - 1)
            for h in range(2):
                s_nb = _dot_nt((q * head_mask[h]).astype(BF16), kw)
                bias = jnp.concatenate([bt_ref[0, h, dr0 + 2 * m] for m in range(kr // 2)], axis=-1)
                s_nb = jnp.where(col_valid, s_nb + bias, NEG_INF)
                chains.append([(s_nb, vw), (s_ctx[h][j * W:(j + 1) * W], v_s[0:L, :])])
        outs = _softmax_pv(chains)
        for j in range(rpb):
            o_ref[0, pl.ds(q0s[j], W), :] = jnp.where(first_head, outs[2 * j], outs[2 * j + 1])
        return carry

    lax.fori_loop(0, rows // rpb, row_block, 0, unroll=2)

    tq = min(L, 256)
    lane_c = lax.broadcasted_iota(jnp.int32, (tq, LANES), 1)
    for i in range(L // tq):
        if need_ctx:
            q = q_ref[0, i * tq:(i + 1) * tq, :] * scale
            hm = [lane_c < C_HEAD_DIM, lane_c >= C_HEAD_DIM]
            s = [_dot_nt(jnp.where(hm[h], q, 0.0).astype(BF16), k_s[0:L, :]) for h in range(2)]
            outs = _softmax_pv([[(si, v_s[0:L, :])] for si in s])
            o_ref[0, i * tq:(i + 1) * tq, :] = jnp.where(lane_c < C_HEAD_DIM, outs[0], outs[1])
        else:
            o_ref[0, i * tq:(i + 1) * tq, :] = jnp.zeros((tq, LANES), F32)


def _na_attention(qkv, bias_tab, L, need_ctx):
    B, T, D3 = qkv.shape
    D = D3 // 3
    HP = D // LANES
    rows = (T - L) // GRID_W
    kr = min(WIN_R, rows)
    assert kr % 2 == 0
    nd = bias_tab.shape[2]
    return pl.pallas_call(
        functools.partial(_na_kernel, L=L, rows=rows, kr=kr, need_ctx=need_ctx, scale=float(C_HEAD_DIM ** -0.5)),
        grid=(B, HP),
        in_specs=[pl.BlockSpec((1, T, LANES), lambda b, p: (b, 0, p)),
                  pl.BlockSpec((1, T, LANES), lambda b, p: (b, 0, HP + p)),
                  pl.BlockSpec((1, T, LANES), lambda b, p: (b, 0, 2 * HP + p)),
                  pl.BlockSpec((1, 2, nd, GRID_W, LANES), lambda b, p: (p, 0, 0, 0, 0))],
        out_specs=pl.BlockSpec((1, T, LANES), lambda b, p: (b, 0, p)),
        out_shape=jax.ShapeDtypeStruct((B, T, D), F32),
        scratch_shapes=[pltpu.VMEM((T, LANES), BF16), pltpu.VMEM((T, LANES), BF16)],
        compiler_params=_params("arbitrary", "arbitrary"),
        name="na_attention",
    )(qkv, qkv, qkv, bias_tab)


def _na_bias_table(rpb):
    H = rpb.shape[0]
    qc = np.arange(GRID_W)[:, None]
    kc = np.arange(GRID_W)[None, :]
    dc = np.clip(kc - qc + (WIN_C - 1), 0, 2 * WIN_C - 2)
    t = rpb[:, :, dc]
    t2 = jnp.concatenate([t[:, :-1], t[:, 1:]], axis=-1)
    return t2.reshape(H // 2, 2, 2 * WIN_R - 2, GRID_W, 2 * GRID_W)


def _bf(x):
    return x.astype(BF16)


def _seg_sum(x, ones_bd):
    hi = _bf(x)
    lo = _bf(x - hi.astype(F32))
    return _dot(hi, ones_bd) + _dot(lo, ones_bd)


def _head_ones(A):
    seg = np.arange(A) // A_HEAD_DIM
    return jnp.asarray(seg[:, None] == seg[None, :], dtype=BF16)


def _rwkv_prep_kernel(z_ref, zp_ref, zn_ref, mu_ref, w0_ref, w2_ref, a0_ref, a2_ref, g2_ref, kk_ref, ka_ref, rk_ref,
                      ones_ref, r_o, v_o, kkn_o, g_o, bonus_o, lw_o, beta_o, kd_o, *, nct, nt, A):
    i = pl.program_id(1)
    za = z_ref[0]
    tm = za.shape[0]
    row = lax.broadcasted_iota(jnp.int32, za.shape, 0)
    seg_first = (i == 0) | (i == nct)
    seg_last = (i == nct - 1) | (i == nt - 1)
    prev_row = jnp.where(seg_first, 0.0, zp_ref[0, 7:8, :])
    next_row = jnp.where(seg_last, 0.0, zn_ref[0, 0:1, :])
    prev = jnp.where(row == 0, prev_row, pltpu.roll(za, 1, 0))
    nxt = jnp.where(row == tm - 1, next_row, pltpu.roll(za, tm - 1, 0))
    zs = za + mu_ref[0:1, :] * (prev - za) + mu_ref[1:2, :] * (nxt - za)
    r = zs[:, 0:A]
    k = zs[:, A:2 * A]
    v = zs[:, 2 * A:3 * A]
    wd = _bf(jnp.tanh(zs[:, 3 * A:3 * A + LANES]))
    ad = _bf(zs[:, 3 * A + LANES:3 * A + 2 * LANES])
    gd = _bf(jax.nn.sigmoid(zs[:, 3 * A + 2 * LANES:3 * A + 3 * LANES]))
    ones = ones_ref[...]
    kk = k * kk_ref[...]
    kkn = kk / jnp.maximum(jnp.sqrt(_seg_sum(kk * kk, ones)), 1e-12)
    kd_sum = None
    for d in range(2):
        w_log = -jax.nn.softplus(-(w0_ref[d:d + 1, :] + _dot(wd, w2_ref[d]))) - 0.5
        lw_o[d, 0] = -jnp.exp(w_log)
        a = jax.nn.sigmoid(a0_ref[d:d + 1, :] + _dot(ad, a2_ref[d]))
        beta_o[d, 0] = kkn * a
        kd = k * (1.0 + (a - 1.0) * ka_ref[...])
        kd_o[d, 0] = kd
        kd_sum = kd if kd_sum is None else kd_sum + kd
    bonus_o[0] = _seg_sum(r * kd_sum * rk_ref[...], ones) * v
    r_o[0] = r
    v_o[0] = v
    kkn_o[0] = kkn
    g_o[0] = _dot(gd, g2_ref[...])


def _rwkv_prep(z, L, a_cols, mu, w0, w2, a0, a2, g2, k_k, k_a, r_k):
    B, T, _ = z.shape
    A = w0.shape[-1]
    assert 2 * LORA_W == LANES and 2 * LORA_A == LANES and LORA_G == LANES and a_cols == 3 * A + 3 * LANES
    tm = _row_tile(L, T - L)
    nt = T // tm
    hb = tm // 8

    def pad_lora(w):
        zero = jnp.zeros_like(w[0])
        return _bf(jnp.stack([jnp.concatenate([w[0], zero], 0), jnp.concatenate([zero, w[1]], 0)]))

    def const(shape):
        return pl.BlockSpec(shape, lambda b, i: (0,) * len(shape))

    tile = pl.BlockSpec((1, tm, A), lambda b, i: (b, i, 0))
    tile_d = pl.BlockSpec((2, 1, tm, A), lambda b, i: (0, b, i, 0))
    sd = jax.ShapeDtypeStruct((B, T, A), F32)
    sd_d = jax.ShapeDtypeStruct((2, B, T, A), F32)
    return pl.pallas_call(
        functools.partial(_rwkv_prep_kernel, nct=L // tm, nt=nt, A=A),
        grid=(B, nt),
        in_specs=[pl.BlockSpec((1, tm, a_cols), lambda b, i: (b, i, 0)),
                  pl.BlockSpec((1, 8, a_cols), lambda b, i: (b, jnp.maximum(i * hb - 1, 0), 0)),
                  pl.BlockSpec((1, 8, a_cols), lambda b, i: (b, jnp.minimum((i + 1) * hb, T // 8 - 1), 0)),
                  const((2, a_cols)), const((2, A)), const((2, 2 * LORA_W, A)), const((2, A)),
                  const((2, 2 * LORA_A, A)), const((LORA_G, A)), const((1, A)), const((1, A)), const((1, A)),
                  const((A, A))],
        out_specs=[tile, tile, tile, tile, tile, tile_d, tile_d, tile_d],
        out_shape=[sd, sd, sd, sd, sd, sd_d, sd_d, sd_d],
        compiler_params=_params("arbitrary", "arbitrary"),
        name="rwkv_prep",
    )(z, z, z, mu, w0, pad_lora(w2), a0, pad_lora(a2), _bf(g2), k_k.reshape(1, A), k_a.reshape(1, A),
      r_k.reshape(1, A), _head_ones(A))


def _tri_inverse(lms, eye, m16, m32):
    d0 = [_bf(jnp.where(m16, lm, 0.0)) for lm in lms]
    t = [eye + d.astype(F32) for d in d0]
    s = [_dot(d, d) for d in d0]
    for step in range(3):
        sb = [_bf(x) for x in s]
        t = [x + _dot(_bf(x), y) for x, y in zip(t, sb)]
        if step < 2:
            s = [_dot(y, y) for y in sb]
    for lvl in (m32 & (~m16), ~m32):
        tb = [_bf(x) for x in t]
        w = [_bf(_dot(_bf(jnp.where(lvl, lm, 0.0)), y)) for lm, y in zip(lms, tb)]
        t = [x + _dot(y, z) for x, y, z in zip(t, tb, w)]
    return t


def _wkv_kernel(*refs, NP, NB):
    C = SCAN_CHUNK
    P = 2 * C
    fwd_refs, bwd_refs, (yf_ref, yb_ref, h_s) = refs[0:6], refs[6:12], refs[12:]

    @pl.when(pl.program_id(1) == 0)
    def _():
        h_s[...] = jnp.zeros(h_s.shape, F32)

    ri = lax.broadcasted_iota(jnp.int32, (P, P), 0)
    ci = lax.broadcasted_iota(jnp.int32, (P, P), 1)
    same = (ri // C) == (ci // C)
    diff = (ri % C) - (ci % C)
    eye_b = ri == ci
    eye = eye_b.astype(F32)
    m16 = (ri // 16) == (ci // 16)
    m32 = (ri // 32) == (ci // 32)
    diff64 = lax.broadcasted_iota(jnp.int32, (C, C), 0) - lax.broadcasted_iota(jnp.int32, (C, C), 1)
    top = lax.broadcasted_iota(jnp.int32, (C, P), 1) < C
    zero_blk = jnp.zeros((P, P), BF16)
    sls = [slice(p * P, (p + 1) * P) for p in range(NP)]

    def bd(x):
        return [_bf(jnp.concatenate([jnp.where(top, x[:, sl], 0.0), jnp.where(top, 0.0, x[:, sl])], axis=0))
                for sl in sls]

    at, rt, bt, kt, bh, kh, vv, etots, before, before_eq = [], [], [], [], [], [], [], [], [], []
    for (r_ref, v_ref, kk_ref, lw_ref, beta_ref, kd_ref), sgn in ((fwd_refs, 1), (bwd_refs, -1)):
        order = diff * sgn
        tri = _bf(((diff64 * sgn) >= 0).astype(F32))
        for s in range(NB):
            before += [same & (order > 0)] * NP
            before_eq += [same & (order >= 0)] * NP
            lw = lw_ref[0, s]
            lw_hi = _bf(lw)
            lw_md = _bf(lw - lw_hi.astype(F32))
            lw_lo = _bf(lw - lw_hi.astype(F32) - lw_md.astype(F32))
            cum = _dot(tri, lw_hi) + _dot(tri, lw_md) + _dot(tri, lw_lo)
            tot = jnp.sum(lw, axis=0, keepdims=True)
            beta = beta_ref[0, s]
            kd = kd_ref[0, s]
            e_neg = jnp.exp(-cum)
            e_tail = jnp.exp(tot - cum)
            at += bd(-kk_ref[s] * jnp.exp(cum - lw))
            rt += bd(r_ref[s] * jnp.exp(cum))
            bt += bd(beta * e_neg)
            kt += bd(kd * e_neg)
            bh += bd(beta * e_tail)
            kh += bd(kd * e_tail)
            vv += bd(v_ref[s])
            etot = jnp.exp(tot)
            etots += [etot[:, sl] for sl in sls]

    ar = [jnp.concatenate([a, r], axis=0) for a, r in zip(at, rt)]
    arb = [_dot_nt(x, b) for x, b in zip(ar, bt)]
    ark = [_dot_nt(x, k) for x, k in zip(ar, kt)]
    lab = [jnp.where(m, x[:P], 0.0) for x, m in zip(arb, before)]
    tinv = _tri_inverse(lab, eye, m16, m32)
    u = [_dot(_bf(jnp.where(m, x[:P], 0.0)), v) for x, v, m in zip(ark, vv, before)]
    x = [_bf(_dot(_bf(t), jnp.concatenate([a, _bf(w)], axis=1))) for t, a, w in zip(tinv, at, u)]
    rhs = [jnp.concatenate([xi, jnp.concatenate([zero_blk, v], axis=1)], axis=0) for xi, v in zip(x, vv)]
    mn = [lax.dot_general(jnp.concatenate([b, k], axis=0), w, (((0,), (0,)), ((), ())), preferred_element_type=F32)
          for b, k, w in zip(bh, kh, rhs)]
    lr = [_bf(jnp.concatenate([jnp.where(m, xb[P:], 0.0), jnp.where(m, xk[P:], 0.0)], axis=1))
          for xb, xk, m in zip(arb, ark, before_eq)]
    qy = [_dot(l, w) for l, w in zip(lr, rhs)]
    qm = [_bf(jnp.concatenate([r.astype(F32) + q[:, :P], jnp.where(eye_b, e, 0.0) + m[:, :P]], axis=0))
          for r, q, m, e in zip(rt, qy, mn, etots)]
    nchain = 2 * NB * NP
    hin = [h_s[i] for i in range(nchain)]
    h_hi = [_bf(h) for h in hin]
    h_lo = [_bf(h - hh.astype(F32)) for h, hh in zip(hin, h_hi)]
    res = [_dot(w, hh) + _dot(w, hl) for w, hh, hl in zip(qm, h_hi, h_lo)]
    for i in range(nchain):
        ybd = res[i][:P] + qy[i][:, P:]
        y_ref = yf_ref if i < NB * NP else yb_ref
        y_ref[(i // NP) % NB, :, sls[i % NP]] = ybd[:C] + ybd[C:]
        h_s[i] = res[i][P:] + mn[i][:, P:]


def _wkv_scan(r, v, kk, lw, beta, kd, L):
    B, T, A = r.shape
    C = SCAN_CHUNK
    nC = T // C
    nct = L // C
    NP = A // (2 * C)

    def rev(c):
        return jnp.where(c < nct, nct - 1 - c, nC - 1 - (c - nct))

    NB = 2 if B % 2 == 0 else 1
    fwd = pl.BlockSpec((NB, C, A), lambda b, c: (b, c, 0))
    bwd = pl.BlockSpec((NB, C, A), lambda b, c: (b, rev(c), 0))
    fwd_d = pl.BlockSpec((1, NB, C, A), lambda b, c: (0, b, c, 0))
    bwd_d = pl.BlockSpec((1, NB, C, A), lambda b, c: (1, b, rev(c), 0))
    sd = jax.ShapeDtypeStruct((B, T, A), F32)
    return pl.pallas_call(
        functools.partial(_wkv_kernel, NP=NP, NB=NB),
        grid=(B // NB, nC),
        in_specs=[fwd, fwd, fwd, fwd_d, fwd_d, fwd_d, bwd, bwd, bwd, bwd_d, bwd_d, bwd_d],
        out_specs=[fwd, bwd],
        out_shape=[sd, sd],
        scratch_shapes=[pltpu.VMEM((2 * NB * NP, 2 * C, 2 * C), F32)],
        compiler_params=_params("arbitrary", "arbitrary"),
        name="wkv_scan",
    )(r, v, kk, lw, beta, kd, r, v, kk, lw, beta, kd)


def _rwkv_post_kernel(yf_ref, yb_ref, bonus_ref, g_ref, lnw_ref, lnb_ref, ones_ref, o_ref):
    y = yf_ref[0] + yb_ref[0]
    ones = ones_ref[...]
    inv_n = 1.0 / A_HEAD_DIM
    d = y - _seg_sum(y, ones) * inv_n
    var = _seg_sum(d * d, ones) * inv_n
    yn = d * lax.rsqrt(var + GN_EPS) * lnw_ref[...] + lnb_ref[...]
    o_ref[0] = (yn + bonus_ref[0]) * g_ref[0]


def _rwkv_post(yf, yb, bonus, g, ln_w, ln_b, L):
    B, T, A = bonus.shape
    tm = _row_tile(L, T - L)
    tile = pl.BlockSpec((1, tm, A), lambda b, i: (b, i, 0))
    vec = pl.BlockSpec((1, A), lambda b, i: (0, 0))
    return pl.pallas_call(
        _rwkv_post_kernel,
        grid=(B, T // tm),
        in_specs=[tile, tile, tile, tile, vec, vec, pl.BlockSpec((A, A), lambda b, i: (0, 0))],
        out_specs=tile,
        out_shape=jax.ShapeDtypeStruct((B, T, A), F32),
        compiler_params=_params("arbitrary", "arbitrary"),
        name="rwkv_post",
    )(yf, yb, bonus, g, ln_w.reshape(1, A), ln_b.reshape(1, A), _head_ones(A))


def _rwkv_mixer(z, L, a_cols, mu, w0, w2, a0, a2, g2, k_k, k_a, r_k, ln_w, ln_b):
    r, v, kk, g, bonus, lw, beta, kd = _rwkv_prep(z, L, a_cols, mu, w0, w2, a0, a2, g2, k_k, k_a, r_k)
    yf, yb = _wkv_scan(r, v, kk, lw, beta, kd, L)
    return _rwkv_post(yf, yb, bonus, g, ln_w, ln_b, L)


def _router_kernel(x_ref, nw_ref, ss_ref, wr_ref, h_ref, aff_ref, *, L):
    i = pl.program_id(1)
    tm = x_ref.shape[1]
    h = _norm_mod(x_ref[0], nw_ref[...], _per_row(ss_ref, i, tm, L, 0), _per_row(ss_ref, i, tm, L, 1))
    h_ref[0] = h.astype(BF16)
    logits = _dot_nt(wr_ref[...], h, HIGHEST)
    m = jnp.max(logits, axis=0, keepdims=True)
    p = jnp.exp(logits - m)
    aff_ref[0] = p / jnp.sum(p, axis=0, keepdims=True)


def _norm_router(xa, nw, ss, w_router, L):
    B, T, D = xa.shape
    E = w_router.shape[1]
    tm = max(m for m in range(LANES, 769, LANES) if T % m == 0)
    return pl.pallas_call(
        functools.partial(_router_kernel, L=L),
        grid=(B, T // tm),
        in_specs=[pl.BlockSpec((1, tm, D), lambda b, i: (b, i, 0)),
                  pl.BlockSpec((1, D), lambda b, i: (0, 0)),
                  pl.BlockSpec((1, 2, 2, D), lambda b, i: (b, 0, 0, 0)),
                  pl.BlockSpec((E, D), lambda b, i: (0, 0))],
        out_specs=[pl.BlockSpec((1, tm, D), lambda b, i: (b, i, 0)),
                   pl.BlockSpec((1, E, tm), lambda b, i: (b, 0, i))],
        out_shape=[jax.ShapeDtypeStruct((B, T, D), BF16), jax.ShapeDtypeStruct((B, E, T), F32)],
        compiler_params=_params("arbitrary", "arbitrary"),
        name="norm_router",
    )(xa, nw.reshape(1, D), ss, w_router.T)


def _select_top(sets):
    keys = [pltpu.bitcast(aff, jnp.int32) for aff, _, _ in sets]
    E = keys[0].shape[0]

    def bisect(_, carry):
        out = []
        for key, (_, cap, _), (lo, hi) in zip(keys, sets, carry):
            mid = lo + ((hi - lo + 1) >> 1)
            ok = jnp.sum(jnp.where(key >= mid, 1.0, 0.0), axis=1, keepdims=True) >= cap
            out.append((jnp.where(ok, mid, lo), jnp.where(ok, hi, mid - 1)))
        return tuple(out)

    init = tuple((jnp.zeros((E, 1), jnp.int32), jnp.full((E, 1), 0x7F800000, jnp.int32)) for _ in sets)
    bounds = lax.fori_loop(0, 32, bisect, init)
    res = []
    for key, (_, cap, tri), (thr, _) in zip(keys, sets, bounds):
        above = key > thr
        tie = key == thr
        need = cap - jnp.sum(jnp.where(above, 1.0, 0.0), axis=1, keepdims=True)
        tie_rank = _dot(jnp.where(tie, 1.0, 0.0).astype(BF16), tri)
        sel = jnp.where(above | (tie & (tie_rank < need)), 1.0, 0.0)
        res.append((sel, _dot(sel.astype(BF16), tri)))
    return res


def _route_gather_kernel(aff_ref, h_ref, x_ref, pos_ref, gate_ref, sel_s, slot_s, tri_s, *, L, cap_l, cap_c):
    b = pl.program_id(0)
    e = pl.program_id(1)
    T = h_ref.shape[1]
    S = T - L

    @pl.when((b == 0) & (e == 0))
    def _():
        n = tri_s.shape[0]
        tri_s[...] = jnp.where(lax.broadcasted_iota(jnp.int32, (n, n), 0) < lax.broadcasted_iota(jnp.int32, (n, n), 1),
                               1.0, 0.0).astype(BF16)

    @pl.when(e == 0)
    def _():
        sets = [(aff_ref[0, :, L:], cap_l, tri_s[0:S, 0:S])]
        if cap_c:
            sets.append((aff_ref[0, :, 0:L], cap_c, tri_s[0:L, 0:L]))
        picked = _select_top(sets)
        sel_s[:, L:], slot_s[:, L:] = picked[0]
        if cap_c:
            sel_s[:, 0:L], slot_s[:, 0:L] = picked[1]

    def gather(lo, n, cap, row0):
        sel = sel_s[pl.ds(e, 1), lo:lo + n]
        slot = slot_s[pl.ds(e, 1), lo:lo + n]
        want = lax.broadcasted_iota(jnp.int32, (cap, n), 0).astype(F32)
        hit = (sel > 0.5) & (slot == want)
        x_ref[0, 0, row0:row0 + cap, :] = _dot(jnp.where(hit, 1.0, 0.0).astype(BF16), h_ref[0, lo:lo + n, :]).astype(BF16)
        tok = lax.broadcasted_iota(jnp.int32, (cap, n), 1) + lo
        pos_ref[0, 0, row0:row0 + cap, :] = jnp.sum(jnp.where(hit, tok, 0), axis=1, keepdims=True)
        aff = aff_ref[0, pl.ds(e, 1), lo:lo + n]
        gate_ref[0, 0, row0:row0 + cap, :] = jnp.sum(jnp.where(hit, aff, 0.0), axis=1, keepdims=True)

    gather(L, S, cap_l, 0)
    if cap_c:
        gather(0, L, cap_c, cap_l)


def _route_gather(aff, h2, L, cap_l, cap_c):
    B, T, D = h2.shape
    E = aff.shape[1]
    Ct = cap_l + cap_c
    return pl.pallas_call(
        functools.partial(_route_gather_kernel, L=L, cap_l=cap_l, cap_c=cap_c),
        grid=(B, E),
        in_specs=[pl.BlockSpec((1, E, T), lambda b, e: (b, 0, 0)),
                  pl.BlockSpec((1, T, D), lambda b, e: (b, 0, 0))],
        out_specs=[pl.BlockSpec((1, 1, Ct, D), lambda b, e: (e, b, 0, 0)),
                   pl.BlockSpec((1, 1, Ct, 1), lambda b, e: (b, e, 0, 0)),
                   pl.BlockSpec((1, 1, Ct, 1), lambda b, e: (b, e, 0, 0))],
        out_shape=[jax.ShapeDtypeStruct((E, B, Ct, D), BF16), jax.ShapeDtypeStruct((B, E, Ct, 1), jnp.int32),
                   jax.ShapeDtypeStruct((B, E, Ct, 1), F32)],
        scratch_shapes=[pltpu.VMEM((E, T), F32), pltpu.VMEM((E, T), F32), pltpu.VMEM((max(L, T - L),) * 2, BF16)],
        compiler_params=_params("arbitrary", "arbitrary"),
        name="moe_route_gather",
    )(aff, h2)


def _ffn_kernel(x_ref, w1_ref, w3_ref, w2_ref, o_ref, *, rm):
    j = pl.program_id(1)
    R = x_ref.shape[1]
    w1 = w1_ref[0, 0].astype(BF16)
    w3 = w3_ref[0, 0].astype(BF16)
    w2 = w2_ref[0, 0].astype(BF16)

    @pl.when(j == 0)
    def _():
        o_ref[...] = jnp.zeros(o_ref.shape, F32)

    def rows(i, carry):
        r0 = pl.multiple_of(i * rm, rm)
        x = x_ref[0, pl.ds(r0, rm), :]
        a = _dot(x, w1)
        b = _dot(x, w3)
        hid = (a * jax.nn.sigmoid(a) * b).astype(BF16)
        o_ref[0, pl.ds(r0, rm), :] += _dot(hid, w2)
        return carry

    lax.fori_loop(0, R // rm, rows, 0, unroll=True)


def _expert_ffn(xin, w1, w3, w2, layer):
    E, R, D = xin.shape
    F = w1.shape[-1]
    tf = min(512, F)
    rm = max(m for m in (768, 512, 256, 128, 64, 32, 16) if R % m == 0)
    return pl.pallas_call(
        functools.partial(_ffn_kernel, rm=rm),
        grid=(E, F // tf),
        in_specs=[pl.BlockSpec((1, R, D), lambda e, j: (e, 0, 0)),
                  pl.BlockSpec((1, 1, D, tf), lambda e, j: (layer, e, 0, j)),
                  pl.BlockSpec((1, 1, D, tf), lambda e, j: (layer, e, 0, j)),
                  pl.BlockSpec((1, 1, tf, D), lambda e, j: (layer, e, j, 0))],
        out_specs=pl.BlockSpec((1, R, D), lambda e, j: (e, 0, 0)),
        out_shape=jax.ShapeDtypeStruct((E, R, D), F32),
        compiler_params=_params("arbitrary", "arbitrary"),
        name="expert_ffn",
    )(xin, w1, w3, w2)


def _combine_kernel(y_ref, gate_ref, posl_ref, posc_ref, x_ref, g_ref, o_ref, yg_s, *, tq, nct, cap_l):
    E, _, Ct, td = y_ref.shape
    T = x_ref.shape[1]
    cap_c = Ct - cap_l
    nl = E * cap_l
    for e in range(E):
        yg_s[e * cap_l:(e + 1) * cap_l, :] = (y_ref[e, 0, :cap_l, :] * gate_ref[0, e, :cap_l, :]).astype(BF16)
        if cap_c:
            yg_s[nl + e * cap_c:nl + (e + 1) * cap_c, :] = (y_ref[e, 0, cap_l:, :] * gate_ref[0, e, cap_l:, :]).astype(BF16)

    def tile(i, gate_row, pos, lo, n):
        r0 = pl.multiple_of(i * tq, tq)
        tok = lax.broadcasted_iota(jnp.int32, (tq, n), 0) + r0
        onehot = jnp.where(tok == pos, 1.0, 0.0).astype(BF16)
        o_ref[0, pl.ds(r0, tq), :] = x_ref[0, pl.ds(r0, tq), :] + gate_row * _dot(onehot, yg_s[lo:lo + n, :])

    for i in range(nct):
        if cap_c:
            tile(i, g_ref[0, 0], posc_ref[0], nl, E * cap_c)
        else:
            o_ref[0, i * tq:(i + 1) * tq, :] = x_ref[0, i * tq:(i + 1) * tq, :]

    def body(i, carry):
        tile(i, g_ref[0, 1], posl_ref[0], 0, nl)
        return carry

    lax.fori_loop(nct, T // tq, body, 0, unroll=2)


def _moe_combine(y, gate, pos_l, pos_c, xa, g2, L, cap_l):
    E, B, Ct, D = y.shape
    T = xa.shape[1]
    td = min(512, D)
    tq = _row_tile(L, T - L)
    return pl.pallas_call(
        functools.partial(_combine_kernel, tq=tq, nct=L // tq, cap_l=cap_l),
        grid=(B, D // td),
        in_specs=[pl.BlockSpec((E, 1, Ct, td), lambda b, j: (0, b, 0, j)),
                  pl.BlockSpec((1, E, Ct, 1), lambda b, j: (b, 0, 0, 0)),
                  pl.BlockSpec((1, 1, pos_l.shape[-1]), lambda b, j: (b, 0, 0)),
                  pl.BlockSpec((1, 1, pos_c.shape[-1]), lambda b, j: (b, 0, 0)),
                  pl.BlockSpec((1, T, td), lambda b, j: (b, 0, j)),
                  pl.BlockSpec((1, 2, 1, td), lambda b, j: (b, 0, 0, j))],
        out_specs=pl.BlockSpec((1, T, td), lambda b, j: (b, 0, j)),
        out_shape=jax.ShapeDtypeStruct((B, T, D), F32),
        scratch_shapes=[pltpu.VMEM((E * Ct, td), BF16)],
        compiler_params=_params("arbitrary", "arbitrary"),
        name="moe_combine",
    )(y, gate, pos_l, pos_c, xa, g2)


def _moe(xa, nw, ss2, g2, w_router, w1, w3, w2, layer, L, need_ctx):
    B, T, D = xa.shape
    S = T - L
    E = w_router.shape[1]
    h2, aff = _norm_router(xa, nw, ss2, w_router, L)
    cap_l = CAPACITY_FACTOR * S // E
    cap_c = CAPACITY_FACTOR * L // E if need_ctx else 0
    Ct = cap_l + cap_c
    xin, pos, gate = _route_gather(aff, h2, L, cap_l, cap_c)
    pos_l = pos[:, :, :cap_l, 0].reshape(B, 1, E * cap_l)
    if cap_c:
        pos_c = pos[:, :, cap_l:, 0].reshape(B, 1, E * cap_c)
    else:
        pos_c = jnp.zeros((B, 1, LANES), jnp.int32)
    y = _expert_ffn(xin.reshape(E, B * Ct, D), w1, w3, w2, layer).reshape(E, B, Ct, D)
    return _moe_combine(y, gate, pos_l, pos_c, xa, g2, L, cap_l)


_PERM_EO = np.concatenate([np.arange(0, QK_ROPE, 2), np.arange(1, QK_ROPE, 2)])
_PERM_OE = np.concatenate([np.arange(1, QK_ROPE, 2), np.arange(0, QK_ROPE, 2)])


def _ab_input_weight(w_in, a_cols):
    D = w_in.shape[0]
    zr = w_in[:, a_cols + Q_RANK + KV_RANK:]
    zero = jnp.zeros((D, LANES - 2 * QK_ROPE), w_in.dtype)
    g1 = jnp.concatenate([zero, zr[:, _PERM_EO], zr[:, _PERM_EO]], axis=1)
    g2 = jnp.concatenate([zero, zr[:, _PERM_OE], zr[:, _PERM_OE]], axis=1)
    return jnp.concatenate([w_in[:, :a_cols + Q_RANK + KV_RANK], g1, g2], axis=1).astype(BF16)


def _mla_weights(w_qup, w_kvup):
    NH = w_qup.shape[1] // (QK_NOPE + QK_ROPE)
    wq = w_qup.reshape(Q_RANK, NH, QK_NOPE + QK_ROPE)
    rope = wq[:, :, QK_NOPE:]
    wq = jnp.concatenate([wq[:, :, :QK_NOPE], rope[:, :, _PERM_EO], rope[:, :, _PERM_OE]], axis=-1)
    wq = wq.reshape(Q_RANK, NH // 2, 2 * LANES).transpose(1, 0, 2)
    wkv = w_kvup.reshape(KV_RANK, NH, QK_NOPE + V_HEAD)
    wk = jnp.concatenate([wkv[:, :, :QK_NOPE], jnp.zeros((KV_RANK, NH, LANES - QK_NOPE), w_kvup.dtype)], axis=-1)
    wk = wk.reshape(KV_RANK, NH // 2, 2 * LANES).transpose(1, 0, 2)
    wv = wkv[:, :, QK_NOPE:].reshape(KV_RANK, NH // 2, 2 * V_HEAD).transpose(1, 0, 2)
    return wq.astype(BF16), wk.astype(BF16), wv.astype(BF16)


def _rope_tables(L, S):
    t = np.arange(S)
    row = (t // GRID_W).astype(np.float32)
    col = (t % GRID_W).astype(np.float32)
    n_freq = QK_ROPE // 4
    inv = (ROPE_BASE ** (-np.arange(n_freq, dtype=np.float32) / n_freq)).astype(np.float32)
    ang = jnp.concatenate([jnp.asarray(row[:, None] * inv), jnp.asarray(col[:, None] * inv)], axis=-1)
    cos = jnp.concatenate([jnp.ones((L, QK_ROPE // 2), F32), jnp.cos(ang)], axis=0)
    sin = jnp.concatenate([jnp.zeros((L, QK_ROPE // 2), F32), jnp.sin(ang)], axis=0)
    T = L + S
    cc = jnp.concatenate([cos, cos], axis=1)
    ss = jnp.concatenate([-sin, sin], axis=1)
    one = jnp.ones((T, LANES - 2 * QK_ROPE), F32)
    zero = jnp.zeros((T, LANES - 2 * QK_ROPE), F32)
    cq = jnp.concatenate([one, cc, ss], axis=1)
    ck = jnp.concatenate([zero, cc, cc], axis=1)
    sk = jnp.concatenate([zero, ss, ss], axis=1)
    return cq, ck, sk


def kernel(x, c, ctx, c_ctx, mod_w, mod_b, norm1_w, norm2_w, final_norm_w, ab_w_in, ab_w_out, rk_mu, rk_w0, rk_w2, rk_a0, rk_a2, rk_g2, rk_kk, rk_ka, rk_rk, rk_ln_w, rk_ln_b, mla_qn_w, mla_w_qup, mla_kvn_w, mla_w_kvup, na_w_qkv, na_rpb, na_w_out, moe_router, moe_w1, moe_w3, moe_w2):
    B, S, D = x.shape
    L = ctx.shape[1]
    depth = mod_w.shape[0]
    A = rk_w0.shape[-1]
    a_cols = rk_mu.shape[-1]

    rows_pad = -(B + 1) % 8
    cvec = jnp.concatenate([c, c_ctx[None], jnp.zeros((rows_pad, D), F32)], axis=0)
    mods = _mod_vectors(cvec, mod_w, mod_b)
    m_lat = mods[:, :B].reshape(depth, B, 6, D)
    m_ctx = jnp.broadcast_to(mods[:, B].reshape(depth, 1, 6, D), (depth, B, 6, D))
    mm = jnp.stack([m_ctx, m_lat], axis=2)

    cq, ck, sk = _rope_tables(L, S)
    xa = jnp.concatenate([ctx, x], axis=1)

    for layer in range(depth):
        need_ctx = layer < depth - 1
        i = layer // 2
        m = mm[layer]
        ss1, g1 = m[:, :, 0:2], m[:, :, 2:3]
        ss2, g2 = m[:, :, 3:5], m[:, :, 5:6]
        if layer % 2 == 0:
            w_in = _ab_input_weight(ab_w_in[i], a_cols)
            z = _norm_linear(xa, norm1_w[layer], ss1, w_in, L)
            o_a = _rwkv_mixer(z, L, a_cols, rk_mu[i], rk_w0[i], rk_w2[i], rk_a0[i], rk_a2[i], rk_g2[i],
                              rk_kk[i], rk_ka[i], rk_rk[i], rk_ln_w[i], rk_ln_b[i])
            wq, wk, wv = _mla_weights(mla_w_qup[i], mla_w_kvup[i])
            o_b = _mla_attention(z, mla_qn_w[i], mla_kvn_w[i], wq, wk, wv, cq, ck, sk, L,
                                 a_cols, a_cols + Q_RANK, a_cols + Q_RANK + KV_RANK)
            xa = _linear_resid([o_a, o_b], ab_w_out[i].astype(BF16), xa, g1, L)
        else:
            qkv = _norm_linear(xa, norm1_w[layer], ss1, na_w_qkv[i].astype(BF16), L, out_dtype=BF16)
            o = _na_attention(qkv, _na_bias_table(na_rpb[i]), L, need_ctx)
            xa = _linear_resid([o], na_w_out[i].astype(BF16), xa, g1, L)
        xa = _moe(xa, norm2_w[layer], ss2, g2, moe_router[layer], moe_w1, moe_w3, moe_w2, layer, L, need_ctx)
    return _final_norm(xa, final_norm_w, L)
```

```python
import functools

import jax
import jax.numpy as jnp
import numpy as np
from jax import lax
from jax.experimental import pallas as pl
from jax.experimental.pallas import tpu as pltpu

F32 = jnp.float32
BF16 = jnp.bfloat16
HIGHEST = lax.Precision.HIGHEST

GRID_W = 64
NORM_EPS = 1e-6
NEG_INF = -1e30
GN_EPS = 64e-5
A_HEAD_DIM = 64
LORA_W = 64
LORA_A = 64
LORA_G = 128
QK_NOPE = 64
QK_ROPE = 32
V_HEAD = 64
Q_RANK = 384
KV_RANK = 256
ROPE_BASE = 10000.0
C_HEAD_DIM = 64
WIN_R = 8
WIN_C = 16
N_EXPERTS = 16
CAPACITY_FACTOR = 2
SCAN_CHUNK = 64
LANES = 128

VMEM_LIMIT = 56 * 1024 * 1024


def _params(*sem):
    return pltpu.CompilerParams(dimension_semantics=sem, vmem_limit_bytes=VMEM_LIMIT)


def _dot(a, b, precision=None):
    return jnp.dot(a, b, preferred_element_type=F32, precision=precision)


def _dot_nt(a, b, precision=None):
    return lax.dot_general(a, b, (((1,), (1,)), ((), ())), preferred_element_type=F32, precision=precision)


def _row_tile(L, S):
    tm = 256
    while L % tm or S % tm:
        tm //= 2
    return tm


def _mod_kernel(c_ref, w_ref, b_ref, o_ref):
    c = c_ref[...]
    sc = c * jax.nn.sigmoid(c)
    o_ref[0] = _dot(sc.astype(BF16), w_ref[0].astype(BF16)) + b_ref[0]


def _mod_vectors(cvec, mod_w, mod_b):
    depth, D, N = mod_w.shape
    R = cvec.shape[0]
    tn = 1024
    return pl.pallas_call(
        _mod_kernel,
        grid=(depth, N // tn),
        in_specs=[pl.BlockSpec((R, D), lambda l, j: (0, 0)),
                  pl.BlockSpec((1, D, tn), lambda l, j: (l, 0, j)),
                  pl.BlockSpec((1, 1, tn), lambda l, j: (l, 0, j))],
        out_specs=pl.BlockSpec((1, R, tn), lambda l, j: (l, 0, j)),
        out_shape=jax.ShapeDtypeStruct((depth, R, N), F32),
        compiler_params=_params("arbitrary", "arbitrary"),
        name="mod_vectors",
    )(cvec, mod_w, mod_b.reshape(depth, 1, N))


def _wide_tile(T):
    return max(m for m in range(8, 769, 8) if T % m == 0)


def _per_row(mod_ref, i, tm, L, k):
    row = lax.broadcasted_iota(jnp.int32, (tm, 1), 0) + i * tm
    return jnp.where(row < L, mod_ref[0, 0, k:k + 1, :], mod_ref[0, 1, k:k + 1, :])


def _norm_mod(x, nw, shift, scale):
    y = x * lax.rsqrt(jnp.mean(x * x, axis=-1, keepdims=True) + NORM_EPS)
    y = y * nw
    return y * (1.0 + scale) + shift


def _norm_linear_kernel(x_ref, nw_ref, ss_ref, w_ref, o_ref, *, L):
    i = pl.program_id(1)
    tm = x_ref.shape[1]
    h = _norm_mod(x_ref[0], nw_ref[...], _per_row(ss_ref, i, tm, L, 0), _per_row(ss_ref, i, tm, L, 1))
    o_ref[0] = _dot(h.astype(BF16), w_ref[...]).astype(o_ref.dtype)


def _norm_linear(xa, nw, ss, w, L, out_dtype=F32):
    B, T, D = xa.shape
    N = w.shape[1]
    tm = _wide_tile(T)
    return pl.pallas_call(
        functools.partial(_norm_linear_kernel, L=L),
        grid=(B, T // tm),
        in_specs=[pl.BlockSpec((1, tm, D), lambda b, i: (b, i, 0)),
                  pl.BlockSpec((1, D), lambda b, i: (0, 0)),
                  pl.BlockSpec((1, 2, 2, D), lambda b, i: (b, 0, 0, 0)),
                  pl.BlockSpec((D, N), lambda b, i: (0, 0))],
        out_specs=pl.BlockSpec((1, tm, N), lambda b, i: (b, i, 0)),
        out_shape=jax.ShapeDtypeStruct((B, T, N), out_dtype),
        compiler_params=_params("arbitrary", "arbitrary"),
        name="norm_linear",
    )(xa, nw.reshape(1, D), ss, w)


def _linear_resid_kernel(*refs, ks, L):
    n = len(ks)
    a_refs, (w_ref, x_ref, g_ref, o_ref) = refs[:n], refs[n:]
    acc = None
    off = 0
    for a_ref, k in zip(a_refs, ks):
        part = _dot(a_ref[0].astype(BF16), w_ref[off:off + k, :])
        acc = part if acc is None else acc + part
        off += k
    o_ref[0] = x_ref[0] + _per_row(g_ref, pl.program_id(1), x_ref.shape[1], L, 0) * acc


def _linear_resid(a_list, w, xa, gate, L):
    B, T, D = xa.shape
    tm = _wide_tile(T)
    ks = tuple(a.shape[-1] for a in a_list)
    in_specs = [pl.BlockSpec((1, tm, k), lambda b, i: (b, i, 0)) for k in ks]
    in_specs += [pl.BlockSpec(w.shape, lambda b, i: (0, 0)),
                 pl.BlockSpec((1, tm, D), lambda b, i: (b, i, 0)),
                 pl.BlockSpec((1, 2, 1, D), lambda b, i: (b, 0, 0, 0))]
    return pl.pallas_call(
        functools.partial(_linear_resid_kernel, ks=ks, L=L),
        grid=(B, T // tm),
        in_specs=in_specs,
        out_specs=pl.BlockSpec((1, tm, D), lambda b, i: (b, i, 0)),
        out_shape=jax.ShapeDtypeStruct((B, T, D), F32),
        compiler_params=_params("arbitrary", "arbitrary"),
        name="linear_resid",
    )(*a_list, w, xa, gate)


def _rms_kernel(x_ref, w_ref, o_ref):
    x = x_ref[0]
    o_ref[0] = x * lax.rsqrt(jnp.mean(x * x, axis=-1, keepdims=True) + NORM_EPS) * w_ref[...]


def _final_norm(xa, w, L):
    B, T, D = xa.shape
    S = T - L
    tm = _row_tile(L, S)
    nct = L // tm
    return pl.pallas_call(
        _rms_kernel,
        grid=(B, S // tm),
        in_specs=[pl.BlockSpec((1, tm, D), lambda b, i: (b, i + nct, 0)),
                  pl.BlockSpec((1, D), lambda b, i: (0, 0))],
        out_specs=pl.BlockSpec((1, tm, D), lambda b, i: (b, i, 0)),
        out_shape=jax.ShapeDtypeStruct((B, S, D), F32),
        compiler_params=_params("arbitrary", "arbitrary"),
        name="final_norm",
    )(xa, w.reshape(1, D))


def _rms(x, w):
    return x * lax.rsqrt(jnp.mean(x * x, axis=-1, keepdims=True) + NORM_EPS) * w


def _softmax_pv(chains):
    m = [functools.reduce(jnp.maximum, [jnp.max(s, axis=-1, keepdims=True) for s, _ in ch]) for ch in chains]
    p = [[jnp.exp(s - mi) for s, _ in ch] for ch, mi in zip(chains, m)]
    l = [functools.reduce(jnp.add, [jnp.sum(x, axis=-1, keepdims=True) for x in pc]) for pc in p]
    o = [functools.reduce(jnp.add, [_dot(x.astype(BF16), v) for x, (_, v) in zip(pc, ch)]) for pc, ch in zip(p, chains)]
    return [oi / li for oi, li in zip(o, l)]


def _mla_kernel(zq_ref, zkv_ref, zr_ref, qn_ref, kvn_ref, wq_ref, wk_ref, wv_ref, cq_ref, ck_ref, sk_ref,
                o_ref, q_s, k_s, v_s, *, L, tq, scale):
    T = zq_ref.shape[1]
    zqn = _rms(zq_ref[0], qn_ref[...]).astype(BF16)
    zkvn = _rms(zkv_ref[0], kvn_ref[...]).astype(BF16)
    qh = _dot(zqn, wq_ref[0])
    kn = _dot(zkvn, wk_ref[0])
    v_s[...] = _dot(zkvn, wv_ref[0]).astype(BF16)
    zr = zr_ref[0]
    kr = zr[:, :LANES] * ck_ref[...] + zr[:, LANES:] * sk_ref[...]
    cq = cq_ref[...] * scale
    for h in range(2):
        q_s[h] = (qh[:, h * LANES:(h + 1) * LANES] * cq).astype(BF16)
        k_s[h] = (kn[:, h * LANES:(h + 1) * LANES] + kr).astype(BF16)
    first_head = lax.broadcasted_iota(jnp.int32, (tq, LANES), 1) < V_HEAD

    def tile(row0, nk):
        s = [_dot_nt(q_s[h, pl.ds(row0, tq), :], k_s[h, 0:nk, :]) for h in range(2)]
        outs = _softmax_pv([[(si, v_s[0:nk, :])] for si in s])
        o_ref[0, pl.ds(row0, tq), :] = jnp.where(first_head, outs[0], outs[1])

    for i in range(L // tq):
        tile(i * tq, L)

    def body(i, carry):
        tile(pl.multiple_of(i * tq, tq), T)
        return carry

    lax.fori_loop(L // tq, T // tq, body, 0, unroll=4)


def _mla_attention(z, qn_w, kvn_w, wq, wk, wv, cq, ck, sk, L, col_q, col_kv, col_r):
    B, T, _ = z.shape
    HP = wq.shape[0]
    tq = _row_tile(L, T - L)
    scale = float((QK_NOPE + QK_ROPE) ** -0.5)
    return pl.pallas_call(
        functools.partial(_mla_kernel, L=L, tq=tq, scale=scale),
        grid=(B, HP),
        in_specs=[pl.BlockSpec((1, T, Q_RANK), lambda b, p: (b, 0, col_q // Q_RANK)),
                  pl.BlockSpec((1, T, KV_RANK), lambda b, p: (b, 0, col_kv // KV_RANK)),
                  pl.BlockSpec((1, T, 2 * LANES), lambda b, p: (b, 0, col_r // (2 * LANES))),
                  pl.BlockSpec((1, Q_RANK), lambda b, p: (0, 0)),
                  pl.BlockSpec((1, KV_RANK), lambda b, p: (0, 0)),
                  pl.BlockSpec((1, Q_RANK, 2 * LANES), lambda b, p: (p, 0, 0)),
                  pl.BlockSpec((1, KV_RANK, 2 * LANES), lambda b, p: (p, 0, 0)),
                  pl.BlockSpec((1, KV_RANK, LANES), lambda b, p: (p, 0, 0)),
                  pl.BlockSpec((T, LANES), lambda b, p: (0, 0)),
                  pl.BlockSpec((T, LANES), lambda b, p: (0, 0)),
                  pl.BlockSpec((T, LANES), lambda b, p: (0, 0))],
        out_specs=pl.BlockSpec((1, T, LANES), lambda b, p: (b, 0, p)),
        out_shape=jax.ShapeDtypeStruct((B, T, HP * LANES), F32),
        scratch_shapes=[pltpu.VMEM((2, T, LANES), BF16), pltpu.VMEM((2, T, LANES), BF16),
                        pltpu.VMEM((T, LANES), BF16)],
        compiler_params=_params("arbitrary", "arbitrary"),
        name="mla_attention",
    )(z, z, z, qn_w.reshape(1, -1), kvn_w.reshape(1, -1), wq, wk, wv, cq, ck, sk)


def _na_kernel(q_ref, k_ref, v_ref, bt_ref, o_ref, k_s, v_s, *, L, rows, kr, need_ctx, scale):
    W = GRID_W
    rpb = 4 if rows % 4 == 0 else 1
    k_s[...] = k_ref[0].astype(BF16)
    v_s[...] = v_ref[0].astype(BF16)
    nwin = kr * W
    lane = lax.broadcasted_iota(jnp.int32, (W, LANES), 1)
    head_mask = [(lane < C_HEAD_DIM).astype(F32), (lane >= C_HEAD_DIM).astype(F32)]
    first_head = lane < C_HEAD_DIM
    qcol = lax.broadcasted_iota(jnp.int32, (W, nwin), 0)
    kcol = lax.broadcasted_iota(jnp.int32, (W, nwin), 1) % W
    cstart = jnp.clip(qcol - WIN_C // 2, 0, W - WIN_C)
    col_valid = (kcol >= cstart) & (kcol < cstart + WIN_C)

    def row_block(rb, carry):
        q_blk = q_ref[0, pl.ds(pl.multiple_of(L + rb * (rpb * W), W), rpb * W), :] * scale
        s_ctx = [_dot_nt((q_blk * jnp.concatenate([head_mask[h]] * rpb, axis=0)).astype(BF16), k_s[0:L, :])
                 for h in range(2)]
        chains, q0s = [], []
        for j in range(rpb):
            r = rb * rpb + j
            rs = jnp.clip(r - kr // 2, 0, rows - kr)
            k0 = pl.multiple_of(L + rs * W, W)
            q0s.append(pl.multiple_of(L + r * W, W))
            q = q_blk[j * W:(j + 1) * W]
            kw = k_s[pl.ds(k0, nwin), :]
            vw = v_s[pl.ds(k0, nwin), :]
            dr0 = rs - r + (WIN_R - 1)
            for h in range(2):
                s_nb = _dot_nt((q * head_mask[h]).astype(BF16), kw)
                bias = jnp.concatenate([bt_ref[0, h, dr0 + 2 * m] for m in range(kr // 2)], axis=-1)
                s_nb = jnp.where(col_valid, s_nb + bias, NEG_INF)
                chains.append([(s_nb, vw), (s_ctx[h][j * W:(j + 1) * W], v_s[0:L, :])])
        outs = _softmax_pv(chains)
        for j in range(rpb):
            o_ref[0, pl.ds(q0s[j], W), :] = jnp.where(first_head, outs[2 * j], outs[2 * j + 1])
        return carry

    lax.fori_loop(0, rows // rpb, row_block, 0, unroll=4)

    tq = min(L, 256)
    lane_c = lax.broadcasted_iota(jnp.int32, (tq, LANES), 1)
    for i in range(L // tq):
        if need_ctx:
            q = q_ref[0, i * tq:(i + 1) * tq, :] * scale
            hm = [lane_c < C_HEAD_DIM, lane_c >= C_HEAD_DIM]
            s = [_dot_nt(jnp.where(hm[h], q, 0.0).astype(BF16), k_s[0:L, :]) for h in range(2)]
            outs = _softmax_pv([[(si, v_s[0:L, :])] for si in s])
            o_ref[0, i * tq:(i + 1) * tq, :] = jnp.where(lane_c < C_HEAD_DIM, outs[0], outs[1])
        else:
            o_ref[0, i * tq:(i + 1) * tq, :] = jnp.zeros((tq, LANES), F32)


def _na_attention(qkv, bias_tab, L, need_ctx):
    B, T, D3 = qkv.shape
    D = D3 // 3
    HP = D // LANES
    rows = (T - L) // GRID_W
    kr = min(WIN_R, rows)
    assert kr % 2 == 0
    nd = bias_tab.shape[2]
    return pl.pallas_call(
        functools.partial(_na_kernel, L=L, rows=rows, kr=kr, need_ctx=need_ctx, scale=float(C_HEAD_DIM ** -0.5)),
        grid=(B, HP),
        in_specs=[pl.BlockSpec((1, T, LANES), lambda b, p: (b, 0, p)),
                  pl.BlockSpec((1, T, LANES), lambda b, p: (b, 0, HP + p)),
                  pl.BlockSpec((1, T, LANES), lambda b, p: (b, 0, 2 * HP + p)),
                  pl.BlockSpec((1, 2, nd, GRID_W, LANES), lambda b, p: (p, 0, 0, 0, 0))],
        out_specs=pl.BlockSpec((1, T, LANES), lambda b, p: (b, 0, p)),
        out_shape=jax.ShapeDtypeStruct((B, T, D), F32),
        scratch_shapes=[pltpu.VMEM((T, LANES), BF16), pltpu.VMEM((T, LANES), BF16)],
        compiler_params=_params("arbitrary", "arbitrary"),
        name="na_attention",
    )(qkv, qkv, qkv, bias_tab)


def _na_bias_table(rpb):
    H = rpb.shape[0]
    qc = np.arange(GRID_W)[:, None]
    kc = np.arange(GRID_W)[None, :]
    dc = np.clip(kc - qc + (WIN_C - 1), 0, 2 * WIN_C - 2)
    pick = jnp.asarray(np.arange(2 * WIN_C - 1)[:, None, None] == dc[None], dtype=F32)
    t = jnp.einsum('hdc,cqk->hdqk', rpb, pick, precision=HIGHEST)
    t2 = jnp.concatenate([t[:, :-1], t[:, 1:]], axis=-1)
    return t2.reshape(H // 2, 2, 2 * WIN_R - 2, GRID_W, 2 * GRID_W)


def _bf(x):
    return x.astype(BF16)


def _seg_sum(x, ones_bd):
    hi = _bf(x)
    lo = _bf(x - hi.astype(F32))
    return _dot(hi, ones_bd) + _dot(lo, ones_bd)


def _head_ones(A):
    seg = np.arange(A) // A_HEAD_DIM
    return jnp.asarray(seg[:, None] == seg[None, :], dtype=BF16)


def _rwkv_prep_kernel(z_ref, zp_ref, zn_ref, mu_ref, w0_ref, w2_ref, a0_ref, a2_ref, g2_ref, kk_ref, ka_ref, rk_ref,
                      ones_ref, r_o, v_o, kkn_o, g_o, bonus_o, lw_o, beta_o, kd_o, *, nct, nt, A):
    i = pl.program_id(1)
    za = z_ref[0]
    tm = za.shape[0]
    row = lax.broadcasted_iota(jnp.int32, za.shape, 0)
    seg_first = (i == 0) | (i == nct)
    seg_last = (i == nct - 1) | (i == nt - 1)
    prev_row = jnp.where(seg_first, 0.0, zp_ref[0, 7:8, :])
    next_row = jnp.where(seg_last, 0.0, zn_ref[0, 0:1, :])
    prev = jnp.where(row == 0, prev_row, pltpu.roll(za, 1, 0))
    nxt = jnp.where(row == tm - 1, next_row, pltpu.roll(za, tm - 1, 0))
    zs = za + mu_ref[0:1, :] * (prev - za) + mu_ref[1:2, :] * (nxt - za)
    r = zs[:, 0:A]
    k = zs[:, A:2 * A]
    v = zs[:, 2 * A:3 * A]
    wd = _bf(jnp.tanh(zs[:, 3 * A:3 * A + LANES]))
    ad = _bf(zs[:, 3 * A + LANES:3 * A + 2 * LANES])
    gd = _bf(jax.nn.sigmoid(zs[:, 3 * A + 2 * LANES:3 * A + 3 * LANES]))
    ones = ones_ref[...]
    kk = k * kk_ref[...]
    kkn = kk / jnp.maximum(jnp.sqrt(_seg_sum(kk * kk, ones)), 1e-12)
    kd_sum = None
    for d in range(2):
        w_log = -jax.nn.softplus(-(w0_ref[d:d + 1, :] + _dot(wd, w2_ref[d]))) - 0.5
        lw_o[d, 0] = -jnp.exp(w_log)
        a = jax.nn.sigmoid(a0_ref[d:d + 1, :] + _dot(ad, a2_ref[d]))
        beta_o[d, 0] = kkn * a
        kd = k * (1.0 + (a - 1.0) * ka_ref[...])
        kd_o[d, 0] = kd
        kd_sum = kd if kd_sum is None else kd_sum + kd
    bonus_o[0] = _seg_sum(r * kd_sum * rk_ref[...], ones) * v
    r_o[0] = r
    v_o[0] = v
    kkn_o[0] = kkn
    g_o[0] = _dot(gd, g2_ref[...])


def _rwkv_prep(z, L, a_cols, mu, w0, w2, a0, a2, g2, k_k, k_a, r_k):
    B, T, _ = z.shape
    A = w0.shape[-1]
    assert 2 * LORA_W == LANES and 2 * LORA_A == LANES and LORA_G == LANES and a_cols == 3 * A + 3 * LANES
    tm = _row_tile(L, T - L)
    nt = T // tm
    hb = tm // 8

    def pad_lora(w):
        zero = jnp.zeros_like(w[0])
        return _bf(jnp.stack([jnp.concatenate([w[0], zero], 0), jnp.concatenate([zero, w[1]], 0)]))

    def const(shape):
        return pl.BlockSpec(shape, lambda b, i: (0,) * len(shape))

    tile = pl.BlockSpec((1, tm, A), lambda b, i: (b, i, 0))
    tile_d = pl.BlockSpec((2, 1, tm, A), lambda b, i: (0, b, i, 0))
    sd = jax.ShapeDtypeStruct((B, T, A), F32)
    sd_d = jax.ShapeDtypeStruct((2, B, T, A), F32)
    return pl.pallas_call(
        functools.partial(_rwkv_prep_kernel, nct=L // tm, nt=nt, A=A),
        grid=(B, nt),
        in_specs=[pl.BlockSpec((1, tm, a_cols), lambda b, i: (b, i, 0)),
                  pl.BlockSpec((1, 8, a_cols), lambda b, i: (b, jnp.maximum(i * hb - 1, 0), 0)),
                  pl.BlockSpec((1, 8, a_cols), lambda b, i: (b, jnp.minimum((i + 1) * hb, T // 8 - 1), 0)),
                  const((2, a_cols)), const((2, A)), const((2, 2 * LORA_W, A)), const((2, A)),
                  const((2, 2 * LORA_A, A)), const((LORA_G, A)), const((1, A)), const((1, A)), const((1, A)),
                  const((A, A))],
        out_specs=[tile, tile, tile, tile, tile, tile_d, tile_d, tile_d],
        out_shape=[sd, sd, sd, sd, sd, sd_d, sd_d, sd_d],
        compiler_params=_params("arbitrary", "arbitrary"),
        name="rwkv_prep",
    )(z, z, z, mu, w0, pad_lora(w2), a0, pad_lora(a2), _bf(g2), k_k.reshape(1, A), k_a.reshape(1, A),
      r_k.reshape(1, A), _head_ones(A))


def _tri_inverse(lms, eye, m16, m32):
    d0 = [_bf(jnp.where(m16, lm, 0.0)) for lm in lms]
    t = [eye + d.astype(F32) for d in d0]
    s = [_dot(d, d) for d in d0]
    for step in range(3):
        sb = [_bf(x) for x in s]
        t = [x + _dot(_bf(x), y) for x, y in zip(t, sb)]
        if step < 2:
            s = [_dot(y, y) for y in sb]
    for lvl in (m32 & (~m16), ~m32):
        tb = [_bf(x) for x in t]
        w = [_bf(_dot(_bf(jnp.where(lvl, lm, 0.0)), y)) for lm, y in zip(lms, tb)]
        t = [x + _dot(y, z) for x, y, z in zip(t, tb, w)]
    return t


def _wkv_kernel(*refs, NP, NB):
    C = SCAN_CHUNK
    P = 2 * C
    fwd_refs, bwd_refs, (yf_ref, yb_ref, h_s) = refs[0:6], refs[6:12], refs[12:]

    @pl.when(pl.program_id(1) == 0)
    def _():
        h_s[...] = jnp.zeros(h_s.shape, F32)

    ri = lax.broadcasted_iota(jnp.int32, (P, P), 0)
    ci = lax.broadcasted_iota(jnp.int32, (P, P), 1)
    same = (ri // C) == (ci // C)
    diff = (ri % C) - (ci % C)
    eye_b = ri == ci
    eye = eye_b.astype(F32)
    m16 = (ri // 16) == (ci // 16)
    m32 = (ri // 32) == (ci // 32)
    diff64 = lax.broadcasted_iota(jnp.int32, (C, C), 0) - lax.broadcasted_iota(jnp.int32, (C, C), 1)
    top = lax.broadcasted_iota(jnp.int32, (C, P), 1) < C
    zero_blk = jnp.zeros((P, P), BF16)
    sls = [slice(p * P, (p + 1) * P) for p in range(NP)]

    def bd(x):
        return [_bf(jnp.concatenate([jnp.where(top, x[:, sl], 0.0), jnp.where(top, 0.0, x[:, sl])], axis=0))
                for sl in sls]

    at, rt, bt, kt, bh, kh, vv, etots, before, before_eq = [], [], [], [], [], [], [], [], [], []
    for (r_ref, v_ref, kk_ref, lw_ref, beta_ref, kd_ref), sgn in ((fwd_refs, 1), (bwd_refs, -1)):
        order = diff * sgn
        tri = _bf(((diff64 * sgn) >= 0).astype(F32))
        for s in range(NB):
            before += [same & (order > 0)] * NP
            before_eq += [same & (order >= 0)] * NP
            lw = lw_ref[0, s]
            lw_hi = _bf(lw)
            lw_md = _bf(lw - lw_hi.astype(F32))
            lw_lo = _bf(lw - lw_hi.astype(F32) - lw_md.astype(F32))
            cum = _dot(tri, lw_hi) + _dot(tri, lw_md) + _dot(tri, lw_lo)
            tot = jnp.sum(lw, axis=0, keepdims=True)
            beta = beta_ref[0, s]
            kd = kd_ref[0, s]
            e_neg = jnp.exp(-cum)
            e_tail = jnp.exp(tot - cum)
            at += bd(-kk_ref[s] * jnp.exp(cum - lw))
            rt += bd(r_ref[s] * jnp.exp(cum))
            bt += bd(beta * e_neg)
            kt += bd(kd * e_neg)
            bh += bd(beta * e_tail)
            kh += bd(kd * e_tail)
            vv += bd(v_ref[s])
            etot = jnp.exp(tot)
            etots += [etot[:, sl] for sl in sls]

    ar = [jnp.concatenate([a, r], axis=0) for a, r in zip(at, rt)]
    arb = [_dot_nt(x, b) for x, b in zip(ar, bt)]
    ark = [_dot_nt(x, k) for x, k in zip(ar, kt)]
    lab = [jnp.where(m, x[:P], 0.0) for x, m in zip(arb, before)]
    tinv = _tri_inverse(lab, eye, m16, m32)
    u = [_dot(_bf(jnp.where(m, x[:P], 0.0)), v) for x, v, m in zip(ark, vv, before)]
    x = [_bf(_dot(_bf(t), jnp.concatenate([a, _bf(w)], axis=1))) for t, a, w in zip(tinv, at, u)]
    rhs = [jnp.concatenate([xi, jnp.concatenate([zero_blk, v], axis=1)], axis=0) for xi, v in zip(x, vv)]
    mn = [lax.dot_general(jnp.concatenate([b, k], axis=0), w, (((0,), (0,)), ((), ())), preferred_element_type=F32)
          for b, k, w in zip(bh, kh, rhs)]
    lr = [_bf(jnp.concatenate([jnp.where(m, xb[P:], 0.0), jnp.where(m, xk[P:], 0.0)], axis=1))
          for xb, xk, m in zip(arb, ark, before_eq)]
    qy = [_dot(l, w) for l, w in zip(lr, rhs)]
    qm = [_bf(jnp.concatenate([r.astype(F32) + q[:, :P], jnp.where(eye_b, e, 0.0) + m[:, :P]], axis=0))
          for r, q, m, e in zip(rt, qy, mn, etots)]
    nchain = 2 * NB * NP
    hin = [h_s[i] for i in range(nchain)]
    h_hi = [_bf(h) for h in hin]
    h_lo = [_bf(h - hh.astype(F32)) for h, hh in zip(hin, h_hi)]
    res = [_dot(w, hh) + _dot(w, hl) for w, hh, hl in zip(qm, h_hi, h_lo)]
    for i in range(nchain):
        ybd = res[i][:P] + qy[i][:, P:]
        y_ref = yf_ref if i < NB * NP else yb_ref
        y_ref[(i // NP) % NB, :, sls[i % NP]] = ybd[:C] + ybd[C:]
        h_s[i] = res[i][P:] + mn[i][:, P:]


def _wkv_scan(r, v, kk, lw, beta, kd, L):
    B, T, A = r.shape
    C = SCAN_CHUNK
    nC = T // C
    nct = L // C
    NP = A // (2 * C)

    def rev(c):
        return jnp.where(c < nct, nct - 1 - c, nC - 1 - (c - nct))

    NB = 2 if B % 2 == 0 else 1
    fwd = pl.BlockSpec((NB, C, A), lambda b, c: (b, c, 0))
    bwd = pl.BlockSpec((NB, C, A), lambda b, c: (b, rev(c), 0))
    fwd_d = pl.BlockSpec((1, NB, C, A), lambda b, c: (0, b, c, 0))
    bwd_d = pl.BlockSpec((1, NB, C, A), lambda b, c: (1, b, rev(c), 0))
    sd = jax.ShapeDtypeStruct((B, T, A), F32)
    return pl.pallas_call(
        functools.partial(_wkv_kernel, NP=NP, NB=NB),
        grid=(B // NB, nC),
        in_specs=[fwd, fwd, fwd, fwd_d, fwd_d, fwd_d, bwd, bwd, bwd, bwd_d, bwd_d, bwd_d],
        out_specs=[fwd, bwd],
        out_shape=[sd, sd],
        scratch_shapes=[pltpu.VMEM((2 * NB * NP, 2 * C, 2 * C), F32)],
        compiler_params=_params("arbitrary", "arbitrary"),
        name="wkv_scan",
    )(r, v, kk, lw, beta, kd, r, v, kk, lw, beta, kd)


def _rwkv_post_kernel(yf_ref, yb_ref, bonus_ref, g_ref, lnw_ref, lnb_ref, ones_ref, o_ref):
    y = yf_ref[0] + yb_ref[0]
    ones = ones_ref[...]
    inv_n = 1.0 / A_HEAD_DIM
    d = y - _seg_sum(y, ones) * inv_n
    var = _seg_sum(d * d, ones) * inv_n
    yn = d * lax.rsqrt(var + GN_EPS) * lnw_ref[...] + lnb_ref[...]
    o_ref[0] = (yn + bonus_ref[0]) * g_ref[0]


def _rwkv_post(yf, yb, bonus, g, ln_w, ln_b, L):
    B, T, A = bonus.shape
    tm = _row_tile(L, T - L)
    tile = pl.BlockSpec((1, tm, A), lambda b, i: (b, i, 0))
    vec = pl.BlockSpec((1, A), lambda b, i: (0, 0))
    return pl.pallas_call(
        _rwkv_post_kernel,
        grid=(B, T // tm),
        in_specs=[tile, tile, tile, tile, vec, vec, pl.BlockSpec((A, A), lambda b, i: (0, 0))],
        out_specs=tile,
        out_shape=jax.ShapeDtypeStruct((B, T, A), F32),
        compiler_params=_params("arbitrary", "arbitrary"),
        name="rwkv_post",
    )(yf, yb, bonus, g, ln_w.reshape(1, A), ln_b.reshape(1, A), _head_ones(A))


def _rwkv_mixer(z, L, a_cols, mu, w0, w2, a0, a2, g2, k_k, k_a, r_k, ln_w, ln_b):
    r, v, kk, g, bonus, lw, beta, kd = _rwkv_prep(z, L, a_cols, mu, w0, w2, a0, a2, g2, k_k, k_a, r_k)
    yf, yb = _wkv_scan(r, v, kk, lw, beta, kd, L)
    return _rwkv_post(yf, yb, bonus, g, ln_w, ln_b, L)


def _router_kernel(x_ref, nw_ref, ss_ref, wr_ref, h_ref, aff_ref, *, L):
    i = pl.program_id(1)
    tm = x_ref.shape[1]
    h = _norm_mod(x_ref[0], nw_ref[...], _per_row(ss_ref, i, tm, L, 0), _per_row(ss_ref, i, tm, L, 1))
    h_ref[0] = h.astype(BF16)
    logits = _dot_nt(wr_ref[...], h, HIGHEST)
    m = jnp.max(logits, axis=0, keepdims=True)
    p = jnp.exp(logits - m)
    aff_ref[0] = p / jnp.sum(p, axis=0, keepdims=True)


def _norm_router(xa, nw, ss, w_router, L):
    B, T, D = xa.shape
    E = w_router.shape[1]
    tm = max(m for m in range(LANES, 769, LANES) if T % m == 0)
    return pl.pallas_call(
        functools.partial(_router_kernel, L=L),
        grid=(B, T // tm),
        in_specs=[pl.BlockSpec((1, tm, D), lambda b, i: (b, i, 0)),
                  pl.BlockSpec((1, D), lambda b, i: (0, 0)),
                  pl.BlockSpec((1, 2, 2, D), lambda b, i: (b, 0, 0, 0)),
                  pl.BlockSpec((E, D), lambda b, i: (0, 0))],
        out_specs=[pl.BlockSpec((1, tm, D), lambda b, i: (b, i, 0)),
                   pl.BlockSpec((1, E, tm), lambda b, i: (b, 0, i))],
        out_shape=[jax.ShapeDtypeStruct((B, T, D), BF16), jax.ShapeDtypeStruct((B, E, T), F32)],
        compiler_params=_params("arbitrary", "arbitrary"),
        name="norm_router",
    )(xa, nw.reshape(1, D), ss, w_router.T)


def _select_top(sets):
    keys = [pltpu.bitcast(aff, jnp.int32) for aff, _, _ in sets]
    E = keys[0].shape[0]

    def bisect(_, carry):
        out = []
        for key, (_, cap, _), (lo, hi) in zip(keys, sets, carry):
            mid = lo + ((hi - lo + 1) >> 1)
            ok = jnp.sum(jnp.where(key >= mid, 1.0, 0.0), axis=1, keepdims=True) >= cap
            out.append((jnp.where(ok, mid, lo), jnp.where(ok, hi, mid - 1)))
        return tuple(out)

    init = tuple((jnp.zeros((E, 1), jnp.int32), jnp.full((E, 1), 0x7F800000, jnp.int32)) for _ in sets)
    bounds = lax.fori_loop(0, 32, bisect, init)
    res = []
    for key, (_, cap, tri), (thr, _) in zip(keys, sets, bounds):
        above = key > thr
        tie = key == thr
        need = cap - jnp.sum(jnp.where(above, 1.0, 0.0), axis=1, keepdims=True)
        tie_rank = _dot(jnp.where(tie, 1.0, 0.0).astype(BF16), tri)
        sel = above | (tie & (tie_rank < need))
        res.append(jnp.where(sel, _dot(jnp.where(sel, 1.0, 0.0).astype(BF16), tri), -1.0))
    return res


def _route_gather_kernel(aff_ref, h_ref, x_ref, pos_ref, gate_ref, slot_s, tri_s, *, L, cap_l, cap_c):
    b = pl.program_id(0)
    e = pl.program_id(1)
    T = h_ref.shape[1]
    S = T - L

    @pl.when((b == 0) & (e == 0))
    def _():
        n = tri_s.shape[0]
        tri_s[...] = jnp.where(lax.broadcasted_iota(jnp.int32, (n, n), 0) < lax.broadcasted_iota(jnp.int32, (n, n), 1),
                               1.0, 0.0).astype(BF16)

    @pl.when(e == 0)
    def _():
        sets = [(aff_ref[0, :, L:], cap_l, tri_s[0:S, 0:S])]
        if cap_c:
            sets.append((aff_ref[0, :, 0:L], cap_c, tri_s[0:L, 0:L]))
        picked = _select_top(sets)
        slot_s[:, L:] = picked[0]
        if cap_c:
            slot_s[:, 0:L] = picked[1]

    def gather(lo, n, cap, row0):
        slot = slot_s[pl.ds(e, 1), lo:lo + n]
        hit = slot == lax.broadcasted_iota(jnp.int32, (cap, n), 0).astype(F32)
        x_ref[0, 0, row0:row0 + cap, :] = _dot(jnp.where(hit, 1.0, 0.0).astype(BF16), h_ref[0, lo:lo + n, :]).astype(BF16)
        tok = lax.broadcasted_iota(jnp.int32, (cap, n), 1) + lo
        pos_ref[0, 0, row0:row0 + cap, :] = jnp.sum(jnp.where(hit, tok, 0), axis=1, keepdims=True)
        aff = aff_ref[0, pl.ds(e, 1), lo:lo + n]
        gate_ref[0, 0, row0:row0 + cap, :] = jnp.sum(jnp.where(hit, aff, 0.0), axis=1, keepdims=True)

    gather(L, S, cap_l, 0)
    if cap_c:
        gather(0, L, cap_c, cap_l)


def _route_gather(aff, h2, L, cap_l, cap_c):
    B, T, D = h2.shape
    E = aff.shape[1]
    Ct = cap_l + cap_c
    return pl.pallas_call(
        functools.partial(_route_gather_kernel, L=L, cap_l=cap_l, cap_c=cap_c),
        grid=(B, E),
        in_specs=[pl.BlockSpec((1, E, T), lambda b, e: (b, 0, 0)),
                  pl.BlockSpec((1, T, D), lambda b, e: (b, 0, 0))],
        out_specs=[pl.BlockSpec((1, 1, Ct, D), lambda b, e: (e, b, 0, 0)),
                   pl.BlockSpec((1, 1, Ct, 1), lambda b, e: (b, e, 0, 0)),
                   pl.BlockSpec((1, 1, Ct, 1), lambda b, e: (b, e, 0, 0))],
        out_shape=[jax.ShapeDtypeStruct((E, B, Ct, D), BF16), jax.ShapeDtypeStruct((B, E, Ct, 1), jnp.int32),
                   jax.ShapeDtypeStruct((B, E, Ct, 1), F32)],
        scratch_shapes=[pltpu.VMEM((E, T), F32), pltpu.VMEM((max(L, T - L),) * 2, BF16)],
        compiler_params=_params("arbitrary", "arbitrary"),
        name="moe_route_gather",
    )(aff, h2)


def _ffn_kernel(x_ref, w1_ref, w3_ref, w2_ref, o_ref, *, rm):
    j = pl.program_id(1)
    R = x_ref.shape[1]
    w1 = w1_ref[0, 0].astype(BF16)
    w3 = w3_ref[0, 0].astype(BF16)
    w2 = w2_ref[0, 0].astype(BF16)

    @pl.when(j == 0)
    def _():
        o_ref[...] = jnp.zeros(o_ref.shape, F32)

    def rows(i, carry):
        r0 = pl.multiple_of(i * rm, rm)
        x = x_ref[0, pl.ds(r0, rm), :]
        a = _dot(x, w1)
        b = _dot(x, w3)
        hid = (a * jax.nn.sigmoid(a) * b).astype(BF16)
        o_ref[0, pl.ds(r0, rm), :] += _dot(hid, w2)
        return carry

    lax.fori_loop(0, R // rm, rows, 0, unroll=True)


def _expert_ffn(xin, w1, w3, w2, layer):
    E, R, D = xin.shape
    F = w1.shape[-1]
    tf = min(512, F)
    rm = max(m for m in (768, 512, 256, 128, 64, 32, 16) if R % m == 0)
    return pl.pallas_call(
        functools.partial(_ffn_kernel, rm=rm),
        grid=(E, F // tf),
        in_specs=[pl.BlockSpec((1, R, D), lambda e, j: (e, 0, 0)),
                  pl.BlockSpec((1, 1, D, tf), lambda e, j: (layer, e, 0, j)),
                  pl.BlockSpec((1, 1, D, tf), lambda e, j: (layer, e, 0, j)),
                  pl.BlockSpec((1, 1, tf, D), lambda e, j: (layer, e, j, 0))],
        out_specs=pl.BlockSpec((1, R, D), lambda e, j: (e, 0, 0)),
        out_shape=jax.ShapeDtypeStruct((E, R, D), F32),
        compiler_params=_params("arbitrary", "arbitrary"),
        name="expert_ffn",
    )(xin, w1, w3, w2)


def _combine_kernel(y_ref, gate_ref, posl_ref, posc_ref, x_ref, g_ref, o_ref, yg_s, *, tq, nct, cap_l):
    E, _, Ct, td = y_ref.shape
    T = x_ref.shape[1]
    cap_c = Ct - cap_l
    nl = E * cap_l
    for e in range(E):
        yg_s[e * cap_l:(e + 1) * cap_l, :] = (y_ref[e, 0, :cap_l, :] * gate_ref[0, e, :cap_l, :]).astype(BF16)
        if cap_c:
            yg_s[nl + e * cap_c:nl + (e + 1) * cap_c, :] = (y_ref[e, 0, cap_l:, :] * gate_ref[0, e, cap_l:, :]).astype(BF16)

    def tile(i, gate_row, pos, lo, n):
        r0 = pl.multiple_of(i * tq, tq)
        tok = lax.broadcasted_iota(jnp.int32, (tq, n), 0) + r0
        onehot = jnp.where(tok == pos, 1.0, 0.0).astype(BF16)
        o_ref[0, pl.ds(r0, tq), :] = x_ref[0, pl.ds(r0, tq), :] + gate_row * _dot(onehot, yg_s[lo:lo + n, :])

    for i in range(nct):
        if cap_c:
            tile(i, g_ref[0, 0], posc_ref[0], nl, E * cap_c)
        else:
            o_ref[0, i * tq:(i + 1) * tq, :] = x_ref[0, i * tq:(i + 1) * tq, :]

    def body(i, carry):
        tile(i, g_ref[0, 1], posl_ref[0], 0, nl)
        return carry

    lax.fori_loop(nct, T // tq, body, 0, unroll=2)


def _moe_combine(y, gate, pos_l, pos_c, xa, g2, L, cap_l):
    E, B, Ct, D = y.shape
    T = xa.shape[1]
    td = min(512, D)
    tq = _row_tile(L, T - L)
    return pl.pallas_call(
        functools.partial(_combine_kernel, tq=tq, nct=L // tq, cap_l=cap_l),
        grid=(B, D // td),
        in_specs=[pl.BlockSpec((E, 1, Ct, td), lambda b, j: (0, b, 0, j)),
                  pl.BlockSpec((1, E, Ct, 1), lambda b, j: (b, 0, 0, 0)),
                  pl.BlockSpec((1, 1, pos_l.shape[-1]), lambda b, j: (b, 0, 0)),
                  pl.BlockSpec((1, 1, pos_c.shape[-1]), lambda b, j: (b, 0, 0)),
                  pl.BlockSpec((1, T, td), lambda b, j: (b, 0, j)),
                  pl.BlockSpec((1, 2, 1, td), lambda b, j: (b, 0, 0, j))],
        out_specs=pl.BlockSpec((1, T, td), lambda b, j: (b, 0, j)),
        out_shape=jax.ShapeDtypeStruct((B, T, D), F32),
        scratch_shapes=[pltpu.VMEM((E * Ct, td), BF16)],
        compiler_params=_params("arbitrary", "arbitrary"),
        name="moe_combine",
    )(y, gate, pos_l, pos_c, xa, g2)


def _moe(xa, nw, ss2, g2, w_router, w1, w3, w2, layer, L, need_ctx):
    B, T, D = xa.shape
    S = T - L
    E = w_router.shape[1]
    h2, aff = _norm_router(xa, nw, ss2, w_router, L)
    cap_l = CAPACITY_FACTOR * S // E
    cap_c = CAPACITY_FACTOR * L // E if need_ctx else 0
    Ct = cap_l + cap_c
    xin, pos, gate = _route_gather(aff, h2, L, cap_l, cap_c)
    pos_l = pos[:, :, :cap_l, 0].reshape(B, 1, E * cap_l)
    if cap_c:
        pos_c = pos[:, :, cap_l:, 0].reshape(B, 1, E * cap_c)
    else:
        pos_c = jnp.zeros((B, 1, LANES), jnp.int32)
    y = _expert_ffn(xin.reshape(E, B * Ct, D), w1, w3, w2, layer).reshape(E, B, Ct, D)
    return _moe_combine(y, gate, pos_l, pos_c, xa, g2, L, cap_l)


_PERM_EO = np.concatenate([np.arange(0, QK_ROPE, 2), np.arange(1, QK_ROPE, 2)])
_PERM_OE = np.concatenate([np.arange(1, QK_ROPE, 2), np.arange(0, QK_ROPE, 2)])


def _ab_input_weight(w_in, a_cols):
    D = w_in.shape[0]
    zr = w_in[:, a_cols + Q_RANK + KV_RANK:]
    zero = jnp.zeros((D, LANES - 2 * QK_ROPE), w_in.dtype)
    g1 = jnp.concatenate([zero, zr[:, _PERM_EO], zr[:, _PERM_EO]], axis=1)
    g2 = jnp.concatenate([zero, zr[:, _PERM_OE], zr[:, _PERM_OE]], axis=1)
    return jnp.concatenate([w_in[:, :a_cols + Q_RANK + KV_RANK], g1, g2], axis=1).astype(BF16)


def _mla_weights(w_qup, w_kvup):
    NH = w_qup.shape[1] // (QK_NOPE + QK_ROPE)
    wq = w_qup.reshape(Q_RANK, NH, QK_NOPE + QK_ROPE)
    rope = wq[:, :, QK_NOPE:]
    wq = jnp.concatenate([wq[:, :, :QK_NOPE], rope[:, :, _PERM_EO], rope[:, :, _PERM_OE]], axis=-1)
    wq = wq.reshape(Q_RANK, NH // 2, 2 * LANES).transpose(1, 0, 2)
    wkv = w_kvup.reshape(KV_RANK, NH, QK_NOPE + V_HEAD)
    wk = jnp.concatenate([wkv[:, :, :QK_NOPE], jnp.zeros((KV_RANK, NH, LANES - QK_NOPE), w_kvup.dtype)], axis=-1)
    wk = wk.reshape(KV_RANK, NH // 2, 2 * LANES).transpose(1, 0, 2)
    wv = wkv[:, :, QK_NOPE:].reshape(KV_RANK, NH // 2, 2 * V_HEAD).transpose(1, 0, 2)
    return wq.astype(BF16), wk.astype(BF16), wv.astype(BF16)


def _rope_tables(L, S):
    t = np.arange(S)
    row = (t // GRID_W).astype(np.float32)
    col = (t % GRID_W).astype(np.float32)
    n_freq = QK_ROPE // 4
    inv = (ROPE_BASE ** (-np.arange(n_freq, dtype=np.float32) / n_freq)).astype(np.float32)
    ang = jnp.concatenate([jnp.asarray(row[:, None] * inv), jnp.asarray(col[:, None] * inv)], axis=-1)
    cos = jnp.concatenate([jnp.ones((L, QK_ROPE // 2), F32), jnp.cos(ang)], axis=0)
    sin = jnp.concatenate([jnp.zeros((L, QK_ROPE // 2), F32), jnp.sin(ang)], axis=0)
    T = L + S
    cc = jnp.concatenate([cos, cos], axis=1)
    ss = jnp.concatenate([-sin, sin], axis=1)
    one = jnp.ones((T, LANES - 2 * QK_ROPE), F32)
    zero = jnp.zeros((T, LANES - 2 * QK_ROPE), F32)
    cq = jnp.concatenate([one, cc, ss], axis=1)
    ck = jnp.concatenate([zero, cc, cc], axis=1)
    sk = jnp.concatenate([zero, ss, ss], axis=1)
    return cq, ck, sk


def kernel(x, c, ctx, c_ctx, mod_w, mod_b, norm1_w, norm2_w, final_norm_w, ab_w_in, ab_w_out, rk_mu, rk_w0, rk_w2, rk_a0, rk_a2, rk_g2, rk_kk, rk_ka, rk_rk, rk_ln_w, rk_ln_b, mla_qn_w, mla_w_qup, mla_kvn_w, mla_w_kvup, na_w_qkv, na_rpb, na_w_out, moe_router, moe_w1, moe_w3, moe_w2):
    B, S, D = x.shape
    L = ctx.shape[1]
    depth = mod_w.shape[0]
    A = rk_w0.shape[-1]
    a_cols = rk_mu.shape[-1]

    rows_pad = -(B + 1) % 8
    cvec = jnp.concatenate([c, c_ctx[None], jnp.zeros((rows_pad, D), F32)], axis=0)
    mods = _mod_vectors(cvec, mod_w, mod_b)
    m_lat = mods[:, :B].reshape(depth, B, 6, D)
    m_ctx = jnp.broadcast_to(mods[:, B].reshape(depth, 1, 6, D), (depth, B, 6, D))
    mm = jnp.stack([m_ctx, m_lat], axis=2)

    cq, ck, sk = _rope_tables(L, S)
    xa = jnp.concatenate([ctx, x], axis=1)

    for layer in range(depth):
        need_ctx = layer < depth - 1
        i = layer // 2
        m = mm[layer]
        ss1, g1 = m[:, :, 0:2], m[:, :, 2:3]
        ss2, g2 = m[:, :, 3:5], m[:, :, 5:6]
        if layer % 2 == 0:
            w_in = _ab_input_weight(ab_w_in[i], a_cols)
            z = _norm_linear(xa, norm1_w[layer], ss1, w_in, L)
            o_a = _rwkv_mixer(z, L, a_cols, rk_mu[i], rk_w0[i], rk_w2[i], rk_a0[i], rk_a2[i], rk_g2[i],
                              rk_kk[i], rk_ka[i], rk_rk[i], rk_ln_w[i], rk_ln_b[i])
            wq, wk, wv = _mla_weights(mla_w_qup[i], mla_w_kvup[i])
            o_b = _mla_attention(z, mla_qn_w[i], mla_kvn_w[i], wq, wk, wv, cq, ck, sk, L,
                                 a_cols, a_cols + Q_RANK, a_cols + Q_RANK + KV_RANK)
            xa = _linear_resid([o_a, o_b], ab_w_out[i].astype(BF16), xa, g1, L)
        else:
            qkv = _norm_linear(xa, norm1_w[layer], ss1, na_w_qkv[i].astype(BF16), L, out_dtype=BF16)
            o = _na_attention(qkv, _na_bias_table(na_rpb[i]), L, need_ctx)
            xa = _linear_resid([o], na_w_out[i].astype(BF16), xa, g1, L)
        xa = _moe(xa, norm2_w[layer], ss2, g2, moe_router[layer], moe_w1, moe_w3, moe_w2, layer, L, need_ctx)
    return _final_norm(xa, final_norm_w, L)
```

```python
import functools

import jax
import jax.numpy as jnp
import numpy as np
from jax import lax
from jax.experimental import pallas as pl
from jax.experimental.pallas import tpu as pltpu

F32 = jnp.float32
BF16 = jnp.bfloat16
HIGHEST = lax.Precision.HIGHEST

GRID_W = 64
NORM_EPS = 1e-6
NEG_INF = -1e30
GN_EPS = 64e-5
A_HEAD_DIM = 64
LORA_W = 64
LORA_A = 64
LORA_G = 128
QK_NOPE = 64
QK_ROPE = 32
V_HEAD = 64
Q_RANK = 384
KV_RANK = 256
ROPE_BASE = 10000.0
C_HEAD_DIM = 64
WIN_R = 8
WIN_C = 16
N_EXPERTS = 16
CAPACITY_FACTOR = 2
SCAN_CHUNK = 64
LANES = 128
MAX_ROW_TILE = 768

VMEM_LIMIT = 56 * 1024 * 1024


def _params(*sem):
    return pltpu.CompilerParams(dimension_semantics=sem, vmem_limit_bytes=VMEM_LIMIT)


def _dot(a, b, precision=None):
    return jnp.dot(a, b, preferred_element_type=F32, precision=precision)


def _dot_nt(a, b, precision=None):
    return lax.dot_general(a, b, (((1,), (1,)), ((), ())), preferred_element_type=F32, precision=precision)


def _row_tile(L, S):
    tm = 256
    while L % tm or S % tm:
        tm //= 2
    return tm


def _mod_kernel(c_ref, w_ref, b_ref, o_ref):
    c = c_ref[...]
    sc = c * jax.nn.sigmoid(c)
    o_ref[0] = _dot(sc.astype(BF16), w_ref[0].astype(BF16)) + b_ref[0]


def _mod_vectors(cvec, mod_w, mod_b):
    depth, D, N = mod_w.shape
    R = cvec.shape[0]
    tn = 1024
    return pl.pallas_call(
        _mod_kernel,
        grid=(depth, N // tn),
        in_specs=[pl.BlockSpec((R, D), lambda l, j: (0, 0)),
                  pl.BlockSpec((1, D, tn), lambda l, j: (l, 0, j)),
                  pl.BlockSpec((1, 1, tn), lambda l, j: (l, 0, j))],
        out_specs=pl.BlockSpec((1, R, tn), lambda l, j: (l, 0, j)),
        out_shape=jax.ShapeDtypeStruct((depth, R, N), F32),
        compiler_params=_params("arbitrary", "arbitrary"),
        name="mod_vectors",
    )(cvec, mod_w, mod_b.reshape(depth, 1, N))


def _wide_tile(T):
    return max(m for m in range(8, MAX_ROW_TILE + 1, 8) if T % m == 0)


def _per_row(mod_ref, i, tm, L, k):
    row = lax.broadcasted_iota(jnp.int32, (tm, 1), 0) + i * tm
    return jnp.where(row < L, mod_ref[0, 0, k:k + 1, :], mod_ref[0, 1, k:k + 1, :])


def _norm_mod(x, nw, shift, scale):
    y = x * lax.rsqrt(jnp.mean(x * x, axis=-1, keepdims=True) + NORM_EPS)
    y = y * nw
    return y * (1.0 + scale) + shift


def _norm_linear_kernel(x_ref, nw_ref, ss_ref, w_ref, o_ref, *, L):
    i = pl.program_id(1)
    tm = x_ref.shape[1]
    h = _norm_mod(x_ref[0], nw_ref[...], _per_row(ss_ref, i, tm, L, 0), _per_row(ss_ref, i, tm, L, 1))
    o_ref[0] = _dot(h.astype(BF16), w_ref[...]).astype(o_ref.dtype)


def _norm_linear(xa, nw, ss, w, L, out_dtype=F32):
    B, T, D = xa.shape
    N = w.shape[1]
    tm = _wide_tile(T)
    return pl.pallas_call(
        functools.partial(_norm_linear_kernel, L=L),
        grid=(B, T // tm),
        in_specs=[pl.BlockSpec((1, tm, D), lambda b, i: (b, i, 0)),
                  pl.BlockSpec((1, D), lambda b, i: (0, 0)),
                  pl.BlockSpec((1, 2, 2, D), lambda b, i: (b, 0, 0, 0)),
                  pl.BlockSpec((D, N), lambda b, i: (0, 0))],
        out_specs=pl.BlockSpec((1, tm, N), lambda b, i: (b, i, 0)),
        out_shape=jax.ShapeDtypeStruct((B, T, N), out_dtype),
        compiler_params=_params("arbitrary", "arbitrary"),
        name="norm_linear",
    )(xa, nw.reshape(1, D), ss, w)


def _linear_resid_kernel(*refs, ks, L):
    n = len(ks)
    a_refs, (w_ref, x_ref, g_ref, o_ref) = refs[:n], refs[n:]
    acc = None
    off = 0
    for a_ref, k in zip(a_refs, ks):
        part = _dot(a_ref[0].astype(BF16), w_ref[off:off + k, :])
        acc = part if acc is None else acc + part
        off += k
    o_ref[0] = x_ref[0] + _per_row(g_ref, pl.program_id(1), x_ref.shape[1], L, 0) * acc


def _linear_resid(a_list, w, xa, gate, L):
    B, T, D = xa.shape
    tm = _wide_tile(T)
    ks = tuple(a.shape[-1] for a in a_list)
    in_specs = [pl.BlockSpec((1, tm, k), lambda b, i: (b, i, 0)) for k in ks]
    in_specs += [pl.BlockSpec(w.shape, lambda b, i: (0, 0)),
                 pl.BlockSpec((1, tm, D), lambda b, i: (b, i, 0)),
                 pl.BlockSpec((1, 2, 1, D), lambda b, i: (b, 0, 0, 0))]
    return pl.pallas_call(
        functools.partial(_linear_resid_kernel, ks=ks, L=L),
        grid=(B, T // tm),
        in_specs=in_specs,
        out_specs=pl.BlockSpec((1, tm, D), lambda b, i: (b, i, 0)),
        out_shape=jax.ShapeDtypeStruct((B, T, D), F32),
        compiler_params=_params("arbitrary", "arbitrary"),
        name="linear_resid",
    )(*a_list, w, xa, gate)


def _rms_kernel(x_ref, w_ref, o_ref):
    x = x_ref[0]
    o_ref[0] = x * lax.rsqrt(jnp.mean(x * x, axis=-1, keepdims=True) + NORM_EPS) * w_ref[...]


def _final_norm(xa, w, L):
    B, T, D = xa.shape
    S = T - L
    tm = _row_tile(L, S)
    nct = L // tm
    return pl.pallas_call(
        _rms_kernel,
        grid=(B, S // tm),
        in_specs=[pl.BlockSpec((1, tm, D), lambda b, i: (b, i + nct, 0)),
                  pl.BlockSpec((1, D), lambda b, i: (0, 0))],
        out_specs=pl.BlockSpec((1, tm, D), lambda b, i: (b, i, 0)),
        out_shape=jax.ShapeDtypeStruct((B, S, D), F32),
        compiler_params=_params("arbitrary", "arbitrary"),
        name="final_norm",
    )(xa, w.reshape(1, D))


def _rms(x, w):
    return x * lax.rsqrt(jnp.mean(x * x, axis=-1, keepdims=True) + NORM_EPS) * w


def _softmax_pv(chains):
    m = [functools.reduce(jnp.maximum, [jnp.max(s, axis=-1, keepdims=True) for s, _ in ch]) for ch in chains]
    p = [[jnp.exp(s - mi) for s, _ in ch] for ch, mi in zip(chains, m)]
    l = [functools.reduce(jnp.add, [jnp.sum(x, axis=-1, keepdims=True) for x in pc]) for pc in p]
    o = [functools.reduce(jnp.add, [_dot(x.astype(BF16), v) for x, (_, v) in zip(pc, ch)]) for pc, ch in zip(p, chains)]
    return [oi / li for oi, li in zip(o, l)]


def _mla_kernel(zq_ref, zkv_ref, zr_ref, qn_ref, kvn_ref, wq_ref, wk_ref, wv_ref, cq_ref, ck_ref, sk_ref,
                o_ref, q_s, k_s, v_s, *, L, tq, scale):
    T = zq_ref.shape[1]
    zqn = _rms(zq_ref[0], qn_ref[...]).astype(BF16)
    zkvn = _rms(zkv_ref[0], kvn_ref[...]).astype(BF16)
    qh = _dot(zqn, wq_ref[0])
    kn = _dot(zkvn, wk_ref[0])
    v_s[...] = _dot(zkvn, wv_ref[0]).astype(BF16)
    zr = zr_ref[0]
    kr = zr[:, :LANES] * ck_ref[...] + zr[:, LANES:] * sk_ref[...]
    cq = cq_ref[...] * scale
    for h in range(2):
        q_s[h] = (qh[:, h * LANES:(h + 1) * LANES] * cq).astype(BF16)
        k_s[h] = (kn[:, h * LANES:(h + 1) * LANES] + kr).astype(BF16)
    first_head = lax.broadcasted_iota(jnp.int32, (tq, LANES), 1) < V_HEAD

    def tile(row0, nk):
        s = [_dot_nt(q_s[h, pl.ds(row0, tq), :], k_s[h, 0:nk, :]) for h in range(2)]
        outs = _softmax_pv([[(si, v_s[0:nk, :])] for si in s])
        o_ref[0, pl.ds(row0, tq), :] = jnp.where(first_head, outs[0], outs[1]).astype(o_ref.dtype)

    for i in range(L // tq):
        tile(i * tq, L)

    def body(i, carry):
        tile(pl.multiple_of(i * tq, tq), T)
        return carry

    lax.fori_loop(L // tq, T // tq, body, 0, unroll=4)


def _mla_attention(z, qn_w, kvn_w, wq, wk, wv, cq, ck, sk, L, col_q, col_kv, col_r):
    B, T, _ = z.shape
    HP = wq.shape[0]
    tq = _row_tile(L, T - L)
    scale = float((QK_NOPE + QK_ROPE) ** -0.5)
    return pl.pallas_call(
        functools.partial(_mla_kernel, L=L, tq=tq, scale=scale),
        grid=(B, HP),
        in_specs=[pl.BlockSpec((1, T, Q_RANK), lambda b, p: (b, 0, col_q // Q_RANK)),
                  pl.BlockSpec((1, T, KV_RANK), lambda b, p: (b, 0, col_kv // KV_RANK)),
                  pl.BlockSpec((1, T, 2 * LANES), lambda b, p: (b, 0, col_r // (2 * LANES))),
                  pl.BlockSpec((1, Q_RANK), lambda b, p: (0, 0)),
                  pl.BlockSpec((1, KV_RANK), lambda b, p: (0, 0)),
                  pl.BlockSpec((1, Q_RANK, 2 * LANES), lambda b, p: (p, 0, 0)),
                  pl.BlockSpec((1, KV_RANK, 2 * LANES), lambda b, p: (p, 0, 0)),
                  pl.BlockSpec((1, KV_RANK, LANES), lambda b, p: (p, 0, 0)),
                  pl.BlockSpec((T, LANES), lambda b, p: (0, 0)),
                  pl.BlockSpec((T, LANES), lambda b, p: (0, 0)),
                  pl.BlockSpec((T, LANES), lambda b, p: (0, 0))],
        out_specs=pl.BlockSpec((1, T, LANES), lambda b, p: (b, 0, p)),
        out_shape=jax.ShapeDtypeStruct((B, T, HP * LANES), BF16),
        scratch_shapes=[pltpu.VMEM((2, T, LANES), BF16), pltpu.VMEM((2, T, LANES), BF16),
                        pltpu.VMEM((T, LANES), BF16)],
        compiler_params=_params("arbitrary", "arbitrary"),
        name="mla_attention",
    )(z, z, z, qn_w.reshape(1, -1), kvn_w.reshape(1, -1), wq, wk, wv, cq, ck, sk)


def _na_kernel(q_ref, k_ref, v_ref, bt_ref, o_ref, k_s, v_s, *, L, rows, kr, need_ctx, scale):
    W = GRID_W
    rpb = 4 if rows % 4 == 0 else 1
    k_s[...] = k_ref[0].astype(BF16)
    v_s[...] = v_ref[0].astype(BF16)
    nwin = kr * W
    lane = lax.broadcasted_iota(jnp.int32, (W, LANES), 1)
    head_mask = [(lane < C_HEAD_DIM).astype(F32), (lane >= C_HEAD_DIM).astype(F32)]
    first_head = lane < C_HEAD_DIM
    qcol = lax.broadcasted_iota(jnp.int32, (W, nwin), 0)
    kcol = lax.broadcasted_iota(jnp.int32, (W, nwin), 1) % W
    cstart = jnp.clip(qcol - WIN_C // 2, 0, W - WIN_C)
    col_valid = (kcol >= cstart) & (kcol < cstart + WIN_C)

    def row_block(rb, carry):
        q_blk = q_ref[0, pl.ds(pl.multiple_of(L + rb * (rpb * W), W), rpb * W), :] * scale
        s_ctx = [_dot_nt((q_blk * jnp.concatenate([head_mask[h]] * rpb, axis=0)).astype(BF16), k_s[0:L, :])
                 for h in range(2)]
        chains, q0s = [], []
        for j in range(rpb):
            r = rb * rpb + j
            rs = jnp.clip(r - kr // 2, 0, rows - kr)
            k0 = pl.multiple_of(L + rs * W, W)
            q0s.append(pl.multiple_of(L + r * W, W))
            q = q_blk[j * W:(j + 1) * W]
            kw = k_s[pl.ds(k0, nwin), :]
            vw = v_s[pl.ds(k0, nwin), :]
            dr0 = rs - r + (WIN_R - 1)
            for h in range(2):
                s_nb = _dot_nt((q * head_mask[h]).astype(BF16), kw)
                bias = jnp.concatenate([bt_ref[0, h, dr0 + 2 * m] for m in range(kr // 2)], axis=-1)
                s_nb = jnp.where(col_valid, s_nb + bias, NEG_INF)
                chains.append([(s_nb, vw), (s_ctx[h][j * W:(j + 1) * W], v_s[0:L, :])])
        outs = _softmax_pv(chains)
        for j in range(rpb):
            o_ref[0, pl.ds(q0s[j], W), :] = jnp.where(first_head, outs[2 * j], outs[2 * j + 1]).astype(o_ref.dtype)
        return carry

    lax.fori_loop(0, rows // rpb, row_block, 0, unroll=4)

    tq = min(L, 256)
    lane_c = lax.broadcasted_iota(jnp.int32, (tq, LANES), 1)
    for i in range(L // tq):
        if need_ctx:
            q = q_ref[0, i * tq:(i + 1) * tq, :] * scale
            hm = [lane_c < C_HEAD_DIM, lane_c >= C_HEAD_DIM]
            s = [_dot_nt(jnp.where(hm[h], q, 0.0).astype(BF16), k_s[0:L, :]) for h in range(2)]
            outs = _softmax_pv([[(si, v_s[0:L, :])] for si in s])
            o_ref[0, i * tq:(i + 1) * tq, :] = jnp.where(lane_c < C_HEAD_DIM, outs[0], outs[1]).astype(o_ref.dtype)
        else:
            o_ref[0, i * tq:(i + 1) * tq, :] = jnp.zeros((tq, LANES), o_ref.dtype)


def _na_attention(qkv, bias_tab, L, need_ctx):
    B, T, D3 = qkv.shape
    D = D3 // 3
    HP = D // LANES
    rows = (T - L) // GRID_W
    kr = min(WIN_R, rows)
    assert kr % 2 == 0
    nd = bias_tab.shape[2]
    return pl.pallas_call(
        functools.partial(_na_kernel, L=L, rows=rows, kr=kr, need_ctx=need_ctx, scale=float(C_HEAD_DIM ** -0.5)),
        grid=(B, HP),
        in_specs=[pl.BlockSpec((1, T, LANES), lambda b, p: (b, 0, p)),
                  pl.BlockSpec((1, T, LANES), lambda b, p: (b, 0, HP + p)),
                  pl.BlockSpec((1, T, LANES), lambda b, p: (b, 0, 2 * HP + p)),
                  pl.BlockSpec((1, 2, nd, GRID_W, LANES), lambda b, p: (p, 0, 0, 0, 0))],
        out_specs=pl.BlockSpec((1, T, LANES), lambda b, p: (b, 0, p)),
        out_shape=jax.ShapeDtypeStruct((B, T, D), BF16),
        scratch_shapes=[pltpu.VMEM((T, LANES), BF16), pltpu.VMEM((T, LANES), BF16)],
        compiler_params=_params("arbitrary", "arbitrary"),
        name="na_attention",
    )(qkv, qkv, qkv, bias_tab)


def _na_bias_table(rpb):
    H = rpb.shape[0]
    qc = np.arange(GRID_W)[:, None]
    kc = np.arange(GRID_W)[None, :]
    dc = np.clip(kc - qc + (WIN_C - 1), 0, 2 * WIN_C - 2)
    pick = jnp.asarray(np.arange(2 * WIN_C - 1)[:, None, None] == dc[None], dtype=F32)
    t = jnp.einsum('hdc,cqk->hdqk', rpb, pick, precision=HIGHEST)
    t2 = jnp.concatenate([t[:, :-1], t[:, 1:]], axis=-1)
    return t2.reshape(H // 2, 2, 2 * WIN_R - 2, GRID_W, 2 * GRID_W)


def _bf(x):
    return x.astype(BF16)


def _seg_sum(x, ones_bd):
    hi = _bf(x)
    lo = _bf(x - hi.astype(F32))
    return _dot(hi, ones_bd) + _dot(lo, ones_bd)


def _head_ones(A):
    seg = np.arange(A) // A_HEAD_DIM
    return jnp.asarray(seg[:, None] == seg[None, :], dtype=BF16)


def _rwkv_prep_kernel(z_ref, zp_ref, zn_ref, mu_ref, w0_ref, w2_ref, a0_ref, a2_ref, g2_ref, kk_ref, ka_ref, rk_ref,
                      ones_ref, r_o, v_o, kkn_o, g_o, bonus_o, lw_o, beta_o, kd_o, *, nct, nt, A):
    i = pl.program_id(1)
    za = z_ref[0]
    tm = za.shape[0]
    row = lax.broadcasted_iota(jnp.int32, za.shape, 0)
    seg_first = (i == 0) | (i == nct)
    seg_last = (i == nct - 1) | (i == nt - 1)
    prev_row = jnp.where(seg_first, 0.0, zp_ref[0, 7:8, :])
    next_row = jnp.where(seg_last, 0.0, zn_ref[0, 0:1, :])
    prev = jnp.where(row == 0, prev_row, pltpu.roll(za, 1, 0))
    nxt = jnp.where(row == tm - 1, next_row, pltpu.roll(za, tm - 1, 0))
    zs = za + mu_ref[0:1, :] * (prev - za) + mu_ref[1:2, :] * (nxt - za)
    r = zs[:, 0:A]
    k = zs[:, A:2 * A]
    v = zs[:, 2 * A:3 * A]
    wd = _bf(jnp.tanh(zs[:, 3 * A:3 * A + LANES]))
    ad = _bf(zs[:, 3 * A + LANES:3 * A + 2 * LANES])
    gd = _bf(jax.nn.sigmoid(zs[:, 3 * A + 2 * LANES:3 * A + 3 * LANES]))
    ones = ones_ref[...]
    kk = k * kk_ref[...]
    kkn = kk / jnp.maximum(jnp.sqrt(_seg_sum(kk * kk, ones)), 1e-12)
    kd_sum = None
    for d in range(2):
        w_log = -jax.nn.softplus(-(w0_ref[d:d + 1, :] + _dot(wd, w2_ref[d]))) - 0.5
        lw_o[d, 0] = -jnp.exp(w_log)
        a = jax.nn.sigmoid(a0_ref[d:d + 1, :] + _dot(ad, a2_ref[d]))
        beta_o[d, 0] = kkn * a
        kd = k * (1.0 + (a - 1.0) * ka_ref[...])
        kd_o[d, 0] = kd
        kd_sum = kd if kd_sum is None else kd_sum + kd
    bonus_o[0] = _seg_sum(r * kd_sum * rk_ref[...], ones) * v
    r_o[0] = r
    v_o[0] = v
    kkn_o[0] = kkn
    g_o[0] = _dot(gd, g2_ref[...])


def _rwkv_prep(z, L, a_cols, mu, w0, w2, a0, a2, g2, k_k, k_a, r_k):
    B, T, _ = z.shape
    A = w0.shape[-1]
    assert 2 * LORA_W == LANES and 2 * LORA_A == LANES and LORA_G == LANES and a_cols == 3 * A + 3 * LANES
    tm = _row_tile(L, T - L)
    nt = T // tm
    hb = tm // 8

    def pad_lora(w):
        zero = jnp.zeros_like(w[0])
        return _bf(jnp.stack([jnp.concatenate([w[0], zero], 0), jnp.concatenate([zero, w[1]], 0)]))

    def const(shape):
        return pl.BlockSpec(shape, lambda b, i: (0,) * len(shape))

    tile = pl.BlockSpec((1, tm, A), lambda b, i: (b, i, 0))
    tile_d = pl.BlockSpec((2, 1, tm, A), lambda b, i: (0, b, i, 0))
    sd = jax.ShapeDtypeStruct((B, T, A), F32)
    sd_d = jax.ShapeDtypeStruct((2, B, T, A), F32)
    return pl.pallas_call(
        functools.partial(_rwkv_prep_kernel, nct=L // tm, nt=nt, A=A),
        grid=(B, nt),
        in_specs=[pl.BlockSpec((1, tm, a_cols), lambda b, i: (b, i, 0)),
                  pl.BlockSpec((1, 8, a_cols), lambda b, i: (b, jnp.maximum(i * hb - 1, 0), 0)),
                  pl.BlockSpec((1, 8, a_cols), lambda b, i: (b, jnp.minimum((i + 1) * hb, T // 8 - 1), 0)),
                  const((2, a_cols)), const((2, A)), const((2, 2 * LORA_W, A)), const((2, A)),
                  const((2, 2 * LORA_A, A)), const((LORA_G, A)), const((1, A)), const((1, A)), const((1, A)),
                  const((A, A))],
        out_specs=[tile, tile, tile, tile, tile, tile_d, tile_d, tile_d],
        out_shape=[sd, sd, sd, sd, sd, sd_d, sd_d, sd_d],
        compiler_params=_params("arbitrary", "arbitrary"),
        name="rwkv_prep",
    )(z, z, z, mu, w0, pad_lora(w2), a0, pad_lora(a2), _bf(g2), k_k.reshape(1, A), k_a.reshape(1, A),
      r_k.reshape(1, A), _head_ones(A))


def _tri_inverse(lms, eye, m16, m32):
    d0 = [_bf(jnp.where(m16, lm, 0.0)) for lm in lms]
    t = [eye + d.astype(F32) for d in d0]
    s = [_dot(d, d) for d in d0]
    for step in range(3):
        sb = [_bf(x) for x in s]
        t = [x + _dot(_bf(x), y) for x, y in zip(t, sb)]
        if step < 2:
            s = [_dot(y, y) for y in sb]
    for lvl in (m32 & (~m16), ~m32):
        tb = [_bf(x) for x in t]
        w = [_bf(_dot(_bf(jnp.where(lvl, lm, 0.0)), y)) for lm, y in zip(lms, tb)]
        t = [x + _dot(y, z) for x, y, z in zip(t, tb, w)]
    return t


def _wkv_kernel(*refs, NP, NB):
    C = SCAN_CHUNK
    P = 2 * C
    fwd_refs, bwd_refs, (yf_ref, yb_ref, h_s) = refs[0:6], refs[6:12], refs[12:]

    @pl.when(pl.program_id(1) == 0)
    def _():
        h_s[...] = jnp.zeros(h_s.shape, F32)

    ri = lax.broadcasted_iota(jnp.int32, (P, P), 0)
    ci = lax.broadcasted_iota(jnp.int32, (P, P), 1)
    same = (ri // C) == (ci // C)
    diff = (ri % C) - (ci % C)
    eye_b = ri == ci
    eye = eye_b.astype(F32)
    m16 = (ri // 16) == (ci // 16)
    m32 = (ri // 32) == (ci // 32)
    diff64 = lax.broadcasted_iota(jnp.int32, (C, C), 0) - lax.broadcasted_iota(jnp.int32, (C, C), 1)
    top = lax.broadcasted_iota(jnp.int32, (C, P), 1) < C
    zero_blk = jnp.zeros((P, P), BF16)
    sls = [slice(p * P, (p + 1) * P) for p in range(NP)]

    def bd(x):
        return [_bf(jnp.concatenate([jnp.where(top, x[:, sl], 0.0), jnp.where(top, 0.0, x[:, sl])], axis=0))
                for sl in sls]

    at, rt, bt, kt, bh, kh, vv, etots, before, before_eq = [], [], [], [], [], [], [], [], [], []
    for (r_ref, v_ref, kk_ref, lw_ref, beta_ref, kd_ref), sgn in ((fwd_refs, 1), (bwd_refs, -1)):
        order = diff * sgn
        tri = _bf(((diff64 * sgn) >= 0).astype(F32))
        for s in range(NB):
            before += [same & (order > 0)] * NP
            before_eq += [same & (order >= 0)] * NP
            lw = lw_ref[0, s]
            lw_hi = _bf(lw)
            lw_md = _bf(lw - lw_hi.astype(F32))
            lw_lo = _bf(lw - lw_hi.astype(F32) - lw_md.astype(F32))
            cum = _dot(tri, lw_hi) + _dot(tri, lw_md) + _dot(tri, lw_lo)
            tot = jnp.sum(lw, axis=0, keepdims=True)
            beta = beta_ref[0, s]
            kd = kd_ref[0, s]
            e_neg = jnp.exp(-cum)
            e_tail = jnp.exp(tot - cum)
            at += bd(-kk_ref[s] * jnp.exp(cum - lw))
            rt += bd(r_ref[s] * jnp.exp(cum))
            bt += bd(beta * e_neg)
            kt += bd(kd * e_neg)
            bh += bd(beta * e_tail)
            kh += bd(kd * e_tail)
            vv += bd(v_ref[s])
            etot = jnp.exp(tot)
            etots += [etot[:, sl] for sl in sls]

    ar = [jnp.concatenate([a, r], axis=0) for a, r in zip(at, rt)]
    arb = [_dot_nt(x, b) for x, b in zip(ar, bt)]
    ark = [_dot_nt(x, k) for x, k in zip(ar, kt)]
    lab = [jnp.where(m, x[:P], 0.0) for x, m in zip(arb, before)]
    tinv = _tri_inverse(lab, eye, m16, m32)
    u = [_dot(_bf(jnp.where(m, x[:P], 0.0)), v) for x, v, m in zip(ark, vv, before)]
    x = [_bf(_dot(_bf(t), jnp.concatenate([a, _bf(w)], axis=1))) for t, a, w in zip(tinv, at, u)]
    rhs = [jnp.concatenate([xi, jnp.concatenate([zero_blk, v], axis=1)], axis=0) for xi, v in zip(x, vv)]
    mn = [lax.dot_general(jnp.concatenate([b, k], axis=0), w, (((0,), (0,)), ((), ())), preferred_element_type=F32)
          for b, k, w in zip(bh, kh, rhs)]
    lr = [_bf(jnp.concatenate([jnp.where(m, xb[P:], 0.0), jnp.where(m, xk[P:], 0.0)], axis=1))
          for xb, xk, m in zip(arb, ark, before_eq)]
    qy = [_dot(l, w) for l, w in zip(lr, rhs)]
    qm = [_bf(jnp.concatenate([r.astype(F32) + q[:, :P], jnp.where(eye_b, e, 0.0) + m[:, :P]], axis=0))
          for r, q, m, e in zip(rt, qy, mn, etots)]
    nchain = 2 * NB * NP
    hin = [h_s[i] for i in range(nchain)]
    h_hi = [_bf(h) for h in hin]
    h_lo = [_bf(h - hh.astype(F32)) for h, hh in zip(hin, h_hi)]
    res = [_dot(w, hh) + _dot(w, hl) for w, hh, hl in zip(qm, h_hi, h_lo)]
    for i in range(nchain):
        ybd = res[i][:P] + qy[i][:, P:]
        y_ref = yf_ref if i < NB * NP else yb_ref
        y_ref[(i // NP) % NB, :, sls[i % NP]] = ybd[:C] + ybd[C:]
        h_s[i] = res[i][P:] + mn[i][:, P:]


def _wkv_scan(r, v, kk, lw, beta, kd, L):
    B, T, A = r.shape
    C = SCAN_CHUNK
    nC = T // C
    nct = L // C
    NP = A // (2 * C)

    def rev(c):
        return jnp.where(c < nct, nct - 1 - c, nC - 1 - (c - nct))

    NB = 2 if B % 2 == 0 else 1
    fwd = pl.BlockSpec((NB, C, A), lambda b, c: (b, c, 0))
    bwd = pl.BlockSpec((NB, C, A), lambda b, c: (b, rev(c), 0))
    fwd_d = pl.BlockSpec((1, NB, C, A), lambda b, c: (0, b, c, 0))
    bwd_d = pl.BlockSpec((1, NB, C, A), lambda b, c: (1, b, rev(c), 0))
    sd = jax.ShapeDtypeStruct((B, T, A), F32)
    return pl.pallas_call(
        functools.partial(_wkv_kernel, NP=NP, NB=NB),
        grid=(B // NB, nC),
        in_specs=[fwd, fwd, fwd, fwd_d, fwd_d, fwd_d, bwd, bwd, bwd, bwd_d, bwd_d, bwd_d],
        out_specs=[fwd, bwd],
        out_shape=[sd, sd],
        scratch_shapes=[pltpu.VMEM((2 * NB * NP, 2 * C, 2 * C), F32)],
        compiler_params=_params("arbitrary", "arbitrary"),
        name="wkv_scan",
    )(r, v, kk, lw, beta, kd, r, v, kk, lw, beta, kd)


def _rwkv_post_kernel(yf_ref, yb_ref, bonus_ref, g_ref, lnw_ref, lnb_ref, ones_ref, o_ref):
    y = yf_ref[0] + yb_ref[0]
    ones = ones_ref[...]
    inv_n = 1.0 / A_HEAD_DIM
    d = y - _seg_sum(y, ones) * inv_n
    var = _seg_sum(d * d, ones) * inv_n
    yn = d * lax.rsqrt(var + GN_EPS) * lnw_ref[...] + lnb_ref[...]
    o_ref[0] = ((yn + bonus_ref[0]) * g_ref[0]).astype(o_ref.dtype)


def _rwkv_post(yf, yb, bonus, g, ln_w, ln_b, L):
    B, T, A = bonus.shape
    tm = _row_tile(L, T - L)
    tile = pl.BlockSpec((1, tm, A), lambda b, i: (b, i, 0))
    vec = pl.BlockSpec((1, A), lambda b, i: (0, 0))
    return pl.pallas_call(
        _rwkv_post_kernel,
        grid=(B, T // tm),
        in_specs=[tile, tile, tile, tile, vec, vec, pl.BlockSpec((A, A), lambda b, i: (0, 0))],
        out_specs=tile,
        out_shape=jax.ShapeDtypeStruct((B, T, A), BF16),
        compiler_params=_params("arbitrary", "arbitrary"),
        name="rwkv_post",
    )(yf, yb, bonus, g, ln_w.reshape(1, A), ln_b.reshape(1, A), _head_ones(A))


def _rwkv_mixer(z, L, a_cols, mu, w0, w2, a0, a2, g2, k_k, k_a, r_k, ln_w, ln_b):
    r, v, kk, g, bonus, lw, beta, kd = _rwkv_prep(z, L, a_cols, mu, w0, w2, a0, a2, g2, k_k, k_a, r_k)
    yf, yb = _wkv_scan(r, v, kk, lw, beta, kd, L)
    return _rwkv_post(yf, yb, bonus, g, ln_w, ln_b, L)


def _router_kernel(x_ref, nw_ref, ss_ref, wr_ref, h_ref, aff_ref, *, L):
    i = pl.program_id(1)
    tm = x_ref.shape[1]
    h = _norm_mod(x_ref[0], nw_ref[...], _per_row(ss_ref, i, tm, L, 0), _per_row(ss_ref, i, tm, L, 1))
    h_ref[0] = h.astype(BF16)
    logits = _dot_nt(wr_ref[...], h, HIGHEST)
    m = jnp.max(logits, axis=0, keepdims=True)
    p = jnp.exp(logits - m)
    aff_ref[0] = p / jnp.sum(p, axis=0, keepdims=True)


def _norm_router(xa, nw, ss, w_router, L):
    B, T, D = xa.shape
    E = w_router.shape[1]
    tm = max(m for m in range(LANES, MAX_ROW_TILE + 1, LANES) if T % m == 0)
    return pl.pallas_call(
        functools.partial(_router_kernel, L=L),
        grid=(B, T // tm),
        in_specs=[pl.BlockSpec((1, tm, D), lambda b, i: (b, i, 0)),
                  pl.BlockSpec((1, D), lambda b, i: (0, 0)),
                  pl.BlockSpec((1, 2, 2, D), lambda b, i: (b, 0, 0, 0)),
                  pl.BlockSpec((E, D), lambda b, i: (0, 0))],
        out_specs=[pl.BlockSpec((1, tm, D), lambda b, i: (b, i, 0)),
                   pl.BlockSpec((1, E, tm), lambda b, i: (b, 0, i))],
        out_shape=[jax.ShapeDtypeStruct((B, T, D), BF16), jax.ShapeDtypeStruct((B, E, T), F32)],
        compiler_params=_params("arbitrary", "arbitrary"),
        name="norm_router",
    )(xa, nw.reshape(1, D), ss, w_router.T)


def _select_top(sets):
    keys = [pltpu.bitcast(aff, jnp.int32) for aff, _, _ in sets]
    E = keys[0].shape[0]

    def bisect(_, carry):
        out = []
        for key, (_, cap, _), (lo, hi) in zip(keys, sets, carry):
            mid = lo + ((hi - lo + 1) >> 1)
            ok = jnp.sum(jnp.where(key >= mid, 1.0, 0.0), axis=1, keepdims=True) >= cap
            out.append((jnp.where(ok, mid, lo), jnp.where(ok, hi, mid - 1)))
        return tuple(out)

    init = tuple((jnp.zeros((E, 1), jnp.int32), jnp.full((E, 1), 0x7F800000, jnp.int32)) for _ in sets)
    bounds = lax.fori_loop(0, 32, bisect, init)
    res = []
    for key, (_, cap, tri), (thr, _) in zip(keys, sets, bounds):
        above = key > thr
        tie = key == thr
        need = cap - jnp.sum(jnp.where(above, 1.0, 0.0), axis=1, keepdims=True)
        tie_rank = _dot(jnp.where(tie, 1.0, 0.0).astype(BF16), tri)
        sel = above | (tie & (tie_rank < need))
        res.append(jnp.where(sel, _dot(jnp.where(sel, 1.0, 0.0).astype(BF16), tri), -1.0))
    return res


def _route_gather_kernel(aff_ref, h_ref, x_ref, pos_ref, gate_ref, slot_s, tri_s, *, L, cap_l, cap_c):
    b = pl.program_id(0)
    e = pl.program_id(1)
    T = h_ref.shape[1]
    S = T - L

    @pl.when((b == 0) & (e == 0))
    def _():
        n = tri_s.shape[0]
        tri_s[...] = jnp.where(lax.broadcasted_iota(jnp.int32, (n, n), 0) < lax.broadcasted_iota(jnp.int32, (n, n), 1),
                               1.0, 0.0).astype(BF16)

    @pl.when(e == 0)
    def _():
        sets = [(aff_ref[0, :, L:], cap_l, tri_s[0:S, 0:S])]
        if cap_c:
            sets.append((aff_ref[0, :, 0:L], cap_c, tri_s[0:L, 0:L]))
        picked = _select_top(sets)
        slot_s[:, L:] = picked[0]
        if cap_c:
            slot_s[:, 0:L] = picked[1]

    def gather(lo, n, cap, row0):
        slot = slot_s[pl.ds(e, 1), lo:lo + n]
        hit = slot == lax.broadcasted_iota(jnp.int32, (cap, n), 0).astype(F32)
        x_ref[0, 0, row0:row0 + cap, :] = _dot(jnp.where(hit, 1.0, 0.0).astype(BF16), h_ref[0, lo:lo + n, :]).astype(BF16)
        tok = lax.broadcasted_iota(jnp.int32, (cap, n), 1) + lo
        pos_ref[0, 0, row0:row0 + cap, :] = jnp.sum(jnp.where(hit, tok, 0), axis=1, keepdims=True)
        aff = aff_ref[0, pl.ds(e, 1), lo:lo + n]
        gate_ref[0, 0, row0:row0 + cap, :] = jnp.sum(jnp.where(hit, aff, 0.0), axis=1, keepdims=True)

    gather(L, S, cap_l, 0)
    if cap_c:
        gather(0, L, cap_c, cap_l)


def _route_gather(aff, h2, L, cap_l, cap_c):
    B, T, D = h2.shape
    E = aff.shape[1]
    Ct = cap_l + cap_c
    return pl.pallas_call(
        functools.partial(_route_gather_kernel, L=L, cap_l=cap_l, cap_c=cap_c),
        grid=(B, E),
        in_specs=[pl.BlockSpec((1, E, T), lambda b, e: (b, 0, 0)),
                  pl.BlockSpec((1, T, D), lambda b, e: (b, 0, 0))],
        out_specs=[pl.BlockSpec((1, 1, Ct, D), lambda b, e: (e, b, 0, 0)),
                   pl.BlockSpec((1, 1, Ct, 1), lambda b, e: (b, e, 0, 0)),
                   pl.BlockSpec((1, 1, Ct, 1), lambda b, e: (b, e, 0, 0))],
        out_shape=[jax.ShapeDtypeStruct((E, B, Ct, D), BF16), jax.ShapeDtypeStruct((B, E, Ct, 1), jnp.int32),
                   jax.ShapeDtypeStruct((B, E, Ct, 1), F32)],
        scratch_shapes=[pltpu.VMEM((E, T), F32), pltpu.VMEM((max(L, T - L),) * 2, BF16)],
        compiler_params=_params("arbitrary", "arbitrary"),
        name="moe_route_gather",
    )(aff, h2)


def _ffn_kernel(x_ref, w1_ref, w3_ref, w2_ref, o_ref, *, rm):
    j = pl.program_id(1)
    R = x_ref.shape[1]
    w1 = w1_ref[0, 0].astype(BF16)
    w3 = w3_ref[0, 0].astype(BF16)
    w2 = w2_ref[0, 0].astype(BF16)

    @pl.when(j == 0)
    def _():
        o_ref[...] = jnp.zeros(o_ref.shape, F32)

    def rows(i, carry):
        r0 = pl.multiple_of(i * rm, rm)
        x = x_ref[0, pl.ds(r0, rm), :]
        a = _dot(x, w1)
        b = _dot(x, w3)
        hid = (a * jax.nn.sigmoid(a) * b).astype(BF16)
        o_ref[0, pl.ds(r0, rm), :] += _dot(hid, w2)
        return carry

    lax.fori_loop(0, R // rm, rows, 0, unroll=True)


def _expert_ffn(xin, w1, w3, w2, layer):
    E, R, D = xin.shape
    F = w1.shape[-1]
    tf = min(512, F)
    rm = max(m for m in (MAX_ROW_TILE, 512, 256, 128, 64, 32, 16) if R % m == 0)
    return pl.pallas_call(
        functools.partial(_ffn_kernel, rm=rm),
        grid=(E, F // tf),
        in_specs=[pl.BlockSpec((1, R, D), lambda e, j: (e, 0, 0)),
                  pl.BlockSpec((1, 1, D, tf), lambda e, j: (layer, e, 0, j)),
                  pl.BlockSpec((1, 1, D, tf), lambda e, j: (layer, e, 0, j)),
                  pl.BlockSpec((1, 1, tf, D), lambda e, j: (layer, e, j, 0))],
        out_specs=pl.BlockSpec((1, R, D), lambda e, j: (e, 0, 0)),
        out_shape=jax.ShapeDtypeStruct((E, R, D), F32),
        compiler_params=_params("arbitrary", "arbitrary"),
        name="expert_ffn",
    )(xin, w1, w3, w2)


def _combine_kernel(y_ref, gate_ref, posl_ref, posc_ref, x_ref, g_ref, o_ref, yg_s, *, tq, nct, cap_l):
    E, _, Ct, td = y_ref.shape
    T = x_ref.shape[1]
    cap_c = Ct - cap_l
    nl = E * cap_l
    for e in range(E):
        yg_s[e * cap_l:(e + 1) * cap_l, :] = (y_ref[e, 0, :cap_l, :] * gate_ref[0, e, :cap_l, :]).astype(BF16)
        if cap_c:
            yg_s[nl + e * cap_c:nl + (e + 1) * cap_c, :] = (y_ref[e, 0, cap_l:, :] * gate_ref[0, e, cap_l:, :]).astype(BF16)

    def tile(i, gate_row, pos, lo, n):
        r0 = pl.multiple_of(i * tq, tq)
        tok = lax.broadcasted_iota(jnp.int32, (tq, n), 0) + r0
        onehot = jnp.where(tok == pos, 1.0, 0.0).astype(BF16)
        o_ref[0, pl.ds(r0, tq), :] = x_ref[0, pl.ds(r0, tq), :] + gate_row * _dot(onehot, yg_s[lo:lo + n, :])

    for i in range(nct):
        if cap_c:
            tile(i, g_ref[0, 0], posc_ref[0], nl, E * cap_c)
        else:
            o_ref[0, i * tq:(i + 1) * tq, :] = x_ref[0, i * tq:(i + 1) * tq, :]

    def body(i, carry):
        tile(i, g_ref[0, 1], posl_ref[0], 0, nl)
        return carry

    lax.fori_loop(nct, T // tq, body, 0, unroll=2)


def _moe_combine(y, gate, pos_l, pos_c, xa, g2, L, cap_l):
    E, B, Ct, D = y.shape
    T = xa.shape[1]
    td = min(512, D)
    tq = _row_tile(L, T - L)
    return pl.pallas_call(
        functools.partial(_combine_kernel, tq=tq, nct=L // tq, cap_l=cap_l),
        grid=(B, D // td),
        in_specs=[pl.BlockSpec((E, 1, Ct, td), lambda b, j: (0, b, 0, j)),
                  pl.BlockSpec((1, E, Ct, 1), lambda b, j: (b, 0, 0, 0)),
                  pl.BlockSpec((1, 1, pos_l.shape[-1]), lambda b, j: (b, 0, 0)),
                  pl.BlockSpec((1, 1, pos_c.shape[-1]), lambda b, j: (b, 0, 0)),
                  pl.BlockSpec((1, T, td), lambda b, j: (b, 0, j)),
                  pl.BlockSpec((1, 2, 1, td), lambda b, j: (b, 0, 0, j))],
        out_specs=pl.BlockSpec((1, T, td), lambda b, j: (b, 0, j)),
        out_shape=jax.ShapeDtypeStruct((B, T, D), F32),
        scratch_shapes=[pltpu.VMEM((E * Ct, td), BF16)],
        compiler_params=_params("arbitrary", "arbitrary"),
        name="moe_combine",
    )(y, gate, pos_l, pos_c, xa, g2)


def _moe(xa, nw, ss2, g2, w_router, w1, w3, w2, layer, L, need_ctx):
    B, T, D = xa.shape
    S = T - L
    E = w_router.shape[1]
    h2, aff = _norm_router(xa, nw, ss2, w_router, L)
    cap_l = CAPACITY_FACTOR * S // E
    cap_c = CAPACITY_FACTOR * L // E if need_ctx else 0
    Ct = cap_l + cap_c
    xin, pos, gate = _route_gather(aff, h2, L, cap_l, cap_c)
    pos_l = pos[:, :, :cap_l, 0].reshape(B, 1, E * cap_l)
    if cap_c:
        pos_c = pos[:, :, cap_l:, 0].reshape(B, 1, E * cap_c)
    else:
        pos_c = jnp.zeros((B, 1, LANES), jnp.int32)
    y = _expert_ffn(xin.reshape(E, B * Ct, D), w1, w3, w2, layer).reshape(E, B, Ct, D)
    return _moe_combine(y, gate, pos_l, pos_c, xa, g2, L, cap_l)


_PERM_EO = np.concatenate([np.arange(0, QK_ROPE, 2), np.arange(1, QK_ROPE, 2)])
_PERM_OE = np.concatenate([np.arange(1, QK_ROPE, 2), np.arange(0, QK_ROPE, 2)])


def _ab_input_weight(w_in, a_cols):
    D = w_in.shape[0]
    zr = w_in[:, a_cols + Q_RANK + KV_RANK:]
    zero = jnp.zeros((D, LANES - 2 * QK_ROPE), w_in.dtype)
    g1 = jnp.concatenate([zero, zr[:, _PERM_EO], zr[:, _PERM_EO]], axis=1)
    g2 = jnp.concatenate([zero, zr[:, _PERM_OE], zr[:, _PERM_OE]], axis=1)
    return jnp.concatenate([w_in[:, :a_cols + Q_RANK + KV_RANK], g1, g2], axis=1).astype(BF16)


def _mla_weights(w_qup, w_kvup):
    NH = w_qup.shape[1] // (QK_NOPE + QK_ROPE)
    wq = w_qup.reshape(Q_RANK, NH, QK_NOPE + QK_ROPE)
    rope = wq[:, :, QK_NOPE:]
    wq = jnp.concatenate([wq[:, :, :QK_NOPE], rope[:, :, _PERM_EO], rope[:, :, _PERM_OE]], axis=-1)
    wq = wq.reshape(Q_RANK, NH // 2, 2 * LANES).transpose(1, 0, 2)
    wkv = w_kvup.reshape(KV_RANK, NH, QK_NOPE + V_HEAD)
    wk = jnp.concatenate([wkv[:, :, :QK_NOPE], jnp.zeros((KV_RANK, NH, LANES - QK_NOPE), w_kvup.dtype)], axis=-1)
    wk = wk.reshape(KV_RANK, NH // 2, 2 * LANES).transpose(1, 0, 2)
    wv = wkv[:, :, QK_NOPE:].reshape(KV_RANK, NH // 2, 2 * V_HEAD).transpose(1, 0, 2)
    return wq.astype(BF16), wk.astype(BF16), wv.astype(BF16)


def _rope_tables(L, S):
    t = np.arange(S)
    row = (t // GRID_W).astype(np.float32)
    col = (t % GRID_W).astype(np.float32)
    n_freq = QK_ROPE // 4
    inv = (ROPE_BASE ** (-np.arange(n_freq, dtype=np.float32) / n_freq)).astype(np.float32)
    ang = jnp.concatenate([jnp.asarray(row[:, None] * inv), jnp.asarray(col[:, None] * inv)], axis=-1)
    cos = jnp.concatenate([jnp.ones((L, QK_ROPE // 2), F32), jnp.cos(ang)], axis=0)
    sin = jnp.concatenate([jnp.zeros((L, QK_ROPE // 2), F32), jnp.sin(ang)], axis=0)
    T = L + S
    cc = jnp.concatenate([cos, cos], axis=1)
    ss = jnp.concatenate([-sin, sin], axis=1)
    one = jnp.ones((T, LANES - 2 * QK_ROPE), F32)
    zero = jnp.zeros((T, LANES - 2 * QK_ROPE), F32)
    cq = jnp.concatenate([one, cc, ss], axis=1)
    ck = jnp.concatenate([zero, cc, cc], axis=1)
    sk = jnp.concatenate([zero, ss, ss], axis=1)
    return cq, ck, sk


def kernel(x, c, ctx, c_ctx, mod_w, mod_b, norm1_w, norm2_w, final_norm_w, ab_w_in, ab_w_out, rk_mu, rk_w0, rk_w2, rk_a0, rk_a2, rk_g2, rk_kk, rk_ka, rk_rk, rk_ln_w, rk_ln_b, mla_qn_w, mla_w_qup, mla_kvn_w, mla_w_kvup, na_w_qkv, na_rpb, na_w_out, moe_router, moe_w1, moe_w3, moe_w2):
    B, S, D = x.shape
    L = ctx.shape[1]
    depth = mod_w.shape[0]
    A = rk_w0.shape[-1]
    a_cols = rk_mu.shape[-1]

    rows_pad = -(B + 1) % 8
    cvec = jnp.concatenate([c, c_ctx[None], jnp.zeros((rows_pad, D), F32)], axis=0)
    mods = _mod_vectors(cvec, mod_w, mod_b)
    m_lat = mods[:, :B].reshape(depth, B, 6, D)
    m_ctx = jnp.broadcast_to(mods[:, B].reshape(depth, 1, 6, D), (depth, B, 6, D))
    mm = jnp.stack([m_ctx, m_lat], axis=2)

    cq, ck, sk = _rope_tables(L, S)
    xa = jnp.concatenate([ctx, x], axis=1)

    for layer in range(depth):
        need_ctx = layer < depth - 1
        i = layer // 2
        m = mm[layer]
        ss1, g1 = m[:, :, 0:2], m[:, :, 2:3]
        ss2, g2 = m[:, :, 3:5], m[:, :, 5:6]
        if layer % 2 == 0:
            w_in = _ab_input_weight(ab_w_in[i], a_cols)
            z = _norm_linear(xa, norm1_w[layer], ss1, w_in, L)
            o_a = _rwkv_mixer(z, L, a_cols, rk_mu[i], rk_w0[i], rk_w2[i], rk_a0[i], rk_a2[i], rk_g2[i],
                              rk_kk[i], rk_ka[i], rk_rk[i], rk_ln_w[i], rk_ln_b[i])
            wq, wk, wv = _mla_weights(mla_w_qup[i], mla_w_kvup[i])
            o_b = _mla_attention(z, mla_qn_w[i], mla_kvn_w[i], wq, wk, wv, cq, ck, sk, L,
                                 a_cols, a_cols + Q_RANK, a_cols + Q_RANK + KV_RANK)
            mixed, w_out = [o_a, o_b], ab_w_out[i]
        else:
            qkv = _norm_linear(xa, norm1_w[layer], ss1, na_w_qkv[i].astype(BF16), L, out_dtype=BF16)
            mixed, w_out = [_na_attention(qkv, _na_bias_table(na_rpb[i]), L, need_ctx)], na_w_out[i]
        xa = _linear_resid(mixed, w_out.astype(BF16), xa, g1, L)
        xa = _moe(xa, norm2_w[layer], ss2, g2, moe_router[layer], moe_w1, moe_w3, moe_w2, layer, L, need_ctx)
    return _final_norm(xa, final_norm_w, L)
```

```python
import functools

import jax
import jax.numpy as jnp
import numpy as np
from jax import lax
from jax.experimental import pallas as pl
from jax.experimental.pallas import tpu as pltpu

F32 = jnp.float32
BF16 = jnp.bfloat16
HIGHEST = lax.Precision.HIGHEST

GRID_W = 64
NORM_EPS = 1e-6
NEG_INF = -1e30
GN_EPS = 64e-5
A_HEAD_DIM = 64
LORA_W = 64
LORA_A = 64
LORA_G = 128
QK_NOPE = 64
QK_ROPE = 32
V_HEAD = 64
Q_RANK = 384
KV_RANK = 256
ROPE_BASE = 10000.0
C_HEAD_DIM = 64
WIN_R = 8
WIN_C = 16
N_EXPERTS = 16
CAPACITY_FACTOR = 2
SCAN_CHUNK = 64
LANES = 128
MAX_ROW_TILE = 768

VMEM_LIMIT = 56 * 1024 * 1024


def _params(*sem):
    return pltpu.CompilerParams(dimension_semantics=sem, vmem_limit_bytes=VMEM_LIMIT)


def _dot(a, b, precision=None):
    return jnp.dot(a, b, preferred_element_type=F32, precision=precision)


def _dot_nt(a, b, precision=None):
    return lax.dot_general(a, b, (((1,), (1,)), ((), ())), preferred_element_type=F32, precision=precision)


def _row_tile(L, S):
    tm = 256
    while L % tm or S % tm:
        tm //= 2
    return tm


def _mod_kernel(c_ref, w_ref, b_ref, o_ref):
    c = c_ref[...]
    sc = c * jax.nn.sigmoid(c)
    o_ref[0] = _dot(sc.astype(BF16), w_ref[0].astype(BF16)) + b_ref[0]


def _mod_vectors(cvec, mod_w, mod_b):
    depth, D, N = mod_w.shape
    R = cvec.shape[0]
    tn = 1024
    return pl.pallas_call(
        _mod_kernel,
        grid=(depth, N // tn),
        in_specs=[pl.BlockSpec((R, D), lambda l, j: (0, 0)),
                  pl.BlockSpec((1, D, tn), lambda l, j: (l, 0, j)),
                  pl.BlockSpec((1, 1, tn), lambda l, j: (l, 0, j))],
        out_specs=pl.BlockSpec((1, R, tn), lambda l, j: (l, 0, j)),
        out_shape=jax.ShapeDtypeStruct((depth, R, N), F32),
        compiler_params=_params("arbitrary", "arbitrary"),
        name="mod_vectors",
    )(cvec, mod_w, mod_b.reshape(depth, 1, N))


def _wide_tile(T):
    return max(m for m in range(8, MAX_ROW_TILE + 1, 8) if T % m == 0)


def _per_row(mod_ref, i, tm, L, k):
    row = lax.broadcasted_iota(jnp.int32, (tm, 1), 0) + i * tm
    return jnp.where(row < L, mod_ref[0, 0, k:k + 1, :], mod_ref[0, 1, k:k + 1, :])


def _norm_mod(x, nw, shift, scale):
    y = x * lax.rsqrt(jnp.mean(x * x, axis=-1, keepdims=True) + NORM_EPS)
    y = y * nw
    return y * (1.0 + scale) + shift


def _norm_linear_kernel(x_ref, nw_ref, ss_ref, w_ref, o_ref, *, L):
    i = pl.program_id(1)
    tm = x_ref.shape[1]
    h = _norm_mod(x_ref[0], nw_ref[...], _per_row(ss_ref, i, tm, L, 0), _per_row(ss_ref, i, tm, L, 1))
    o_ref[0] = _dot(h.astype(BF16), w_ref[...]).astype(o_ref.dtype)


def _norm_linear(xa, nw, ss, w, L, out_dtype=F32):
    B, T, D = xa.shape
    N = w.shape[1]
    tm = _wide_tile(T)
    return pl.pallas_call(
        functools.partial(_norm_linear_kernel, L=L),
        grid=(B, T // tm),
        in_specs=[pl.BlockSpec((1, tm, D), lambda b, i: (b, i, 0)),
                  pl.BlockSpec((1, D), lambda b, i: (0, 0)),
                  pl.BlockSpec((1, 2, 2, D), lambda b, i: (b, 0, 0, 0)),
                  pl.BlockSpec((D, N), lambda b, i: (0, 0))],
        out_specs=pl.BlockSpec((1, tm, N), lambda b, i: (b, i, 0)),
        out_shape=jax.ShapeDtypeStruct((B, T, N), out_dtype),
        compiler_params=_params("arbitrary", "arbitrary"),
        name="norm_linear",
    )(xa, nw.reshape(1, D), ss, w)


def _linear_resid_kernel(*refs, ks, L):
    n = len(ks)
    a_refs, (w_ref, x_ref, g_ref, o_ref) = refs[:n], refs[n:]
    acc = None
    off = 0
    for a_ref, k in zip(a_refs, ks):
        part = _dot(a_ref[0].astype(BF16), w_ref[off:off + k, :])
        acc = part if acc is None else acc + part
        off += k
    o_ref[0] = x_ref[0] + _per_row(g_ref, pl.program_id(1), x_ref.shape[1], L, 0) * acc


def _linear_resid(a_list, w, xa, gate, L):
    B, T, D = xa.shape
    tm = _wide_tile(T)
    ks = tuple(a.shape[-1] for a in a_list)
    in_specs = [pl.BlockSpec((1, tm, k), lambda b, i: (b, i, 0)) for k in ks]
    in_specs += [pl.BlockSpec(w.shape, lambda b, i: (0, 0)),
                 pl.BlockSpec((1, tm, D), lambda b, i: (b, i, 0)),
                 pl.BlockSpec((1, 2, 1, D), lambda b, i: (b, 0, 0, 0))]
    return pl.pallas_call(
        functools.partial(_linear_resid_kernel, ks=ks, L=L),
        grid=(B, T // tm),
        in_specs=in_specs,
        out_specs=pl.BlockSpec((1, tm, D), lambda b, i: (b, i, 0)),
        out_shape=jax.ShapeDtypeStruct((B, T, D), F32),
        compiler_params=_params("arbitrary", "arbitrary"),
        name="linear_resid",
    )(*a_list, w, xa, gate)


def _rms_kernel(x_ref, w_ref, o_ref):
    x = x_ref[0]
    o_ref[0] = x * lax.rsqrt(jnp.mean(x * x, axis=-1, keepdims=True) + NORM_EPS) * w_ref[...]


def _final_norm(xa, w, L):
    B, T, D = xa.shape
    S = T - L
    tm = _row_tile(L, S)
    nct = L // tm
    return pl.pallas_call(
        _rms_kernel,
        grid=(B, S // tm),
        in_specs=[pl.BlockSpec((1, tm, D), lambda b, i: (b, i + nct, 0)),
                  pl.BlockSpec((1, D), lambda b, i: (0, 0))],
        out_specs=pl.BlockSpec((1, tm, D), lambda b, i: (b, i, 0)),
        out_shape=jax.ShapeDtypeStruct((B, S, D), F32),
        compiler_params=_params("arbitrary", "arbitrary"),
        name="final_norm",
    )(xa, w.reshape(1, D))


def _rms(x, w):
    return x * lax.rsqrt(jnp.mean(x * x, axis=-1, keepdims=True) + NORM_EPS) * w


def _softmax_pv(chains):
    m = [functools.reduce(jnp.maximum, [jnp.max(s, axis=-1, keepdims=True) for s, _ in ch]) for ch in chains]
    p = [[jnp.exp(s - mi) for s, _ in ch] for ch, mi in zip(chains, m)]
    l = [functools.reduce(jnp.add, [jnp.sum(x, axis=-1, keepdims=True) for x in pc]) for pc in p]
    o = [functools.reduce(jnp.add, [_dot(x.astype(BF16), v) for x, (_, v) in zip(pc, ch)]) for pc, ch in zip(p, chains)]
    return [oi / li for oi, li in zip(o, l)]


def _mla_kernel(zq_ref, zkv_ref, zr_ref, qn_ref, kvn_ref, wq_ref, wk_ref, wv_ref, cq_ref, ck_ref, sk_ref,
                o_ref, q_s, k_s, v_s, *, L, tq, scale):
    T = zq_ref.shape[1]
    zqn = _rms(zq_ref[0], qn_ref[...]).astype(BF16)
    zkvn = _rms(zkv_ref[0], kvn_ref[...]).astype(BF16)
    qh = _dot(zqn, wq_ref[0])
    kn = _dot(zkvn, wk_ref[0])
    v_s[...] = _dot(zkvn, wv_ref[0]).astype(BF16)
    zr = zr_ref[0]
    kr = zr[:, :LANES] * ck_ref[...] + zr[:, LANES:] * sk_ref[...]
    cq = cq_ref[...] * scale
    for h in range(2):
        q_s[h] = (qh[:, h * LANES:(h + 1) * LANES] * cq).astype(BF16)
        k_s[h] = (kn[:, h * LANES:(h + 1) * LANES] + kr).astype(BF16)
    def tile(row0, nk, rows):
        first_head = lax.broadcasted_iota(jnp.int32, (rows, LANES), 1) < V_HEAD
        s = [_dot_nt(q_s[h, pl.ds(row0, rows), :], k_s[h, 0:nk, :]) for h in range(2)]
        outs = _softmax_pv([[(si, v_s[0:nk, :])] for si in s])
        o_ref[0, pl.ds(row0, rows), :] = jnp.where(first_head, outs[0], outs[1]).astype(o_ref.dtype)

    for i in range(L // tq):
        tile(i * tq, L, tq)
    tl = 2 * tq if (T - L) % (2 * tq) == 0 else tq
    for i in range((T - L) // tl):
        tile(L + i * tl, T, tl)


def _mla_attention(z, qn_w, kvn_w, wq, wk, wv, cq, ck, sk, L, col_q, col_kv, col_r):
    B, T, _ = z.shape
    HP = wq.shape[0]
    tq = _row_tile(L, T - L)
    scale = float((QK_NOPE + QK_ROPE) ** -0.5)
    return pl.pallas_call(
        functools.partial(_mla_kernel, L=L, tq=tq, scale=scale),
        grid=(B, HP),
        in_specs=[pl.BlockSpec((1, T, Q_RANK), lambda b, p: (b, 0, col_q // Q_RANK)),
                  pl.BlockSpec((1, T, KV_RANK), lambda b, p: (b, 0, col_kv // KV_RANK)),
                  pl.BlockSpec((1, T, 2 * LANES), lambda b, p: (b, 0, col_r // (2 * LANES))),
                  pl.BlockSpec((1, Q_RANK), lambda b, p: (0, 0)),
                  pl.BlockSpec((1, KV_RANK), lambda b, p: (0, 0)),
                  pl.BlockSpec((1, Q_RANK, 2 * LANES), lambda b, p: (p, 0, 0)),
                  pl.BlockSpec((1, KV_RANK, 2 * LANES), lambda b, p: (p, 0, 0)),
                  pl.BlockSpec((1, KV_RANK, LANES), lambda b, p: (p, 0, 0)),
                  pl.BlockSpec((T, LANES), lambda b, p: (0, 0)),
                  pl.BlockSpec((T, LANES), lambda b, p: (0, 0)),
                  pl.BlockSpec((T, LANES), lambda b, p: (0, 0))],
        out_specs=pl.BlockSpec((1, T, LANES), lambda b, p: (b, 0, p)),
        out_shape=jax.ShapeDtypeStruct((B, T, HP * LANES), BF16),
        scratch_shapes=[pltpu.VMEM((2, T, LANES), BF16), pltpu.VMEM((2, T, LANES), BF16),
                        pltpu.VMEM((T, LANES), BF16)],
        compiler_params=_params("arbitrary", "arbitrary"),
        name="mla_attention",
    )(z, z, z, qn_w.reshape(1, -1), kvn_w.reshape(1, -1), wq, wk, wv, cq, ck, sk)


def _na_kernel(q_ref, k_ref, v_ref, bt_ref, o_ref, k_s, v_s, *, L, rows, kr, need_ctx, scale):
    W = GRID_W
    rpb = 4 if rows % 4 == 0 else 1
    k_s[...] = k_ref[0].astype(BF16)
    v_s[...] = v_ref[0].astype(BF16)
    nwin = kr * W
    lane = lax.broadcasted_iota(jnp.int32, (W, LANES), 1)
    head_mask = [(lane < C_HEAD_DIM).astype(F32), (lane >= C_HEAD_DIM).astype(F32)]
    first_head = lane < C_HEAD_DIM
    qcol = lax.broadcasted_iota(jnp.int32, (W, nwin), 0)
    kcol = lax.broadcasted_iota(jnp.int32, (W, nwin), 1) % W
    cstart = jnp.clip(qcol - WIN_C // 2, 0, W - WIN_C)
    col_valid = (kcol >= cstart) & (kcol < cstart + WIN_C)

    def row_block(rb, carry):
        q_blk = q_ref[0, pl.ds(pl.multiple_of(L + rb * (rpb * W), W), rpb * W), :] * scale
        s_ctx = [_dot_nt((q_blk * jnp.concatenate([head_mask[h]] * rpb, axis=0)).astype(BF16), k_s[0:L, :])
                 for h in range(2)]
        chains, q0s = [], []
        for j in range(rpb):
            r = rb * rpb + j
            rs = jnp.clip(r - kr // 2, 0, rows - kr)
            k0 = pl.multiple_of(L + rs * W, W)
            q0s.append(pl.multiple_of(L + r * W, W))
            q = q_blk[j * W:(j + 1) * W]
            kw = k_s[pl.ds(k0, nwin), :]
            vw = v_s[pl.ds(k0, nwin), :]
            dr0 = rs - r + (WIN_R - 1)
            for h in range(2):
                s_nb = _dot_nt((q * head_mask[h]).astype(BF16), kw)
                bias = jnp.concatenate([bt_ref[0, h, dr0 + 2 * m] for m in range(kr // 2)], axis=-1)
                s_nb = jnp.where(col_valid, s_nb + bias, NEG_INF)
                chains.append([(s_nb, vw), (s_ctx[h][j * W:(j + 1) * W], v_s[0:L, :])])
        outs = _softmax_pv(chains)
        for j in range(rpb):
            o_ref[0, pl.ds(q0s[j], W), :] = jnp.where(first_head, outs[2 * j], outs[2 * j + 1]).astype(o_ref.dtype)
        return carry

    lax.fori_loop(0, rows // rpb, row_block, 0, unroll=4)

    tq = min(L, 256)
    lane_c = lax.broadcasted_iota(jnp.int32, (tq, LANES), 1)
    for i in range(L // tq):
        if need_ctx:
            q = q_ref[0, i * tq:(i + 1) * tq, :] * scale
            hm = [lane_c < C_HEAD_DIM, lane_c >= C_HEAD_DIM]
            s = [_dot_nt(jnp.where(hm[h], q, 0.0).astype(BF16), k_s[0:L, :]) for h in range(2)]
            outs = _softmax_pv([[(si, v_s[0:L, :])] for si in s])
            o_ref[0, i * tq:(i + 1) * tq, :] = jnp.where(lane_c < C_HEAD_DIM, outs[0], outs[1]).astype(o_ref.dtype)
        else:
            o_ref[0, i * tq:(i + 1) * tq, :] = jnp.zeros((tq, LANES), o_ref.dtype)


def _na_attention(qkv, bias_tab, L, need_ctx):
    B, T, D3 = qkv.shape
    D = D3 // 3
    HP = D // LANES
    rows = (T - L) // GRID_W
    kr = min(WIN_R, rows)
    assert kr % 2 == 0
    nd = bias_tab.shape[2]
    return pl.pallas_call(
        functools.partial(_na_kernel, L=L, rows=rows, kr=kr, need_ctx=need_ctx, scale=float(C_HEAD_DIM ** -0.5)),
        grid=(B, HP),
        in_specs=[pl.BlockSpec((1, T, LANES), lambda b, p: (b, 0, p)),
                  pl.BlockSpec((1, T, LANES), lambda b, p: (b, 0, HP + p)),
                  pl.BlockSpec((1, T, LANES), lambda b, p: (b, 0, 2 * HP + p)),
                  pl.BlockSpec((1, 2, nd, GRID_W, LANES), lambda b, p: (p, 0, 0, 0, 0))],
        out_specs=pl.BlockSpec((1, T, LANES), lambda b, p: (b, 0, p)),
        out_shape=jax.ShapeDtypeStruct((B, T, D), BF16),
        scratch_shapes=[pltpu.VMEM((T, LANES), BF16), pltpu.VMEM((T, LANES), BF16)],
        compiler_params=_params("arbitrary", "arbitrary"),
        name="na_attention",
    )(qkv, qkv, qkv, bias_tab)


def _na_bias_table(rpb):
    H = rpb.shape[0]
    qc = np.arange(GRID_W)[:, None]
    kc = np.arange(GRID_W)[None, :]
    dc = np.clip(kc - qc + (WIN_C - 1), 0, 2 * WIN_C - 2)
    pick = jnp.asarray(np.arange(2 * WIN_C - 1)[:, None, None] == dc[None], dtype=F32)
    t = jnp.einsum('hdc,cqk->hdqk', rpb, pick, precision=HIGHEST)
    t2 = jnp.concatenate([t[:, :-1], t[:, 1:]], axis=-1)
    return t2.reshape(H // 2, 2, 2 * WIN_R - 2, GRID_W, 2 * GRID_W)


def _bf(x):
    return x.astype(BF16)


def _seg_sum(x, ones_bd):
    hi = _bf(x)
    lo = _bf(x - hi.astype(F32))
    return _dot(hi, ones_bd) + _dot(lo, ones_bd)


def _head_ones(A):
    seg = np.arange(A) // A_HEAD_DIM
    return jnp.asarray(seg[:, None] == seg[None, :], dtype=BF16)


def _rwkv_prep_kernel(z_ref, zp_ref, zn_ref, mu_ref, w0_ref, w2_ref, a0_ref, a2_ref, g2_ref, kk_ref, ka_ref, rk_ref,
                      ones_ref, r_o, v_o, kkn_o, g_o, bonus_o, lw_o, beta_o, kd_o, *, nct, nt, A):
    i = pl.program_id(1)
    za = z_ref[0]
    tm = za.shape[0]
    row = lax.broadcasted_iota(jnp.int32, za.shape, 0)
    seg_first = (i == 0) | (i == nct)
    seg_last = (i == nct - 1) | (i == nt - 1)
    prev_row = jnp.where(seg_first, 0.0, zp_ref[0, 7:8, :])
    next_row = jnp.where(seg_last, 0.0, zn_ref[0, 0:1, :])
    prev = jnp.where(row == 0, prev_row, pltpu.roll(za, 1, 0))
    nxt = jnp.where(row == tm - 1, next_row, pltpu.roll(za, tm - 1, 0))
    zs = za + mu_ref[0:1, :] * (prev - za) + mu_ref[1:2, :] * (nxt - za)
    r = zs[:, 0:A]
    k = zs[:, A:2 * A]
    v = zs[:, 2 * A:3 * A]
    wd = _bf(jnp.tanh(zs[:, 3 * A:3 * A + LANES]))
    ad = _bf(zs[:, 3 * A + LANES:3 * A + 2 * LANES])
    gd = _bf(jax.nn.sigmoid(zs[:, 3 * A + 2 * LANES:3 * A + 3 * LANES]))
    ones = ones_ref[...]
    kk = k * kk_ref[...]
    kkn = kk / jnp.maximum(jnp.sqrt(_seg_sum(kk * kk, ones)), 1e-12)
    kd_sum = None
    for d in range(2):
        w_log = -jax.nn.softplus(-(w0_ref[d:d + 1, :] + _dot(wd, w2_ref[d]))) - 0.5
        lw_o[d, 0] = -jnp.exp(w_log)
        a = jax.nn.sigmoid(a0_ref[d:d + 1, :] + _dot(ad, a2_ref[d]))
        beta_o[d, 0] = kkn * a
        kd = k * (1.0 + (a - 1.0) * ka_ref[...])
        kd_o[d, 0] = kd
        kd_sum = kd if kd_sum is None else kd_sum + kd
    bonus_o[0] = _seg_sum(r * kd_sum * rk_ref[...], ones) * v
    r_o[0] = r
    v_o[0] = v
    kkn_o[0] = kkn
    g_o[0] = _dot(gd, g2_ref[...])


def _rwkv_prep(z, L, a_cols, mu, w0, w2, a0, a2, g2, k_k, k_a, r_k):
    B, T, _ = z.shape
    A = w0.shape[-1]
    assert 2 * LORA_W == LANES and 2 * LORA_A == LANES and LORA_G == LANES and a_cols == 3 * A + 3 * LANES
    tm = _row_tile(L, T - L)
    nt = T // tm
    hb = tm // 8

    def pad_lora(w):
        zero = jnp.zeros_like(w[0])
        return _bf(jnp.stack([jnp.concatenate([w[0], zero], 0), jnp.concatenate([zero, w[1]], 0)]))

    def const(shape):
        return pl.BlockSpec(shape, lambda b, i: (0,) * len(shape))

    tile = pl.BlockSpec((1, tm, A), lambda b, i: (b, i, 0))
    tile_d = pl.BlockSpec((2, 1, tm, A), lambda b, i: (0, b, i, 0))
    sd = jax.ShapeDtypeStruct((B, T, A), F32)
    sd_d = jax.ShapeDtypeStruct((2, B, T, A), F32)
    return pl.pallas_call(
        functools.partial(_rwkv_prep_kernel, nct=L // tm, nt=nt, A=A),
        grid=(B, nt),
        in_specs=[pl.BlockSpec((1, tm, a_cols), lambda b, i: (b, i, 0)),
                  pl.BlockSpec((1, 8, a_cols), lambda b, i: (b, jnp.maximum(i * hb - 1, 0), 0)),
                  pl.BlockSpec((1, 8, a_cols), lambda b, i: (b, jnp.minimum((i + 1) * hb, T // 8 - 1), 0)),
                  const((2, a_cols)), const((2, A)), const((2, 2 * LORA_W, A)), const((2, A)),
                  const((2, 2 * LORA_A, A)), const((LORA_G, A)), const((1, A)), const((1, A)), const((1, A)),
                  const((A, A))],
        out_specs=[tile, tile, tile, tile, tile, tile_d, tile_d, tile_d],
        out_shape=[sd, sd, sd, sd, sd, sd_d, sd_d, sd_d],
        compiler_params=_params("arbitrary", "arbitrary"),
        name="rwkv_prep",
    )(z, z, z, mu, w0, pad_lora(w2), a0, pad_lora(a2), _bf(g2), k_k.reshape(1, A), k_a.reshape(1, A),
      r_k.reshape(1, A), _head_ones(A))


def _tri_inverse(lms, eye, m16, m32):
    d0 = [_bf(jnp.where(m16, lm, 0.0)) for lm in lms]
    t = [eye + d.astype(F32) for d in d0]
    s = [_dot(d, d) for d in d0]
    for step in range(3):
        sb = [_bf(x) for x in s]
        t = [x + _dot(_bf(x), y) for x, y in zip(t, sb)]
        if step < 2:
            s = [_dot(y, y) for y in sb]
    for lvl in (m32 & (~m16), ~m32):
        tb = [_bf(x) for x in t]
        w = [_bf(_dot(_bf(jnp.where(lvl, lm, 0.0)), y)) for lm, y in zip(lms, tb)]
        t = [x + _dot(y, z) for x, y, z in zip(t, tb, w)]
    return t


def _wkv_kernel(*refs, NP, NB):
    C = SCAN_CHUNK
    P = 2 * C
    fwd_refs, bwd_refs, (yf_ref, yb_ref, h_s) = refs[0:6], refs[6:12], refs[12:]

    @pl.when(pl.program_id(1) == 0)
    def _():
        h_s[...] = jnp.zeros(h_s.shape, F32)

    ri = lax.broadcasted_iota(jnp.int32, (P, P), 0)
    ci = lax.broadcasted_iota(jnp.int32, (P, P), 1)
    same = (ri // C) == (ci // C)
    diff = (ri % C) - (ci % C)
    eye_b = ri == ci
    eye = eye_b.astype(F32)
    m16 = (ri // 16) == (ci // 16)
    m32 = (ri // 32) == (ci // 32)
    diff64 = lax.broadcasted_iota(jnp.int32, (C, C), 0) - lax.broadcasted_iota(jnp.int32, (C, C), 1)
    top = lax.broadcasted_iota(jnp.int32, (C, P), 1) < C
    zero_blk = jnp.zeros((P, P), BF16)
    sls = [slice(p * P, (p + 1) * P) for p in range(NP)]

    def bd(x):
        return [_bf(jnp.concatenate([jnp.where(top, x[:, sl], 0.0), jnp.where(top, 0.0, x[:, sl])], axis=0))
                for sl in sls]

    at, rt, bt, kt, bh, kh, vv, etots, before, before_eq = [], [], [], [], [], [], [], [], [], []
    for (r_ref, v_ref, kk_ref, lw_ref, beta_ref, kd_ref), sgn in ((fwd_refs, 1), (bwd_refs, -1)):
        order = diff * sgn
        tri = _bf(((diff64 * sgn) >= 0).astype(F32))
        for s in range(NB):
            before += [same & (order > 0)] * NP
            before_eq += [same & (order >= 0)] * NP
            lw = lw_ref[0, s]
            lw_hi = _bf(lw)
            lw_md = _bf(lw - lw_hi.astype(F32))
            lw_lo = _bf(lw - lw_hi.astype(F32) - lw_md.astype(F32))
            cum = _dot(tri, lw_hi) + _dot(tri, lw_md) + _dot(tri, lw_lo)
            tot = jnp.sum(lw, axis=0, keepdims=True)
            beta = beta_ref[0, s]
            kd = kd_ref[0, s]
            e_neg = jnp.exp(-cum)
            e_tail = jnp.exp(tot - cum)
            at += bd(-kk_ref[s] * jnp.exp(cum - lw))
            rt += bd(r_ref[s] * jnp.exp(cum))
            bt += bd(beta * e_neg)
            kt += bd(kd * e_neg)
            bh += bd(beta * e_tail)
            kh += bd(kd * e_tail)
            vv += bd(v_ref[s])
            etot = jnp.exp(tot)
            etots += [etot[:, sl] for sl in sls]

    ar = [jnp.concatenate([a, r], axis=0) for a, r in zip(at, rt)]
    arb = [_dot_nt(x, b) for x, b in zip(ar, bt)]
    ark = [_dot_nt(x, k) for x, k in zip(ar, kt)]
    lab = [jnp.where(m, x[:P], 0.0) for x, m in zip(arb, before)]
    tinv = _tri_inverse(lab, eye, m16, m32)
    u = [_dot(_bf(jnp.where(m, x[:P], 0.0)), v) for x, v, m in zip(ark, vv, before)]
    x = [_bf(_dot(_bf(t), jnp.concatenate([a, _bf(w)], axis=1))) for t, a, w in zip(tinv, at, u)]
    rhs = [jnp.concatenate([xi, jnp.concatenate([zero_blk, v], axis=1)], axis=0) for xi, v in zip(x, vv)]
    mn = [lax.dot_general(jnp.concatenate([b, k], axis=0), w, (((0,), (0,)), ((), ())), preferred_element_type=F32)
          for b, k, w in zip(bh, kh, rhs)]
    lr = [_bf(jnp.concatenate([jnp.where(m, xb[P:], 0.0), jnp.where(m, xk[P:], 0.0)], axis=1))
          for xb, xk, m in zip(arb, ark, before_eq)]
    qy = [_dot(l, w) for l, w in zip(lr, rhs)]
    qm = [_bf(jnp.concatenate([r.astype(F32) + q[:, :P], jnp.where(eye_b, e, 0.0) + m[:, :P]], axis=0))
          for r, q, m, e in zip(rt, qy, mn, etots)]
    nchain = 2 * NB * NP
    hin = [h_s[i] for i in range(nchain)]
    h_hi = [_bf(h) for h in hin]
    h_lo = [_bf(h - hh.astype(F32)) for h, hh in zip(hin, h_hi)]
    res = [_dot(w, hh) + _dot(w, hl) for w, hh, hl in zip(qm, h_hi, h_lo)]
    for i in range(nchain):
        ybd = res[i][:P] + qy[i][:, P:]
        y_ref = yf_ref if i < NB * NP else yb_ref
        y_ref[(i // NP) % NB, :, sls[i % NP]] = ybd[:C] + ybd[C:]
        h_s[i] = res[i][P:] + mn[i][:, P:]


def _wkv_scan(r, v, kk, lw, beta, kd, L):
    B, T, A = r.shape
    C = SCAN_CHUNK
    nC = T // C
    nct = L // C
    NP = A // (2 * C)

    def rev(c):
        return jnp.where(c < nct, nct - 1 - c, nC - 1 - (c - nct))

    NB = 2 if B % 2 == 0 else 1
    fwd = pl.BlockSpec((NB, C, A), lambda b, c: (b, c, 0))
    bwd = pl.BlockSpec((NB, C, A), lambda b, c: (b, rev(c), 0))
    fwd_d = pl.BlockSpec((1, NB, C, A), lambda b, c: (0, b, c, 0))
    bwd_d = pl.BlockSpec((1, NB, C, A), lambda b, c: (1, b, rev(c), 0))
    sd = jax.ShapeDtypeStruct((B, T, A), F32)
    return pl.pallas_call(
        functools.partial(_wkv_kernel, NP=NP, NB=NB),
        grid=(B // NB, nC),
        in_specs=[fwd, fwd, fwd, fwd_d, fwd_d, fwd_d, bwd, bwd, bwd, bwd_d, bwd_d, bwd_d],
        out_specs=[fwd, bwd],
        out_shape=[sd, sd],
        scratch_shapes=[pltpu.VMEM((2 * NB * NP, 2 * C, 2 * C), F32)],
        compiler_params=_params("arbitrary", "arbitrary"),
        name="wkv_scan",
    )(r, v, kk, lw, beta, kd, r, v, kk, lw, beta, kd)


def _rwkv_post_kernel(yf_ref, yb_ref, bonus_ref, g_ref, lnw_ref, lnb_ref, ones_ref, o_ref):
    y = yf_ref[0] + yb_ref[0]
    ones = ones_ref[...]
    inv_n = 1.0 / A_HEAD_DIM
    d = y - _seg_sum(y, ones) * inv_n
    var = _seg_sum(d * d, ones) * inv_n
    yn = d * lax.rsqrt(var + GN_EPS) * lnw_ref[...] + lnb_ref[...]
    o_ref[0] = ((yn + bonus_ref[0]) * g_ref[0]).astype(o_ref.dtype)


def _rwkv_post(yf, yb, bonus, g, ln_w, ln_b, L):
    B, T, A = bonus.shape
    tm = _row_tile(L, T - L)
    tile = pl.BlockSpec((1, tm, A), lambda b, i: (b, i, 0))
    vec = pl.BlockSpec((1, A), lambda b, i: (0, 0))
    return pl.pallas_call(
        _rwkv_post_kernel,
        grid=(B, T // tm),
        in_specs=[tile, tile, tile, tile, vec, vec, pl.BlockSpec((A, A), lambda b, i: (0, 0))],
        out_specs=tile,
        out_shape=jax.ShapeDtypeStruct((B, T, A), BF16),
        compiler_params=_params("arbitrary", "arbitrary"),
        name="rwkv_post",
    )(yf, yb, bonus, g, ln_w.reshape(1, A), ln_b.reshape(1, A), _head_ones(A))


def _rwkv_mixer(z, L, a_cols, mu, w0, w2, a0, a2, g2, k_k, k_a, r_k, ln_w, ln_b):
    r, v, kk, g, bonus, lw, beta, kd = _rwkv_prep(z, L, a_cols, mu, w0, w2, a0, a2, g2, k_k, k_a, r_k)
    yf, yb = _wkv_scan(r, v, kk, lw, beta, kd, L)
    return _rwkv_post(yf, yb, bonus, g, ln_w, ln_b, L)


def _router_kernel(x_ref, nw_ref, ss_ref, wr_ref, h_ref, aff_ref, *, L):
    i = pl.program_id(1)
    tm = x_ref.shape[1]
    h = _norm_mod(x_ref[0], nw_ref[...], _per_row(ss_ref, i, tm, L, 0), _per_row(ss_ref, i, tm, L, 1))
    h_ref[0] = h.astype(BF16)
    logits = _dot_nt(wr_ref[...], h, HIGHEST)
    m = jnp.max(logits, axis=0, keepdims=True)
    p = jnp.exp(logits - m)
    aff_ref[0] = p / jnp.sum(p, axis=0, keepdims=True)


def _norm_router(xa, nw, ss, w_router, L):
    B, T, D = xa.shape
    E = w_router.shape[1]
    tm = max(m for m in range(LANES, MAX_ROW_TILE + 1, LANES) if T % m == 0)
    return pl.pallas_call(
        functools.partial(_router_kernel, L=L),
        grid=(B, T // tm),
        in_specs=[pl.BlockSpec((1, tm, D), lambda b, i: (b, i, 0)),
                  pl.BlockSpec((1, D), lambda b, i: (0, 0)),
                  pl.BlockSpec((1, 2, 2, D), lambda b, i: (b, 0, 0, 0)),
                  pl.BlockSpec((E, D), lambda b, i: (0, 0))],
        out_specs=[pl.BlockSpec((1, tm, D), lambda b, i: (b, i, 0)),
                   pl.BlockSpec((1, E, tm), lambda b, i: (b, 0, i))],
        out_shape=[jax.ShapeDtypeStruct((B, T, D), BF16), jax.ShapeDtypeStruct((B, E, T), F32)],
        compiler_params=_params("arbitrary", "arbitrary"),
        name="norm_router",
    )(xa, nw.reshape(1, D), ss, w_router.T)


def _select_top(sets):
    keys = [pltpu.bitcast(aff, jnp.int32) for aff, _, _ in sets]
    E = keys[0].shape[0]

    def count_ge(key, thr):
        return jnp.sum(jnp.where(key >= thr, 1.0, 0.0), axis=1, keepdims=True)

    def narrow(_, carry):
        out = []
        for key, (_, cap, _), (lo, hi) in zip(keys, sets, carry):
            q = (hi - lo + 3) >> 2
            m1 = jnp.minimum(lo + q, hi)
            m2 = jnp.minimum(lo + 2 * q, hi)
            m3 = jnp.minimum(lo + 3 * q, hi)
            ok1, ok2, ok3 = count_ge(key, m1) >= cap, count_ge(key, m2) >= cap, count_ge(key, m3) >= cap
            new_lo = jnp.where(ok3, m3, jnp.where(ok2, m2, jnp.where(ok1, m1, lo)))
            new_hi = jnp.where(ok3, hi, jnp.where(ok2, m3 - 1, jnp.where(ok1, m2 - 1, m1 - 1)))
            out.append((new_lo, jnp.maximum(new_hi, new_lo)))
        return tuple(out)

    init = tuple((jnp.zeros((E, 1), jnp.int32), jnp.full((E, 1), 0x7F800000, jnp.int32)) for _ in sets)
    bounds = lax.fori_loop(0, 17, narrow, init)

    def prefix(x, tri):
        pb = tri.shape[0]
        parts, offset = [], jnp.zeros((E, 1), F32)
        for j in range(x.shape[1] // pb):
            xj = x[:, j * pb:(j + 1) * pb]
            inner = _dot(xj.astype(BF16), tri)
            parts.append(inner + offset)
            offset = offset + inner[:, pb - 1:pb] + xj[:, pb - 1:pb]
        return parts[0] if len(parts) == 1 else jnp.concatenate(parts, axis=1)

    res = []
    for key, (_, cap, tri), (thr, _) in zip(keys, sets, bounds):
        above = key > thr
        tie = key == thr
        need = cap - jnp.sum(jnp.where(above, 1.0, 0.0), axis=1, keepdims=True)
        tie_rank = prefix(jnp.where(tie, 1.0, 0.0), tri)
        sel = above | (tie & (tie_rank < need))
        res.append(jnp.where(sel, prefix(jnp.where(sel, 1.0, 0.0), tri), -1.0))
    return res


def _route_gather_kernel(aff_ref, h_ref, x_ref, pos_ref, gate_ref, slot_s, tri_s, *, L, cap_l, cap_c):
    b = pl.program_id(0)
    e = pl.program_id(1)
    T = h_ref.shape[1]
    S = T - L

    @pl.when((b == 0) & (e == 0))
    def _():
        n = tri_s.shape[0]
        tri_s[...] = jnp.where(lax.broadcasted_iota(jnp.int32, (n, n), 0) < lax.broadcasted_iota(jnp.int32, (n, n), 1),
                               1.0, 0.0).astype(BF16)

    @pl.when(e == 0)
    def _():
        sets = [(aff_ref[0, :, L:], cap_l, tri_s[...])]
        if cap_c:
            sets.append((aff_ref[0, :, 0:L], cap_c, tri_s[...]))
        picked = _select_top(sets)
        slot_s[:, L:] = picked[0]
        if cap_c:
            slot_s[:, 0:L] = picked[1]

    def gather(lo, n, cap, row0):
        slot = slot_s[pl.ds(e, 1), lo:lo + n]
        hit = slot == lax.broadcasted_iota(jnp.int32, (cap, n), 0).astype(F32)
        x_ref[0, 0, row0:row0 + cap, :] = _dot(jnp.where(hit, 1.0, 0.0).astype(BF16), h_ref[0, lo:lo + n, :]).astype(BF16)
        tok = lax.broadcasted_iota(jnp.int32, (cap, n), 1) + lo
        pos_ref[0, 0, row0:row0 + cap, :] = jnp.sum(jnp.where(hit, tok, 0), axis=1, keepdims=True)
        aff = aff_ref[0, pl.ds(e, 1), lo:lo + n]
        gate_ref[0, 0, row0:row0 + cap, :] = jnp.sum(jnp.where(hit, aff, 0.0), axis=1, keepdims=True)

    gather(L, S, cap_l, 0)
    if cap_c:
        gather(0, L, cap_c, cap_l)


def _route_gather(aff, h2, L, cap_l, cap_c):
    B, T, D = h2.shape
    E = aff.shape[1]
    Ct = cap_l + cap_c
    return pl.pallas_call(
        functools.partial(_route_gather_kernel, L=L, cap_l=cap_l, cap_c=cap_c),
        grid=(B, E),
        in_specs=[pl.BlockSpec((1, E, T), lambda b, e: (b, 0, 0)),
                  pl.BlockSpec((1, T, D), lambda b, e: (b, 0, 0))],
        out_specs=[pl.BlockSpec((1, 1, Ct, D), lambda b, e: (e, b, 0, 0)),
                   pl.BlockSpec((1, 1, Ct, 1), lambda b, e: (b, e, 0, 0)),
                   pl.BlockSpec((1, 1, Ct, 1), lambda b, e: (b, e, 0, 0))],
        out_shape=[jax.ShapeDtypeStruct((E, B, Ct, D), BF16), jax.ShapeDtypeStruct((B, E, Ct, 1), jnp.int32),
                   jax.ShapeDtypeStruct((B, E, Ct, 1), F32)],
        scratch_shapes=[pltpu.VMEM((E, T), F32), pltpu.VMEM((_row_tile(L, T - L),) * 2, BF16)],
        compiler_params=_params("arbitrary", "arbitrary"),
        name="moe_route_gather",
    )(aff, h2)


def _ffn_kernel(x_ref, w1_ref, w3_ref, w2_ref, o_ref, *, rm):
    j = pl.program_id(1)
    R = x_ref.shape[1]
    w1 = w1_ref[0, 0].astype(BF16)
    w3 = w3_ref[0, 0].astype(BF16)
    w2 = w2_ref[0, 0].astype(BF16)

    @pl.when(j == 0)
    def _():
        o_ref[...] = jnp.zeros(o_ref.shape, F32)

    def rows(i, carry):
        r0 = pl.multiple_of(i * rm, rm)
        x = x_ref[0, pl.ds(r0, rm), :]
        a = _dot(x, w1)
        b = _dot(x, w3)
        hid = (a * jax.nn.sigmoid(a) * b).astype(BF16)
        o_ref[0, pl.ds(r0, rm), :] += _dot(hid, w2)
        return carry

    lax.fori_loop(0, R // rm, rows, 0, unroll=True)


def _expert_ffn(xin, w1, w3, w2, layer):
    E, R, D = xin.shape
    F = w1.shape[-1]
    tf = min(512, F)
    rm = max(m for m in (MAX_ROW_TILE, 512, 256, 128, 64, 32, 16) if R % m == 0)
    return pl.pallas_call(
        functools.partial(_ffn_kernel, rm=rm),
        grid=(E, F // tf),
        in_specs=[pl.BlockSpec((1, R, D), lambda e, j: (e, 0, 0)),
                  pl.BlockSpec((1, 1, D, tf), lambda e, j: (layer, e, 0, j)),
                  pl.BlockSpec((1, 1, D, tf), lambda e, j: (layer, e, 0, j)),
                  pl.BlockSpec((1, 1, tf, D), lambda e, j: (layer, e, j, 0))],
        out_specs=pl.BlockSpec((1, R, D), lambda e, j: (e, 0, 0)),
        out_shape=jax.ShapeDtypeStruct((E, R, D), F32),
        compiler_params=_params("arbitrary", "arbitrary"),
        name="expert_ffn",
    )(xin, w1, w3, w2)


def _combine_kernel(y_ref, gate_ref, posl_ref, posc_ref, x_ref, g_ref, o_ref, yg_s, *, tq, nct, cap_l):
    E, _, Ct, td = y_ref.shape
    T = x_ref.shape[1]
    cap_c = Ct - cap_l
    nl = E * cap_l
    for e in range(E):
        yg_s[e * cap_l:(e + 1) * cap_l, :] = (y_ref[e, 0, :cap_l, :] * gate_ref[0, e, :cap_l, :]).astype(BF16)
        if cap_c:
            yg_s[nl + e * cap_c:nl + (e + 1) * cap_c, :] = (y_ref[e, 0, cap_l:, :] * gate_ref[0, e, cap_l:, :]).astype(BF16)

    def tile(i, gate_row, pos, lo, n):
        r0 = pl.multiple_of(i * tq, tq)
        tok = lax.broadcasted_iota(jnp.int32, (tq, n), 0) + r0
        onehot = jnp.where(tok == pos, 1.0, 0.0).astype(BF16)
        o_ref[0, pl.ds(r0, tq), :] = x_ref[0, pl.ds(r0, tq), :] + gate_row * _dot(onehot, yg_s[lo:lo + n, :])

    for i in range(nct):
        if cap_c:
            tile(i, g_ref[0, 0], posc_ref[0], nl, E * cap_c)
        else:
            o_ref[0, i * tq:(i + 1) * tq, :] = x_ref[0, i * tq:(i + 1) * tq, :]

    def body(i, carry):
        tile(i, g_ref[0, 1], posl_ref[0], 0, nl)
        return carry

    lax.fori_loop(nct, T // tq, body, 0, unroll=2)


def _moe_combine(y, gate, pos_l, pos_c, xa, g2, L, cap_l):
    E, B, Ct, D = y.shape
    T = xa.shape[1]
    td = min(512, D)
    tq = _row_tile(L, T - L)
    return pl.pallas_call(
        functools.partial(_combine_kernel, tq=tq, nct=L // tq, cap_l=cap_l),
        grid=(B, D // td),
        in_specs=[pl.BlockSpec((E, 1, Ct, td), lambda b, j: (0, b, 0, j)),
                  pl.BlockSpec((1, E, Ct, 1), lambda b, j: (b, 0, 0, 0)),
                  pl.BlockSpec((1, 1, pos_l.shape[-1]), lambda b, j: (b, 0, 0)),
                  pl.BlockSpec((1, 1, pos_c.shape[-1]), lambda b, j: (b, 0, 0)),
                  pl.BlockSpec((1, T, td), lambda b, j: (b, 0, j)),
                  pl.BlockSpec((1, 2, 1, td), lambda b, j: (b, 0, 0, j))],
        out_specs=pl.BlockSpec((1, T, td), lambda b, j: (b, 0, j)),
        out_shape=jax.ShapeDtypeStruct((B, T, D), F32),
        scratch_shapes=[pltpu.VMEM((E * Ct, td), BF16)],
        compiler_params=_params("arbitrary", "arbitrary"),
        name="moe_combine",
    )(y, gate, pos_l, pos_c, xa, g2)


def _moe(xa, nw, ss2, g2, w_router, w1, w3, w2, layer, L, need_ctx):
    B, T, D = xa.shape
    S = T - L
    E = w_router.shape[1]
    h2, aff = _norm_router(xa, nw, ss2, w_router, L)
    cap_l = CAPACITY_FACTOR * S // E
    cap_c = CAPACITY_FACTOR * L // E if need_ctx else 0
    Ct = cap_l + cap_c
    xin, pos, gate = _route_gather(aff, h2, L, cap_l, cap_c)
    pos_l = pos[:, :, :cap_l, 0].reshape(B, 1, E * cap_l)
    if cap_c:
        pos_c = pos[:, :, cap_l:, 0].reshape(B, 1, E * cap_c)
    else:
        pos_c = jnp.zeros((B, 1, LANES), jnp.int32)
    y = _expert_ffn(xin.reshape(E, B * Ct, D), w1, w3, w2, layer).reshape(E, B, Ct, D)
    return _moe_combine(y, gate, pos_l, pos_c, xa, g2, L, cap_l)


_PERM_EO = np.concatenate([np.arange(0, QK_ROPE, 2), np.arange(1, QK_ROPE, 2)])
_PERM_OE = np.concatenate([np.arange(1, QK_ROPE, 2), np.arange(0, QK_ROPE, 2)])


def _ab_input_weight(w_in, a_cols):
    D = w_in.shape[0]
    zr = w_in[:, a_cols + Q_RANK + KV_RANK:]
    zero = jnp.zeros((D, LANES - 2 * QK_ROPE), w_in.dtype)
    g1 = jnp.concatenate([zero, zr[:, _PERM_EO], zr[:, _PERM_EO]], axis=1)
    g2 = jnp.concatenate([zero, zr[:, _PERM_OE], zr[:, _PERM_OE]], axis=1)
    return jnp.concatenate([w_in[:, :a_cols + Q_RANK + KV_RANK], g1, g2], axis=1).astype(BF16)


def _mla_weights(w_qup, w_kvup):
    NH = w_qup.shape[1] // (QK_NOPE + QK_ROPE)
    wq = w_qup.reshape(Q_RANK, NH, QK_NOPE + QK_ROPE)
    rope = wq[:, :, QK_NOPE:]
    wq = jnp.concatenate([wq[:, :, :QK_NOPE], rope[:, :, _PERM_EO], rope[:, :, _PERM_OE]], axis=-1)
    wq = wq.reshape(Q_RANK, NH // 2, 2 * LANES).transpose(1, 0, 2)
    wkv = w_kvup.reshape(KV_RANK, NH, QK_NOPE + V_HEAD)
    wk = jnp.concatenate([wkv[:, :, :QK_NOPE], jnp.zeros((KV_RANK, NH, LANES - QK_NOPE), w_kvup.dtype)], axis=-1)
    wk = wk.reshape(KV_RANK, NH // 2, 2 * LANES).transpose(1, 0, 2)
    wv = wkv[:, :, QK_NOPE:].reshape(KV_RANK, NH // 2, 2 * V_HEAD).transpose(1, 0, 2)
    return wq.astype(BF16), wk.astype(BF16), wv.astype(BF16)


def _rope_tables(L, S):
    t = np.arange(S)
    row = (t // GRID_W).astype(np.float32)
    col = (t % GRID_W).astype(np.float32)
    n_freq = QK_ROPE // 4
    inv = (ROPE_BASE ** (-np.arange(n_freq, dtype=np.float32) / n_freq)).astype(np.float32)
    ang = jnp.concatenate([jnp.asarray(row[:, None] * inv), jnp.asarray(col[:, None] * inv)], axis=-1)
    cos = jnp.concatenate([jnp.ones((L, QK_ROPE // 2), F32), jnp.cos(ang)], axis=0)
    sin = jnp.concatenate([jnp.zeros((L, QK_ROPE // 2), F32), jnp.sin(ang)], axis=0)
    T = L + S
    cc = jnp.concatenate([cos, cos], axis=1)
    ss = jnp.concatenate([-sin, sin], axis=1)
    one = jnp.ones((T, LANES - 2 * QK_ROPE), F32)
    zero = jnp.zeros((T, LANES - 2 * QK_ROPE), F32)
    cq = jnp.concatenate([one, cc, ss], axis=1)
    ck = jnp.concatenate([zero, cc, cc], axis=1)
    sk = jnp.concatenate([zero, ss, ss], axis=1)
    return cq, ck, sk


def kernel(x, c, ctx, c_ctx, mod_w, mod_b, norm1_w, norm2_w, final_norm_w, ab_w_in, ab_w_out, rk_mu, rk_w0, rk_w2, rk_a0, rk_a2, rk_g2, rk_kk, rk_ka, rk_rk, rk_ln_w, rk_ln_b, mla_qn_w, mla_w_qup, mla_kvn_w, mla_w_kvup, na_w_qkv, na_rpb, na_w_out, moe_router, moe_w1, moe_w3, moe_w2):
    B, S, D = x.shape
    L = ctx.shape[1]
    depth = mod_w.shape[0]
    A = rk_w0.shape[-1]
    a_cols = rk_mu.shape[-1]

    rows_pad = -(B + 1) % 8
    cvec = jnp.concatenate([c, c_ctx[None], jnp.zeros((rows_pad, D), F32)], axis=0)
    mods = _mod_vectors(cvec, mod_w, mod_b)
    m_lat = mods[:, :B].reshape(depth, B, 6, D)
    m_ctx = jnp.broadcast_to(mods[:, B].reshape(depth, 1, 6, D), (depth, B, 6, D))
    mm = jnp.stack([m_ctx, m_lat], axis=2)

    cq, ck, sk = _rope_tables(L, S)
    xa = jnp.concatenate([ctx, x], axis=1)

    for layer in range(depth):
        need_ctx = layer < depth - 1
        i = layer // 2
        m = mm[layer]
        ss1, g1 = m[:, :, 0:2], m[:, :, 2:3]
        ss2, g2 = m[:, :, 3:5], m[:, :, 5:6]
        if layer % 2 == 0:
            w_in = _ab_input_weight(ab_w_in[i], a_cols)
            z = _norm_linear(xa, norm1_w[layer], ss1, w_in, L)
            o_a = _rwkv_mixer(z, L, a_cols, rk_mu[i], rk_w0[i], rk_w2[i], rk_a0[i], rk_a2[i], rk_g2[i],
                              rk_kk[i], rk_ka[i], rk_rk[i], rk_ln_w[i], rk_ln_b[i])
            wq, wk, wv = _mla_weights(mla_w_qup[i], mla_w_kvup[i])
            o_b = _mla_attention(z, mla_qn_w[i], mla_kvn_w[i], wq, wk, wv, cq, ck, sk, L,
                                 a_cols, a_cols + Q_RANK, a_cols + Q_RANK + KV_RANK)
            mixed, w_out = [o_a, o_b], ab_w_out[i]
        else:
            qkv = _norm_linear(xa, norm1_w[layer], ss1, na_w_qkv[i].astype(BF16), L, out_dtype=BF16)
            mixed, w_out = [_na_attention(qkv, _na_bias_table(na_rpb[i]), L, need_ctx)], na_w_out[i]
        xa = _linear_resid(mixed, w_out.astype(BF16), xa, g1, L)
        xa = _moe(xa, norm2_w[layer], ss2, g2, moe_router[layer], moe_w1, moe_w3, moe_w2, layer, L, need_ctx)
    return _final_norm(xa, final_norm_w, L)
```

```python
import functools

import jax
import jax.numpy as jnp
import numpy as np
from jax import lax
from jax.experimental import pallas as pl
from jax.experimental.pallas import tpu as pltpu

F32 = jnp.float32
BF16 = jnp.bfloat16
HIGHEST = lax.Precision.HIGHEST

GRID_W = 64
NORM_EPS = 1e-6
NEG_INF = -1e30
GN_EPS = 64e-5
A_HEAD_DIM = 64
LORA_W = 64
LORA_A = 64
LORA_G = 128
QK_NOPE = 64
QK_ROPE = 32
V_HEAD = 64
Q_RANK = 384
KV_RANK = 256
ROPE_BASE = 10000.0
C_HEAD_DIM = 64
WIN_R = 8
WIN_C = 16
N_EXPERTS = 16
CAPACITY_FACTOR = 2
SCAN_CHUNK = 64
DECAY_FLOOR_SCALE = float(np.exp(-0.5))
LANES = 128
MAX_ROW_TILE = 768

VMEM_LIMIT = 56 * 1024 * 1024


def _params(*sem):
    return pltpu.CompilerParams(dimension_semantics=sem, vmem_limit_bytes=VMEM_LIMIT)


def _dot(a, b, precision=None):
    return jnp.dot(a, b, preferred_element_type=F32, precision=precision)


def _dot_nt(a, b, precision=None):
    return lax.dot_general(a, b, (((1,), (1,)), ((), ())), preferred_element_type=F32, precision=precision)


def _row_tile(L, S):
    tm = 256
    while L % tm or S % tm:
        tm //= 2
    return tm


def _mod_kernel(c_ref, w_ref, b_ref, o_ref):
    c = c_ref[...]
    sc = c * jax.nn.sigmoid(c)
    o_ref[0] = _dot(sc.astype(BF16), w_ref[0].astype(BF16)) + b_ref[0]


def _mod_vectors(cvec, mod_w, mod_b):
    depth, D, N = mod_w.shape
    R = cvec.shape[0]
    tn = 1024
    return pl.pallas_call(
        _mod_kernel,
        grid=(depth, N // tn),
        in_specs=[pl.BlockSpec((R, D), lambda l, j: (0, 0)),
                  pl.BlockSpec((1, D, tn), lambda l, j: (l, 0, j)),
                  pl.BlockSpec((1, 1, tn), lambda l, j: (l, 0, j))],
        out_specs=pl.BlockSpec((1, R, tn), lambda l, j: (l, 0, j)),
        out_shape=jax.ShapeDtypeStruct((depth, R, N), F32),
        compiler_params=_params("arbitrary", "arbitrary"),
        name="mod_vectors",
    )(cvec, mod_w, mod_b.reshape(depth, 1, N))


def _wide_tile(T):
    return max(m for m in range(8, MAX_ROW_TILE + 1, 8) if T % m == 0)


def _per_row(mod_ref, i, tm, L, k):
    row = lax.broadcasted_iota(jnp.int32, (tm, 1), 0) + i * tm
    return jnp.where(row < L, mod_ref[0, 0, k:k + 1, :], mod_ref[0, 1, k:k + 1, :])


def _norm_mod(x, nw, shift, scale):
    y = x * lax.rsqrt(jnp.mean(x * x, axis=-1, keepdims=True) + NORM_EPS)
    y = y * nw
    return y * (1.0 + scale) + shift


def _norm_linear_kernel(x_ref, nw_ref, ss_ref, w_ref, o_ref, *, L):
    i = pl.program_id(1)
    tm = x_ref.shape[1]
    h = _norm_mod(x_ref[0], nw_ref[...], _per_row(ss_ref, i, tm, L, 0), _per_row(ss_ref, i, tm, L, 1))
    o_ref[0] = _dot(h.astype(BF16), w_ref[...]).astype(o_ref.dtype)


def _norm_linear(xa, nw, ss, w, L, out_dtype=F32):
    B, T, D = xa.shape
    N = w.shape[1]
    tm = _wide_tile(T)
    return pl.pallas_call(
        functools.partial(_norm_linear_kernel, L=L),
        grid=(B, T // tm),
        in_specs=[pl.BlockSpec((1, tm, D), lambda b, i: (b, i, 0)),
                  pl.BlockSpec((1, D), lambda b, i: (0, 0)),
                  pl.BlockSpec((1, 2, 2, D), lambda b, i: (b, 0, 0, 0)),
                  pl.BlockSpec((D, N), lambda b, i: (0, 0))],
        out_specs=pl.BlockSpec((1, tm, N), lambda b, i: (b, i, 0)),
        out_shape=jax.ShapeDtypeStruct((B, T, N), out_dtype),
        compiler_params=_params("arbitrary", "arbitrary"),
        name="norm_linear",
    )(xa, nw.reshape(1, D), ss, w)


def _linear_resid_kernel(*refs, ks, L):
    n = len(ks)
    a_refs, (w_ref, x_ref, g_ref, o_ref) = refs[:n], refs[n:]
    acc = None
    off = 0
    for a_ref, k in zip(a_refs, ks):
        part = _dot(a_ref[0].astype(BF16), w_ref[off:off + k, :])
        acc = part if acc is None else acc + part
        off += k
    o_ref[0] = x_ref[0] + _per_row(g_ref, pl.program_id(1), x_ref.shape[1], L, 0) * acc


def _linear_resid(a_list, w, xa, gate, L):
    B, T, D = xa.shape
    tm = _wide_tile(T)
    ks = tuple(a.shape[-1] for a in a_list)
    in_specs = [pl.BlockSpec((1, tm, k), lambda b, i: (b, i, 0)) for k in ks]
    in_specs += [pl.BlockSpec(w.shape, lambda b, i: (0, 0)),
                 pl.BlockSpec((1, tm, D), lambda b, i: (b, i, 0)),
                 pl.BlockSpec((1, 2, 1, D), lambda b, i: (b, 0, 0, 0))]
    return pl.pallas_call(
        functools.partial(_linear_resid_kernel, ks=ks, L=L),
        grid=(B, T // tm),
        in_specs=in_specs,
        out_specs=pl.BlockSpec((1, tm, D), lambda b, i: (b, i, 0)),
        out_shape=jax.ShapeDtypeStruct((B, T, D), F32),
        compiler_params=_params("arbitrary", "arbitrary"),
        name="linear_resid",
    )(*a_list, w, xa, gate)


def _rms_kernel(x_ref, w_ref, o_ref):
    x = x_ref[0]
    o_ref[0] = x * lax.rsqrt(jnp.mean(x * x, axis=-1, keepdims=True) + NORM_EPS) * w_ref[...]


def _final_norm(xa, w, L):
    B, T, D = xa.shape
    S = T - L
    tm = _row_tile(L, S)
    nct = L // tm
    return pl.pallas_call(
        _rms_kernel,
        grid=(B, S // tm),
        in_specs=[pl.BlockSpec((1, tm, D), lambda b, i: (b, i + nct, 0)),
                  pl.BlockSpec((1, D), lambda b, i: (0, 0))],
        out_specs=pl.BlockSpec((1, tm, D), lambda b, i: (b, i, 0)),
        out_shape=jax.ShapeDtypeStruct((B, S, D), F32),
        compiler_params=_params("arbitrary", "arbitrary"),
        name="final_norm",
    )(xa, w.reshape(1, D))


def _rms(x, w):
    return x * lax.rsqrt(jnp.mean(x * x, axis=-1, keepdims=True) + NORM_EPS) * w


def _softmax_pv(chains):
    m = [functools.reduce(jnp.maximum, [jnp.max(s, axis=-1, keepdims=True) for s, _ in ch]) for ch in chains]
    p = [[jnp.exp(s - mi) for s, _ in ch] for ch, mi in zip(chains, m)]
    l = [functools.reduce(jnp.add, [jnp.sum(x, axis=-1, keepdims=True) for x in pc]) for pc in p]
    o = [functools.reduce(jnp.add, [_dot(x.astype(BF16), v) for x, (_, v) in zip(pc, ch)]) for pc, ch in zip(p, chains)]
    return [oi / li for oi, li in zip(o, l)]


def _mla_kernel(zq_ref, zkv_ref, zr_ref, qn_ref, kvn_ref, wq_ref, wk_ref, wv_ref, cq_ref, ck_ref, sk_ref,
                o_ref, q_s, k_s, v_s, *, L, tq, scale):
    T = zq_ref.shape[1]
    zqn = _rms(zq_ref[0], qn_ref[...]).astype(BF16)
    zkvn = _rms(zkv_ref[0], kvn_ref[...]).astype(BF16)
    qh = _dot(zqn, wq_ref[0])
    kn = _dot(zkvn, wk_ref[0])
    v_s[...] = _dot(zkvn, wv_ref[0]).astype(BF16)
    zr = zr_ref[0]
    kr = zr[:, :LANES] * ck_ref[...] + zr[:, LANES:] * sk_ref[...]
    cq = cq_ref[...] * scale
    for h in range(2):
        q_s[h] = (qh[:, h * LANES:(h + 1) * LANES] * cq).astype(BF16)
        k_s[h] = (kn[:, h * LANES:(h + 1) * LANES] + kr).astype(BF16)
    def tile(row0, nk, rows):
        first_head = lax.broadcasted_iota(jnp.int32, (rows, LANES), 1) < V_HEAD
        s = [_dot_nt(q_s[h, pl.ds(row0, rows), :], k_s[h, 0:nk, :]) for h in range(2)]
        outs = _softmax_pv([[(si, v_s[0:nk, :])] for si in s])
        o_ref[0, pl.ds(row0, rows), :] = jnp.where(first_head, outs[0], outs[1]).astype(o_ref.dtype)

    for i in range(L // tq):
        tile(i * tq, L, tq)
    tl = 2 * tq if (T - L) % (2 * tq) == 0 else tq
    for i in range((T - L) // tl):
        tile(L + i * tl, T, tl)


def _mla_attention(z, qn_w, kvn_w, wq, wk, wv, cq, ck, sk, L, col_q, col_kv, col_r):
    B, T, _ = z.shape
    HP = wq.shape[0]
    tq = _row_tile(L, T - L)
    scale = float((QK_NOPE + QK_ROPE) ** -0.5)
    return pl.pallas_call(
        functools.partial(_mla_kernel, L=L, tq=tq, scale=scale),
        grid=(B, HP),
        in_specs=[pl.BlockSpec((1, T, Q_RANK), lambda b, p: (b, 0, col_q // Q_RANK)),
                  pl.BlockSpec((1, T, KV_RANK), lambda b, p: (b, 0, col_kv // KV_RANK)),
                  pl.BlockSpec((1, T, 2 * LANES), lambda b, p: (b, 0, col_r // (2 * LANES))),
                  pl.BlockSpec((1, Q_RANK), lambda b, p: (0, 0)),
                  pl.BlockSpec((1, KV_RANK), lambda b, p: (0, 0)),
                  pl.BlockSpec((1, Q_RANK, 2 * LANES), lambda b, p: (p, 0, 0)),
                  pl.BlockSpec((1, KV_RANK, 2 * LANES), lambda b, p: (p, 0, 0)),
                  pl.BlockSpec((1, KV_RANK, LANES), lambda b, p: (p, 0, 0)),
                  pl.BlockSpec((T, LANES), lambda b, p: (0, 0)),
                  pl.BlockSpec((T, LANES), lambda b, p: (0, 0)),
                  pl.BlockSpec((T, LANES), lambda b, p: (0, 0))],
        out_specs=pl.BlockSpec((1, T, LANES), lambda b, p: (b, 0, p)),
        out_shape=jax.ShapeDtypeStruct((B, T, HP * LANES), BF16),
        scratch_shapes=[pltpu.VMEM((2, T, LANES), BF16), pltpu.VMEM((2, T, LANES), BF16),
                        pltpu.VMEM((T, LANES), BF16)],
        compiler_params=_params("arbitrary", "arbitrary"),
        name="mla_attention",
    )(z, z, z, qn_w.reshape(1, -1), kvn_w.reshape(1, -1), wq, wk, wv, cq, ck, sk)


def _na_kernel(q_ref, k_ref, v_ref, bt_ref, o_ref, k_s, v_s, *, L, rows, kr, need_ctx, scale):
    W = GRID_W
    rpb = 4 if rows % 4 == 0 else 1
    k_s[...] = k_ref[0].astype(BF16)
    v_s[...] = v_ref[0].astype(BF16)
    nwin = kr * W
    lane = lax.broadcasted_iota(jnp.int32, (W, LANES), 1)
    head_mask = [(lane < C_HEAD_DIM).astype(F32), (lane >= C_HEAD_DIM).astype(F32)]
    first_head = lane < C_HEAD_DIM
    qcol = lax.broadcasted_iota(jnp.int32, (W, nwin), 0)
    kcol = lax.broadcasted_iota(jnp.int32, (W, nwin), 1) % W
    cstart = jnp.clip(qcol - WIN_C // 2, 0, W - WIN_C)
    col_valid = (kcol >= cstart) & (kcol < cstart + WIN_C)

    def row_block(rb, carry):
        q_blk = q_ref[0, pl.ds(pl.multiple_of(L + rb * (rpb * W), W), rpb * W), :] * scale
        s_ctx = [_dot_nt((q_blk * jnp.concatenate([head_mask[h]] * rpb, axis=0)).astype(BF16), k_s[0:L, :])
                 for h in range(2)]
        chains, q0s = [], []
        for j in range(rpb):
            r = rb * rpb + j
            rs = jnp.clip(r - kr // 2, 0, rows - kr)
            k0 = pl.multiple_of(L + rs * W, W)
            q0s.append(pl.multiple_of(L + r * W, W))
            q = q_blk[j * W:(j + 1) * W]
            kw = k_s[pl.ds(k0, nwin), :]
            vw = v_s[pl.ds(k0, nwin), :]
            dr0 = rs - r + (WIN_R - 1)
            for h in range(2):
                s_nb = _dot_nt((q * head_mask[h]).astype(BF16), kw)
                bias = jnp.concatenate([bt_ref[0, h, dr0 + 2 * m] for m in range(kr // 2)], axis=-1)
                s_nb = jnp.where(col_valid, s_nb + bias, NEG_INF)
                chains.append([(s_nb, vw), (s_ctx[h][j * W:(j + 1) * W], v_s[0:L, :])])
        outs = _softmax_pv(chains)
        for j in range(rpb):
            o_ref[0, pl.ds(q0s[j], W), :] = jnp.where(first_head, outs[2 * j], outs[2 * j + 1]).astype(o_ref.dtype)
        return carry

    lax.fori_loop(0, rows // rpb, row_block, 0, unroll=4)

    tq = min(L, 256)
    lane_c = lax.broadcasted_iota(jnp.int32, (tq, LANES), 1)
    for i in range(L // tq):
        if need_ctx:
            q = q_ref[0, i * tq:(i + 1) * tq, :] * scale
            hm = [lane_c < C_HEAD_DIM, lane_c >= C_HEAD_DIM]
            s = [_dot_nt(jnp.where(hm[h], q, 0.0).astype(BF16), k_s[0:L, :]) for h in range(2)]
            outs = _softmax_pv([[(si, v_s[0:L, :])] for si in s])
            o_ref[0, i * tq:(i + 1) * tq, :] = jnp.where(lane_c < C_HEAD_DIM, outs[0], outs[1]).astype(o_ref.dtype)
        else:
            o_ref[0, i * tq:(i + 1) * tq, :] = jnp.zeros((tq, LANES), o_ref.dtype)


def _na_attention(qkv, bias_tab, L, need_ctx):
    B, T, D3 = qkv.shape
    D = D3 // 3
    HP = D // LANES
    rows = (T - L) // GRID_W
    kr = min(WIN_R, rows)
    assert kr % 2 == 0
    nd = bias_tab.shape[2]
    return pl.pallas_call(
        functools.partial(_na_kernel, L=L, rows=rows, kr=kr, need_ctx=need_ctx, scale=float(C_HEAD_DIM ** -0.5)),
        grid=(B, HP),
        in_specs=[pl.BlockSpec((1, T, LANES), lambda b, p: (b, 0, p)),
                  pl.BlockSpec((1, T, LANES), lambda b, p: (b, 0, HP + p)),
                  pl.BlockSpec((1, T, LANES), lambda b, p: (b, 0, 2 * HP + p)),
                  pl.BlockSpec((1, 2, nd, GRID_W, LANES), lambda b, p: (p, 0, 0, 0, 0))],
        out_specs=pl.BlockSpec((1, T, LANES), lambda b, p: (b, 0, p)),
        out_shape=jax.ShapeDtypeStruct((B, T, D), BF16),
        scratch_shapes=[pltpu.VMEM((T, LANES), BF16), pltpu.VMEM((T, LANES), BF16)],
        compiler_params=_params("arbitrary", "arbitrary"),
        name="na_attention",
    )(qkv, qkv, qkv, bias_tab)


def _na_bias_table(rpb):
    H = rpb.shape[0]
    qc = np.arange(GRID_W)[:, None]
    kc = np.arange(GRID_W)[None, :]
    dc = np.clip(kc - qc + (WIN_C - 1), 0, 2 * WIN_C - 2)
    pick = jnp.asarray(np.arange(2 * WIN_C - 1)[:, None, None] == dc[None], dtype=F32)
    t = jnp.einsum('hdc,cqk->hdqk', rpb, pick, precision=HIGHEST)
    t2 = jnp.concatenate([t[:, :-1], t[:, 1:]], axis=-1)
    return t2.reshape(H // 2, 2, 2 * WIN_R - 2, GRID_W, 2 * GRID_W)


def _bf(x):
    return x.astype(BF16)


def _seg_sum(x, ones_bd):
    hi = _bf(x)
    lo = _bf(x - hi.astype(F32))
    return _dot(hi, ones_bd) + _dot(lo, ones_bd)


def _head_ones(A):
    seg = np.arange(A) // A_HEAD_DIM
    return jnp.asarray(seg[:, None] == seg[None, :], dtype=BF16)


def _rwkv_prep_kernel(z_ref, zp_ref, zn_ref, mu_ref, w0_ref, w2_ref, a0_ref, a2_ref, g2_ref, kk_ref, ka_ref, rk_ref,
                      ones_ref, r_o, v_o, kkn_o, g_o, bonus_o, lw_o, beta_o, kd_o, *, nct, nt, A):
    i = pl.program_id(1)
    za = z_ref[0]
    tm = za.shape[0]
    row = lax.broadcasted_iota(jnp.int32, za.shape, 0)
    seg_first = (i == 0) | (i == nct)
    seg_last = (i == nct - 1) | (i == nt - 1)
    prev_row = jnp.where(seg_first, 0.0, zp_ref[0, 7:8, :])
    next_row = jnp.where(seg_last, 0.0, zn_ref[0, 0:1, :])
    prev = jnp.where(row == 0, prev_row, pltpu.roll(za, 1, 0))
    nxt = jnp.where(row == tm - 1, next_row, pltpu.roll(za, tm - 1, 0))
    zs = za + mu_ref[0:1, :] * (prev - za) + mu_ref[1:2, :] * (nxt - za)
    r = zs[:, 0:A]
    k = zs[:, A:2 * A]
    v = zs[:, 2 * A:3 * A]
    wd = _bf(jnp.tanh(zs[:, 3 * A:3 * A + LANES]))
    ad = _bf(zs[:, 3 * A + LANES:3 * A + 2 * LANES])
    gd = _bf(jax.nn.sigmoid(zs[:, 3 * A + 2 * LANES:3 * A + 3 * LANES]))
    ones = ones_ref[...]
    kk = k * kk_ref[...]
    kkn = kk / jnp.maximum(jnp.sqrt(_seg_sum(kk * kk, ones)), 1e-12)
    kd_sum = None
    for d in range(2):
        u = w0_ref[d:d + 1, :] + _dot(wd, w2_ref[d])
        lw_o[d, 0] = -DECAY_FLOOR_SCALE * jax.nn.sigmoid(u)
        a = jax.nn.sigmoid(a0_ref[d:d + 1, :] + _dot(ad, a2_ref[d]))
        beta_o[d, 0] = kkn * a
        kd = k * (1.0 + (a - 1.0) * ka_ref[...])
        kd_o[d, 0] = kd
        kd_sum = kd if kd_sum is None else kd_sum + kd
    bonus_o[0] = _seg_sum(r * kd_sum * rk_ref[...], ones) * v
    r_o[0] = r
    v_o[0] = v
    kkn_o[0] = kkn
    g_o[0] = _dot(gd, g2_ref[...])


def _rwkv_prep(z, L, a_cols, mu, w0, w2, a0, a2, g2, k_k, k_a, r_k):
    B, T, _ = z.shape
    A = w0.shape[-1]
    assert 2 * LORA_W == LANES and 2 * LORA_A == LANES and LORA_G == LANES and a_cols == 3 * A + 3 * LANES
    tm = _row_tile(L, T - L)
    nt = T // tm
    hb = tm // 8

    def pad_lora(w):
        zero = jnp.zeros_like(w[0])
        return _bf(jnp.stack([jnp.concatenate([w[0], zero], 0), jnp.concatenate([zero, w[1]], 0)]))

    def const(shape):
        return pl.BlockSpec(shape, lambda b, i: (0,) * len(shape))

    tile = pl.BlockSpec((1, tm, A), lambda b, i: (b, i, 0))
    tile_d = pl.BlockSpec((2, 1, tm, A), lambda b, i: (0, b, i, 0))
    sd = jax.ShapeDtypeStruct((B, T, A), F32)
    sd_d = jax.ShapeDtypeStruct((2, B, T, A), F32)
    return pl.pallas_call(
        functools.partial(_rwkv_prep_kernel, nct=L // tm, nt=nt, A=A),
        grid=(B, nt),
        in_specs=[pl.BlockSpec((1, tm, a_cols), lambda b, i: (b, i, 0)),
                  pl.BlockSpec((1, 8, a_cols), lambda b, i: (b, jnp.maximum(i * hb - 1, 0), 0)),
                  pl.BlockSpec((1, 8, a_cols), lambda b, i: (b, jnp.minimum((i + 1) * hb, T // 8 - 1), 0)),
                  const((2, a_cols)), const((2, A)), const((2, 2 * LORA_W, A)), const((2, A)),
                  const((2, 2 * LORA_A, A)), const((LORA_G, A)), const((1, A)), const((1, A)), const((1, A)),
                  const((A, A))],
        out_specs=[tile, tile, tile, tile, tile, tile_d, tile_d, tile_d],
        out_shape=[sd, sd, sd, sd, sd, sd_d, sd_d, sd_d],
        compiler_params=_params("arbitrary", "arbitrary"),
        name="rwkv_prep",
    )(z, z, z, mu, w0, pad_lora(w2), a0, pad_lora(a2), _bf(g2), k_k.reshape(1, A), k_a.reshape(1, A),
      r_k.reshape(1, A), _head_ones(A))


def _tri_inverse(lms, eye, m16, m32):
    d0 = [_bf(jnp.where(m16, lm, 0.0)) for lm in lms]
    t = [eye + d.astype(F32) for d in d0]
    s = [_dot(d, d) for d in d0]
    for step in range(3):
        sb = [_bf(x) for x in s]
        t = [x + _dot(_bf(x), y) for x, y in zip(t, sb)]
        if step < 2:
            s = [_dot(y, y) for y in sb]
    for lvl in (m32 & (~m16), ~m32):
        tb = [_bf(x) for x in t]
        w = [_bf(_dot(_bf(jnp.where(lvl, lm, 0.0)), y)) for lm, y in zip(lms, tb)]
        t = [x + _dot(y, z) for x, y, z in zip(t, tb, w)]
    return t


def _wkv_kernel(*refs, NP, NB):
    C = SCAN_CHUNK
    P = 2 * C
    fwd_refs, bwd_refs, (yf_ref, yb_ref, h_s) = refs[0:6], refs[6:12], refs[12:]

    @pl.when(pl.program_id(1) == 0)
    def _():
        h_s[...] = jnp.zeros(h_s.shape, F32)

    ri = lax.broadcasted_iota(jnp.int32, (P, P), 0)
    ci = lax.broadcasted_iota(jnp.int32, (P, P), 1)
    same = (ri // C) == (ci // C)
    diff = (ri % C) - (ci % C)
    eye_b = ri == ci
    eye = eye_b.astype(F32)
    m16 = (ri // 16) == (ci // 16)
    m32 = (ri // 32) == (ci // 32)
    diff64 = lax.broadcasted_iota(jnp.int32, (C, C), 0) - lax.broadcasted_iota(jnp.int32, (C, C), 1)
    top = lax.broadcasted_iota(jnp.int32, (C, P), 1) < C
    zero_blk = jnp.zeros((P, P), BF16)
    sls = [slice(p * P, (p + 1) * P) for p in range(NP)]

    def bd(x):
        return [_bf(jnp.concatenate([jnp.where(top, x[:, sl], 0.0), jnp.where(top, 0.0, x[:, sl])], axis=0))
                for sl in sls]

    at, rt, bt, kt, bh, kh, vv, etots, before, before_eq = [], [], [], [], [], [], [], [], [], []
    for (r_ref, v_ref, kk_ref, lw_ref, beta_ref, kd_ref), sgn in ((fwd_refs, 1), (bwd_refs, -1)):
        order = diff * sgn
        tri = _bf(((diff64 * sgn) >= 0).astype(F32))
        for s in range(NB):
            before += [same & (order > 0)] * NP
            before_eq += [same & (order >= 0)] * NP
            lw = lw_ref[0, s]
            lw_hi = _bf(lw)
            lw_md = _bf(lw - lw_hi.astype(F32))
            lw_lo = _bf(lw - lw_hi.astype(F32) - lw_md.astype(F32))
            cum = _dot(tri, lw_hi) + _dot(tri, lw_md) + _dot(tri, lw_lo)
            tot = jnp.sum(lw, axis=0, keepdims=True)
            beta = beta_ref[0, s]
            kd = kd_ref[0, s]
            e_neg = jnp.exp(-cum)
            e_tail = jnp.exp(tot - cum)
            at += bd(-kk_ref[s] * jnp.exp(cum - lw))
            rt += bd(r_ref[s] * jnp.exp(cum))
            bt += bd(beta * e_neg)
            kt += bd(kd * e_neg)
            bh += bd(beta * e_tail)
            kh += bd(kd * e_tail)
            vv += bd(v_ref[s])
            etot = jnp.exp(tot)
            etots += [etot[:, sl] for sl in sls]

    ar = [jnp.concatenate([a, r], axis=0) for a, r in zip(at, rt)]
    arb = [_dot_nt(x, b) for x, b in zip(ar, bt)]
    ark = [_dot_nt(x, k) for x, k in zip(ar, kt)]
    lab = [jnp.where(m, x[:P], 0.0) for x, m in zip(arb, before)]
    tinv = _tri_inverse(lab, eye, m16, m32)
    u = [_dot(_bf(jnp.where(m, x[:P], 0.0)), v) for x, v, m in zip(ark, vv, before)]
    x = [_bf(_dot(_bf(t), jnp.concatenate([a, _bf(w)], axis=1))) for t, a, w in zip(tinv, at, u)]
    rhs = [jnp.concatenate([xi, jnp.concatenate([zero_blk, v], axis=1)], axis=0) for xi, v in zip(x, vv)]
    mn = [lax.dot_general(jnp.concatenate([b, k], axis=0), w, (((0,), (0,)), ((), ())), preferred_element_type=F32)
          for b, k, w in zip(bh, kh, rhs)]
    lr = [_bf(jnp.concatenate([jnp.where(m, xb[P:], 0.0), jnp.where(m, xk[P:], 0.0)], axis=1))
          for xb, xk, m in zip(arb, ark, before_eq)]
    qy = [_dot(l, w) for l, w in zip(lr, rhs)]
    qm = [_bf(jnp.concatenate([r.astype(F32) + q[:, :P], jnp.where(eye_b, e, 0.0) + m[:, :P]], axis=0))
          for r, q, m, e in zip(rt, qy, mn, etots)]
    nchain = 2 * NB * NP
    hin = [h_s[i] for i in range(nchain)]
    h_hi = [_bf(h) for h in hin]
    h_lo = [_bf(h - hh.astype(F32)) for h, hh in zip(hin, h_hi)]
    res = [_dot(w, hh) + _dot(w, hl) for w, hh, hl in zip(qm, h_hi, h_lo)]
    for i in range(nchain):
        ybd = res[i][:P] + qy[i][:, P:]
        y_ref = yf_ref if i < NB * NP else yb_ref
        y_ref[(i // NP) % NB, :, sls[i % NP]] = ybd[:C] + ybd[C:]
        h_s[i] = res[i][P:] + mn[i][:, P:]


def _wkv_scan(r, v, kk, lw, beta, kd, L):
    B, T, A = r.shape
    C = SCAN_CHUNK
    nC = T // C
    nct = L // C
    NP = A // (2 * C)

    def rev(c):
        return jnp.where(c < nct, nct - 1 - c, nC - 1 - (c - nct))

    NB = 2 if B % 2 == 0 else 1
    fwd = pl.BlockSpec((NB, C, A), lambda b, c: (b, c, 0))
    bwd = pl.BlockSpec((NB, C, A), lambda b, c: (b, rev(c), 0))
    fwd_d = pl.BlockSpec((1, NB, C, A), lambda b, c: (0, b, c, 0))
    bwd_d = pl.BlockSpec((1, NB, C, A), lambda b, c: (1, b, rev(c), 0))
    sd = jax.ShapeDtypeStruct((B, T, A), F32)
    return pl.pallas_call(
        functools.partial(_wkv_kernel, NP=NP, NB=NB),
        grid=(B // NB, nC),
        in_specs=[fwd, fwd, fwd, fwd_d, fwd_d, fwd_d, bwd, bwd, bwd, bwd_d, bwd_d, bwd_d],
        out_specs=[fwd, bwd],
        out_shape=[sd, sd],
        scratch_shapes=[pltpu.VMEM((2 * NB * NP, 2 * C, 2 * C), F32)],
        compiler_params=_params("arbitrary", "arbitrary"),
        name="wkv_scan",
    )(r, v, kk, lw, beta, kd, r, v, kk, lw, beta, kd)


def _rwkv_post_kernel(yf_ref, yb_ref, bonus_ref, g_ref, lnw_ref, lnb_ref, ones_ref, o_ref):
    y = yf_ref[0] + yb_ref[0]
    ones = ones_ref[...]
    inv_n = 1.0 / A_HEAD_DIM
    d = y - _seg_sum(y, ones) * inv_n
    var = _seg_sum(d * d, ones) * inv_n
    yn = d * lax.rsqrt(var + GN_EPS) * lnw_ref[...] + lnb_ref[...]
    o_ref[0] = ((yn + bonus_ref[0]) * g_ref[0]).astype(o_ref.dtype)


def _rwkv_post(yf, yb, bonus, g, ln_w, ln_b, L):
    B, T, A = bonus.shape
    tm = _row_tile(L, T - L)
    tile = pl.BlockSpec((1, tm, A), lambda b, i: (b, i, 0))
    vec = pl.BlockSpec((1, A), lambda b, i: (0, 0))
    return pl.pallas_call(
        _rwkv_post_kernel,
        grid=(B, T // tm),
        in_specs=[tile, tile, tile, tile, vec, vec, pl.BlockSpec((A, A), lambda b, i: (0, 0))],
        out_specs=tile,
        out_shape=jax.ShapeDtypeStruct((B, T, A), BF16),
        compiler_params=_params("arbitrary", "arbitrary"),
        name="rwkv_post",
    )(yf, yb, bonus, g, ln_w.reshape(1, A), ln_b.reshape(1, A), _head_ones(A))


def _rwkv_mixer(z, L, a_cols, mu, w0, w2, a0, a2, g2, k_k, k_a, r_k, ln_w, ln_b):
    r, v, kk, g, bonus, lw, beta, kd = _rwkv_prep(z, L, a_cols, mu, w0, w2, a0, a2, g2, k_k, k_a, r_k)
    yf, yb = _wkv_scan(r, v, kk, lw, beta, kd, L)
    return _rwkv_post(yf, yb, bonus, g, ln_w, ln_b, L)


def _router_kernel(x_ref, nw_ref, ss_ref, wr_ref, h_ref, aff_ref, *, L):
    i = pl.program_id(1)
    tm = x_ref.shape[1]
    h = _norm_mod(x_ref[0], nw_ref[...], _per_row(ss_ref, i, tm, L, 0), _per_row(ss_ref, i, tm, L, 1))
    h_ref[0] = h.astype(BF16)
    logits = _dot_nt(wr_ref[...], h, HIGHEST)
    m = jnp.max(logits, axis=0, keepdims=True)
    p = jnp.exp(logits - m)
    aff_ref[0] = p / jnp.sum(p, axis=0, keepdims=True)


def _norm_router(xa, nw, ss, w_router, L):
    B, T, D = xa.shape
    E = w_router.shape[1]
    tm = max(m for m in range(LANES, MAX_ROW_TILE + 1, LANES) if T % m == 0)
    return pl.pallas_call(
        functools.partial(_router_kernel, L=L),
        grid=(B, T // tm),
        in_specs=[pl.BlockSpec((1, tm, D), lambda b, i: (b, i, 0)),
                  pl.BlockSpec((1, D), lambda b, i: (0, 0)),
                  pl.BlockSpec((1, 2, 2, D), lambda b, i: (b, 0, 0, 0)),
                  pl.BlockSpec((E, D), lambda b, i: (0, 0))],
        out_specs=[pl.BlockSpec((1, tm, D), lambda b, i: (b, i, 0)),
                   pl.BlockSpec((1, E, tm), lambda b, i: (b, 0, i))],
        out_shape=[jax.ShapeDtypeStruct((B, T, D), BF16), jax.ShapeDtypeStruct((B, E, T), F32)],
        compiler_params=_params("arbitrary", "arbitrary"),
        name="norm_router",
    )(xa, nw.reshape(1, D), ss, w_router.T)


def _select_top(sets):
    keys = [pltpu.bitcast(aff, jnp.int32) for aff, _, _ in sets]
    E = keys[0].shape[0]

    def count_ge(key, thr):
        return jnp.sum(jnp.where(key >= thr, 1.0, 0.0), axis=1, keepdims=True)

    def narrow(_, carry):
        out = []
        for key, (_, cap, _), (lo, hi) in zip(keys, sets, carry):
            q = (hi - lo + 3) >> 2
            m1 = jnp.minimum(lo + q, hi)
            m2 = jnp.minimum(lo + 2 * q, hi)
            m3 = jnp.minimum(lo + 3 * q, hi)
            ok1, ok2, ok3 = count_ge(key, m1) >= cap, count_ge(key, m2) >= cap, count_ge(key, m3) >= cap
            new_lo = jnp.where(ok3, m3, jnp.where(ok2, m2, jnp.where(ok1, m1, lo)))
            new_hi = jnp.where(ok3, hi, jnp.where(ok2, m3 - 1, jnp.where(ok1, m2 - 1, m1 - 1)))
            out.append((new_lo, jnp.maximum(new_hi, new_lo)))
        return tuple(out)

    init = tuple((jnp.zeros((E, 1), jnp.int32), jnp.full((E, 1), 0x7F800000, jnp.int32)) for _ in sets)
    bounds = lax.fori_loop(0, 17, narrow, init)

    def prefix(x, tri):
        pb = tri.shape[0]
        parts, offset = [], jnp.zeros((E, 1), F32)
        for j in range(x.shape[1] // pb):
            xj = x[:, j * pb:(j + 1) * pb]
            inner = _dot(xj.astype(BF16), tri)
            parts.append(inner + offset)
            offset = offset + inner[:, pb - 1:pb] + xj[:, pb - 1:pb]
        return parts[0] if len(parts) == 1 else jnp.concatenate(parts, axis=1)

    res = []
    for key, (_, cap, tri), (thr, _) in zip(keys, sets, bounds):
        above = key > thr
        tie = key == thr
        need = cap - jnp.sum(jnp.where(above, 1.0, 0.0), axis=1, keepdims=True)
        tie_rank = prefix(jnp.where(tie, 1.0, 0.0), tri)
        sel = above | (tie & (tie_rank < need))
        res.append(jnp.where(sel, prefix(jnp.where(sel, 1.0, 0.0), tri), -1.0))
    return res


def _route_gather_kernel(aff_ref, h_ref, x_ref, pos_ref, gate_ref, slot_s, tri_s, *, L, cap_l, cap_c):
    b = pl.program_id(0)
    e = pl.program_id(1)
    T = h_ref.shape[1]
    S = T - L

    @pl.when((b == 0) & (e == 0))
    def _():
        n = tri_s.shape[0]
        tri_s[...] = jnp.where(lax.broadcasted_iota(jnp.int32, (n, n), 0) < lax.broadcasted_iota(jnp.int32, (n, n), 1),
                               1.0, 0.0).astype(BF16)

    @pl.when(e == 0)
    def _():
        sets = [(aff_ref[0, :, L:], cap_l, tri_s[...])]
        if cap_c:
            sets.append((aff_ref[0, :, 0:L], cap_c, tri_s[...]))
        picked = _select_top(sets)
        slot_s[:, L:] = picked[0]
        if cap_c:
            slot_s[:, 0:L] = picked[1]

    def gather(lo, n, cap, row0):
        slot = slot_s[pl.ds(e, 1), lo:lo + n]
        hit = slot == lax.broadcasted_iota(jnp.int32, (cap, n), 0).astype(F32)
        x_ref[0, 0, row0:row0 + cap, :] = _dot(jnp.where(hit, 1.0, 0.0).astype(BF16), h_ref[0, lo:lo + n, :]).astype(BF16)
        tok = lax.broadcasted_iota(jnp.int32, (cap, n), 1) + lo
        pos_ref[0, 0, row0:row0 + cap, :] = jnp.sum(jnp.where(hit, tok, 0), axis=1, keepdims=True)
        aff = aff_ref[0, pl.ds(e, 1), lo:lo + n]
        gate_ref[0, 0, row0:row0 + cap, :] = jnp.sum(jnp.where(hit, aff, 0.0), axis=1, keepdims=True)

    gather(L, S, cap_l, 0)
    if cap_c:
        gather(0, L, cap_c, cap_l)


def _route_gather(aff, h2, L, cap_l, cap_c):
    B, T, D = h2.shape
    E = aff.shape[1]
    Ct = cap_l + cap_c
    return pl.pallas_call(
        functools.partial(_route_gather_kernel, L=L, cap_l=cap_l, cap_c=cap_c),
        grid=(B, E),
        in_specs=[pl.BlockSpec((1, E, T), lambda b, e: (b, 0, 0)),
                  pl.BlockSpec((1, T, D), lambda b, e: (b, 0, 0))],
        out_specs=[pl.BlockSpec((1, 1, Ct, D), lambda b, e: (e, b, 0, 0)),
                   pl.BlockSpec((1, 1, Ct, 1), lambda b, e: (b, e, 0, 0)),
                   pl.BlockSpec((1, 1, Ct, 1), lambda b, e: (b, e, 0, 0))],
        out_shape=[jax.ShapeDtypeStruct((E, B, Ct, D), BF16), jax.ShapeDtypeStruct((B, E, Ct, 1), jnp.int32),
                   jax.ShapeDtypeStruct((B, E, Ct, 1), F32)],
        scratch_shapes=[pltpu.VMEM((E, T), F32), pltpu.VMEM((_row_tile(L, T - L),) * 2, BF16)],
        compiler_params=_params("arbitrary", "arbitrary"),
        name="moe_route_gather",
    )(aff, h2)


def _ffn_kernel(x_ref, w1_ref, w3_ref, w2_ref, o_ref, *, rm):
    j = pl.program_id(1)
    R = x_ref.shape[1]
    w1 = w1_ref[0, 0].astype(BF16)
    w3 = w3_ref[0, 0].astype(BF16)
    w2 = w2_ref[0, 0].astype(BF16)

    @pl.when(j == 0)
    def _():
        o_ref[...] = jnp.zeros(o_ref.shape, F32)

    def rows(i, carry):
        r0 = pl.multiple_of(i * rm, rm)
        x = x_ref[0, pl.ds(r0, rm), :]
        a = _dot(x, w1)
        b = _dot(x, w3)
        hid = (a * jax.nn.sigmoid(a) * b).astype(BF16)
        o_ref[0, pl.ds(r0, rm), :] += _dot(hid, w2)
        return carry

    lax.fori_loop(0, R // rm, rows, 0, unroll=True)


def _expert_ffn(xin, w1, w3, w2, layer):
    E, R, D = xin.shape
    F = w1.shape[-1]
    tf = min(512, F)
    rm = max(m for m in (MAX_ROW_TILE, 512, 256, 128, 64, 32, 16) if R % m == 0)
    return pl.pallas_call(
        functools.partial(_ffn_kernel, rm=rm),
        grid=(E, F // tf),
        in_specs=[pl.BlockSpec((1, R, D), lambda e, j: (e, 0, 0)),
                  pl.BlockSpec((1, 1, D, tf), lambda e, j: (layer, e, 0, j)),
                  pl.BlockSpec((1, 1, D, tf), lambda e, j: (layer, e, 0, j)),
                  pl.BlockSpec((1, 1, tf, D), lambda e, j: (layer, e, j, 0))],
        out_specs=pl.BlockSpec((1, R, D), lambda e, j: (e, 0, 0)),
        out_shape=jax.ShapeDtypeStruct((E, R, D), F32),
        compiler_params=_params("arbitrary", "arbitrary"),
        name="expert_ffn",
    )(xin, w1, w3, w2)


def _combine_kernel(y_ref, gate_ref, posl_ref, posc_ref, x_ref, g_ref, o_ref, yg_s, *, tq, nct, cap_l):
    E, _, Ct, td = y_ref.shape
    T = x_ref.shape[1]
    cap_c = Ct - cap_l
    nl = E * cap_l
    for e in range(E):
        yg_s[e * cap_l:(e + 1) * cap_l, :] = (y_ref[e, 0, :cap_l, :] * gate_ref[0, e, :cap_l, :]).astype(BF16)
        if cap_c:
            yg_s[nl + e * cap_c:nl + (e + 1) * cap_c, :] = (y_ref[e, 0, cap_l:, :] * gate_ref[0, e, cap_l:, :]).astype(BF16)

    def tile(i, gate_row, pos, lo, n):
        r0 = pl.multiple_of(i * tq, tq)
        tok = lax.broadcasted_iota(jnp.int32, (tq, n), 0) + r0
        onehot = jnp.where(tok == pos, 1.0, 0.0).astype(BF16)
        o_ref[0, pl.ds(r0, tq), :] = x_ref[0, pl.ds(r0, tq), :] + gate_row * _dot(onehot, yg_s[lo:lo + n, :])

    for i in range(nct):
        if cap_c:
            tile(i, g_ref[0, 0], posc_ref[0], nl, E * cap_c)
        else:
            o_ref[0, i * tq:(i + 1) * tq, :] = x_ref[0, i * tq:(i + 1) * tq, :]

    def body(i, carry):
        tile(i, g_ref[0, 1], posl_ref[0], 0, nl)
        return carry

    lax.fori_loop(nct, T // tq, body, 0, unroll=2)


def _moe_combine(y, gate, pos_l, pos_c, xa, g2, L, cap_l):
    E, B, Ct, D = y.shape
    T = xa.shape[1]
    td = min(512, D)
    tq = _row_tile(L, T - L)
    return pl.pallas_call(
        functools.partial(_combine_kernel, tq=tq, nct=L // tq, cap_l=cap_l),
        grid=(B, D // td),
        in_specs=[pl.BlockSpec((E, 1, Ct, td), lambda b, j: (0, b, 0, j)),
                  pl.BlockSpec((1, E, Ct, 1), lambda b, j: (b, 0, 0, 0)),
                  pl.BlockSpec((1, 1, pos_l.shape[-1]), lambda b, j: (b, 0, 0)),
                  pl.BlockSpec((1, 1, pos_c.shape[-1]), lambda b, j: (b, 0, 0)),
                  pl.BlockSpec((1, T, td), lambda b, j: (b, 0, j)),
                  pl.BlockSpec((1, 2, 1, td), lambda b, j: (b, 0, 0, j))],
        out_specs=pl.BlockSpec((1, T, td), lambda b, j: (b, 0, j)),
        out_shape=jax.ShapeDtypeStruct((B, T, D), F32),
        scratch_shapes=[pltpu.VMEM((E * Ct, td), BF16)],
        compiler_params=_params("arbitrary", "arbitrary"),
        name="moe_combine",
    )(y, gate, pos_l, pos_c, xa, g2)


def _moe(xa, nw, ss2, g2, w_router, w1, w3, w2, layer, L, need_ctx):
    B, T, D = xa.shape
    S = T - L
    E = w_router.shape[1]
    h2, aff = _norm_router(xa, nw, ss2, w_router, L)
    cap_l = CAPACITY_FACTOR * S // E
    cap_c = CAPACITY_FACTOR * L // E if need_ctx else 0
    Ct = cap_l + cap_c
    xin, pos, gate = _route_gather(aff, h2, L, cap_l, cap_c)
    pos_l = pos[:, :, :cap_l, 0].reshape(B, 1, E * cap_l)
    if cap_c:
        pos_c = pos[:, :, cap_l:, 0].reshape(B, 1, E * cap_c)
    else:
        pos_c = jnp.zeros((B, 1, LANES), jnp.int32)
    y = _expert_ffn(xin.reshape(E, B * Ct, D), w1, w3, w2, layer).reshape(E, B, Ct, D)
    return _moe_combine(y, gate, pos_l, pos_c, xa, g2, L, cap_l)


_PERM_EO = np.concatenate([np.arange(0, QK_ROPE, 2), np.arange(1, QK_ROPE, 2)])
_PERM_OE = np.concatenate([np.arange(1, QK_ROPE, 2), np.arange(0, QK_ROPE, 2)])


def _ab_input_weight(w_in, a_cols):
    D = w_in.shape[0]
    zr = w_in[:, a_cols + Q_RANK + KV_RANK:]
    zero = jnp.zeros((D, LANES - 2 * QK_ROPE), w_in.dtype)
    g1 = jnp.concatenate([zero, zr[:, _PERM_EO], zr[:, _PERM_EO]], axis=1)
    g2 = jnp.concatenate([zero, zr[:, _PERM_OE], zr[:, _PERM_OE]], axis=1)
    return jnp.concatenate([w_in[:, :a_cols + Q_RANK + KV_RANK], g1, g2], axis=1).astype(BF16)


def _mla_weights(w_qup, w_kvup):
    NH = w_qup.shape[1] // (QK_NOPE + QK_ROPE)
    wq = w_qup.reshape(Q_RANK, NH, QK_NOPE + QK_ROPE)
    rope = wq[:, :, QK_NOPE:]
    wq = jnp.concatenate([wq[:, :, :QK_NOPE], rope[:, :, _PERM_EO], rope[:, :, _PERM_OE]], axis=-1)
    wq = wq.reshape(Q_RANK, NH // 2, 2 * LANES).transpose(1, 0, 2)
    wkv = w_kvup.reshape(KV_RANK, NH, QK_NOPE + V_HEAD)
    wk = jnp.concatenate([wkv[:, :, :QK_NOPE], jnp.zeros((KV_RANK, NH, LANES - QK_NOPE), w_kvup.dtype)], axis=-1)
    wk = wk.reshape(KV_RANK, NH // 2, 2 * LANES).transpose(1, 0, 2)
    wv = wkv[:, :, QK_NOPE:].reshape(KV_RANK, NH // 2, 2 * V_HEAD).transpose(1, 0, 2)
    return wq.astype(BF16), wk.astype(BF16), wv.astype(BF16)


def _rope_tables(L, S):
    t = np.arange(S)
    row = (t // GRID_W).astype(np.float32)
    col = (t % GRID_W).astype(np.float32)
    n_freq = QK_ROPE // 4
    inv = (ROPE_BASE ** (-np.arange(n_freq, dtype=np.float32) / n_freq)).astype(np.float32)
    ang = jnp.concatenate([jnp.asarray(row[:, None] * inv), jnp.asarray(col[:, None] * inv)], axis=-1)
    cos = jnp.concatenate([jnp.ones((L, QK_ROPE // 2), F32), jnp.cos(ang)], axis=0)
    sin = jnp.concatenate([jnp.zeros((L, QK_ROPE // 2), F32), jnp.sin(ang)], axis=0)
    T = L + S
    cc = jnp.concatenate([cos, cos], axis=1)
    ss = jnp.concatenate([-sin, sin], axis=1)
    one = jnp.ones((T, LANES - 2 * QK_ROPE), F32)
    zero = jnp.zeros((T, LANES - 2 * QK_ROPE), F32)
    cq = jnp.concatenate([one, cc, ss], axis=1)
    ck = jnp.concatenate([zero, cc, cc], axis=1)
    sk = jnp.concatenate([zero, ss, ss], axis=1)
    return cq, ck, sk


def kernel(x, c, ctx, c_ctx, mod_w, mod_b, norm1_w, norm2_w, final_norm_w, ab_w_in, ab_w_out, rk_mu, rk_w0, rk_w2, rk_a0, rk_a2, rk_g2, rk_kk, rk_ka, rk_rk, rk_ln_w, rk_ln_b, mla_qn_w, mla_w_qup, mla_kvn_w, mla_w_kvup, na_w_qkv, na_rpb, na_w_out, moe_router, moe_w1, moe_w3, moe_w2):
    B, S, D = x.shape
    L = ctx.shape[1]
    depth = mod_w.shape[0]
    A = rk_w0.shape[-1]
    a_cols = rk_mu.shape[-1]

    rows_pad = -(B + 1) % 8
    cvec = jnp.concatenate([c, c_ctx[None], jnp.zeros((rows_pad, D), F32)], axis=0)
    mods = _mod_vectors(cvec, mod_w, mod_b)
    m_lat = mods[:, :B].reshape(depth, B, 6, D)
    m_ctx = jnp.broadcast_to(mods[:, B].reshape(depth, 1, 6, D), (depth, B, 6, D))
    mm = jnp.stack([m_ctx, m_lat], axis=2)

    cq, ck, sk = _rope_tables(L, S)
    xa = jnp.concatenate([ctx, x], axis=1)

    for layer in range(depth):
        need_ctx = layer < depth - 1
        i = layer // 2
        m = mm[layer]
        ss1, g1 = m[:, :, 0:2], m[:, :, 2:3]
        ss2, g2 = m[:, :, 3:5], m[:, :, 5:6]
        if layer % 2 == 0:
            w_in = _ab_input_weight(ab_w_in[i], a_cols)
            z = _norm_linear(xa, norm1_w[layer], ss1, w_in, L)
            o_a = _rwkv_mixer(z, L, a_cols, rk_mu[i], rk_w0[i], rk_w2[i], rk_a0[i], rk_a2[i], rk_g2[i],
                              rk_kk[i], rk_ka[i], rk_rk[i], rk_ln_w[i], rk_ln_b[i])
            wq, wk, wv = _mla_weights(mla_w_qup[i], mla_w_kvup[i])
            o_b = _mla_attention(z, mla_qn_w[i], mla_kvn_w[i], wq, wk, wv, cq, ck, sk, L,
                                 a_cols, a_cols + Q_RANK, a_cols + Q_RANK + KV_RANK)
            mixed, w_out = [o_a, o_b], ab_w_out[i]
        else:
            qkv = _norm_linear(xa, norm1_w[layer], ss1, na_w_qkv[i].astype(BF16), L, out_dtype=BF16)
            mixed, w_out = [_na_attention(qkv, _na_bias_table(na_rpb[i]), L, need_ctx)], na_w_out[i]
        xa = _linear_resid(mixed, w_out.astype(BF16), xa, g1, L)
        xa = _moe(xa, norm2_w[layer], ss2, g2, moe_router[layer], moe_w1, moe_w3, moe_w2, layer, L, need_ctx)
    return _final_norm(xa, final_norm_w, L)
```

```python
import functools

import jax
import jax.numpy as jnp
import numpy as np
from jax import lax
from jax.experimental import pallas as pl
from jax.experimental.pallas import tpu as pltpu

F32 = jnp.float32
BF16 = jnp.bfloat16
HIGHEST = lax.Precision.HIGHEST

GRID_W = 64
NORM_EPS = 1e-6
NEG_INF = -1e30
GN_EPS = 64e-5
A_HEAD_DIM = 64
LORA_W = 64
LORA_A = 64
LORA_G = 128
QK_NOPE = 64
QK_ROPE = 32
V_HEAD = 64
Q_RANK = 384
KV_RANK = 256
ROPE_BASE = 10000.0
C_HEAD_DIM = 64
WIN_R = 8
WIN_C = 16
N_EXPERTS = 16
CAPACITY_FACTOR = 2
SCAN_CHUNK = 64
DECAY_FLOOR_SCALE = float(np.exp(-0.5))
LANES = 128
MAX_ROW_TILE = 768

VMEM_LIMIT = 56 * 1024 * 1024


def _params(*sem):
    return pltpu.CompilerParams(dimension_semantics=sem, vmem_limit_bytes=VMEM_LIMIT)


def _dot(a, b, precision=None):
    return jnp.dot(a, b, preferred_element_type=F32, precision=precision)


def _dot_nt(a, b, precision=None):
    return lax.dot_general(a, b, (((1,), (1,)), ((), ())), preferred_element_type=F32, precision=precision)


def _row_tile(L, S):
    tm = 256
    while L % tm or S % tm:
        tm //= 2
    return tm


def _mod_kernel(c_ref, w_ref, b_ref, o_ref):
    c = c_ref[...]
    sc = c * jax.nn.sigmoid(c)
    o_ref[0] = _dot(sc.astype(BF16), w_ref[0].astype(BF16)) + b_ref[0]


def _mod_vectors(cvec, mod_w, mod_b):
    depth, D, N = mod_w.shape
    R = cvec.shape[0]
    tn = 1024
    return pl.pallas_call(
        _mod_kernel,
        grid=(depth, N // tn),
        in_specs=[pl.BlockSpec((R, D), lambda l, j: (0, 0)),
                  pl.BlockSpec((1, D, tn), lambda l, j: (l, 0, j)),
                  pl.BlockSpec((1, 1, tn), lambda l, j: (l, 0, j))],
        out_specs=pl.BlockSpec((1, R, tn), lambda l, j: (l, 0, j)),
        out_shape=jax.ShapeDtypeStruct((depth, R, N), F32),
        compiler_params=_params("arbitrary", "arbitrary"),
        name="mod_vectors",
    )(cvec, mod_w, mod_b.reshape(depth, 1, N))


def _wide_tile(T):
    return max(m for m in range(8, MAX_ROW_TILE + 1, 8) if T % m == 0)


def _per_row(mod_ref, i, tm, L, k):
    row = lax.broadcasted_iota(jnp.int32, (tm, 1), 0) + i * tm
    return jnp.where(row < L, mod_ref[0, 0, k:k + 1, :], mod_ref[0, 1, k:k + 1, :])


def _norm_mod(x, nw, shift, scale):
    y = x * lax.rsqrt(jnp.mean(x * x, axis=-1, keepdims=True) + NORM_EPS)
    y = y * nw
    return y * (1.0 + scale) + shift


def _norm_linear_kernel(x_ref, nw_ref, ss_ref, w_ref, o_ref, *, L):
    i = pl.program_id(1)
    tm = x_ref.shape[1]
    h = _norm_mod(x_ref[0], nw_ref[...], _per_row(ss_ref, i, tm, L, 0), _per_row(ss_ref, i, tm, L, 1))
    o_ref[0] = _dot(h.astype(BF16), w_ref[...]).astype(o_ref.dtype)


def _norm_linear(xa, nw, ss, w, L, out_dtype=F32):
    B, T, D = xa.shape
    N = w.shape[1]
    tm = _wide_tile(T)
    return pl.pallas_call(
        functools.partial(_norm_linear_kernel, L=L),
        grid=(B, T // tm),
        in_specs=[pl.BlockSpec((1, tm, D), lambda b, i: (b, i, 0)),
                  pl.BlockSpec((1, D), lambda b, i: (0, 0)),
                  pl.BlockSpec((1, 2, 2, D), lambda b, i: (b, 0, 0, 0)),
                  pl.BlockSpec((D, N), lambda b, i: (0, 0))],
        out_specs=pl.BlockSpec((1, tm, N), lambda b, i: (b, i, 0)),
        out_shape=jax.ShapeDtypeStruct((B, T, N), out_dtype),
        compiler_params=_params("arbitrary", "arbitrary"),
        name="norm_linear",
    )(xa, nw.reshape(1, D), ss, w)


def _linear_resid_kernel(*refs, ks, L):
    n = len(ks)
    a_refs, (w_ref, x_ref, g_ref, o_ref) = refs[:n], refs[n:]
    acc = None
    off = 0
    for a_ref, k in zip(a_refs, ks):
        part = _dot(a_ref[0].astype(BF16), w_ref[off:off + k, :])
        acc = part if acc is None else acc + part
        off += k
    o_ref[0] = x_ref[0] + _per_row(g_ref, pl.program_id(1), x_ref.shape[1], L, 0) * acc


def _linear_resid(a_list, w, xa, gate, L):
    B, T, D = xa.shape
    tm = _wide_tile(T)
    ks = tuple(a.shape[-1] for a in a_list)
    in_specs = [pl.BlockSpec((1, tm, k), lambda b, i: (b, i, 0)) for k in ks]
    in_specs += [pl.BlockSpec(w.shape, lambda b, i: (0, 0)),
                 pl.BlockSpec((1, tm, D), lambda b, i: (b, i, 0)),
                 pl.BlockSpec((1, 2, 1, D), lambda b, i: (b, 0, 0, 0))]
    return pl.pallas_call(
        functools.partial(_linear_resid_kernel, ks=ks, L=L),
        grid=(B, T // tm),
        in_specs=in_specs,
        out_specs=pl.BlockSpec((1, tm, D), lambda b, i: (b, i, 0)),
        out_shape=jax.ShapeDtypeStruct((B, T, D), F32),
        compiler_params=_params("arbitrary", "arbitrary"),
        name="linear_resid",
    )(*a_list, w, xa, gate)


def _rms_kernel(x_ref, w_ref, o_ref):
    x = x_ref[0]
    o_ref[0] = x * lax.rsqrt(jnp.mean(x * x, axis=-1, keepdims=True) + NORM_EPS) * w_ref[...]


def _final_norm(xa, w, L):
    B, T, D = xa.shape
    S = T - L
    tm = _row_tile(L, S)
    nct = L // tm
    return pl.pallas_call(
        _rms_kernel,
        grid=(B, S // tm),
        in_specs=[pl.BlockSpec((1, tm, D), lambda b, i: (b, i + nct, 0)),
                  pl.BlockSpec((1, D), lambda b, i: (0, 0))],
        out_specs=pl.BlockSpec((1, tm, D), lambda b, i: (b, i, 0)),
        out_shape=jax.ShapeDtypeStruct((B, S, D), F32),
        compiler_params=_params("arbitrary", "arbitrary"),
        name="final_norm",
    )(xa, w.reshape(1, D))


def _rms(x, w):
    return x * lax.rsqrt(jnp.mean(x * x, axis=-1, keepdims=True) + NORM_EPS) * w


def _softmax_pv(chains):
    m = [functools.reduce(jnp.maximum, [jnp.max(s, axis=-1, keepdims=True) for s, _ in ch]) for ch in chains]
    p = [[jnp.exp(s - mi) for s, _ in ch] for ch, mi in zip(chains, m)]
    l = [functools.reduce(jnp.add, [jnp.sum(x, axis=-1, keepdims=True) for x in pc]) for pc in p]
    o = [functools.reduce(jnp.add, [_dot(x.astype(BF16), v) for x, (_, v) in zip(pc, ch)]) for pc, ch in zip(p, chains)]
    return [oi / li for oi, li in zip(o, l)]


def _mla_kernel(zq_ref, zkv_ref, zr_ref, qn_ref, kvn_ref, wq_ref, wk_ref, wv_ref, cq_ref, ck_ref, sk_ref,
                o_ref, q_s, k_s, v_s, *, L, tq, scale):
    T = zq_ref.shape[1]
    zqn = _rms(zq_ref[0], qn_ref[...]).astype(BF16)
    zkvn = _rms(zkv_ref[0], kvn_ref[...]).astype(BF16)
    qh = _dot(zqn, wq_ref[0])
    kn = _dot(zkvn, wk_ref[0])
    v_s[...] = _dot(zkvn, wv_ref[0]).astype(BF16)
    zr = zr_ref[0]
    kr = zr[:, :LANES] * ck_ref[...] + zr[:, LANES:] * sk_ref[...]
    cq = cq_ref[...] * scale
    for h in range(2):
        q_s[h] = (qh[:, h * LANES:(h + 1) * LANES] * cq).astype(BF16)
        k_s[h] = (kn[:, h * LANES:(h + 1) * LANES] + kr).astype(BF16)
    def tile(row0, nk, rows):
        first_head = lax.broadcasted_iota(jnp.int32, (rows, LANES), 1) < V_HEAD
        s = [_dot_nt(q_s[h, pl.ds(row0, rows), :], k_s[h, 0:nk, :]) for h in range(2)]
        outs = _softmax_pv([[(si, v_s[0:nk, :])] for si in s])
        o_ref[0, pl.ds(row0, rows), :] = jnp.where(first_head, outs[0], outs[1]).astype(o_ref.dtype)

    for i in range(L // tq):
        tile(i * tq, L, tq)
    tl = 2 * tq if (T - L) % (2 * tq) == 0 else tq
    for i in range((T - L) // tl):
        tile(L + i * tl, T, tl)


def _mla_attention(z, qn_w, kvn_w, wq, wk, wv, cq, ck, sk, L, col_q, col_kv, col_r):
    B, T, _ = z.shape
    HP = wq.shape[0]
    tq = _row_tile(L, T - L)
    scale = float((QK_NOPE + QK_ROPE) ** -0.5)
    return pl.pallas_call(
        functools.partial(_mla_kernel, L=L, tq=tq, scale=scale),
        grid=(B, HP),
        in_specs=[pl.BlockSpec((1, T, Q_RANK), lambda b, p: (b, 0, col_q // Q_RANK)),
                  pl.BlockSpec((1, T, KV_RANK), lambda b, p: (b, 0, col_kv // KV_RANK)),
                  pl.BlockSpec((1, T, 2 * LANES), lambda b, p: (b, 0, col_r // (2 * LANES))),
                  pl.BlockSpec((1, Q_RANK), lambda b, p: (0, 0)),
                  pl.BlockSpec((1, KV_RANK), lambda b, p: (0, 0)),
                  pl.BlockSpec((1, Q_RANK, 2 * LANES), lambda b, p: (p, 0, 0)),
                  pl.BlockSpec((1, KV_RANK, 2 * LANES), lambda b, p: (p, 0, 0)),
                  pl.BlockSpec((1, KV_RANK, LANES), lambda b, p: (p, 0, 0)),
                  pl.BlockSpec((T, LANES), lambda b, p: (0, 0)),
                  pl.BlockSpec((T, LANES), lambda b, p: (0, 0)),
                  pl.BlockSpec((T, LANES), lambda b, p: (0, 0))],
        out_specs=pl.BlockSpec((1, T, LANES), lambda b, p: (b, 0, p)),
        out_shape=jax.ShapeDtypeStruct((B, T, HP * LANES), BF16),
        scratch_shapes=[pltpu.VMEM((2, T, LANES), BF16), pltpu.VMEM((2, T, LANES), BF16),
                        pltpu.VMEM((T, LANES), BF16)],
        compiler_params=_params("arbitrary", "arbitrary"),
        name="mla_attention",
    )(z, z, z, qn_w.reshape(1, -1), kvn_w.reshape(1, -1), wq, wk, wv, cq, ck, sk)


def _na_kernel(q_ref, k_ref, v_ref, bt_ref, o_ref, k_s, v_s, *, L, rows, kr, need_ctx, scale):
    W = GRID_W
    rpb = 4 if rows % 4 == 0 else 1
    k_s[...] = k_ref[0].astype(BF16)
    v_s[...] = v_ref[0].astype(BF16)
    nwin = kr * W
    lane = lax.broadcasted_iota(jnp.int32, (W, LANES), 1)
    head_mask = [(lane < C_HEAD_DIM).astype(F32), (lane >= C_HEAD_DIM).astype(F32)]
    first_head = lane < C_HEAD_DIM
    qcol = lax.broadcasted_iota(jnp.int32, (W, nwin), 0)
    kcol = lax.broadcasted_iota(jnp.int32, (W, nwin), 1) % W
    cstart = jnp.clip(qcol - WIN_C // 2, 0, W - WIN_C)
    col_valid = (kcol >= cstart) & (kcol < cstart + WIN_C)

    def row_block(rb, carry):
        q_blk = q_ref[0, pl.ds(pl.multiple_of(L + rb * (rpb * W), W), rpb * W), :] * scale
        s_ctx = [_dot_nt((q_blk * jnp.concatenate([head_mask[h]] * rpb, axis=0)).astype(BF16), k_s[0:L, :])
                 for h in range(2)]
        chains, q0s = [], []
        for j in range(rpb):
            r = rb * rpb + j
            rs = jnp.clip(r - kr // 2, 0, rows - kr)
            k0 = pl.multiple_of(L + rs * W, W)
            q0s.append(pl.multiple_of(L + r * W, W))
            q = q_blk[j * W:(j + 1) * W]
            kw = k_s[pl.ds(k0, nwin), :]
            vw = v_s[pl.ds(k0, nwin), :]
            dr0 = rs - r + (WIN_R - 1)
            for h in range(2):
                s_nb = _dot_nt((q * head_mask[h]).astype(BF16), kw)
                bias = jnp.concatenate([bt_ref[0, h, dr0 + 2 * m] for m in range(kr // 2)], axis=-1)
                s_nb = jnp.where(col_valid, s_nb + bias, NEG_INF)
                chains.append([(s_nb, vw), (s_ctx[h][j * W:(j + 1) * W], v_s[0:L, :])])
        outs = _softmax_pv(chains)
        for j in range(rpb):
            o_ref[0, pl.ds(q0s[j], W), :] = jnp.where(first_head, outs[2 * j], outs[2 * j + 1]).astype(o_ref.dtype)
        return carry

    lax.fori_loop(0, rows // rpb, row_block, 0, unroll=4)

    tq = min(L, 256)
    lane_c = lax.broadcasted_iota(jnp.int32, (tq, LANES), 1)
    for i in range(L // tq):
        if need_ctx:
            q = q_ref[0, i * tq:(i + 1) * tq, :] * scale
            hm = [lane_c < C_HEAD_DIM, lane_c >= C_HEAD_DIM]
            s = [_dot_nt(jnp.where(hm[h], q, 0.0).astype(BF16), k_s[0:L, :]) for h in range(2)]
            outs = _softmax_pv([[(si, v_s[0:L, :])] for si in s])
            o_ref[0, i * tq:(i + 1) * tq, :] = jnp.where(lane_c < C_HEAD_DIM, outs[0], outs[1]).astype(o_ref.dtype)
        else:
            o_ref[0, i * tq:(i + 1) * tq, :] = jnp.zeros((tq, LANES), o_ref.dtype)


def _na_attention(qkv, bias_tab, L, need_ctx):
    B, T, D3 = qkv.shape
    D = D3 // 3
    HP = D // LANES
    rows = (T - L) // GRID_W
    kr = min(WIN_R, rows)
    assert kr % 2 == 0
    nd = bias_tab.shape[2]
    return pl.pallas_call(
        functools.partial(_na_kernel, L=L, rows=rows, kr=kr, need_ctx=need_ctx, scale=float(C_HEAD_DIM ** -0.5)),
        grid=(B, HP),
        in_specs=[pl.BlockSpec((1, T, LANES), lambda b, p: (b, 0, p)),
                  pl.BlockSpec((1, T, LANES), lambda b, p: (b, 0, HP + p)),
                  pl.BlockSpec((1, T, LANES), lambda b, p: (b, 0, 2 * HP + p)),
                  pl.BlockSpec((1, 2, nd, GRID_W, LANES), lambda b, p: (p, 0, 0, 0, 0))],
        out_specs=pl.BlockSpec((1, T, LANES), lambda b, p: (b, 0, p)),
        out_shape=jax.ShapeDtypeStruct((B, T, D), BF16),
        scratch_shapes=[pltpu.VMEM((T, LANES), BF16), pltpu.VMEM((T, LANES), BF16)],
        compiler_params=_params("arbitrary", "arbitrary"),
        name="na_attention",
    )(qkv, qkv, qkv, bias_tab)


def _na_bias_table(rpb):
    H = rpb.shape[0]
    qc = np.arange(GRID_W)[:, None]
    kc = np.arange(GRID_W)[None, :]
    dc = np.clip(kc - qc + (WIN_C - 1), 0, 2 * WIN_C - 2)
    pick = jnp.asarray(np.arange(2 * WIN_C - 1)[:, None, None] == dc[None], dtype=F32)
    t = jnp.einsum('hdc,cqk->hdqk', rpb, pick, precision=HIGHEST)
    t2 = jnp.concatenate([t[:, :-1], t[:, 1:]], axis=-1)
    return t2.reshape(H // 2, 2, 2 * WIN_R - 2, GRID_W, 2 * GRID_W)


def _bf(x):
    return x.astype(BF16)


def _seg_sum(x, ones_bd):
    hi = _bf(x)
    lo = _bf(x - hi.astype(F32))
    return _dot(hi, ones_bd) + _dot(lo, ones_bd)


def _head_ones(A):
    seg = np.arange(A) // A_HEAD_DIM
    return jnp.asarray(seg[:, None] == seg[None, :], dtype=BF16)


def _rwkv_prep_kernel(z_ref, zp_ref, zn_ref, mu_ref, w0_ref, w2_ref, a0_ref, a2_ref, g2_ref, kk_ref, ka_ref, rk_ref,
                      ones_ref, r_o, v_o, kkn_o, g_o, bonus_o, lw_o, beta_o, kd_o, *, nct, nt, A):
    i = pl.program_id(1)
    za = z_ref[0]
    tm = za.shape[0]
    row = lax.broadcasted_iota(jnp.int32, za.shape, 0)
    seg_first = (i == 0) | (i == nct)
    seg_last = (i == nct - 1) | (i == nt - 1)
    prev_row = jnp.where(seg_first, 0.0, zp_ref[0, 7:8, :])
    next_row = jnp.where(seg_last, 0.0, zn_ref[0, 0:1, :])
    prev = jnp.where(row == 0, prev_row, pltpu.roll(za, 1, 0))
    nxt = jnp.where(row == tm - 1, next_row, pltpu.roll(za, tm - 1, 0))
    zs = za + mu_ref[0:1, :] * (prev - za) + mu_ref[1:2, :] * (nxt - za)
    r = zs[:, 0:A]
    k = zs[:, A:2 * A]
    v = zs[:, 2 * A:3 * A]
    wd = _bf(jnp.tanh(zs[:, 3 * A:3 * A + LANES]))
    ad = _bf(zs[:, 3 * A + LANES:3 * A + 2 * LANES])
    gd = _bf(jax.nn.sigmoid(zs[:, 3 * A + 2 * LANES:3 * A + 3 * LANES]))
    ones = ones_ref[...]
    kk = k * kk_ref[...]
    kkn = kk / jnp.maximum(jnp.sqrt(_seg_sum(kk * kk, ones)), 1e-12)
    kd_sum = None
    for d in range(2):
        u = w0_ref[d:d + 1, :] + _dot(wd, w2_ref[d])
        lw_o[d, 0] = -DECAY_FLOOR_SCALE * jax.nn.sigmoid(u)
        a = jax.nn.sigmoid(a0_ref[d:d + 1, :] + _dot(ad, a2_ref[d]))
        beta_o[d, 0] = kkn * a
        kd = k * (1.0 + (a - 1.0) * ka_ref[...])
        kd_o[d, 0] = kd
        kd_sum = kd if kd_sum is None else kd_sum + kd
    bonus_o[0] = _seg_sum(r * kd_sum * rk_ref[...], ones) * v
    r_o[0] = r
    v_o[0] = v
    kkn_o[0] = kkn
    g_o[0] = _dot(gd, g2_ref[...])


def _rwkv_prep(z, L, a_cols, mu, w0, w2, a0, a2, g2, k_k, k_a, r_k):
    B, T, _ = z.shape
    A = w0.shape[-1]
    assert 2 * LORA_W == LANES and 2 * LORA_A == LANES and LORA_G == LANES and a_cols == 3 * A + 3 * LANES
    tm = _row_tile(L, T - L)
    nt = T // tm
    hb = tm // 8

    def pad_lora(w):
        zero = jnp.zeros_like(w[0])
        return _bf(jnp.stack([jnp.concatenate([w[0], zero], 0), jnp.concatenate([zero, w[1]], 0)]))

    def const(shape):
        return pl.BlockSpec(shape, lambda b, i: (0,) * len(shape))

    tile = pl.BlockSpec((1, tm, A), lambda b, i: (b, i, 0))
    tile_d = pl.BlockSpec((2, 1, tm, A), lambda b, i: (0, b, i, 0))
    sd = jax.ShapeDtypeStruct((B, T, A), F32)
    sd_d = jax.ShapeDtypeStruct((2, B, T, A), F32)
    return pl.pallas_call(
        functools.partial(_rwkv_prep_kernel, nct=L // tm, nt=nt, A=A),
        grid=(B, nt),
        in_specs=[pl.BlockSpec((1, tm, a_cols), lambda b, i: (b, i, 0)),
                  pl.BlockSpec((1, 8, a_cols), lambda b, i: (b, jnp.maximum(i * hb - 1, 0), 0)),
                  pl.BlockSpec((1, 8, a_cols), lambda b, i: (b, jnp.minimum((i + 1) * hb, T // 8 - 1), 0)),
                  const((2, a_cols)), const((2, A)), const((2, 2 * LORA_W, A)), const((2, A)),
                  const((2, 2 * LORA_A, A)), const((LORA_G, A)), const((1, A)), const((1, A)), const((1, A)),
                  const((A, A))],
        out_specs=[tile, tile, tile, tile, tile, tile_d, tile_d, tile_d],
        out_shape=[sd, sd, sd, sd, sd, sd_d, sd_d, sd_d],
        compiler_params=_params("arbitrary", "arbitrary"),
        name="rwkv_prep",
    )(z, z, z, mu, w0, pad_lora(w2), a0, pad_lora(a2), _bf(g2), k_k.reshape(1, A), k_a.reshape(1, A),
      r_k.reshape(1, A), _head_ones(A))


def _tri_inverse(lms, eye, m16, m32):
    d0 = [_bf(jnp.where(m16, lm, 0.0)) for lm in lms]
    t = [eye + d.astype(F32) for d in d0]
    s = [_dot(d, d) for d in d0]
    for step in range(3):
        sb = [_bf(x) for x in s]
        t = [x + _dot(_bf(x), y) for x, y in zip(t, sb)]
        if step < 2:
            s = [_dot(y, y) for y in sb]
    for lvl in (m32 & (~m16), ~m32):
        tb = [_bf(x) for x in t]
        w = [_bf(_dot(_bf(jnp.where(lvl, lm, 0.0)), y)) for lm, y in zip(lms, tb)]
        t = [x + _dot(y, z) for x, y, z in zip(t, tb, w)]
    return t


def _wkv_kernel(*refs, NP, NB):
    C = SCAN_CHUNK
    P = 2 * C
    fwd_refs, bwd_refs, (yf_ref, yb_ref, h_s) = refs[0:6], refs[6:12], refs[12:]

    @pl.when(pl.program_id(1) == 0)
    def _():
        h_s[...] = jnp.zeros(h_s.shape, F32)

    ri = lax.broadcasted_iota(jnp.int32, (P, P), 0)
    ci = lax.broadcasted_iota(jnp.int32, (P, P), 1)
    same = (ri // C) == (ci // C)
    diff = (ri % C) - (ci % C)
    eye_b = ri == ci
    eye = eye_b.astype(F32)
    m16 = (ri // 16) == (ci // 16)
    m32 = (ri // 32) == (ci // 32)
    diff64 = lax.broadcasted_iota(jnp.int32, (C, C), 0) - lax.broadcasted_iota(jnp.int32, (C, C), 1)
    top = lax.broadcasted_iota(jnp.int32, (C, P), 1) < C
    zero_blk = jnp.zeros((P, P), BF16)
    sls = [slice(p * P, (p + 1) * P) for p in range(NP)]

    def bd(x):
        return [_bf(jnp.concatenate([jnp.where(top, x[:, sl], 0.0), jnp.where(top, 0.0, x[:, sl])], axis=0))
                for sl in sls]

    at, rt, bt, kt, bh, kh, vv, etots, before, before_eq = [], [], [], [], [], [], [], [], [], []
    for (r_ref, v_ref, kk_ref, lw_ref, beta_ref, kd_ref), sgn in ((fwd_refs, 1), (bwd_refs, -1)):
        order = diff * sgn
        tri = _bf(((diff64 * sgn) >= 0).astype(F32))
        for s in range(NB):
            before += [same & (order > 0)] * NP
            before_eq += [same & (order >= 0)] * NP
            lw = lw_ref[0, s]
            lw_hi = _bf(lw)
            lw_md = _bf(lw - lw_hi.astype(F32))
            lw_lo = _bf(lw - lw_hi.astype(F32) - lw_md.astype(F32))
            cum = _dot(tri, lw_hi) + _dot(tri, lw_md) + _dot(tri, lw_lo)
            tot = jnp.sum(lw, axis=0, keepdims=True)
            beta = beta_ref[0, s]
            kd = kd_ref[0, s]
            e_neg = jnp.exp(-cum)
            e_tail = jnp.exp(tot - cum)
            at += bd(-kk_ref[s] * jnp.exp(cum - lw))
            rt += bd(r_ref[s] * jnp.exp(cum))
            bt += bd(beta * e_neg)
            kt += bd(kd * e_neg)
            bh += bd(beta * e_tail)
            kh += bd(kd * e_tail)
            vv += bd(v_ref[s])
            etot = jnp.exp(tot)
            etots += [etot[:, sl] for sl in sls]

    ar = [jnp.concatenate([a, r], axis=0) for a, r in zip(at, rt)]
    arb = [_dot_nt(x, b) for x, b in zip(ar, bt)]
    ark = [_dot_nt(x, k) for x, k in zip(ar, kt)]
    lab = [jnp.where(m, x[:P], 0.0) for x, m in zip(arb, before)]
    tinv = _tri_inverse(lab, eye, m16, m32)
    u = [_dot(_bf(jnp.where(m, x[:P], 0.0)), v) for x, v, m in zip(ark, vv, before)]
    x = [_bf(_dot(_bf(t), jnp.concatenate([a, _bf(w)], axis=1))) for t, a, w in zip(tinv, at, u)]
    rhs = [jnp.concatenate([xi, jnp.concatenate([zero_blk, v], axis=1)], axis=0) for xi, v in zip(x, vv)]
    mn = [lax.dot_general(jnp.concatenate([b, k], axis=0), w, (((0,), (0,)), ((), ())), preferred_element_type=F32)
          for b, k, w in zip(bh, kh, rhs)]
    lr = [_bf(jnp.concatenate([jnp.where(m, xb[P:], 0.0), jnp.where(m, xk[P:], 0.0)], axis=1))
          for xb, xk, m in zip(arb, ark, before_eq)]
    qy = [_dot(l, w) for l, w in zip(lr, rhs)]
    qm = [_bf(jnp.concatenate([r.astype(F32) + q[:, :P], jnp.where(eye_b, e, 0.0) + m[:, :P]], axis=0))
          for r, q, m, e in zip(rt, qy, mn, etots)]
    nchain = 2 * NB * NP
    hin = [h_s[i] for i in range(nchain)]
    h_hi = [_bf(h) for h in hin]
    h_lo = [_bf(h - hh.astype(F32)) for h, hh in zip(hin, h_hi)]
    res = [_dot(w, hh) + _dot(w, hl) for w, hh, hl in zip(qm, h_hi, h_lo)]
    for i in range(nchain):
        ybd = res[i][:P] + qy[i][:, P:]
        y_ref = yf_ref if i < NB * NP else yb_ref
        y_ref[(i // NP) % NB, :, sls[i % NP]] = ybd[:C] + ybd[C:]
        h_s[i] = res[i][P:] + mn[i][:, P:]


def _wkv_scan(r, v, kk, lw, beta, kd, L):
    B, T, A = r.shape
    C = SCAN_CHUNK
    nC = T // C
    nct = L // C
    NP = A // (2 * C)

    def rev(c):
        return jnp.where(c < nct, nct - 1 - c, nC - 1 - (c - nct))

    NB = 2 if B % 2 == 0 else 1
    fwd = pl.BlockSpec((NB, C, A), lambda b, c: (b, c, 0))
    bwd = pl.BlockSpec((NB, C, A), lambda b, c: (b, rev(c), 0))
    fwd_d = pl.BlockSpec((1, NB, C, A), lambda b, c: (0, b, c, 0))
    bwd_d = pl.BlockSpec((1, NB, C, A), lambda b, c: (1, b, rev(c), 0))
    sd = jax.ShapeDtypeStruct((B, T, A), F32)
    return pl.pallas_call(
        functools.partial(_wkv_kernel, NP=NP, NB=NB),
        grid=(B // NB, nC),
        in_specs=[fwd, fwd, fwd, fwd_d, fwd_d, fwd_d, bwd, bwd, bwd, bwd_d, bwd_d, bwd_d],
        out_specs=[fwd, bwd],
        out_shape=[sd, sd],
        scratch_shapes=[pltpu.VMEM((2 * NB * NP, 2 * C, 2 * C), F32)],
        compiler_params=_params("arbitrary", "arbitrary"),
        name="wkv_scan",
    )(r, v, kk, lw, beta, kd, r, v, kk, lw, beta, kd)


def _rwkv_post_kernel(yf_ref, yb_ref, bonus_ref, g_ref, lnw_ref, lnb_ref, ones_ref, o_ref):
    y = yf_ref[0] + yb_ref[0]
    ones = ones_ref[...]
    inv_n = 1.0 / A_HEAD_DIM
    d = y - _seg_sum(y, ones) * inv_n
    var = _seg_sum(d * d, ones) * inv_n
    yn = d * lax.rsqrt(var + GN_EPS) * lnw_ref[...] + lnb_ref[...]
    o_ref[0] = ((yn + bonus_ref[0]) * g_ref[0]).astype(o_ref.dtype)


def _rwkv_post(yf, yb, bonus, g, ln_w, ln_b, L):
    B, T, A = bonus.shape
    tm = _wide_tile(T)
    tile = pl.BlockSpec((1, tm, A), lambda b, i: (b, i, 0))
    vec = pl.BlockSpec((1, A), lambda b, i: (0, 0))
    return pl.pallas_call(
        _rwkv_post_kernel,
        grid=(B, T // tm),
        in_specs=[tile, tile, tile, tile, vec, vec, pl.BlockSpec((A, A), lambda b, i: (0, 0))],
        out_specs=tile,
        out_shape=jax.ShapeDtypeStruct((B, T, A), BF16),
        compiler_params=_params("arbitrary", "arbitrary"),
        name="rwkv_post",
    )(yf, yb, bonus, g, ln_w.reshape(1, A), ln_b.reshape(1, A), _head_ones(A))


def _rwkv_mixer(z, L, a_cols, mu, w0, w2, a0, a2, g2, k_k, k_a, r_k, ln_w, ln_b):
    r, v, kk, g, bonus, lw, beta, kd = _rwkv_prep(z, L, a_cols, mu, w0, w2, a0, a2, g2, k_k, k_a, r_k)
    yf, yb = _wkv_scan(r, v, kk, lw, beta, kd, L)
    return _rwkv_post(yf, yb, bonus, g, ln_w, ln_b, L)


def _router_kernel(x_ref, nw_ref, ss_ref, wr_ref, h_ref, aff_ref, *, L):
    i = pl.program_id(1)
    tm = x_ref.shape[1]
    h = _norm_mod(x_ref[0], nw_ref[...], _per_row(ss_ref, i, tm, L, 0), _per_row(ss_ref, i, tm, L, 1))
    h_ref[0] = h.astype(BF16)
    logits = _dot_nt(wr_ref[...], h, HIGHEST)
    m = jnp.max(logits, axis=0, keepdims=True)
    p = jnp.exp(logits - m)
    aff_ref[0] = p / jnp.sum(p, axis=0, keepdims=True)


def _norm_router(xa, nw, ss, w_router, L):
    B, T, D = xa.shape
    E = w_router.shape[1]
    tm = max(m for m in range(LANES, MAX_ROW_TILE + 1, LANES) if T % m == 0)
    return pl.pallas_call(
        functools.partial(_router_kernel, L=L),
        grid=(B, T // tm),
        in_specs=[pl.BlockSpec((1, tm, D), lambda b, i: (b, i, 0)),
                  pl.BlockSpec((1, D), lambda b, i: (0, 0)),
                  pl.BlockSpec((1, 2, 2, D), lambda b, i: (b, 0, 0, 0)),
                  pl.BlockSpec((E, D), lambda b, i: (0, 0))],
        out_specs=[pl.BlockSpec((1, tm, D), lambda b, i: (b, i, 0)),
                   pl.BlockSpec((1, E, tm), lambda b, i: (b, 0, i))],
        out_shape=[jax.ShapeDtypeStruct((B, T, D), BF16), jax.ShapeDtypeStruct((B, E, T), F32)],
        compiler_params=_params("arbitrary", "arbitrary"),
        name="norm_router",
    )(xa, nw.reshape(1, D), ss, w_router.T)


def _select_top(sets):
    keys = [pltpu.bitcast(aff, jnp.int32) for aff, _, _ in sets]
    E = keys[0].shape[0]

    def count_ge(key, thr):
        return jnp.sum(jnp.where(key >= thr, 1.0, 0.0), axis=1, keepdims=True)

    def narrow(_, carry):
        out = []
        for key, (_, cap, _), (lo, hi) in zip(keys, sets, carry):
            q = (hi - lo + 3) >> 2
            m1 = jnp.minimum(lo + q, hi)
            m2 = jnp.minimum(lo + 2 * q, hi)
            m3 = jnp.minimum(lo + 3 * q, hi)
            ok1, ok2, ok3 = count_ge(key, m1) >= cap, count_ge(key, m2) >= cap, count_ge(key, m3) >= cap
            new_lo = jnp.where(ok3, m3, jnp.where(ok2, m2, jnp.where(ok1, m1, lo)))
            new_hi = jnp.where(ok3, hi, jnp.where(ok2, m3 - 1, jnp.where(ok1, m2 - 1, m1 - 1)))
            out.append((new_lo, jnp.maximum(new_hi, new_lo)))
        return tuple(out)

    init = tuple((jnp.zeros((E, 1), jnp.int32), jnp.full((E, 1), 0x7F800000, jnp.int32)) for _ in sets)
    bounds = lax.fori_loop(0, 17, narrow, init)

    def prefix(x, tri):
        pb = tri.shape[0]
        parts, offset = [], jnp.zeros((E, 1), F32)
        for j in range(x.shape[1] // pb):
            xj = x[:, j * pb:(j + 1) * pb]
            inner = _dot(xj.astype(BF16), tri)
            parts.append(inner + offset)
            offset = offset + inner[:, pb - 1:pb] + xj[:, pb - 1:pb]
        return parts[0] if len(parts) == 1 else jnp.concatenate(parts, axis=1)

    res = []
    for key, (_, cap, tri), (thr, _) in zip(keys, sets, bounds):
        above = key > thr
        tie = key == thr
        need = cap - jnp.sum(jnp.where(above, 1.0, 0.0), axis=1, keepdims=True)
        tie_rank = prefix(jnp.where(tie, 1.0, 0.0), tri)
        sel = above | (tie & (tie_rank < need))
        res.append(jnp.where(sel, prefix(jnp.where(sel, 1.0, 0.0), tri), -1.0))
    return res


def _route_gather_kernel(aff_ref, h_ref, x_ref, pos_ref, gate_ref, slot_s, tri_s, *, L, cap_l, cap_c):
    b = pl.program_id(0)
    e = pl.program_id(1)
    T = h_ref.shape[1]
    S = T - L

    @pl.when((b == 0) & (e == 0))
    def _():
        n = tri_s.shape[0]
        tri_s[...] = jnp.where(lax.broadcasted_iota(jnp.int32, (n, n), 0) < lax.broadcasted_iota(jnp.int32, (n, n), 1),
                               1.0, 0.0).astype(BF16)

    @pl.when(e == 0)
    def _():
        sets = [(aff_ref[0, :, L:], cap_l, tri_s[...])]
        if cap_c:
            sets.append((aff_ref[0, :, 0:L], cap_c, tri_s[...]))
        picked = _select_top(sets)
        slot_s[:, L:] = picked[0]
        if cap_c:
            slot_s[:, 0:L] = picked[1]

    def gather(lo, n, cap, row0):
        slot = slot_s[pl.ds(e, 1), lo:lo + n]
        hit = slot == lax.broadcasted_iota(jnp.int32, (cap, n), 0).astype(F32)
        x_ref[0, 0, row0:row0 + cap, :] = _dot(jnp.where(hit, 1.0, 0.0).astype(BF16), h_ref[0, lo:lo + n, :]).astype(BF16)
        tok = lax.broadcasted_iota(jnp.int32, (cap, n), 1) + lo
        pos_ref[0, 0, row0:row0 + cap, :] = jnp.sum(jnp.where(hit, tok, 0), axis=1, keepdims=True)
        aff = aff_ref[0, pl.ds(e, 1), lo:lo + n]
        gate_ref[0, 0, row0:row0 + cap, :] = jnp.sum(jnp.where(hit, aff, 0.0), axis=1, keepdims=True)

    gather(L, S, cap_l, 0)
    if cap_c:
        gather(0, L, cap_c, cap_l)


def _route_gather(aff, h2, L, cap_l, cap_c):
    B, T, D = h2.shape
    E = aff.shape[1]
    Ct = cap_l + cap_c
    return pl.pallas_call(
        functools.partial(_route_gather_kernel, L=L, cap_l=cap_l, cap_c=cap_c),
        grid=(B, E),
        in_specs=[pl.BlockSpec((1, E, T), lambda b, e: (b, 0, 0)),
                  pl.BlockSpec((1, T, D), lambda b, e: (b, 0, 0))],
        out_specs=[pl.BlockSpec((1, 1, Ct, D), lambda b, e: (e, b, 0, 0)),
                   pl.BlockSpec((1, 1, Ct, 1), lambda b, e: (b, e, 0, 0)),
                   pl.BlockSpec((1, 1, Ct, 1), lambda b, e: (b, e, 0, 0))],
        out_shape=[jax.ShapeDtypeStruct((E, B, Ct, D), BF16), jax.ShapeDtypeStruct((B, E, Ct, 1), jnp.int32),
                   jax.ShapeDtypeStruct((B, E, Ct, 1), F32)],
        scratch_shapes=[pltpu.VMEM((E, T), F32), pltpu.VMEM((_row_tile(L, T - L),) * 2, BF16)],
        compiler_params=_params("arbitrary", "arbitrary"),
        name="moe_route_gather",
    )(aff, h2)


def _ffn_kernel(x_ref, w1_ref, w3_ref, w2_ref, o_ref, *, rm):
    j = pl.program_id(1)
    R = x_ref.shape[1]
    w1 = w1_ref[0, 0].astype(BF16)
    w3 = w3_ref[0, 0].astype(BF16)
    w2 = w2_ref[0, 0].astype(BF16)

    @pl.when(j == 0)
    def _():
        o_ref[...] = jnp.zeros(o_ref.shape, F32)

    def rows(i, carry):
        r0 = pl.multiple_of(i * rm, rm)
        x = x_ref[0, pl.ds(r0, rm), :]
        a = _dot(x, w1)
        b = _dot(x, w3)
        hid = (a * jax.nn.sigmoid(a) * b).astype(BF16)
        o_ref[0, pl.ds(r0, rm), :] += _dot(hid, w2)
        return carry

    lax.fori_loop(0, R // rm, rows, 0, unroll=True)


def _expert_ffn(xin, w1, w3, w2, layer):
    E, R, D = xin.shape
    F = w1.shape[-1]
    tf = min(512, F)
    rm = max(m for m in (MAX_ROW_TILE, 512, 256, 128, 64, 32, 16) if R % m == 0)
    return pl.pallas_call(
        functools.partial(_ffn_kernel, rm=rm),
        grid=(E, F // tf),
        in_specs=[pl.BlockSpec((1, R, D), lambda e, j: (e, 0, 0)),
                  pl.BlockSpec((1, 1, D, tf), lambda e, j: (layer, e, 0, j)),
                  pl.BlockSpec((1, 1, D, tf), lambda e, j: (layer, e, 0, j)),
                  pl.BlockSpec((1, 1, tf, D), lambda e, j: (layer, e, j, 0))],
        out_specs=pl.BlockSpec((1, R, D), lambda e, j: (e, 0, 0)),
        out_shape=jax.ShapeDtypeStruct((E, R, D), F32),
        compiler_params=_params("arbitrary", "arbitrary"),
        name="expert_ffn",
    )(xin, w1, w3, w2)


def _combine_kernel(y_ref, gate_ref, posl_ref, posc_ref, x_ref, g_ref, o_ref, yg_s, *, tq, nct, cap_l):
    E, _, Ct, td = y_ref.shape
    T = x_ref.shape[1]
    cap_c = Ct - cap_l
    nl = E * cap_l
    for e in range(E):
        yg_s[e * cap_l:(e + 1) * cap_l, :] = (y_ref[e, 0, :cap_l, :] * gate_ref[0, e, :cap_l, :]).astype(BF16)
        if cap_c:
            yg_s[nl + e * cap_c:nl + (e + 1) * cap_c, :] = (y_ref[e, 0, cap_l:, :] * gate_ref[0, e, cap_l:, :]).astype(BF16)

    def tile(i, gate_row, pos, lo, n):
        r0 = pl.multiple_of(i * tq, tq)
        tok = lax.broadcasted_iota(jnp.int32, (tq, n), 0) + r0
        onehot = jnp.where(tok == pos, 1.0, 0.0).astype(BF16)
        o_ref[0, pl.ds(r0, tq), :] = x_ref[0, pl.ds(r0, tq), :] + gate_row * _dot(onehot, yg_s[lo:lo + n, :])

    for i in range(nct):
        if cap_c:
            tile(i, g_ref[0, 0], posc_ref[0], nl, E * cap_c)
        else:
            o_ref[0, i * tq:(i + 1) * tq, :] = x_ref[0, i * tq:(i + 1) * tq, :]

    def body(i, carry):
        tile(i, g_ref[0, 1], posl_ref[0], 0, nl)
        return carry

    lax.fori_loop(nct, T // tq, body, 0, unroll=2)


def _moe_combine(y, gate, pos_l, pos_c, xa, g2, L, cap_l):
    E, B, Ct, D = y.shape
    T = xa.shape[1]
    td = min(512, D)
    tq = _row_tile(L, T - L)
    return pl.pallas_call(
        functools.partial(_combine_kernel, tq=tq, nct=L // tq, cap_l=cap_l),
        grid=(B, D // td),
        in_specs=[pl.BlockSpec((E, 1, Ct, td), lambda b, j: (0, b, 0, j)),
                  pl.BlockSpec((1, E, Ct, 1), lambda b, j: (b, 0, 0, 0)),
                  pl.BlockSpec((1, 1, pos_l.shape[-1]), lambda b, j: (b, 0, 0)),
                  pl.BlockSpec((1, 1, pos_c.shape[-1]), lambda b, j: (b, 0, 0)),
                  pl.BlockSpec((1, T, td), lambda b, j: (b, 0, j)),
                  pl.BlockSpec((1, 2, 1, td), lambda b, j: (b, 0, 0, j))],
        out_specs=pl.BlockSpec((1, T, td), lambda b, j: (b, 0, j)),
        out_shape=jax.ShapeDtypeStruct((B, T, D), F32),
        scratch_shapes=[pltpu.VMEM((E * Ct, td), BF16)],
        compiler_params=_params("arbitrary", "arbitrary"),
        name="moe_combine",
    )(y, gate, pos_l, pos_c, xa, g2)


def _moe(xa, nw, ss2, g2, w_router, w1, w3, w2, layer, L, need_ctx):
    B, T, D = xa.shape
    S = T - L
    E = w_router.shape[1]
    h2, aff = _norm_router(xa, nw, ss2, w_router, L)
    cap_l = CAPACITY_FACTOR * S // E
    cap_c = CAPACITY_FACTOR * L // E if need_ctx else 0
    Ct = cap_l + cap_c
    xin, pos, gate = _route_gather(aff, h2, L, cap_l, cap_c)
    pos_l = pos[:, :, :cap_l, 0].reshape(B, 1, E * cap_l)
    if cap_c:
        pos_c = pos[:, :, cap_l:, 0].reshape(B, 1, E * cap_c)
    else:
        pos_c = jnp.zeros((B, 1, LANES), jnp.int32)
    y = _expert_ffn(xin.reshape(E, B * Ct, D), w1, w3, w2, layer).reshape(E, B, Ct, D)
    return _moe_combine(y, gate, pos_l, pos_c, xa, g2, L, cap_l)


_PERM_EO = np.concatenate([np.arange(0, QK_ROPE, 2), np.arange(1, QK_ROPE, 2)])
_PERM_OE = np.concatenate([np.arange(1, QK_ROPE, 2), np.arange(0, QK_ROPE, 2)])


def _ab_input_weight(w_in, a_cols):
    D = w_in.shape[0]
    zr = w_in[:, a_cols + Q_RANK + KV_RANK:]
    zero = jnp.zeros((D, LANES - 2 * QK_ROPE), w_in.dtype)
    g1 = jnp.concatenate([zero, zr[:, _PERM_EO], zr[:, _PERM_EO]], axis=1)
    g2 = jnp.concatenate([zero, zr[:, _PERM_OE], zr[:, _PERM_OE]], axis=1)
    return jnp.concatenate([w_in[:, :a_cols + Q_RANK + KV_RANK], g1, g2], axis=1).astype(BF16)


def _mla_weights(w_qup, w_kvup):
    NH = w_qup.shape[1] // (QK_NOPE + QK_ROPE)
    wq = w_qup.reshape(Q_RANK, NH, QK_NOPE + QK_ROPE)
    rope = wq[:, :, QK_NOPE:]
    wq = jnp.concatenate([wq[:, :, :QK_NOPE], rope[:, :, _PERM_EO], rope[:, :, _PERM_OE]], axis=-1)
    wq = wq.reshape(Q_RANK, NH // 2, 2 * LANES).transpose(1, 0, 2)
    wkv = w_kvup.reshape(KV_RANK, NH, QK_NOPE + V_HEAD)
    wk = jnp.concatenate([wkv[:, :, :QK_NOPE], jnp.zeros((KV_RANK, NH, LANES - QK_NOPE), w_kvup.dtype)], axis=-1)
    wk = wk.reshape(KV_RANK, NH // 2, 2 * LANES).transpose(1, 0, 2)
    wv = wkv[:, :, QK_NOPE:].reshape(KV_RANK, NH // 2, 2 * V_HEAD).transpose(1, 0, 2)
    return wq.astype(BF16), wk.astype(BF16), wv.astype(BF16)


def _rope_tables(L, S):
    t = np.arange(S)
    row = (t // GRID_W).astype(np.float32)
    col = (t % GRID_W).astype(np.float32)
    n_freq = QK_ROPE // 4
    inv = (ROPE_BASE ** (-np.arange(n_freq, dtype=np.float32) / n_freq)).astype(np.float32)
    ang = jnp.concatenate([jnp.asarray(row[:, None] * inv), jnp.asarray(col[:, None] * inv)], axis=-1)
    cos = jnp.concatenate([jnp.ones((L, QK_ROPE // 2), F32), jnp.cos(ang)], axis=0)
    sin = jnp.concatenate([jnp.zeros((L, QK_ROPE // 2), F32), jnp.sin(ang)], axis=0)
    T = L + S
    cc = jnp.concatenate([cos, cos], axis=1)
    ss = jnp.concatenate([-sin, sin], axis=1)
    one = jnp.ones((T, LANES - 2 * QK_ROPE), F32)
    zero = jnp.zeros((T, LANES - 2 * QK_ROPE), F32)
    cq = jnp.concatenate([one, cc, ss], axis=1)
    ck = jnp.concatenate([zero, cc, cc], axis=1)
    sk = jnp.concatenate([zero, ss, ss], axis=1)
    return cq, ck, sk


def kernel(x, c, ctx, c_ctx, mod_w, mod_b, norm1_w, norm2_w, final_norm_w, ab_w_in, ab_w_out, rk_mu, rk_w0, rk_w2, rk_a0, rk_a2, rk_g2, rk_kk, rk_ka, rk_rk, rk_ln_w, rk_ln_b, mla_qn_w, mla_w_qup, mla_kvn_w, mla_w_kvup, na_w_qkv, na_rpb, na_w_out, moe_router, moe_w1, moe_w3, moe_w2):
    B, S, D = x.shape
    L = ctx.shape[1]
    depth = mod_w.shape[0]
    A = rk_w0.shape[-1]
    a_cols = rk_mu.shape[-1]

    rows_pad = -(B + 1) % 8
    cvec = jnp.concatenate([c, c_ctx[None], jnp.zeros((rows_pad, D), F32)], axis=0)
    mods = _mod_vectors(cvec, mod_w, mod_b)
    m_lat = mods[:, :B].reshape(depth, B, 6, D)
    m_ctx = jnp.broadcast_to(mods[:, B].reshape(depth, 1, 6, D), (depth, B, 6, D))
    mm = jnp.stack([m_ctx, m_lat], axis=2)

    cq, ck, sk = _rope_tables(L, S)
    xa = jnp.concatenate([ctx, x], axis=1)

    for layer in range(depth):
        need_ctx = layer < depth - 1
        i = layer // 2
        m = mm[layer]
        ss1, g1 = m[:, :, 0:2], m[:, :, 2:3]
        ss2, g2 = m[:, :, 3:5], m[:, :, 5:6]
        if layer % 2 == 0:
            w_in = _ab_input_weight(ab_w_in[i], a_cols)
            z = _norm_linear(xa, norm1_w[layer], ss1, w_in, L)
            o_a = _rwkv_mixer(z, L, a_cols, rk_mu[i], rk_w0[i], rk_w2[i], rk_a0[i], rk_a2[i], rk_g2[i],
                              rk_kk[i], rk_ka[i], rk_rk[i], rk_ln_w[i], rk_ln_b[i])
            wq, wk, wv = _mla_weights(mla_w_qup[i], mla_w_kvup[i])
            o_b = _mla_attention(z, mla_qn_w[i], mla_kvn_w[i], wq, wk, wv, cq, ck, sk, L,
                                 a_cols, a_cols + Q_RANK, a_cols + Q_RANK + KV_RANK)
            mixed, w_out = [o_a, o_b], ab_w_out[i]
        else:
            qkv = _norm_linear(xa, norm1_w[layer], ss1, na_w_qkv[i].astype(BF16), L, out_dtype=BF16)
            mixed, w_out = [_na_attention(qkv, _na_bias_table(na_rpb[i]), L, need_ctx)], na_w_out[i]
        xa = _linear_resid(mixed, w_out.astype(BF16), xa, g1, L)
        xa = _moe(xa, norm2_w[layer], ss2, g2, moe_router[layer], moe_w1, moe_w3, moe_w2, layer, L, need_ctx)
    return _final_norm(xa, final_norm_w, L)
```

```python
import functools

import jax
import jax.numpy as jnp
import numpy as np
from jax import lax
from jax.experimental import pallas as pl
from jax.experimental.pallas import tpu as pltpu

F32 = jnp.float32
BF16 = jnp.bfloat16
HIGHEST = lax.Precision.HIGHEST

GRID_W = 64
NORM_EPS = 1e-6
NEG_INF = -1e30
GN_EPS = 64e-5
A_HEAD_DIM = 64
LORA_W = 64
LORA_A = 64
LORA_G = 128
QK_NOPE = 64
QK_ROPE = 32
V_HEAD = 64
Q_RANK = 384
KV_RANK = 256
ROPE_BASE = 10000.0
C_HEAD_DIM = 64
WIN_R = 8
WIN_C = 16
N_EXPERTS = 16
CAPACITY_FACTOR = 2
SCAN_CHUNK = 64
DECAY_FLOOR_SCALE = float(np.exp(-0.5))
LANES = 128
MAX_ROW_TILE = 768

VMEM_LIMIT = 56 * 1024 * 1024


def _params(*sem):
    return pltpu.CompilerParams(dimension_semantics=sem, vmem_limit_bytes=VMEM_LIMIT)


def _dot(a, b, precision=None):
    return jnp.dot(a, b, preferred_element_type=F32, precision=precision)


def _dot_nt(a, b, precision=None):
    return lax.dot_general(a, b, (((1,), (1,)), ((), ())), preferred_element_type=F32, precision=precision)


def _row_tile(L, S):
    tm = 256
    while L % tm or S % tm:
        tm //= 2
    return tm


def _mod_kernel(c_ref, w_ref, b_ref, o_ref):
    c = c_ref[...]
    sc = c * jax.nn.sigmoid(c)
    o_ref[0] = _dot(sc.astype(BF16), w_ref[0].astype(BF16)) + b_ref[0]


def _mod_vectors(cvec, mod_w, mod_b):
    depth, D, N = mod_w.shape
    R = cvec.shape[0]
    tn = 1024
    return pl.pallas_call(
        _mod_kernel,
        grid=(depth, N // tn),
        in_specs=[pl.BlockSpec((R, D), lambda l, j: (0, 0)),
                  pl.BlockSpec((1, D, tn), lambda l, j: (l, 0, j)),
                  pl.BlockSpec((1, 1, tn), lambda l, j: (l, 0, j))],
        out_specs=pl.BlockSpec((1, R, tn), lambda l, j: (l, 0, j)),
        out_shape=jax.ShapeDtypeStruct((depth, R, N), F32),
        compiler_params=_params("arbitrary", "arbitrary"),
        name="mod_vectors",
    )(cvec, mod_w, mod_b.reshape(depth, 1, N))


def _wide_tile(T):
    return max(m for m in range(8, MAX_ROW_TILE + 1, 8) if T % m == 0)


def _per_row(mod_ref, i, tm, L, k):
    row = lax.broadcasted_iota(jnp.int32, (tm, 1), 0) + i * tm
    return jnp.where(row < L, mod_ref[0, 0, k:k + 1, :], mod_ref[0, 1, k:k + 1, :])


def _norm_mod(x, nw, shift, scale):
    y = x * lax.rsqrt(jnp.mean(x * x, axis=-1, keepdims=True) + NORM_EPS)
    y = y * nw
    return y * (1.0 + scale) + shift


def _norm_linear_kernel(x_ref, nw_ref, ss_ref, w_ref, o_ref, *, L):
    i = pl.program_id(1)
    tm = x_ref.shape[1]
    h = _norm_mod(x_ref[0], nw_ref[...], _per_row(ss_ref, i, tm, L, 0), _per_row(ss_ref, i, tm, L, 1))
    o_ref[0] = _dot(h.astype(BF16), w_ref[...]).astype(o_ref.dtype)


def _norm_linear(xa, nw, ss, w, L, out_dtype=F32):
    B, T, D = xa.shape
    N = w.shape[1]
    tm = _wide_tile(T)
    return pl.pallas_call(
        functools.partial(_norm_linear_kernel, L=L),
        grid=(B, T // tm),
        in_specs=[pl.BlockSpec((1, tm, D), lambda b, i: (b, i, 0)),
                  pl.BlockSpec((1, D), lambda b, i: (0, 0)),
                  pl.BlockSpec((1, 2, 2, D), lambda b, i: (b, 0, 0, 0)),
                  pl.BlockSpec((D, N), lambda b, i: (0, 0))],
        out_specs=pl.BlockSpec((1, tm, N), lambda b, i: (b, i, 0)),
        out_shape=jax.ShapeDtypeStruct((B, T, N), out_dtype),
        compiler_params=_params("arbitrary", "arbitrary"),
        name="norm_linear",
    )(xa, nw.reshape(1, D), ss, w)


def _linear_resid_kernel(*refs, ks, L):
    n = len(ks)
    a_refs, (w_ref, x_ref, g_ref, o_ref) = refs[:n], refs[n:]
    acc = None
    off = 0
    for a_ref, k in zip(a_refs, ks):
        part = _dot(a_ref[0].astype(BF16), w_ref[off:off + k, :])
        acc = part if acc is None else acc + part
        off += k
    o_ref[0] = x_ref[0] + _per_row(g_ref, pl.program_id(1), x_ref.shape[1], L, 0) * acc


def _linear_resid(a_list, w, xa, gate, L):
    B, T, D = xa.shape
    tm = _wide_tile(T)
    ks = tuple(a.shape[-1] for a in a_list)
    in_specs = [pl.BlockSpec((1, tm, k), lambda b, i: (b, i, 0)) for k in ks]
    in_specs += [pl.BlockSpec(w.shape, lambda b, i: (0, 0)),
                 pl.BlockSpec((1, tm, D), lambda b, i: (b, i, 0)),
                 pl.BlockSpec((1, 2, 1, D), lambda b, i: (b, 0, 0, 0))]
    return pl.pallas_call(
        functools.partial(_linear_resid_kernel, ks=ks, L=L),
        grid=(B, T // tm),
        in_specs=in_specs,
        out_specs=pl.BlockSpec((1, tm, D), lambda b, i: (b, i, 0)),
        out_shape=jax.ShapeDtypeStruct((B, T, D), F32),
        compiler_params=_params("arbitrary", "arbitrary"),
        name="linear_resid",
    )(*a_list, w, xa, gate)


def _rms_kernel(x_ref, w_ref, o_ref):
    x = x_ref[...]
    o_ref[...] = x * lax.rsqrt(jnp.mean(x * x, axis=-1, keepdims=True) + NORM_EPS) * w_ref[...]


def _final_norm(xa, w, L):
    B, T, D = xa.shape
    S = T - L
    tm = _row_tile(L, S)
    nct = L // tm
    nb = max(n for n in (4, 2, 1) if B % n == 0)
    return pl.pallas_call(
        _rms_kernel,
        grid=(B // nb, S // tm),
        in_specs=[pl.BlockSpec((nb, tm, D), lambda b, i: (b, i + nct, 0)),
                  pl.BlockSpec((1, D), lambda b, i: (0, 0))],
        out_specs=pl.BlockSpec((nb, tm, D), lambda b, i: (b, i, 0)),
        out_shape=jax.ShapeDtypeStruct((B, S, D), F32),
        compiler_params=_params("arbitrary", "arbitrary"),
        name="final_norm",
    )(xa, w.reshape(1, D))


def _rms(x, w):
    return x * lax.rsqrt(jnp.mean(x * x, axis=-1, keepdims=True) + NORM_EPS) * w


def _softmax_pv(chains):
    m = [functools.reduce(jnp.maximum, [jnp.max(s, axis=-1, keepdims=True) for s, _ in ch]) for ch in chains]
    p = [[jnp.exp(s - mi) for s, _ in ch] for ch, mi in zip(chains, m)]
    l = [functools.reduce(jnp.add, [jnp.sum(x, axis=-1, keepdims=True) for x in pc]) for pc in p]
    o = [functools.reduce(jnp.add, [_dot(x.astype(BF16), v) for x, (_, v) in zip(pc, ch)]) for pc, ch in zip(p, chains)]
    return [oi / li for oi, li in zip(o, l)]


def _mla_kernel(zq_ref, zkv_ref, zr_ref, qn_ref, kvn_ref, wq_ref, wk_ref, wv_ref, cq_ref, ck_ref, sk_ref,
                o_ref, q_s, k_s, v_s, *, L, tq, scale):
    T = zq_ref.shape[1]
    zqn = _rms(zq_ref[0], qn_ref[...]).astype(BF16)
    zkvn = _rms(zkv_ref[0], kvn_ref[...]).astype(BF16)
    qh = _dot(zqn, wq_ref[0])
    kn = _dot(zkvn, wk_ref[0])
    v_s[...] = _dot(zkvn, wv_ref[0]).astype(BF16)
    zr = zr_ref[0]
    kr = zr[:, :LANES] * ck_ref[...] + zr[:, LANES:] * sk_ref[...]
    cq = cq_ref[...] * scale
    for h in range(2):
        q_s[h] = (qh[:, h * LANES:(h + 1) * LANES] * cq).astype(BF16)
        k_s[h] = (kn[:, h * LANES:(h + 1) * LANES] + kr).astype(BF16)
    def tile(row0, nk, rows):
        first_head = lax.broadcasted_iota(jnp.int32, (rows, LANES), 1) < V_HEAD
        s = [_dot_nt(q_s[h, pl.ds(row0, rows), :], k_s[h, 0:nk, :]) for h in range(2)]
        outs = _softmax_pv([[(si, v_s[0:nk, :])] for si in s])
        o_ref[0, pl.ds(row0, rows), :] = jnp.where(first_head, outs[0], outs[1]).astype(o_ref.dtype)

    for i in range(L // tq):
        tile(i * tq, L, tq)
    tl = 2 * tq if (T - L) % (2 * tq) == 0 else tq
    for i in range((T - L) // tl):
        tile(L + i * tl, T, tl)


def _mla_attention(z, qn_w, kvn_w, wq, wk, wv, cq, ck, sk, L, col_q, col_kv, col_r):
    B, T, _ = z.shape
    HP = wq.shape[0]
    tq = _row_tile(L, T - L)
    scale = float((QK_NOPE + QK_ROPE) ** -0.5)
    return pl.pallas_call(
        functools.partial(_mla_kernel, L=L, tq=tq, scale=scale),
        grid=(B, HP),
        in_specs=[pl.BlockSpec((1, T, Q_RANK), lambda b, p: (b, 0, col_q // Q_RANK)),
                  pl.BlockSpec((1, T, KV_RANK), lambda b, p: (b, 0, col_kv // KV_RANK)),
                  pl.BlockSpec((1, T, 2 * LANES), lambda b, p: (b, 0, col_r // (2 * LANES))),
                  pl.BlockSpec((1, Q_RANK), lambda b, p: (0, 0)),
                  pl.BlockSpec((1, KV_RANK), lambda b, p: (0, 0)),
                  pl.BlockSpec((1, Q_RANK, 2 * LANES), lambda b, p: (p, 0, 0)),
                  pl.BlockSpec((1, KV_RANK, 2 * LANES), lambda b, p: (p, 0, 0)),
                  pl.BlockSpec((1, KV_RANK, LANES), lambda b, p: (p, 0, 0)),
                  pl.BlockSpec((T, LANES), lambda b, p: (0, 0)),
                  pl.BlockSpec((T, LANES), lambda b, p: (0, 0)),
                  pl.BlockSpec((T, LANES), lambda b, p: (0, 0))],
        out_specs=pl.BlockSpec((1, T, LANES), lambda b, p: (b, 0, p)),
        out_shape=jax.ShapeDtypeStruct((B, T, HP * LANES), BF16),
        scratch_shapes=[pltpu.VMEM((2, T, LANES), BF16), pltpu.VMEM((2, T, LANES), BF16),
                        pltpu.VMEM((T, LANES), BF16)],
        compiler_params=_params("arbitrary", "arbitrary"),
        name="mla_attention",
    )(z, z, z, qn_w.reshape(1, -1), kvn_w.reshape(1, -1), wq, wk, wv, cq, ck, sk)


def _na_kernel(q_ref, k_ref, v_ref, bt_ref, o_ref, k_s, v_s, *, L, rows, kr, need_ctx, scale):
    W = GRID_W
    rpb = 4 if rows % 4 == 0 else 1
    k_s[...] = k_ref[0].astype(BF16)
    v_s[...] = v_ref[0].astype(BF16)
    nwin = kr * W
    lane = lax.broadcasted_iota(jnp.int32, (W, LANES), 1)
    head_mask = [(lane < C_HEAD_DIM).astype(F32), (lane >= C_HEAD_DIM).astype(F32)]
    first_head = lane < C_HEAD_DIM
    qcol = lax.broadcasted_iota(jnp.int32, (W, nwin), 0)
    kcol = lax.broadcasted_iota(jnp.int32, (W, nwin), 1) % W
    cstart = jnp.clip(qcol - WIN_C // 2, 0, W - WIN_C)
    col_valid = (kcol >= cstart) & (kcol < cstart + WIN_C)

    def row_block(rb, carry):
        q_blk = q_ref[0, pl.ds(pl.multiple_of(L + rb * (rpb * W), W), rpb * W), :] * scale
        s_ctx = [_dot_nt((q_blk * jnp.concatenate([head_mask[h]] * rpb, axis=0)).astype(BF16), k_s[0:L, :])
                 for h in range(2)]
        chains, q0s = [], []
        for j in range(rpb):
            r = rb * rpb + j
            rs = jnp.clip(r - kr // 2, 0, rows - kr)
            k0 = pl.multiple_of(L + rs * W, W)
            q0s.append(pl.multiple_of(L + r * W, W))
            q = q_blk[j * W:(j + 1) * W]
            kw = k_s[pl.ds(k0, nwin), :]
            vw = v_s[pl.ds(k0, nwin), :]
            dr0 = rs - r + (WIN_R - 1)
            for h in range(2):
                s_nb = _dot_nt((q * head_mask[h]).astype(BF16), kw)
                bias = jnp.concatenate([bt_ref[0, h, dr0 + 2 * m] for m in range(kr // 2)], axis=-1)
                s_nb = jnp.where(col_valid, s_nb + bias, NEG_INF)
                chains.append([(s_nb, vw), (s_ctx[h][j * W:(j + 1) * W], v_s[0:L, :])])
        outs = _softmax_pv(chains)
        for j in range(rpb):
            o_ref[0, pl.ds(q0s[j], W), :] = jnp.where(first_head, outs[2 * j], outs[2 * j + 1]).astype(o_ref.dtype)
        return carry

    lax.fori_loop(0, rows // rpb, row_block, 0, unroll=4)

    tq = min(L, 256)
    lane_c = lax.broadcasted_iota(jnp.int32, (tq, LANES), 1)
    for i in range(L // tq):
        if need_ctx:
            q = q_ref[0, i * tq:(i + 1) * tq, :] * scale
            hm = [lane_c < C_HEAD_DIM, lane_c >= C_HEAD_DIM]
            s = [_dot_nt(jnp.where(hm[h], q, 0.0).astype(BF16), k_s[0:L, :]) for h in range(2)]
            outs = _softmax_pv([[(si, v_s[0:L, :])] for si in s])
            o_ref[0, i * tq:(i + 1) * tq, :] = jnp.where(lane_c < C_HEAD_DIM, outs[0], outs[1]).astype(o_ref.dtype)
        else:
            o_ref[0, i * tq:(i + 1) * tq, :] = jnp.zeros((tq, LANES), o_ref.dtype)


def _na_attention(qkv, bias_tab, L, need_ctx):
    B, T, D3 = qkv.shape
    D = D3 // 3
    HP = D // LANES
    rows = (T - L) // GRID_W
    kr = min(WIN_R, rows)
    assert kr % 2 == 0
    nd = bias_tab.shape[2]
    return pl.pallas_call(
        functools.partial(_na_kernel, L=L, rows=rows, kr=kr, need_ctx=need_ctx, scale=float(C_HEAD_DIM ** -0.5)),
        grid=(B, HP),
        in_specs=[pl.BlockSpec((1, T, LANES), lambda b, p: (b, 0, p)),
                  pl.BlockSpec((1, T, LANES), lambda b, p: (b, 0, HP + p)),
                  pl.BlockSpec((1, T, LANES), lambda b, p: (b, 0, 2 * HP + p)),
                  pl.BlockSpec((1, 2, nd, GRID_W, LANES), lambda b, p: (p, 0, 0, 0, 0))],
        out_specs=pl.BlockSpec((1, T, LANES), lambda b, p: (b, 0, p)),
        out_shape=jax.ShapeDtypeStruct((B, T, D), BF16),
        scratch_shapes=[pltpu.VMEM((T, LANES), BF16), pltpu.VMEM((T, LANES), BF16)],
        compiler_params=_params("arbitrary", "arbitrary"),
        name="na_attention",
    )(qkv, qkv, qkv, bias_tab)


def _na_bias_table(rpb):
    H = rpb.shape[0]
    qc = np.arange(GRID_W)[:, None]
    kc = np.arange(GRID_W)[None, :]
    dc = np.clip(kc - qc + (WIN_C - 1), 0, 2 * WIN_C - 2)
    pick = jnp.asarray(np.arange(2 * WIN_C - 1)[:, None, None] == dc[None], dtype=F32)
    t = jnp.einsum('hdc,cqk->hdqk', rpb, pick, precision=HIGHEST)
    t2 = jnp.concatenate([t[:, :-1], t[:, 1:]], axis=-1)
    return t2.reshape(H // 2, 2, 2 * WIN_R - 2, GRID_W, 2 * GRID_W)


def _bf(x):
    return x.astype(BF16)


def _seg_sum(x, ones_bd):
    hi = _bf(x)
    lo = _bf(x - hi.astype(F32))
    return _dot(hi, ones_bd) + _dot(lo, ones_bd)


def _head_ones(A):
    seg = np.arange(A) // A_HEAD_DIM
    return jnp.asarray(seg[:, None] == seg[None, :], dtype=BF16)


def _rwkv_prep_kernel(z_ref, zp_ref, zn_ref, mu_ref, w0_ref, w2_ref, a0_ref, a2_ref, g2_ref, kk_ref, ka_ref, rk_ref,
                      ones_ref, r_o, v_o, kkn_o, g_o, bonus_o, lw_o, beta_o, kd_o, *, nct, nt, A):
    i = pl.program_id(1)
    za = z_ref[0]
    tm = za.shape[0]
    row = lax.broadcasted_iota(jnp.int32, za.shape, 0)
    seg_first = (i == 0) | (i == nct)
    seg_last = (i == nct - 1) | (i == nt - 1)
    prev_row = jnp.where(seg_first, 0.0, zp_ref[0, 7:8, :])
    next_row = jnp.where(seg_last, 0.0, zn_ref[0, 0:1, :])
    prev = jnp.where(row == 0, prev_row, pltpu.roll(za, 1, 0))
    nxt = jnp.where(row == tm - 1, next_row, pltpu.roll(za, tm - 1, 0))
    zs = za + mu_ref[0:1, :] * (prev - za) + mu_ref[1:2, :] * (nxt - za)
    r = zs[:, 0:A]
    k = zs[:, A:2 * A]
    v = zs[:, 2 * A:3 * A]
    wd = _bf(jnp.tanh(zs[:, 3 * A:3 * A + LANES]))
    ad = _bf(zs[:, 3 * A + LANES:3 * A + 2 * LANES])
    gd = _bf(jax.nn.sigmoid(zs[:, 3 * A + 2 * LANES:3 * A + 3 * LANES]))
    ones = ones_ref[...]
    kk = k * kk_ref[...]
    kkn = kk / jnp.maximum(jnp.sqrt(_seg_sum(kk * kk, ones)), 1e-12)
    kd_sum = None
    for d in range(2):
        u = w0_ref[d:d + 1, :] + _dot(wd, w2_ref[d])
        lw_o[d, 0] = -DECAY_FLOOR_SCALE * jax.nn.sigmoid(u)
        a = jax.nn.sigmoid(a0_ref[d:d + 1, :] + _dot(ad, a2_ref[d]))
        beta_o[d, 0] = kkn * a
        kd = k * (1.0 + (a - 1.0) * ka_ref[...])
        kd_o[d, 0] = kd
        kd_sum = kd if kd_sum is None else kd_sum + kd
    bonus_o[0] = _seg_sum(r * kd_sum * rk_ref[...], ones) * v
    r_o[0] = r
    v_o[0] = v
    kkn_o[0] = kkn
    g_o[0] = _dot(gd, g2_ref[...])


def _rwkv_prep(z, L, a_cols, mu, w0, w2, a0, a2, g2, k_k, k_a, r_k):
    B, T, _ = z.shape
    A = w0.shape[-1]
    assert 2 * LORA_W == LANES and 2 * LORA_A == LANES and LORA_G == LANES and a_cols == 3 * A + 3 * LANES
    tm = _row_tile(L, T - L)
    nt = T // tm
    hb = tm // 8

    def pad_lora(w):
        zero = jnp.zeros_like(w[0])
        return _bf(jnp.stack([jnp.concatenate([w[0], zero], 0), jnp.concatenate([zero, w[1]], 0)]))

    def const(shape):
        return pl.BlockSpec(shape, lambda b, i: (0,) * len(shape))

    tile = pl.BlockSpec((1, tm, A), lambda b, i: (b, i, 0))
    tile_d = pl.BlockSpec((2, 1, tm, A), lambda b, i: (0, b, i, 0))
    sd = jax.ShapeDtypeStruct((B, T, A), F32)
    sd_d = jax.ShapeDtypeStruct((2, B, T, A), F32)
    return pl.pallas_call(
        functools.partial(_rwkv_prep_kernel, nct=L // tm, nt=nt, A=A),
        grid=(B, nt),
        in_specs=[pl.BlockSpec((1, tm, a_cols), lambda b, i: (b, i, 0)),
                  pl.BlockSpec((1, 8, a_cols), lambda b, i: (b, jnp.maximum(i * hb - 1, 0), 0)),
                  pl.BlockSpec((1, 8, a_cols), lambda b, i: (b, jnp.minimum((i + 1) * hb, T // 8 - 1), 0)),
                  const((2, a_cols)), const((2, A)), const((2, 2 * LORA_W, A)), const((2, A)),
                  const((2, 2 * LORA_A, A)), const((LORA_G, A)), const((1, A)), const((1, A)), const((1, A)),
                  const((A, A))],
        out_specs=[tile, tile, tile, tile, tile, tile_d, tile_d, tile_d],
        out_shape=[sd, sd, sd, sd, sd, sd_d, sd_d, sd_d],
        compiler_params=_params("arbitrary", "arbitrary"),
        name="rwkv_prep",
    )(z, z, z, mu, w0, pad_lora(w2), a0, pad_lora(a2), _bf(g2), k_k.reshape(1, A), k_a.reshape(1, A),
      r_k.reshape(1, A), _head_ones(A))


def _tri_inverse(lms, eye, m16, m32):
    d0 = [_bf(jnp.where(m16, lm, 0.0)) for lm in lms]
    t = [eye + d.astype(F32) for d in d0]
    s = [_dot(d, d) for d in d0]
    for step in range(3):
        sb = [_bf(x) for x in s]
        t = [x + _dot(_bf(x), y) for x, y in zip(t, sb)]
        if step < 2:
            s = [_dot(y, y) for y in sb]
    for lvl in (m32 & (~m16), ~m32):
        tb = [_bf(x) for x in t]
        w = [_bf(_dot(_bf(jnp.where(lvl, lm, 0.0)), y)) for lm, y in zip(lms, tb)]
        t = [x + _dot(y, z) for x, y, z in zip(t, tb, w)]
    return t


def _wkv_kernel(*refs, NP, NB):
    C = SCAN_CHUNK
    P = 2 * C
    fwd_refs, bwd_refs, (yf_ref, yb_ref, h_s) = refs[0:6], refs[6:12], refs[12:]

    @pl.when(pl.program_id(1) == 0)
    def _():
        h_s[...] = jnp.zeros(h_s.shape, F32)

    ri = lax.broadcasted_iota(jnp.int32, (P, P), 0)
    ci = lax.broadcasted_iota(jnp.int32, (P, P), 1)
    same = (ri // C) == (ci // C)
    diff = (ri % C) - (ci % C)
    eye_b = ri == ci
    eye = eye_b.astype(F32)
    m16 = (ri // 16) == (ci // 16)
    m32 = (ri // 32) == (ci // 32)
    diff64 = lax.broadcasted_iota(jnp.int32, (C, C), 0) - lax.broadcasted_iota(jnp.int32, (C, C), 1)
    top = lax.broadcasted_iota(jnp.int32, (C, P), 1) < C
    zero_blk = jnp.zeros((P, P), BF16)
    sls = [slice(p * P, (p + 1) * P) for p in range(NP)]

    def bd(x):
        return [_bf(jnp.concatenate([jnp.where(top, x[:, sl], 0.0), jnp.where(top, 0.0, x[:, sl])], axis=0))
                for sl in sls]

    at, rt, bt, kt, bh, kh, vv, etots, before, before_eq = [], [], [], [], [], [], [], [], [], []
    for (r_ref, v_ref, kk_ref, lw_ref, beta_ref, kd_ref), sgn in ((fwd_refs, 1), (bwd_refs, -1)):
        order = diff * sgn
        tri = _bf(((diff64 * sgn) >= 0).astype(F32))
        for s in range(NB):
            before += [same & (order > 0)] * NP
            before_eq += [same & (order >= 0)] * NP
            lw = lw_ref[0, s]
            lw_hi = _bf(lw)
            lw_md = _bf(lw - lw_hi.astype(F32))
            lw_lo = _bf(lw - lw_hi.astype(F32) - lw_md.astype(F32))
            cum = _dot(tri, lw_hi) + _dot(tri, lw_md) + _dot(tri, lw_lo)
            tot = jnp.sum(lw, axis=0, keepdims=True)
            beta = beta_ref[0, s]
            kd = kd_ref[0, s]
            e_neg = jnp.exp(-cum)
            e_tail = jnp.exp(tot - cum)
            at += bd(-kk_ref[s] * jnp.exp(cum - lw))
            rt += bd(r_ref[s] * jnp.exp(cum))
            bt += bd(beta * e_neg)
            kt += bd(kd * e_neg)
            bh += bd(beta * e_tail)
            kh += bd(kd * e_tail)
            vv += bd(v_ref[s])
            etot = jnp.exp(tot)
            etots += [etot[:, sl] for sl in sls]

    ar = [jnp.concatenate([a, r], axis=0) for a, r in zip(at, rt)]
    arb = [_dot_nt(x, b) for x, b in zip(ar, bt)]
    ark = [_dot_nt(x, k) for x, k in zip(ar, kt)]
    lab = [jnp.where(m, x[:P], 0.0) for x, m in zip(arb, before)]
    tinv = _tri_inverse(lab, eye, m16, m32)
    u = [_dot(_bf(jnp.where(m, x[:P], 0.0)), v) for x, v, m in zip(ark, vv, before)]
    x = [_bf(_dot(_bf(t), jnp.concatenate([a, _bf(w)], axis=1))) for t, a, w in zip(tinv, at, u)]
    rhs = [jnp.concatenate([xi, jnp.concatenate([zero_blk, v], axis=1)], axis=0) for xi, v in zip(x, vv)]
    mn = [lax.dot_general(jnp.concatenate([b, k], axis=0), w, (((0,), (0,)), ((), ())), preferred_element_type=F32)
          for b, k, w in zip(bh, kh, rhs)]
    lr = [_bf(jnp.concatenate([jnp.where(m, xb[P:], 0.0), jnp.where(m, xk[P:], 0.0)], axis=1))
          for xb, xk, m in zip(arb, ark, before_eq)]
    qy = [_dot(l, w) for l, w in zip(lr, rhs)]
    qm = [_bf(jnp.concatenate([r.astype(F32) + q[:, :P], jnp.where(eye_b, e, 0.0) + m[:, :P]], axis=0))
          for r, q, m, e in zip(rt, qy, mn, etots)]
    nchain = 2 * NB * NP
    hin = [h_s[i] for i in range(nchain)]
    h_hi = [_bf(h) for h in hin]
    h_lo = [_bf(h - hh.astype(F32)) for h, hh in zip(hin, h_hi)]
    res = [_dot(w, hh) + _dot(w, hl) for w, hh, hl in zip(qm, h_hi, h_lo)]
    for i in range(nchain):
        ybd = res[i][:P] + qy[i][:, P:]
        y_ref = yf_ref if i < NB * NP else yb_ref
        y_ref[(i // NP) % NB, :, sls[i % NP]] = ybd[:C] + ybd[C:]
        h_s[i] = res[i][P:] + mn[i][:, P:]


def _wkv_scan(r, v, kk, lw, beta, kd, L):
    B, T, A = r.shape
    C = SCAN_CHUNK
    nC = T // C
    nct = L // C
    NP = A // (2 * C)

    def rev(c):
        return jnp.where(c < nct, nct - 1 - c, nC - 1 - (c - nct))

    NB = 2 if B % 2 == 0 else 1
    fwd = pl.BlockSpec((NB, C, A), lambda b, c: (b, c, 0))
    bwd = pl.BlockSpec((NB, C, A), lambda b, c: (b, rev(c), 0))
    fwd_d = pl.BlockSpec((1, NB, C, A), lambda b, c: (0, b, c, 0))
    bwd_d = pl.BlockSpec((1, NB, C, A), lambda b, c: (1, b, rev(c), 0))
    sd = jax.ShapeDtypeStruct((B, T, A), F32)
    return pl.pallas_call(
        functools.partial(_wkv_kernel, NP=NP, NB=NB),
        grid=(B // NB, nC),
        in_specs=[fwd, fwd, fwd, fwd_d, fwd_d, fwd_d, bwd, bwd, bwd, bwd_d, bwd_d, bwd_d],
        out_specs=[fwd, bwd],
        out_shape=[sd, sd],
        scratch_shapes=[pltpu.VMEM((2 * NB * NP, 2 * C, 2 * C), F32)],
        compiler_params=_params("arbitrary", "arbitrary"),
        name="wkv_scan",
    )(r, v, kk, lw, beta, kd, r, v, kk, lw, beta, kd)


def _rwkv_post_kernel(yf_ref, yb_ref, bonus_ref, g_ref, lnw_ref, lnb_ref, ones_ref, o_ref):
    y = yf_ref[0] + yb_ref[0]
    ones = ones_ref[...]
    inv_n = 1.0 / A_HEAD_DIM
    d = y - _seg_sum(y, ones) * inv_n
    var = _seg_sum(d * d, ones) * inv_n
    yn = d * lax.rsqrt(var + GN_EPS) * lnw_ref[...] + lnb_ref[...]
    o_ref[0] = ((yn + bonus_ref[0]) * g_ref[0]).astype(o_ref.dtype)


def _rwkv_post(yf, yb, bonus, g, ln_w, ln_b, L):
    B, T, A = bonus.shape
    tm = _wide_tile(T)
    tile = pl.BlockSpec((1, tm, A), lambda b, i: (b, i, 0))
    vec = pl.BlockSpec((1, A), lambda b, i: (0, 0))
    return pl.pallas_call(
        _rwkv_post_kernel,
        grid=(B, T // tm),
        in_specs=[tile, tile, tile, tile, vec, vec, pl.BlockSpec((A, A), lambda b, i: (0, 0))],
        out_specs=tile,
        out_shape=jax.ShapeDtypeStruct((B, T, A), BF16),
        compiler_params=_params("arbitrary", "arbitrary"),
        name="rwkv_post",
    )(yf, yb, bonus, g, ln_w.reshape(1, A), ln_b.reshape(1, A), _head_ones(A))


def _rwkv_mixer(z, L, a_cols, mu, w0, w2, a0, a2, g2, k_k, k_a, r_k, ln_w, ln_b):
    r, v, kk, g, bonus, lw, beta, kd = _rwkv_prep(z, L, a_cols, mu, w0, w2, a0, a2, g2, k_k, k_a, r_k)
    yf, yb = _wkv_scan(r, v, kk, lw, beta, kd, L)
    return _rwkv_post(yf, yb, bonus, g, ln_w, ln_b, L)


def _router_kernel(x_ref, nw_ref, ss_ref, wr_ref, h_ref, aff_ref, *, L):
    i = pl.program_id(1)
    tm = x_ref.shape[1]
    h = _norm_mod(x_ref[0], nw_ref[...], _per_row(ss_ref, i, tm, L, 0), _per_row(ss_ref, i, tm, L, 1))
    h_hi = h.astype(BF16)
    h_ref[0] = h_hi
    h_lo = _bf(h - h_hi.astype(F32))
    w = wr_ref[...]
    w_hi = _bf(w)
    w_lo = _bf(w - w_hi.astype(F32))
    logits = _dot_nt(w_hi, h_hi) + _dot_nt(w_hi, h_lo) + _dot_nt(w_lo, h_hi)
    m = jnp.max(logits, axis=0, keepdims=True)
    p = jnp.exp(logits - m)
    aff_ref[0] = p / jnp.sum(p, axis=0, keepdims=True)


def _norm_router(xa, nw, ss, w_router, L):
    B, T, D = xa.shape
    E = w_router.shape[1]
    tm = max(m for m in range(LANES, MAX_ROW_TILE + 1, LANES) if T % m == 0)
    return pl.pallas_call(
        functools.partial(_router_kernel, L=L),
        grid=(B, T // tm),
        in_specs=[pl.BlockSpec((1, tm, D), lambda b, i: (b, i, 0)),
                  pl.BlockSpec((1, D), lambda b, i: (0, 0)),
                  pl.BlockSpec((1, 2, 2, D), lambda b, i: (b, 0, 0, 0)),
                  pl.BlockSpec((E, D), lambda b, i: (0, 0))],
        out_specs=[pl.BlockSpec((1, tm, D), lambda b, i: (b, i, 0)),
                   pl.BlockSpec((1, E, tm), lambda b, i: (b, 0, i))],
        out_shape=[jax.ShapeDtypeStruct((B, T, D), BF16), jax.ShapeDtypeStruct((B, E, T), F32)],
        compiler_params=_params("arbitrary", "arbitrary"),
        name="norm_router",
    )(xa, nw.reshape(1, D), ss, w_router.T)


def _select_top(sets):
    keys = [pltpu.bitcast(aff, jnp.int32) for aff, _, _ in sets]
    E = keys[0].shape[0]

    def count_ge(key, thr):
        return jnp.sum(jnp.where(key >= thr, 1.0, 0.0), axis=1, keepdims=True)

    def narrow(_, carry):
        out = []
        for key, (_, cap, _), (lo, hi) in zip(keys, sets, carry):
            q = (hi - lo + 3) >> 2
            m1 = jnp.minimum(lo + q, hi)
            m2 = jnp.minimum(lo + 2 * q, hi)
            m3 = jnp.minimum(lo + 3 * q, hi)
            ok1, ok2, ok3 = count_ge(key, m1) >= cap, count_ge(key, m2) >= cap, count_ge(key, m3) >= cap
            new_lo = jnp.where(ok3, m3, jnp.where(ok2, m2, jnp.where(ok1, m1, lo)))
            new_hi = jnp.where(ok3, hi, jnp.where(ok2, m3 - 1, jnp.where(ok1, m2 - 1, m1 - 1)))
            out.append((new_lo, jnp.maximum(new_hi, new_lo)))
        return tuple(out)

    init = tuple((jnp.zeros((E, 1), jnp.int32), jnp.full((E, 1), 0x7F800000, jnp.int32)) for _ in sets)
    bounds = lax.fori_loop(0, 17, narrow, init)

    def prefix(x, tri):
        pb = tri.shape[0]
        parts, offset = [], jnp.zeros((E, 1), F32)
        for j in range(x.shape[1] // pb):
            xj = x[:, j * pb:(j + 1) * pb]
            inner = _dot(xj.astype(BF16), tri)
            parts.append(inner + offset)
            offset = offset + inner[:, pb - 1:pb] + xj[:, pb - 1:pb]
        return parts[0] if len(parts) == 1 else jnp.concatenate(parts, axis=1)

    res = []
    for key, (_, cap, tri), (thr, _) in zip(keys, sets, bounds):
        above = key > thr
        tie = key == thr
        need = cap - jnp.sum(jnp.where(above, 1.0, 0.0), axis=1, keepdims=True)
        tie_rank = prefix(jnp.where(tie, 1.0, 0.0), tri)
        sel = above | (tie & (tie_rank < need))
        res.append(jnp.where(sel, prefix(jnp.where(sel, 1.0, 0.0), tri), -1.0))
    return res


def _route_gather_kernel(aff_ref, h_ref, x_ref, pos_ref, gate_ref, slot_s, tri_s, *, L, cap_l, cap_c):
    b = pl.program_id(0)
    e = pl.program_id(1)
    T = h_ref.shape[1]
    S = T - L

    @pl.when((b == 0) & (e == 0))
    def _():
        n = tri_s.shape[0]
        tri_s[...] = jnp.where(lax.broadcasted_iota(jnp.int32, (n, n), 0) < lax.broadcasted_iota(jnp.int32, (n, n), 1),
                               1.0, 0.0).astype(BF16)

    @pl.when(e == 0)
    def _():
        sets = [(aff_ref[0, :, L:], cap_l, tri_s[...])]
        if cap_c:
            sets.append((aff_ref[0, :, 0:L], cap_c, tri_s[...]))
        picked = _select_top(sets)
        slot_s[:, L:] = picked[0]
        if cap_c:
            slot_s[:, 0:L] = picked[1]

    def gather(lo, n, cap, row0):
        slot = slot_s[pl.ds(e, 1), lo:lo + n]
        hit = slot == lax.broadcasted_iota(jnp.int32, (cap, n), 0).astype(F32)
        x_ref[0, 0, row0:row0 + cap, :] = _dot(jnp.where(hit, 1.0, 0.0).astype(BF16), h_ref[0, lo:lo + n, :]).astype(BF16)
        tok = lax.broadcasted_iota(jnp.int32, (cap, n), 1) + lo
        pos_ref[0, 0, row0:row0 + cap, :] = jnp.sum(jnp.where(hit, tok, 0), axis=1, keepdims=True)
        aff = aff_ref[0, pl.ds(e, 1), lo:lo + n]
        gate_ref[0, 0, row0:row0 + cap, :] = jnp.sum(jnp.where(hit, aff, 0.0), axis=1, keepdims=True)

    gather(L, S, cap_l, 0)
    if cap_c:
        gather(0, L, cap_c, cap_l)


def _route_gather(aff, h2, L, cap_l, cap_c):
    B, T, D = h2.shape
    E = aff.shape[1]
    Ct = cap_l + cap_c
    return pl.pallas_call(
        functools.partial(_route_gather_kernel, L=L, cap_l=cap_l, cap_c=cap_c),
        grid=(B, E),
        in_specs=[pl.BlockSpec((1, E, T), lambda b, e: (b, 0, 0)),
                  pl.BlockSpec((1, T, D), lambda b, e: (b, 0, 0))],
        out_specs=[pl.BlockSpec((1, 1, Ct, D), lambda b, e: (e, b, 0, 0)),
                   pl.BlockSpec((1, 1, Ct, 1), lambda b, e: (b, e, 0, 0)),
                   pl.BlockSpec((1, 1, Ct, 1), lambda b, e: (b, e, 0, 0))],
        out_shape=[jax.ShapeDtypeStruct((E, B, Ct, D), BF16), jax.ShapeDtypeStruct((B, E, Ct, 1), jnp.int32),
                   jax.ShapeDtypeStruct((B, E, Ct, 1), F32)],
        scratch_shapes=[pltpu.VMEM((E, T), F32), pltpu.VMEM((_row_tile(L, T - L),) * 2, BF16)],
        compiler_params=_params("arbitrary", "arbitrary"),
        name="moe_route_gather",
    )(aff, h2)


def _ffn_kernel(x_ref, w1_ref, w3_ref, w2_ref, o_ref, *, rm):
    j = pl.program_id(1)
    R = x_ref.shape[1]
    w1 = w1_ref[0, 0].astype(BF16)
    w3 = w3_ref[0, 0].astype(BF16)
    w2 = w2_ref[0, 0].astype(BF16)

    @pl.when(j == 0)
    def _():
        o_ref[...] = jnp.zeros(o_ref.shape, F32)

    def rows(i, carry):
        r0 = pl.multiple_of(i * rm, rm)
        x = x_ref[0, pl.ds(r0, rm), :]
        a = _dot(x, w1)
        b = _dot(x, w3)
        hid = (a * jax.nn.sigmoid(a) * b).astype(BF16)
        o_ref[0, pl.ds(r0, rm), :] += _dot(hid, w2)
        return carry

    lax.fori_loop(0, R // rm, rows, 0, unroll=True)


def _expert_ffn(xin, w1, w3, w2, layer):
    E, R, D = xin.shape
    F = w1.shape[-1]
    tf = min(512, F)
    rm = max(m for m in (MAX_ROW_TILE, 512, 256, 128, 64, 32, 16) if R % m == 0)
    return pl.pallas_call(
        functools.partial(_ffn_kernel, rm=rm),
        grid=(E, F // tf),
        in_specs=[pl.BlockSpec((1, R, D), lambda e, j: (e, 0, 0)),
                  pl.BlockSpec((1, 1, D, tf), lambda e, j: (layer, e, 0, j)),
                  pl.BlockSpec((1, 1, D, tf), lambda e, j: (layer, e, 0, j)),
                  pl.BlockSpec((1, 1, tf, D), lambda e, j: (layer, e, j, 0))],
        out_specs=pl.BlockSpec((1, R, D), lambda e, j: (e, 0, 0)),
        out_shape=jax.ShapeDtypeStruct((E, R, D), F32),
        compiler_params=_params("arbitrary", "arbitrary"),
        name="expert_ffn",
    )(xin, w1, w3, w2)


def _combine_kernel(y_ref, gate_ref, posl_ref, posc_ref, x_ref, g_ref, o_ref, yg_s, *, tq, nct, cap_l):
    E, _, Ct, td = y_ref.shape
    T = x_ref.shape[1]
    cap_c = Ct - cap_l
    nl = E * cap_l
    for e in range(E):
        yg_s[e * cap_l:(e + 1) * cap_l, :] = (y_ref[e, 0, :cap_l, :] * gate_ref[0, e, :cap_l, :]).astype(BF16)
        if cap_c:
            yg_s[nl + e * cap_c:nl + (e + 1) * cap_c, :] = (y_ref[e, 0, cap_l:, :] * gate_ref[0, e, cap_l:, :]).astype(BF16)

    def tile(i, gate_row, pos, lo, n):
        r0 = pl.multiple_of(i * tq, tq)
        tok = lax.broadcasted_iota(jnp.int32, (tq, n), 0) + r0
        onehot = jnp.where(tok == pos, 1.0, 0.0).astype(BF16)
        o_ref[0, pl.ds(r0, tq), :] = x_ref[0, pl.ds(r0, tq), :] + gate_row * _dot(onehot, yg_s[lo:lo + n, :])

    for i in range(nct):
        if cap_c:
            tile(i, g_ref[0, 0], posc_ref[0], nl, E * cap_c)
        else:
            o_ref[0, i * tq:(i + 1) * tq, :] = x_ref[0, i * tq:(i + 1) * tq, :]

    def body(i, carry):
        tile(i, g_ref[0, 1], posl_ref[0], 0, nl)
        return carry

    lax.fori_loop(nct, T // tq, body, 0, unroll=2)


def _moe_combine(y, gate, pos_l, pos_c, xa, g2, L, cap_l):
    E, B, Ct, D = y.shape
    T = xa.shape[1]
    td = min(512, D)
    tq = _row_tile(L, T - L)
    return pl.pallas_call(
        functools.partial(_combine_kernel, tq=tq, nct=L // tq, cap_l=cap_l),
        grid=(B, D // td),
        in_specs=[pl.BlockSpec((E, 1, Ct, td), lambda b, j: (0, b, 0, j)),
                  pl.BlockSpec((1, E, Ct, 1), lambda b, j: (b, 0, 0, 0)),
                  pl.BlockSpec((1, 1, pos_l.shape[-1]), lambda b, j: (b, 0, 0)),
                  pl.BlockSpec((1, 1, pos_c.shape[-1]), lambda b, j: (b, 0, 0)),
                  pl.BlockSpec((1, T, td), lambda b, j: (b, 0, j)),
                  pl.BlockSpec((1, 2, 1, td), lambda b, j: (b, 0, 0, j))],
        out_specs=pl.BlockSpec((1, T, td), lambda b, j: (b, 0, j)),
        out_shape=jax.ShapeDtypeStruct((B, T, D), F32),
        scratch_shapes=[pltpu.VMEM((E * Ct, td), BF16)],
        compiler_params=_params("arbitrary", "arbitrary"),
        name="moe_combine",
    )(y, gate, pos_l, pos_c, xa, g2)


def _moe(xa, nw, ss2, g2, w_router, w1, w3, w2, layer, L, need_ctx):
    B, T, D = xa.shape
    S = T - L
    E = w_router.shape[1]
    h2, aff = _norm_router(xa, nw, ss2, w_router, L)
    cap_l = CAPACITY_FACTOR * S // E
    cap_c = CAPACITY_FACTOR * L // E if need_ctx else 0
    Ct = cap_l + cap_c
    xin, pos, gate = _route_gather(aff, h2, L, cap_l, cap_c)
    pos_l = pos[:, :, :cap_l, 0].reshape(B, 1, E * cap_l)
    if cap_c:
        pos_c = pos[:, :, cap_l:, 0].reshape(B, 1, E * cap_c)
    else:
        pos_c = jnp.zeros((B, 1, LANES), jnp.int32)
    y = _expert_ffn(xin.reshape(E, B * Ct, D), w1, w3, w2, layer).reshape(E, B, Ct, D)
    return _moe_combine(y, gate, pos_l, pos_c, xa, g2, L, cap_l)


_PERM_EO = np.concatenate([np.arange(0, QK_ROPE, 2), np.arange(1, QK_ROPE, 2)])
_PERM_OE = np.concatenate([np.arange(1, QK_ROPE, 2), np.arange(0, QK_ROPE, 2)])


def _ab_input_weight(w_in, a_cols):
    D = w_in.shape[0]
    zr = w_in[:, a_cols + Q_RANK + KV_RANK:]
    zero = jnp.zeros((D, LANES - 2 * QK_ROPE), w_in.dtype)
    g1 = jnp.concatenate([zero, zr[:, _PERM_EO], zr[:, _PERM_EO]], axis=1)
    g2 = jnp.concatenate([zero, zr[:, _PERM_OE], zr[:, _PERM_OE]], axis=1)
    return jnp.concatenate([w_in[:, :a_cols + Q_RANK + KV_RANK], g1, g2], axis=1).astype(BF16)


def _mla_weights(w_qup, w_kvup):
    NH = w_qup.shape[1] // (QK_NOPE + QK_ROPE)
    wq = w_qup.reshape(Q_RANK, NH, QK_NOPE + QK_ROPE)
    rope = wq[:, :, QK_NOPE:]
    wq = jnp.concatenate([wq[:, :, :QK_NOPE], rope[:, :, _PERM_EO], rope[:, :, _PERM_OE]], axis=-1)
    wq = wq.reshape(Q_RANK, NH // 2, 2 * LANES).transpose(1, 0, 2)
    wkv = w_kvup.reshape(KV_RANK, NH, QK_NOPE + V_HEAD)
    wk = jnp.concatenate([wkv[:, :, :QK_NOPE], jnp.zeros((KV_RANK, NH, LANES - QK_NOPE), w_kvup.dtype)], axis=-1)
    wk = wk.reshape(KV_RANK, NH // 2, 2 * LANES).transpose(1, 0, 2)
    wv = wkv[:, :, QK_NOPE:].reshape(KV_RANK, NH // 2, 2 * V_HEAD).transpose(1, 0, 2)
    return wq.astype(BF16), wk.astype(BF16), wv.astype(BF16)


def _rope_tables(L, S):
    t = np.arange(S)
    row = (t // GRID_W).astype(np.float32)
    col = (t % GRID_W).astype(np.float32)
    n_freq = QK_ROPE // 4
    inv = (ROPE_BASE ** (-np.arange(n_freq, dtype=np.float32) / n_freq)).astype(np.float32)
    ang = jnp.concatenate([jnp.asarray(row[:, None] * inv), jnp.asarray(col[:, None] * inv)], axis=-1)
    cos = jnp.concatenate([jnp.ones((L, QK_ROPE // 2), F32), jnp.cos(ang)], axis=0)
    sin = jnp.concatenate([jnp.zeros((L, QK_ROPE // 2), F32), jnp.sin(ang)], axis=0)
    T = L + S
    cc = jnp.concatenate([cos, cos], axis=1)
    ss = jnp.concatenate([-sin, sin], axis=1)
    one = jnp.ones((T, LANES - 2 * QK_ROPE), F32)
    zero = jnp.zeros((T, LANES - 2 * QK_ROPE), F32)
    cq = jnp.concatenate([one, cc, ss], axis=1)
    ck = jnp.concatenate([zero, cc, cc], axis=1)
    sk = jnp.concatenate([zero, ss, ss], axis=1)
    return cq, ck, sk


def kernel(x, c, ctx, c_ctx, mod_w, mod_b, norm1_w, norm2_w, final_norm_w, ab_w_in, ab_w_out, rk_mu, rk_w0, rk_w2, rk_a0, rk_a2, rk_g2, rk_kk, rk_ka, rk_rk, rk_ln_w, rk_ln_b, mla_qn_w, mla_w_qup, mla_kvn_w, mla_w_kvup, na_w_qkv, na_rpb, na_w_out, moe_router, moe_w1, moe_w3, moe_w2):
    B, S, D = x.shape
    L = ctx.shape[1]
    depth = mod_w.shape[0]
    A = rk_w0.shape[-1]
    a_cols = rk_mu.shape[-1]

    rows_pad = -(B + 1) % 8
    cvec = jnp.concatenate([c, c_ctx[None], jnp.zeros((rows_pad, D), F32)], axis=0)
    mods = _mod_vectors(cvec, mod_w, mod_b)
    m_lat = mods[:, :B].reshape(depth, B, 6, D)
    m_ctx = jnp.broadcast_to(mods[:, B].reshape(depth, 1, 6, D), (depth, B, 6, D))
    mm = jnp.stack([m_ctx, m_lat], axis=2)

    cq, ck, sk = _rope_tables(L, S)
    xa = jnp.concatenate([ctx, x], axis=1)

    for layer in range(depth):
        need_ctx = layer < depth - 1
        i = layer // 2
        m = mm[layer]
        ss1, g1 = m[:, :, 0:2], m[:, :, 2:3]
        ss2, g2 = m[:, :, 3:5], m[:, :, 5:6]
        if layer % 2 == 0:
            w_in = _ab_input_weight(ab_w_in[i], a_cols)
            z = _norm_linear(xa, norm1_w[layer], ss1, w_in, L)
            o_a = _rwkv_mixer(z, L, a_cols, rk_mu[i], rk_w0[i], rk_w2[i], rk_a0[i], rk_a2[i], rk_g2[i],
                              rk_kk[i], rk_ka[i], rk_rk[i], rk_ln_w[i], rk_ln_b[i])
            wq, wk, wv = _mla_weights(mla_w_qup[i], mla_w_kvup[i])
            o_b = _mla_attention(z, mla_qn_w[i], mla_kvn_w[i], wq, wk, wv, cq, ck, sk, L,
                                 a_cols, a_cols + Q_RANK, a_cols + Q_RANK + KV_RANK)
            mixed, w_out = [o_a, o_b], ab_w_out[i]
        else:
            qkv = _norm_linear(xa, norm1_w[layer], ss1, na_w_qkv[i].astype(BF16), L, out_dtype=BF16)
            mixed, w_out = [_na_attention(qkv, _na_bias_table(na_rpb[i]), L, need_ctx)], na_w_out[i]
        xa = _linear_resid(mixed, w_out.astype(BF16), xa, g1, L)
        xa = _moe(xa, norm2_w[layer], ss2, g2, moe_router[layer], moe_w1, moe_w3, moe_w2, layer, L, need_ctx)
    return _final_norm(xa, final_norm_w, L)
```

```python
import functools

import jax
import jax.numpy as jnp
import numpy as np
from jax import lax
from jax.experimental import pallas as pl
from jax.experimental.pallas import tpu as pltpu

F32 = jnp.float32
BF16 = jnp.bfloat16
HIGHEST = lax.Precision.HIGHEST

GRID_W = 64
NORM_EPS = 1e-6
NEG_INF = -1e30
GN_EPS = 64e-5
A_HEAD_DIM = 64
LORA_W = 64
LORA_A = 64
LORA_G = 128
QK_NOPE = 64
QK_ROPE = 32
V_HEAD = 64
Q_RANK = 384
KV_RANK = 256
ROPE_BASE = 10000.0
C_HEAD_DIM = 64
WIN_R = 8
WIN_C = 16
N_EXPERTS = 16
CAPACITY_FACTOR = 2
SCAN_CHUNK = 64
DECAY_FLOOR_SCALE = float(np.exp(-0.5))
LANES = 128
MAX_ROW_TILE = 768

VMEM_LIMIT = 56 * 1024 * 1024


def _params(*sem):
    return pltpu.CompilerParams(dimension_semantics=sem, vmem_limit_bytes=VMEM_LIMIT)


def _dot(a, b, precision=None):
    return jnp.dot(a, b, preferred_element_type=F32, precision=precision)


def _dot_nt(a, b, precision=None):
    return lax.dot_general(a, b, (((1,), (1,)), ((), ())), preferred_element_type=F32, precision=precision)


def _row_tile(L, S):
    tm = 256
    while L % tm or S % tm:
        tm //= 2
    return tm


def _mod_kernel(c_ref, w_ref, b_ref, o_ref):
    c = c_ref[...]
    sc = c * jax.nn.sigmoid(c)
    o_ref[0] = _dot(sc.astype(BF16), w_ref[0].astype(BF16)) + b_ref[0]


def _mod_vectors(cvec, mod_w, mod_b):
    depth, D, N = mod_w.shape
    R = cvec.shape[0]
    tn = 1024
    return pl.pallas_call(
        _mod_kernel,
        grid=(depth, N // tn),
        in_specs=[pl.BlockSpec((R, D), lambda l, j: (0, 0)),
                  pl.BlockSpec((1, D, tn), lambda l, j: (l, 0, j)),
                  pl.BlockSpec((1, 1, tn), lambda l, j: (l, 0, j))],
        out_specs=pl.BlockSpec((1, R, tn), lambda l, j: (l, 0, j)),
        out_shape=jax.ShapeDtypeStruct((depth, R, N), F32),
        compiler_params=_params("arbitrary", "arbitrary"),
        name="mod_vectors",
    )(cvec, mod_w, mod_b.reshape(depth, 1, N))


def _wide_tile(T):
    return max(m for m in range(8, MAX_ROW_TILE + 1, 8) if T % m == 0)


def _per_row(mod_ref, i, tm, L, k):
    row = lax.broadcasted_iota(jnp.int32, (tm, 1), 0) + i * tm
    return jnp.where(row < L, mod_ref[0, 0, k:k + 1, :], mod_ref[0, 1, k:k + 1, :])


def _norm_mod(x, nw, shift, scale):
    y = x * lax.rsqrt(jnp.mean(x * x, axis=-1, keepdims=True) + NORM_EPS)
    y = y * nw
    return y * (1.0 + scale) + shift


def _norm_linear_kernel(x_ref, nw_ref, ss_ref, w_ref, o_ref, *, L):
    i = pl.program_id(1)
    tm = x_ref.shape[1]
    h = _norm_mod(x_ref[0], nw_ref[...], _per_row(ss_ref, i, tm, L, 0), _per_row(ss_ref, i, tm, L, 1))
    o_ref[0] = _dot(h.astype(BF16), w_ref[...]).astype(o_ref.dtype)


def _norm_linear(xa, nw, ss, w, L, out_dtype=F32):
    B, T, D = xa.shape
    N = w.shape[1]
    tm = _wide_tile(T)
    return pl.pallas_call(
        functools.partial(_norm_linear_kernel, L=L),
        grid=(B, T // tm),
        in_specs=[pl.BlockSpec((1, tm, D), lambda b, i: (b, i, 0)),
                  pl.BlockSpec((1, D), lambda b, i: (0, 0)),
                  pl.BlockSpec((1, 2, 2, D), lambda b, i: (b, 0, 0, 0)),
                  pl.BlockSpec((D, N), lambda b, i: (0, 0))],
        out_specs=pl.BlockSpec((1, tm, N), lambda b, i: (b, i, 0)),
        out_shape=jax.ShapeDtypeStruct((B, T, N), out_dtype),
        compiler_params=_params("arbitrary", "arbitrary"),
        name="norm_linear",
    )(xa, nw.reshape(1, D), ss, w)


def _linear_resid_kernel(*refs, ks, L):
    n = len(ks)
    a_refs, (w_ref, x_ref, g_ref, o_ref) = refs[:n], refs[n:]
    acc = None
    off = 0
    for a_ref, k in zip(a_refs, ks):
        part = _dot(a_ref[0].astype(BF16), w_ref[off:off + k, :])
        acc = part if acc is None else acc + part
        off += k
    o_ref[0] = x_ref[0] + _per_row(g_ref, pl.program_id(1), x_ref.shape[1], L, 0) * acc


def _linear_resid(a_list, w, xa, gate, L):
    B, T, D = xa.shape
    tm = _wide_tile(T)
    ks = tuple(a.shape[-1] for a in a_list)
    in_specs = [pl.BlockSpec((1, tm, k), lambda b, i: (b, i, 0)) for k in ks]
    in_specs += [pl.BlockSpec(w.shape, lambda b, i: (0, 0)),
                 pl.BlockSpec((1, tm, D), lambda b, i: (b, i, 0)),
                 pl.BlockSpec((1, 2, 1, D), lambda b, i: (b, 0, 0, 0))]
    return pl.pallas_call(
        functools.partial(_linear_resid_kernel, ks=ks, L=L),
        grid=(B, T // tm),
        in_specs=in_specs,
        out_specs=pl.BlockSpec((1, tm, D), lambda b, i: (b, i, 0)),
        out_shape=jax.ShapeDtypeStruct((B, T, D), F32),
        compiler_params=_params("arbitrary", "arbitrary"),
        name="linear_resid",
    )(*a_list, w, xa, gate)


def _rms_kernel(x_ref, w_ref, o_ref):
    x = x_ref[...]
    o_ref[...] = x * lax.rsqrt(jnp.mean(x * x, axis=-1, keepdims=True) + NORM_EPS) * w_ref[...]


def _final_norm(xa, w, L):
    B, T, D = xa.shape
    S = T - L
    tm = _row_tile(L, S)
    nct = L // tm
    nb = max(n for n in (4, 2, 1) if B % n == 0)
    return pl.pallas_call(
        _rms_kernel,
        grid=(B // nb, S // tm),
        in_specs=[pl.BlockSpec((nb, tm, D), lambda b, i: (b, i + nct, 0)),
                  pl.BlockSpec((1, D), lambda b, i: (0, 0))],
        out_specs=pl.BlockSpec((nb, tm, D), lambda b, i: (b, i, 0)),
        out_shape=jax.ShapeDtypeStruct((B, S, D), F32),
        compiler_params=_params("arbitrary", "arbitrary"),
        name="final_norm",
    )(xa, w.reshape(1, D))


def _rms(x, w):
    return x * lax.rsqrt(jnp.mean(x * x, axis=-1, keepdims=True) + NORM_EPS) * w


def _softmax_pv(chains):
    m = [functools.reduce(jnp.maximum, [jnp.max(s, axis=-1, keepdims=True) for s, _ in ch]) for ch in chains]
    p = [[jnp.exp(s - mi) for s, _ in ch] for ch, mi in zip(chains, m)]
    l = [functools.reduce(jnp.add, [jnp.sum(x, axis=-1, keepdims=True) for x in pc]) for pc in p]
    o = [functools.reduce(jnp.add, [_dot(x.astype(BF16), v) for x, (_, v) in zip(pc, ch)]) for pc, ch in zip(p, chains)]
    return [oi / li for oi, li in zip(o, l)]


def _mla_kernel(zq_ref, zkv_ref, zr_ref, qn_ref, kvn_ref, wq_ref, wk_ref, wv_ref, cq_ref, ck_ref, sk_ref,
                o_ref, q_s, k_s, v_s, *, L, tq, scale):
    T = zq_ref.shape[1]
    zqn = _rms(zq_ref[0], qn_ref[...]).astype(BF16)
    zkvn = _rms(zkv_ref[0], kvn_ref[...]).astype(BF16)
    qh = _dot(zqn, wq_ref[0])
    kn = _dot(zkvn, wk_ref[0])
    v_s[...] = _dot(zkvn, wv_ref[0]).astype(BF16)
    zr = zr_ref[0]
    kr = zr[:, :LANES] * ck_ref[...] + zr[:, LANES:] * sk_ref[...]
    cq = cq_ref[...] * scale
    for h in range(2):
        q_s[h] = (qh[:, h * LANES:(h + 1) * LANES] * cq).astype(BF16)
        k_s[h] = (kn[:, h * LANES:(h + 1) * LANES] + kr).astype(BF16)
    def tile(row0, nk, rows):
        first_head = lax.broadcasted_iota(jnp.int32, (rows, LANES), 1) < V_HEAD
        s = [_dot_nt(q_s[h, pl.ds(row0, rows), :], k_s[h, 0:nk, :]) for h in range(2)]
        outs = _softmax_pv([[(si, v_s[0:nk, :])] for si in s])
        o_ref[0, pl.ds(row0, rows), :] = jnp.where(first_head, outs[0], outs[1]).astype(o_ref.dtype)

    for i in range(L // tq):
        tile(i * tq, L, tq)
    tl = 2 * tq if (T - L) % (2 * tq) == 0 else tq
    for i in range((T - L) // tl):
        tile(L + i * tl, T, tl)


def _mla_attention(z, qn_w, kvn_w, wq, wk, wv, cq, ck, sk, L, col_q, col_kv, col_r):
    B, T, _ = z.shape
    HP = wq.shape[0]
    tq = _row_tile(L, T - L)
    scale = float((QK_NOPE + QK_ROPE) ** -0.5)
    return pl.pallas_call(
        functools.partial(_mla_kernel, L=L, tq=tq, scale=scale),
        grid=(B, HP),
        in_specs=[pl.BlockSpec((1, T, Q_RANK), lambda b, p: (b, 0, col_q // Q_RANK)),
                  pl.BlockSpec((1, T, KV_RANK), lambda b, p: (b, 0, col_kv // KV_RANK)),
                  pl.BlockSpec((1, T, 2 * LANES), lambda b, p: (b, 0, col_r // (2 * LANES))),
                  pl.BlockSpec((1, Q_RANK), lambda b, p: (0, 0)),
                  pl.BlockSpec((1, KV_RANK), lambda b, p: (0, 0)),
                  pl.BlockSpec((1, Q_RANK, 2 * LANES), lambda b, p: (p, 0, 0)),
                  pl.BlockSpec((1, KV_RANK, 2 * LANES), lambda b, p: (p, 0, 0)),
                  pl.BlockSpec((1, KV_RANK, LANES), lambda b, p: (p, 0, 0)),
                  pl.BlockSpec((T, LANES), lambda b, p: (0, 0)),
                  pl.BlockSpec((T, LANES), lambda b, p: (0, 0)),
                  pl.BlockSpec((T, LANES), lambda b, p: (0, 0))],
        out_specs=pl.BlockSpec((1, T, LANES), lambda b, p: (b, 0, p)),
        out_shape=jax.ShapeDtypeStruct((B, T, HP * LANES), BF16),
        scratch_shapes=[pltpu.VMEM((2, T, LANES), BF16), pltpu.VMEM((2, T, LANES), BF16),
                        pltpu.VMEM((T, LANES), BF16)],
        compiler_params=_params("arbitrary", "arbitrary"),
        name="mla_attention",
    )(z, z, z, qn_w.reshape(1, -1), kvn_w.reshape(1, -1), wq, wk, wv, cq, ck, sk)


def _na_kernel(q_ref, k_ref, v_ref, bt_ref, o_ref, k_s, v_s, *, L, rows, kr, need_ctx, scale):
    W = GRID_W
    rpb = 4 if rows % 4 == 0 else 1
    k_s[...] = k_ref[0].astype(BF16)
    v_s[...] = v_ref[0].astype(BF16)
    nwin = kr * W
    lane = lax.broadcasted_iota(jnp.int32, (W, LANES), 1)
    head_mask = [(lane < C_HEAD_DIM).astype(F32), (lane >= C_HEAD_DIM).astype(F32)]
    first_head = lane < C_HEAD_DIM
    qcol = lax.broadcasted_iota(jnp.int32, (W, nwin), 0)
    kcol = lax.broadcasted_iota(jnp.int32, (W, nwin), 1) % W
    cstart = jnp.clip(qcol - WIN_C // 2, 0, W - WIN_C)
    col_valid = (kcol >= cstart) & (kcol < cstart + WIN_C)

    def row_block(rb, carry):
        q_blk = q_ref[0, pl.ds(pl.multiple_of(L + rb * (rpb * W), W), rpb * W), :] * scale
        s_ctx = [_dot_nt((q_blk * jnp.concatenate([head_mask[h]] * rpb, axis=0)).astype(BF16), k_s[0:L, :])
                 for h in range(2)]
        chains, q0s = [], []
        for j in range(rpb):
            r = rb * rpb + j
            rs = jnp.clip(r - kr // 2, 0, rows - kr)
            k0 = pl.multiple_of(L + rs * W, W)
            q0s.append(pl.multiple_of(L + r * W, W))
            q = q_blk[j * W:(j + 1) * W]
            kw = k_s[pl.ds(k0, nwin), :]
            vw = v_s[pl.ds(k0, nwin), :]
            dr0 = rs - r + (WIN_R - 1)
            for h in range(2):
                s_nb = _dot_nt((q * head_mask[h]).astype(BF16), kw)
                bias = jnp.concatenate([bt_ref[0, h, dr0 + 2 * m] for m in range(kr // 2)], axis=-1)
                s_nb = jnp.where(col_valid, s_nb + bias, NEG_INF)
                chains.append([(s_nb, vw), (s_ctx[h][j * W:(j + 1) * W], v_s[0:L, :])])
        outs = _softmax_pv(chains)
        for j in range(rpb):
            o_ref[0, pl.ds(q0s[j], W), :] = jnp.where(first_head, outs[2 * j], outs[2 * j + 1]).astype(o_ref.dtype)
        return carry

    lax.fori_loop(0, rows // rpb, row_block, 0, unroll=4)

    tq = min(L, 256)
    lane_c = lax.broadcasted_iota(jnp.int32, (tq, LANES), 1)
    for i in range(L // tq):
        if need_ctx:
            q = q_ref[0, i * tq:(i + 1) * tq, :] * scale
            hm = [lane_c < C_HEAD_DIM, lane_c >= C_HEAD_DIM]
            s = [_dot_nt(jnp.where(hm[h], q, 0.0).astype(BF16), k_s[0:L, :]) for h in range(2)]
            outs = _softmax_pv([[(si, v_s[0:L, :])] for si in s])
            o_ref[0, i * tq:(i + 1) * tq, :] = jnp.where(lane_c < C_HEAD_DIM, outs[0], outs[1]).astype(o_ref.dtype)
        else:
            o_ref[0, i * tq:(i + 1) * tq, :] = jnp.zeros((tq, LANES), o_ref.dtype)


def _na_attention(qkv, bias_tab, L, need_ctx):
    B, T, D3 = qkv.shape
    D = D3 // 3
    HP = D // LANES
    rows = (T - L) // GRID_W
    kr = min(WIN_R, rows)
    assert kr % 2 == 0
    nd = bias_tab.shape[2]
    return pl.pallas_call(
        functools.partial(_na_kernel, L=L, rows=rows, kr=kr, need_ctx=need_ctx, scale=float(C_HEAD_DIM ** -0.5)),
        grid=(B, HP),
        in_specs=[pl.BlockSpec((1, T, LANES), lambda b, p: (b, 0, p)),
                  pl.BlockSpec((1, T, LANES), lambda b, p: (b, 0, HP + p)),
                  pl.BlockSpec((1, T, LANES), lambda b, p: (b, 0, 2 * HP + p)),
                  pl.BlockSpec((1, 2, nd, GRID_W, LANES), lambda b, p: (p, 0, 0, 0, 0))],
        out_specs=pl.BlockSpec((1, T, LANES), lambda b, p: (b, 0, p)),
        out_shape=jax.ShapeDtypeStruct((B, T, D), BF16),
        scratch_shapes=[pltpu.VMEM((T, LANES), BF16), pltpu.VMEM((T, LANES), BF16)],
        compiler_params=_params("arbitrary", "arbitrary"),
        name="na_attention",
    )(qkv, qkv, qkv, bias_tab)


def _na_bias_table(rpb):
    H = rpb.shape[0]
    qc = np.arange(GRID_W)[:, None]
    kc = np.arange(GRID_W)[None, :]
    dc = np.clip(kc - qc + (WIN_C - 1), 0, 2 * WIN_C - 2)
    pick = jnp.asarray(np.arange(2 * WIN_C - 1)[:, None, None] == dc[None], dtype=F32)
    t = jnp.einsum('hdc,cqk->hdqk', rpb, pick, precision=HIGHEST)
    t2 = jnp.concatenate([t[:, :-1], t[:, 1:]], axis=-1)
    return t2.reshape(H // 2, 2, 2 * WIN_R - 2, GRID_W, 2 * GRID_W)


def _bf(x):
    return x.astype(BF16)


def _seg_sum(x, ones_bd):
    hi = _bf(x)
    lo = _bf(x - hi.astype(F32))
    return _dot(hi, ones_bd) + _dot(lo, ones_bd)


def _head_ones(A):
    seg = np.arange(A) // A_HEAD_DIM
    return jnp.asarray(seg[:, None] == seg[None, :], dtype=BF16)


def _rwkv_prep_kernel(z_ref, zp_ref, zn_ref, mu_ref, w0_ref, w2_ref, a0_ref, a2_ref, g2_ref, kk_ref, ka_ref, rk_ref,
                      ones_ref, r_o, v_o, kkn_o, g_o, bonus_o, lw_o, beta_o, kd_o, *, nct, nt, A):
    for s in range(z_ref.shape[0]):
        _rwkv_prep_sample(s, z_ref, zp_ref, zn_ref, mu_ref, w0_ref, w2_ref, a0_ref, a2_ref, g2_ref, kk_ref, ka_ref,
                          rk_ref, ones_ref, r_o, v_o, kkn_o, g_o, bonus_o, lw_o, beta_o, kd_o, nct=nct, nt=nt, A=A)


def _rwkv_prep_sample(s, z_ref, zp_ref, zn_ref, mu_ref, w0_ref, w2_ref, a0_ref, a2_ref, g2_ref, kk_ref, ka_ref, rk_ref,
                      ones_ref, r_o, v_o, kkn_o, g_o, bonus_o, lw_o, beta_o, kd_o, *, nct, nt, A):
    i = pl.program_id(1)
    za = z_ref[s]
    tm = za.shape[0]
    row = lax.broadcasted_iota(jnp.int32, za.shape, 0)
    seg_first = (i == 0) | (i == nct)
    seg_last = (i == nct - 1) | (i == nt - 1)
    prev_row = jnp.where(seg_first, 0.0, zp_ref[s, 7:8, :])
    next_row = jnp.where(seg_last, 0.0, zn_ref[s, 0:1, :])
    prev = jnp.where(row == 0, prev_row, pltpu.roll(za, 1, 0))
    nxt = jnp.where(row == tm - 1, next_row, pltpu.roll(za, tm - 1, 0))
    zs = za + mu_ref[0:1, :] * (prev - za) + mu_ref[1:2, :] * (nxt - za)
    r = zs[:, 0:A]
    k = zs[:, A:2 * A]
    v = zs[:, 2 * A:3 * A]
    wd = _bf(jnp.tanh(zs[:, 3 * A:3 * A + LANES]))
    ad = _bf(zs[:, 3 * A + LANES:3 * A + 2 * LANES])
    gd = _bf(jax.nn.sigmoid(zs[:, 3 * A + 2 * LANES:3 * A + 3 * LANES]))
    ones = ones_ref[...]
    kk = k * kk_ref[...]
    kkn = kk / jnp.maximum(jnp.sqrt(_seg_sum(kk * kk, ones)), 1e-12)
    kd_sum = None
    for d in range(2):
        u = w0_ref[d:d + 1, :] + _dot(wd, w2_ref[d])
        lw_o[d, s] = -DECAY_FLOOR_SCALE * jax.nn.sigmoid(u)
        a = jax.nn.sigmoid(a0_ref[d:d + 1, :] + _dot(ad, a2_ref[d]))
        beta_o[d, s] = kkn * a
        kd = k * (1.0 + (a - 1.0) * ka_ref[...])
        kd_o[d, s] = kd
        kd_sum = kd if kd_sum is None else kd_sum + kd
    bonus_o[s] = _seg_sum(r * kd_sum * rk_ref[...], ones) * v
    r_o[s] = r
    v_o[s] = v
    kkn_o[s] = kkn
    g_o[s] = _dot(gd, g2_ref[...])


def _rwkv_prep(z, L, a_cols, mu, w0, w2, a0, a2, g2, k_k, k_a, r_k):
    B, T, _ = z.shape
    A = w0.shape[-1]
    assert 2 * LORA_W == LANES and 2 * LORA_A == LANES and LORA_G == LANES and a_cols == 3 * A + 3 * LANES
    tm = _row_tile(L, T - L)
    nt = T // tm
    hb = tm // 8

    def pad_lora(w):
        zero = jnp.zeros_like(w[0])
        return _bf(jnp.stack([jnp.concatenate([w[0], zero], 0), jnp.concatenate([zero, w[1]], 0)]))

    def const(shape):
        return pl.BlockSpec(shape, lambda b, i: (0,) * len(shape))

    nb = 2 if B % 2 == 0 else 1
    tile = pl.BlockSpec((nb, tm, A), lambda b, i: (b, i, 0))
    tile_d = pl.BlockSpec((2, nb, tm, A), lambda b, i: (0, b, i, 0))
    sd = jax.ShapeDtypeStruct((B, T, A), F32)
    sd_d = jax.ShapeDtypeStruct((2, B, T, A), F32)
    return pl.pallas_call(
        functools.partial(_rwkv_prep_kernel, nct=L // tm, nt=nt, A=A),
        grid=(B // nb, nt),
        in_specs=[pl.BlockSpec((nb, tm, a_cols), lambda b, i: (b, i, 0)),
                  pl.BlockSpec((nb, 8, a_cols), lambda b, i: (b, jnp.maximum(i * hb - 1, 0), 0)),
                  pl.BlockSpec((nb, 8, a_cols), lambda b, i: (b, jnp.minimum((i + 1) * hb, T // 8 - 1), 0)),
                  const((2, a_cols)), const((2, A)), const((2, 2 * LORA_W, A)), const((2, A)),
                  const((2, 2 * LORA_A, A)), const((LORA_G, A)), const((1, A)), const((1, A)), const((1, A)),
                  const((A, A))],
        out_specs=[tile, tile, tile, tile, tile, tile_d, tile_d, tile_d],
        out_shape=[sd, sd, sd, sd, sd, sd_d, sd_d, sd_d],
        compiler_params=_params("arbitrary", "arbitrary"),
        name="rwkv_prep",
    )(z, z, z, mu, w0, pad_lora(w2), a0, pad_lora(a2), _bf(g2), k_k.reshape(1, A), k_a.reshape(1, A),
      r_k.reshape(1, A), _head_ones(A))


def _tri_inverse(lms, eye, m16, m32):
    d0 = [_bf(jnp.where(m16, lm, 0.0)) for lm in lms]
    t = [eye + d.astype(F32) for d in d0]
    s = [_dot(d, d) for d in d0]
    for step in range(3):
        sb = [_bf(x) for x in s]
        t = [x + _dot(_bf(x), y) for x, y in zip(t, sb)]
        if step < 2:
            s = [_dot(y, y) for y in sb]
    for lvl in (m32 & (~m16), ~m32):
        tb = [_bf(x) for x in t]
        w = [_bf(_dot(_bf(jnp.where(lvl, lm, 0.0)), y)) for lm, y in zip(lms, tb)]
        t = [x + _dot(y, z) for x, y, z in zip(t, tb, w)]
    return t


def _wkv_kernel(*refs, NP, NB):
    C = SCAN_CHUNK
    P = 2 * C
    fwd_refs, bwd_refs, (yf_ref, yb_ref, h_s) = refs[0:6], refs[6:12], refs[12:]

    @pl.when(pl.program_id(1) == 0)
    def _():
        h_s[...] = jnp.zeros(h_s.shape, F32)

    ri = lax.broadcasted_iota(jnp.int32, (P, P), 0)
    ci = lax.broadcasted_iota(jnp.int32, (P, P), 1)
    same = (ri // C) == (ci // C)
    diff = (ri % C) - (ci % C)
    eye_b = ri == ci
    eye = eye_b.astype(F32)
    m16 = (ri // 16) == (ci // 16)
    m32 = (ri // 32) == (ci // 32)
    diff64 = lax.broadcasted_iota(jnp.int32, (C, C), 0) - lax.broadcasted_iota(jnp.int32, (C, C), 1)
    top = lax.broadcasted_iota(jnp.int32, (C, P), 1) < C
    zero_blk = jnp.zeros((P, P), BF16)
    sls = [slice(p * P, (p + 1) * P) for p in range(NP)]

    def bd(x):
        return [_bf(jnp.concatenate([jnp.where(top, x[:, sl], 0.0), jnp.where(top, 0.0, x[:, sl])], axis=0))
                for sl in sls]

    at, rt, bt, kt, bh, kh, vv, etots, before, before_eq = [], [], [], [], [], [], [], [], [], []
    for (r_ref, v_ref, kk_ref, lw_ref, beta_ref, kd_ref), sgn in ((fwd_refs, 1), (bwd_refs, -1)):
        order = diff * sgn
        tri = _bf(((diff64 * sgn) >= 0).astype(F32))
        for s in range(NB):
            before += [same & (order > 0)] * NP
            before_eq += [same & (order >= 0)] * NP
            lw = lw_ref[0, s]
            lw_hi = _bf(lw)
            lw_md = _bf(lw - lw_hi.astype(F32))
            lw_lo = _bf(lw - lw_hi.astype(F32) - lw_md.astype(F32))
            cum = _dot(tri, lw_hi) + _dot(tri, lw_md) + _dot(tri, lw_lo)
            tot = jnp.sum(lw, axis=0, keepdims=True)
            beta = beta_ref[0, s]
            kd = kd_ref[0, s]
            e_neg = jnp.exp(-cum)
            e_tail = jnp.exp(tot - cum)
            at += bd(-kk_ref[s] * jnp.exp(cum - lw))
            rt += bd(r_ref[s] * jnp.exp(cum))
            bt += bd(beta * e_neg)
            kt += bd(kd * e_neg)
            bh += bd(beta * e_tail)
            kh += bd(kd * e_tail)
            vv += bd(v_ref[s])
            etot = jnp.exp(tot)
            etots += [etot[:, sl] for sl in sls]

    ar = [jnp.concatenate([a, r], axis=0) for a, r in zip(at, rt)]
    arb = [_dot_nt(x, b) for x, b in zip(ar, bt)]
    ark = [_dot_nt(x, k) for x, k in zip(ar, kt)]
    lab = [jnp.where(m, x[:P], 0.0) for x, m in zip(arb, before)]
    tinv = _tri_inverse(lab, eye, m16, m32)
    u = [_dot(_bf(jnp.where(m, x[:P], 0.0)), v) for x, v, m in zip(ark, vv, before)]
    x = [_bf(_dot(_bf(t), jnp.concatenate([a, _bf(w)], axis=1))) for t, a, w in zip(tinv, at, u)]
    rhs = [jnp.concatenate([xi, jnp.concatenate([zero_blk, v], axis=1)], axis=0) for xi, v in zip(x, vv)]
    mn = [lax.dot_general(jnp.concatenate([b, k], axis=0), w, (((0,), (0,)), ((), ())), preferred_element_type=F32)
          for b, k, w in zip(bh, kh, rhs)]
    lr = [_bf(jnp.concatenate([jnp.where(m, xb[P:], 0.0), jnp.where(m, xk[P:], 0.0)], axis=1))
          for xb, xk, m in zip(arb, ark, before_eq)]
    qy = [_dot(l, w) for l, w in zip(lr, rhs)]
    qm = [_bf(jnp.concatenate([r.astype(F32) + q[:, :P], jnp.where(eye_b, e, 0.0) + m[:, :P]], axis=0))
          for r, q, m, e in zip(rt, qy, mn, etots)]
    nchain = 2 * NB * NP
    hin = [h_s[i] for i in range(nchain)]
    h_hi = [_bf(h) for h in hin]
    h_lo = [_bf(h - hh.astype(F32)) for h, hh in zip(hin, h_hi)]
    res = [_dot(w, hh) + _dot(w, hl) for w, hh, hl in zip(qm, h_hi, h_lo)]
    for i in range(nchain):
        ybd = res[i][:P] + qy[i][:, P:]
        y_ref = yf_ref if i < NB * NP else yb_ref
        y_ref[(i // NP) % NB, :, sls[i % NP]] = ybd[:C] + ybd[C:]
        h_s[i] = res[i][P:] + mn[i][:, P:]


def _wkv_scan(r, v, kk, lw, beta, kd, L):
    B, T, A = r.shape
    C = SCAN_CHUNK
    nC = T // C
    nct = L // C
    NP = A // (2 * C)

    def rev(c):
        return jnp.where(c < nct, nct - 1 - c, nC - 1 - (c - nct))

    NB = 2 if B % 2 == 0 else 1
    fwd = pl.BlockSpec((NB, C, A), lambda b, c: (b, c, 0))
    bwd = pl.BlockSpec((NB, C, A), lambda b, c: (b, rev(c), 0))
    fwd_d = pl.BlockSpec((1, NB, C, A), lambda b, c: (0, b, c, 0))
    bwd_d = pl.BlockSpec((1, NB, C, A), lambda b, c: (1, b, rev(c), 0))
    sd = jax.ShapeDtypeStruct((B, T, A), F32)
    return pl.pallas_call(
        functools.partial(_wkv_kernel, NP=NP, NB=NB),
        grid=(B // NB, nC),
        in_specs=[fwd, fwd, fwd, fwd_d, fwd_d, fwd_d, bwd, bwd, bwd, bwd_d, bwd_d, bwd_d],
        out_specs=[fwd, bwd],
        out_shape=[sd, sd],
        scratch_shapes=[pltpu.VMEM((2 * NB * NP, 2 * C, 2 * C), F32)],
        compiler_params=_params("arbitrary", "arbitrary"),
        name="wkv_scan",
    )(r, v, kk, lw, beta, kd, r, v, kk, lw, beta, kd)


def _rwkv_post_kernel(yf_ref, yb_ref, bonus_ref, g_ref, lnw_ref, lnb_ref, ones_ref, o_ref):
    y = yf_ref[0] + yb_ref[0]
    ones = ones_ref[...]
    inv_n = 1.0 / A_HEAD_DIM
    d = y - _seg_sum(y, ones) * inv_n
    var = _seg_sum(d * d, ones) * inv_n
    yn = d * lax.rsqrt(var + GN_EPS) * lnw_ref[...] + lnb_ref[...]
    o_ref[0] = ((yn + bonus_ref[0]) * g_ref[0]).astype(o_ref.dtype)


def _rwkv_post(yf, yb, bonus, g, ln_w, ln_b, L):
    B, T, A = bonus.shape
    tm = _wide_tile(T)
    tile = pl.BlockSpec((1, tm, A), lambda b, i: (b, i, 0))
    vec = pl.BlockSpec((1, A), lambda b, i: (0, 0))
    return pl.pallas_call(
        _rwkv_post_kernel,
        grid=(B, T // tm),
        in_specs=[tile, tile, tile, tile, vec, vec, pl.BlockSpec((A, A), lambda b, i: (0, 0))],
        out_specs=tile,
        out_shape=jax.ShapeDtypeStruct((B, T, A), BF16),
        compiler_params=_params("arbitrary", "arbitrary"),
        name="rwkv_post",
    )(yf, yb, bonus, g, ln_w.reshape(1, A), ln_b.reshape(1, A), _head_ones(A))


def _rwkv_mixer(z, L, a_cols, mu, w0, w2, a0, a2, g2, k_k, k_a, r_k, ln_w, ln_b):
    r, v, kk, g, bonus, lw, beta, kd = _rwkv_prep(z, L, a_cols, mu, w0, w2, a0, a2, g2, k_k, k_a, r_k)
    yf, yb = _wkv_scan(r, v, kk, lw, beta, kd, L)
    return _rwkv_post(yf, yb, bonus, g, ln_w, ln_b, L)


def _router_kernel(x_ref, nw_ref, ss_ref, wr_ref, h_ref, aff_ref, *, L):
    i = pl.program_id(1)
    tm = x_ref.shape[1]
    h = _norm_mod(x_ref[0], nw_ref[...], _per_row(ss_ref, i, tm, L, 0), _per_row(ss_ref, i, tm, L, 1))
    h_hi = h.astype(BF16)
    h_ref[0] = h_hi
    h_lo = _bf(h - h_hi.astype(F32))
    w = wr_ref[...]
    w_hi = _bf(w)
    w_lo = _bf(w - w_hi.astype(F32))
    logits = _dot_nt(w_hi, h_hi) + _dot_nt(w_hi, h_lo) + _dot_nt(w_lo, h_hi)
    m = jnp.max(logits, axis=0, keepdims=True)
    p = jnp.exp(logits - m)
    aff_ref[0] = p / jnp.sum(p, axis=0, keepdims=True)


def _norm_router(xa, nw, ss, w_router, L):
    B, T, D = xa.shape
    E = w_router.shape[1]
    tm = max(m for m in range(LANES, MAX_ROW_TILE + 1, LANES) if T % m == 0)
    return pl.pallas_call(
        functools.partial(_router_kernel, L=L),
        grid=(B, T // tm),
        in_specs=[pl.BlockSpec((1, tm, D), lambda b, i: (b, i, 0)),
                  pl.BlockSpec((1, D), lambda b, i: (0, 0)),
                  pl.BlockSpec((1, 2, 2, D), lambda b, i: (b, 0, 0, 0)),
                  pl.BlockSpec((E, D), lambda b, i: (0, 0))],
        out_specs=[pl.BlockSpec((1, tm, D), lambda b, i: (b, i, 0)),
                   pl.BlockSpec((1, E, tm), lambda b, i: (b, 0, i))],
        out_shape=[jax.ShapeDtypeStruct((B, T, D), BF16), jax.ShapeDtypeStruct((B, E, T), F32)],
        compiler_params=_params("arbitrary", "arbitrary"),
        name="norm_router",
    )(xa, nw.reshape(1, D), ss, w_router.T)


def _select_top(sets):
    keys = [pltpu.bitcast(aff, jnp.int32) for aff, _, _ in sets]
    E = keys[0].shape[0]

    def count_ge(key, thr):
        return jnp.sum(jnp.where(key >= thr, 1.0, 0.0), axis=1, keepdims=True)

    def narrow(_, carry):
        out = []
        for key, (_, cap, _), (lo, hi) in zip(keys, sets, carry):
            q = (hi - lo + 3) >> 2
            m1 = jnp.minimum(lo + q, hi)
            m2 = jnp.minimum(lo + 2 * q, hi)
            m3 = jnp.minimum(lo + 3 * q, hi)
            ok1, ok2, ok3 = count_ge(key, m1) >= cap, count_ge(key, m2) >= cap, count_ge(key, m3) >= cap
            new_lo = jnp.where(ok3, m3, jnp.where(ok2, m2, jnp.where(ok1, m1, lo)))
            new_hi = jnp.where(ok3, hi, jnp.where(ok2, m3 - 1, jnp.where(ok1, m2 - 1, m1 - 1)))
            out.append((new_lo, jnp.maximum(new_hi, new_lo)))
        return tuple(out)

    init = tuple((jnp.zeros((E, 1), jnp.int32), jnp.full((E, 1), 0x7F800000, jnp.int32)) for _ in sets)
    bounds = lax.fori_loop(0, 17, narrow, init)

    def prefix(x, tri):
        pb = tri.shape[0]
        parts, offset = [], jnp.zeros((E, 1), F32)
        for j in range(x.shape[1] // pb):
            xj = x[:, j * pb:(j + 1) * pb]
            inner = _dot(xj.astype(BF16), tri)
            parts.append(inner + offset)
            offset = offset + inner[:, pb - 1:pb] + xj[:, pb - 1:pb]
        return parts[0] if len(parts) == 1 else jnp.concatenate(parts, axis=1)

    res = []
    for key, (_, cap, tri), (thr, _) in zip(keys, sets, bounds):
        above = key > thr
        tie = key == thr
        need = cap - jnp.sum(jnp.where(above, 1.0, 0.0), axis=1, keepdims=True)
        tie_rank = prefix(jnp.where(tie, 1.0, 0.0), tri)
        sel = above | (tie & (tie_rank < need))
        res.append(jnp.where(sel, prefix(jnp.where(sel, 1.0, 0.0), tri), -1.0))
    return res


def _route_gather_kernel(aff_ref, h_ref, x_ref, pos_ref, gate_ref, slot_s, tri_s, *, L, cap_l, cap_c):
    b = pl.program_id(0)
    e = pl.program_id(1)
    T = h_ref.shape[1]
    S = T - L

    @pl.when((b == 0) & (e == 0))
    def _():
        n = tri_s.shape[0]
        tri_s[...] = jnp.where(lax.broadcasted_iota(jnp.int32, (n, n), 0) < lax.broadcasted_iota(jnp.int32, (n, n), 1),
                               1.0, 0.0).astype(BF16)

    @pl.when(e == 0)
    def _():
        sets = [(aff_ref[0, :, L:], cap_l, tri_s[...])]
        if cap_c:
            sets.append((aff_ref[0, :, 0:L], cap_c, tri_s[...]))
        picked = _select_top(sets)
        slot_s[:, L:] = picked[0]
        if cap_c:
            slot_s[:, 0:L] = picked[1]

    def gather(lo, n, cap, row0):
        slot = slot_s[pl.ds(e, 1), lo:lo + n]
        hit = slot == lax.broadcasted_iota(jnp.int32, (cap, n), 0).astype(F32)
        x_ref[0, 0, row0:row0 + cap, :] = _dot(jnp.where(hit, 1.0, 0.0).astype(BF16), h_ref[0, lo:lo + n, :]).astype(BF16)
        tok = lax.broadcasted_iota(jnp.int32, (cap, n), 1) + lo
        pos_ref[0, 0, row0:row0 + cap, :] = jnp.sum(jnp.where(hit, tok, 0), axis=1, keepdims=True)
        aff = aff_ref[0, pl.ds(e, 1), lo:lo + n]
        gate_ref[0, 0, row0:row0 + cap, :] = jnp.sum(jnp.where(hit, aff, 0.0), axis=1, keepdims=True)

    gather(L, S, cap_l, 0)
    if cap_c:
        gather(0, L, cap_c, cap_l)


def _route_gather(aff, h2, L, cap_l, cap_c):
    B, T, D = h2.shape
    E = aff.shape[1]
    Ct = cap_l + cap_c
    return pl.pallas_call(
        functools.partial(_route_gather_kernel, L=L, cap_l=cap_l, cap_c=cap_c),
        grid=(B, E),
        in_specs=[pl.BlockSpec((1, E, T), lambda b, e: (b, 0, 0)),
                  pl.BlockSpec((1, T, D), lambda b, e: (b, 0, 0))],
        out_specs=[pl.BlockSpec((1, 1, Ct, D), lambda b, e: (e, b, 0, 0)),
                   pl.BlockSpec((1, 1, Ct, 1), lambda b, e: (b, e, 0, 0)),
                   pl.BlockSpec((1, 1, Ct, 1), lambda b, e: (b, e, 0, 0))],
        out_shape=[jax.ShapeDtypeStruct((E, B, Ct, D), BF16), jax.ShapeDtypeStruct((B, E, Ct, 1), jnp.int32),
                   jax.ShapeDtypeStruct((B, E, Ct, 1), F32)],
        scratch_shapes=[pltpu.VMEM((E, T), F32), pltpu.VMEM((_row_tile(L, T - L),) * 2, BF16)],
        compiler_params=_params("arbitrary", "arbitrary"),
        name="moe_route_gather",
    )(aff, h2)


def _ffn_kernel(x_ref, w1_ref, w3_ref, w2_ref, o_ref, *, rm):
    j = pl.program_id(1)
    R = x_ref.shape[1]
    w1 = w1_ref[0, 0].astype(BF16)
    w3 = w3_ref[0, 0].astype(BF16)
    w2 = w2_ref[0, 0].astype(BF16)

    @pl.when(j == 0)
    def _():
        o_ref[...] = jnp.zeros(o_ref.shape, F32)

    def rows(i, carry):
        r0 = pl.multiple_of(i * rm, rm)
        x = x_ref[0, pl.ds(r0, rm), :]
        a = _dot(x, w1)
        b = _dot(x, w3)
        hid = (a * jax.nn.sigmoid(a) * b).astype(BF16)
        o_ref[0, pl.ds(r0, rm), :] += _dot(hid, w2)
        return carry

    lax.fori_loop(0, R // rm, rows, 0, unroll=True)


def _expert_ffn(xin, w1, w3, w2, layer):
    E, R, D = xin.shape
    F = w1.shape[-1]
    tf = min(512, F)
    rm = max(m for m in (MAX_ROW_TILE, 512, 256, 128, 64, 32, 16) if R % m == 0)
    return pl.pallas_call(
        functools.partial(_ffn_kernel, rm=rm),
        grid=(E, F // tf),
        in_specs=[pl.BlockSpec((1, R, D), lambda e, j: (e, 0, 0)),
                  pl.BlockSpec((1, 1, D, tf), lambda e, j: (layer, e, 0, j)),
                  pl.BlockSpec((1, 1, D, tf), lambda e, j: (layer, e, 0, j)),
                  pl.BlockSpec((1, 1, tf, D), lambda e, j: (layer, e, j, 0))],
        out_specs=pl.BlockSpec((1, R, D), lambda e, j: (e, 0, 0)),
        out_shape=jax.ShapeDtypeStruct((E, R, D), F32),
        compiler_params=_params("arbitrary", "arbitrary"),
        name="expert_ffn",
    )(xin, w1, w3, w2)


def _combine_kernel(y_ref, gate_ref, posl_ref, posc_ref, x_ref, g_ref, o_ref, yg_s, *, tq, nct, cap_l):
    E, _, Ct, td = y_ref.shape
    T = x_ref.shape[1]
    cap_c = Ct - cap_l
    nl = E * cap_l
    for e in range(E):
        yg_s[e * cap_l:(e + 1) * cap_l, :] = (y_ref[e, 0, :cap_l, :] * gate_ref[0, e, :cap_l, :]).astype(BF16)
        if cap_c:
            yg_s[nl + e * cap_c:nl + (e + 1) * cap_c, :] = (y_ref[e, 0, cap_l:, :] * gate_ref[0, e, cap_l:, :]).astype(BF16)

    def tile(i, gate_row, pos, lo, n):
        r0 = pl.multiple_of(i * tq, tq)
        tok = lax.broadcasted_iota(jnp.int32, (tq, n), 0) + r0
        onehot = jnp.where(tok == pos, 1.0, 0.0).astype(BF16)
        o_ref[0, pl.ds(r0, tq), :] = x_ref[0, pl.ds(r0, tq), :] + gate_row * _dot(onehot, yg_s[lo:lo + n, :])

    for i in range(nct):
        if cap_c:
            tile(i, g_ref[0, 0], posc_ref[0], nl, E * cap_c)
        else:
            o_ref[0, i * tq:(i + 1) * tq, :] = x_ref[0, i * tq:(i + 1) * tq, :]

    def body(i, carry):
        tile(i, g_ref[0, 1], posl_ref[0], 0, nl)
        return carry

    lax.fori_loop(nct, T // tq, body, 0, unroll=2)


def _moe_combine(y, gate, pos_l, pos_c, xa, g2, L, cap_l):
    E, B, Ct, D = y.shape
    T = xa.shape[1]
    td = min(512, D)
    tq = _row_tile(L, T - L)
    return pl.pallas_call(
        functools.partial(_combine_kernel, tq=tq, nct=L // tq, cap_l=cap_l),
        grid=(B, D // td),
        in_specs=[pl.BlockSpec((E, 1, Ct, td), lambda b, j: (0, b, 0, j)),
                  pl.BlockSpec((1, E, Ct, 1), lambda b, j: (b, 0, 0, 0)),
                  pl.BlockSpec((1, 1, pos_l.shape[-1]), lambda b, j: (b, 0, 0)),
                  pl.BlockSpec((1, 1, pos_c.shape[-1]), lambda b, j: (b, 0, 0)),
                  pl.BlockSpec((1, T, td), lambda b, j: (b, 0, j)),
                  pl.BlockSpec((1, 2, 1, td), lambda b, j: (b, 0, 0, j))],
        out_specs=pl.BlockSpec((1, T, td), lambda b, j: (b, 0, j)),
        out_shape=jax.ShapeDtypeStruct((B, T, D), F32),
        scratch_shapes=[pltpu.VMEM((E * Ct, td), BF16)],
        compiler_params=_params("arbitrary", "arbitrary"),
        name="moe_combine",
    )(y, gate, pos_l, pos_c, xa, g2)


def _moe(xa, nw, ss2, g2, w_router, w1, w3, w2, layer, L, need_ctx):
    B, T, D = xa.shape
    S = T - L
    E = w_router.shape[1]
    h2, aff = _norm_router(xa, nw, ss2, w_router, L)
    cap_l = CAPACITY_FACTOR * S // E
    cap_c = CAPACITY_FACTOR * L // E if need_ctx else 0
    Ct = cap_l + cap_c
    xin, pos, gate = _route_gather(aff, h2, L, cap_l, cap_c)
    pos_l = pos[:, :, :cap_l, 0].reshape(B, 1, E * cap_l)
    if cap_c:
        pos_c = pos[:, :, cap_l:, 0].reshape(B, 1, E * cap_c)
    else:
        pos_c = jnp.zeros((B, 1, LANES), jnp.int32)
    y = _expert_ffn(xin.reshape(E, B * Ct, D), w1, w3, w2, layer).reshape(E, B, Ct, D)
    return _moe_combine(y, gate, pos_l, pos_c, xa, g2, L, cap_l)


_PERM_EO = np.concatenate([np.arange(0, QK_ROPE, 2), np.arange(1, QK_ROPE, 2)])
_PERM_OE = np.concatenate([np.arange(1, QK_ROPE, 2), np.arange(0, QK_ROPE, 2)])


def _ab_input_weight(w_in, a_cols):
    D = w_in.shape[0]
    zr = w_in[:, a_cols + Q_RANK + KV_RANK:]
    zero = jnp.zeros((D, LANES - 2 * QK_ROPE), w_in.dtype)
    g1 = jnp.concatenate([zero, zr[:, _PERM_EO], zr[:, _PERM_EO]], axis=1)
    g2 = jnp.concatenate([zero, zr[:, _PERM_OE], zr[:, _PERM_OE]], axis=1)
    return jnp.concatenate([w_in[:, :a_cols + Q_RANK + KV_RANK], g1, g2], axis=1).astype(BF16)


def _mla_weights(w_qup, w_kvup):
    NH = w_qup.shape[1] // (QK_NOPE + QK_ROPE)
    wq = w_qup.reshape(Q_RANK, NH, QK_NOPE + QK_ROPE)
    rope = wq[:, :, QK_NOPE:]
    wq = jnp.concatenate([wq[:, :, :QK_NOPE], rope[:, :, _PERM_EO], rope[:, :, _PERM_OE]], axis=-1)
    wq = wq.reshape(Q_RANK, NH // 2, 2 * LANES).transpose(1, 0, 2)
    wkv = w_kvup.reshape(KV_RANK, NH, QK_NOPE + V_HEAD)
    wk = jnp.concatenate([wkv[:, :, :QK_NOPE], jnp.zeros((KV_RANK, NH, LANES - QK_NOPE), w_kvup.dtype)], axis=-1)
    wk = wk.reshape(KV_RANK, NH // 2, 2 * LANES).transpose(1, 0, 2)
    wv = wkv[:, :, QK_NOPE:].reshape(KV_RANK, NH // 2, 2 * V_HEAD).transpose(1, 0, 2)
    return wq.astype(BF16), wk.astype(BF16), wv.astype(BF16)


def _rope_tables(L, S):
    t = np.arange(S)
    row = (t // GRID_W).astype(np.float32)
    col = (t % GRID_W).astype(np.float32)
    n_freq = QK_ROPE // 4
    inv = (ROPE_BASE ** (-np.arange(n_freq, dtype=np.float32) / n_freq)).astype(np.float32)
    ang = jnp.concatenate([jnp.asarray(row[:, None] * inv), jnp.asarray(col[:, None] * inv)], axis=-1)
    cos = jnp.concatenate([jnp.ones((L, QK_ROPE // 2), F32), jnp.cos(ang)], axis=0)
    sin = jnp.concatenate([jnp.zeros((L, QK_ROPE // 2), F32), jnp.sin(ang)], axis=0)
    T = L + S
    cc = jnp.concatenate([cos, cos], axis=1)
    ss = jnp.concatenate([-sin, sin], axis=1)
    one = jnp.ones((T, LANES - 2 * QK_ROPE), F32)
    zero = jnp.zeros((T, LANES - 2 * QK_ROPE), F32)
    cq = jnp.concatenate([one, cc, ss], axis=1)
    ck = jnp.concatenate([zero, cc, cc], axis=1)
    sk = jnp.concatenate([zero, ss, ss], axis=1)
    return cq, ck, sk


def kernel(x, c, ctx, c_ctx, mod_w, mod_b, norm1_w, norm2_w, final_norm_w, ab_w_in, ab_w_out, rk_mu, rk_w0, rk_w2, rk_a0, rk_a2, rk_g2, rk_kk, rk_ka, rk_rk, rk_ln_w, rk_ln_b, mla_qn_w, mla_w_qup, mla_kvn_w, mla_w_kvup, na_w_qkv, na_rpb, na_w_out, moe_router, moe_w1, moe_w3, moe_w2):
    B, S, D = x.shape
    L = ctx.shape[1]
    depth = mod_w.shape[0]
    A = rk_w0.shape[-1]
    a_cols = rk_mu.shape[-1]

    rows_pad = -(B + 1) % 8
    cvec = jnp.concatenate([c, c_ctx[None], jnp.zeros((rows_pad, D), F32)], axis=0)
    mods = _mod_vectors(cvec, mod_w, mod_b)
    m_lat = mods[:, :B].reshape(depth, B, 6, D)
    m_ctx = jnp.broadcast_to(mods[:, B].reshape(depth, 1, 6, D), (depth, B, 6, D))
    mm = jnp.stack([m_ctx, m_lat], axis=2)

    cq, ck, sk = _rope_tables(L, S)
    xa = jnp.concatenate([ctx, x], axis=1)

    for layer in range(depth):
        need_ctx = layer < depth - 1
        i = layer // 2
        m = mm[layer]
        ss1, g1 = m[:, :, 0:2], m[:, :, 2:3]
        ss2, g2 = m[:, :, 3:5], m[:, :, 5:6]
        if layer % 2 == 0:
            w_in = _ab_input_weight(ab_w_in[i], a_cols)
            z = _norm_linear(xa, norm1_w[layer], ss1, w_in, L)
            o_a = _rwkv_mixer(z, L, a_cols, rk_mu[i], rk_w0[i], rk_w2[i], rk_a0[i], rk_a2[i], rk_g2[i],
                              rk_kk[i], rk_ka[i], rk_rk[i], rk_ln_w[i], rk_ln_b[i])
            wq, wk, wv = _mla_weights(mla_w_qup[i], mla_w_kvup[i])
            o_b = _mla_attention(z, mla_qn_w[i], mla_kvn_w[i], wq, wk, wv, cq, ck, sk, L,
                                 a_cols, a_cols + Q_RANK, a_cols + Q_RANK + KV_RANK)
            mixed, w_out = [o_a, o_b], ab_w_out[i]
        else:
            qkv = _norm_linear(xa, norm1_w[layer], ss1, na_w_qkv[i].astype(BF16), L, out_dtype=BF16)
            mixed, w_out = [_na_attention(qkv, _na_bias_table(na_rpb[i]), L, need_ctx)], na_w_out[i]
        xa = _linear_resid(mixed, w_out.astype(BF16), xa, g1, L)
        xa = _moe(xa, norm2_w[layer], ss2, g2, moe_router[layer], moe_w1, moe_w3, moe_w2, layer, L, need_ctx)
    return _final_norm(xa, final_norm_w, L)
```

```python
import functools

import jax
import jax.numpy as jnp
import numpy as np
from jax import lax
from jax.experimental import pallas as pl
from jax.experimental.pallas import tpu as pltpu

F32 = jnp.float32
BF16 = jnp.bfloat16
HIGHEST = lax.Precision.HIGHEST

GRID_W = 64
NORM_EPS = 1e-6
NEG_INF = -1e30
GN_EPS = 64e-5
A_HEAD_DIM = 64
LORA_W = 64
LORA_A = 64
LORA_G = 128
QK_NOPE = 64
QK_ROPE = 32
V_HEAD = 64
Q_RANK = 384
KV_RANK = 256
ROPE_BASE = 10000.0
C_HEAD_DIM = 64
WIN_R = 8
WIN_C = 16
N_EXPERTS = 16
CAPACITY_FACTOR = 2
SCAN_CHUNK = 64
DECAY_FLOOR_SCALE = float(np.exp(-0.5))
LANES = 128
MAX_ROW_TILE = 768

VMEM_LIMIT = 56 * 1024 * 1024


def _params(*sem):
    return pltpu.CompilerParams(dimension_semantics=sem, vmem_limit_bytes=VMEM_LIMIT)


def _dot(a, b, precision=None):
    return jnp.dot(a, b, preferred_element_type=F32, precision=precision)


def _dot_nt(a, b, precision=None):
    return lax.dot_general(a, b, (((1,), (1,)), ((), ())), preferred_element_type=F32, precision=precision)


def _row_tile(L, S):
    tm = 256
    while L % tm or S % tm:
        tm //= 2
    return tm


def _mod_kernel(c_ref, w_ref, b_ref, o_ref):
    c = c_ref[...]
    sc = c * jax.nn.sigmoid(c)
    o_ref[0] = _dot(sc.astype(BF16), w_ref[0].astype(BF16)) + b_ref[0]


def _mod_vectors(cvec, mod_w, mod_b):
    depth, D, N = mod_w.shape
    R = cvec.shape[0]
    tn = 1024
    return pl.pallas_call(
        _mod_kernel,
        grid=(depth, N // tn),
        in_specs=[pl.BlockSpec((R, D), lambda l, j: (0, 0)),
                  pl.BlockSpec((1, D, tn), lambda l, j: (l, 0, j)),
                  pl.BlockSpec((1, 1, tn), lambda l, j: (l, 0, j))],
        out_specs=pl.BlockSpec((1, R, tn), lambda l, j: (l, 0, j)),
        out_shape=jax.ShapeDtypeStruct((depth, R, N), F32),
        compiler_params=_params("arbitrary", "arbitrary"),
        name="mod_vectors",
    )(cvec, mod_w, mod_b.reshape(depth, 1, N))


def _wide_tile(T):
    return max(m for m in range(8, MAX_ROW_TILE + 1, 8) if T % m == 0)


def _per_row(mod_ref, i, tm, L, k):
    row = lax.broadcasted_iota(jnp.int32, (tm, 1), 0) + i * tm
    return jnp.where(row < L, mod_ref[0, 0, k:k + 1, :], mod_ref[0, 1, k:k + 1, :])


def _norm_mod(x, nw, shift, scale):
    y = x * lax.rsqrt(jnp.mean(x * x, axis=-1, keepdims=True) + NORM_EPS)
    y = y * nw
    return y * (1.0 + scale) + shift


def _norm_linear_kernel(x_ref, nw_ref, ss_ref, w_ref, o_ref, *, L):
    i = pl.program_id(1)
    tm = x_ref.shape[1]
    h = _norm_mod(x_ref[0], nw_ref[...], _per_row(ss_ref, i, tm, L, 0), _per_row(ss_ref, i, tm, L, 1))
    o_ref[0] = _dot(h.astype(BF16), w_ref[...]).astype(o_ref.dtype)


def _norm_linear(xa, nw, ss, w, L, out_dtype=F32):
    B, T, D = xa.shape
    N = w.shape[1]
    tm = _wide_tile(T)
    return pl.pallas_call(
        functools.partial(_norm_linear_kernel, L=L),
        grid=(B, T // tm),
        in_specs=[pl.BlockSpec((1, tm, D), lambda b, i: (b, i, 0)),
                  pl.BlockSpec((1, D), lambda b, i: (0, 0)),
                  pl.BlockSpec((1, 2, 2, D), lambda b, i: (b, 0, 0, 0)),
                  pl.BlockSpec((D, N), lambda b, i: (0, 0))],
        out_specs=pl.BlockSpec((1, tm, N), lambda b, i: (b, i, 0)),
        out_shape=jax.ShapeDtypeStruct((B, T, N), out_dtype),
        compiler_params=_params("arbitrary", "arbitrary"),
        name="norm_linear",
    )(xa, nw.reshape(1, D), ss, w)


def _linear_resid_kernel(*refs, ks, L):
    n = len(ks)
    a_refs, (w_ref, x_ref, g_ref, o_ref) = refs[:n], refs[n:]
    acc = None
    off = 0
    for a_ref, k in zip(a_refs, ks):
        part = _dot(a_ref[0].astype(BF16), w_ref[off:off + k, :])
        acc = part if acc is None else acc + part
        off += k
    o_ref[0] = x_ref[0] + _per_row(g_ref, pl.program_id(1), x_ref.shape[1], L, 0) * acc


def _linear_resid(a_list, w, xa, gate, L):
    B, T, D = xa.shape
    tm = _wide_tile(T)
    ks = tuple(a.shape[-1] for a in a_list)
    in_specs = [pl.BlockSpec((1, tm, k), lambda b, i: (b, i, 0)) for k in ks]
    in_specs += [pl.BlockSpec(w.shape, lambda b, i: (0, 0)),
                 pl.BlockSpec((1, tm, D), lambda b, i: (b, i, 0)),
                 pl.BlockSpec((1, 2, 1, D), lambda b, i: (b, 0, 0, 0))]
    return pl.pallas_call(
        functools.partial(_linear_resid_kernel, ks=ks, L=L),
        grid=(B, T // tm),
        in_specs=in_specs,
        out_specs=pl.BlockSpec((1, tm, D), lambda b, i: (b, i, 0)),
        out_shape=jax.ShapeDtypeStruct((B, T, D), F32),
        compiler_params=_params("arbitrary", "arbitrary"),
        name="linear_resid",
    )(*a_list, w, xa, gate)


def _rms_kernel(x_ref, w_ref, o_ref):
    x = x_ref[...]
    o_ref[...] = x * lax.rsqrt(jnp.mean(x * x, axis=-1, keepdims=True) + NORM_EPS) * w_ref[...]


def _final_norm(xa, w, L):
    B, T, D = xa.shape
    S = T - L
    tm = _row_tile(L, S)
    nct = L // tm
    nb = max(n for n in (4, 2, 1) if B % n == 0)
    return pl.pallas_call(
        _rms_kernel,
        grid=(B // nb, S // tm),
        in_specs=[pl.BlockSpec((nb, tm, D), lambda b, i: (b, i + nct, 0)),
                  pl.BlockSpec((1, D), lambda b, i: (0, 0))],
        out_specs=pl.BlockSpec((nb, tm, D), lambda b, i: (b, i, 0)),
        out_shape=jax.ShapeDtypeStruct((B, S, D), F32),
        compiler_params=_params("arbitrary", "arbitrary"),
        name="final_norm",
    )(xa, w.reshape(1, D))


def _rms(x, w):
    return x * lax.rsqrt(jnp.mean(x * x, axis=-1, keepdims=True) + NORM_EPS) * w


def _softmax_pv(chains):
    m = [functools.reduce(jnp.maximum, [jnp.max(s, axis=-1, keepdims=True) for s, _ in ch]) for ch in chains]
    p = [[jnp.exp(s - mi) for s, _ in ch] for ch, mi in zip(chains, m)]
    l = [functools.reduce(jnp.add, [jnp.sum(x, axis=-1, keepdims=True) for x in pc]) for pc in p]
    o = [functools.reduce(jnp.add, [_dot(x.astype(BF16), v) for x, (_, v) in zip(pc, ch)]) for pc, ch in zip(p, chains)]
    return [oi / li for oi, li in zip(o, l)]


def _mla_kernel(zq_ref, zkv_ref, zr_ref, qn_ref, kvn_ref, wq_ref, wk_ref, wv_ref, cq_ref, ck_ref, sk_ref,
                o_ref, q_s, k_s, v_s, *, L, tq, scale):
    T = zq_ref.shape[1]
    zqn = _rms(zq_ref[0], qn_ref[...]).astype(BF16)
    zkvn = _rms(zkv_ref[0], kvn_ref[...]).astype(BF16)
    qh = _dot(zqn, wq_ref[0])
    kn = _dot(zkvn, wk_ref[0])
    v_s[...] = _dot(zkvn, wv_ref[0]).astype(BF16)
    zr = zr_ref[0]
    kr = zr[:, :LANES] * ck_ref[...] + zr[:, LANES:] * sk_ref[...]
    cq = cq_ref[...] * scale
    for h in range(2):
        q_s[h] = (qh[:, h * LANES:(h + 1) * LANES] * cq).astype(BF16)
        k_s[h] = (kn[:, h * LANES:(h + 1) * LANES] + kr).astype(BF16)
    def tile(row0, nk, rows):
        first_head = lax.broadcasted_iota(jnp.int32, (rows, LANES), 1) < V_HEAD
        s = [_dot_nt(q_s[h, pl.ds(row0, rows), :], k_s[h, 0:nk, :]) for h in range(2)]
        outs = _softmax_pv([[(si, v_s[0:nk, :])] for si in s])
        o_ref[0, pl.ds(row0, rows), :] = jnp.where(first_head, outs[0], outs[1]).astype(o_ref.dtype)

    for i in range(L // tq):
        tile(i * tq, L, tq)
    tl = 2 * tq if (T - L) % (2 * tq) == 0 else tq
    for i in range((T - L) // tl):
        tile(L + i * tl, T, tl)


def _mla_attention(z, qn_w, kvn_w, wq, wk, wv, cq, ck, sk, L, col_q, col_kv, col_r):
    B, T, _ = z.shape
    HP = wq.shape[0]
    tq = _row_tile(L, T - L)
    scale = float((QK_NOPE + QK_ROPE) ** -0.5)
    return pl.pallas_call(
        functools.partial(_mla_kernel, L=L, tq=tq, scale=scale),
        grid=(B, HP),
        in_specs=[pl.BlockSpec((1, T, Q_RANK), lambda b, p: (b, 0, col_q // Q_RANK)),
                  pl.BlockSpec((1, T, KV_RANK), lambda b, p: (b, 0, col_kv // KV_RANK)),
                  pl.BlockSpec((1, T, 2 * LANES), lambda b, p: (b, 0, col_r // (2 * LANES))),
                  pl.BlockSpec((1, Q_RANK), lambda b, p: (0, 0)),
                  pl.BlockSpec((1, KV_RANK), lambda b, p: (0, 0)),
                  pl.BlockSpec((1, Q_RANK, 2 * LANES), lambda b, p: (p, 0, 0)),
                  pl.BlockSpec((1, KV_RANK, 2 * LANES), lambda b, p: (p, 0, 0)),
                  pl.BlockSpec((1, KV_RANK, LANES), lambda b, p: (p, 0, 0)),
                  pl.BlockSpec((T, LANES), lambda b, p: (0, 0)),
                  pl.BlockSpec((T, LANES), lambda b, p: (0, 0)),
                  pl.BlockSpec((T, LANES), lambda b, p: (0, 0))],
        out_specs=pl.BlockSpec((1, T, LANES), lambda b, p: (b, 0, p)),
        out_shape=jax.ShapeDtypeStruct((B, T, HP * LANES), BF16),
        scratch_shapes=[pltpu.VMEM((2, T, LANES), BF16), pltpu.VMEM((2, T, LANES), BF16),
                        pltpu.VMEM((T, LANES), BF16)],
        compiler_params=_params("arbitrary", "arbitrary"),
        name="mla_attention",
    )(z, z, z, qn_w.reshape(1, -1), kvn_w.reshape(1, -1), wq, wk, wv, cq, ck, sk)


def _na_kernel(q_ref, k_ref, v_ref, bt_ref, o_ref, k_s, v_s, *, L, rows, kr, need_ctx, scale):
    W = GRID_W
    rpb = 4 if rows % 4 == 0 else 1
    k_s[...] = k_ref[0].astype(BF16)
    v_s[...] = v_ref[0].astype(BF16)
    nwin = kr * W
    lane = lax.broadcasted_iota(jnp.int32, (W, LANES), 1)
    head_mask = [(lane < C_HEAD_DIM).astype(F32), (lane >= C_HEAD_DIM).astype(F32)]
    first_head = lane < C_HEAD_DIM
    qcol = lax.broadcasted_iota(jnp.int32, (W, nwin), 0)
    kcol = lax.broadcasted_iota(jnp.int32, (W, nwin), 1) % W
    cstart = jnp.clip(qcol - WIN_C // 2, 0, W - WIN_C)
    col_valid = (kcol >= cstart) & (kcol < cstart + WIN_C)

    def row_block(rb, carry):
        q_blk = q_ref[0, pl.ds(pl.multiple_of(L + rb * (rpb * W), W), rpb * W), :] * scale
        s_ctx = [_dot_nt((q_blk * jnp.concatenate([head_mask[h]] * rpb, axis=0)).astype(BF16), k_s[0:L, :])
                 for h in range(2)]
        chains, q0s = [], []
        for j in range(rpb):
            r = rb * rpb + j
            rs = jnp.clip(r - kr // 2, 0, rows - kr)
            k0 = pl.multiple_of(L + rs * W, W)
            q0s.append(pl.multiple_of(L + r * W, W))
            q = q_blk[j * W:(j + 1) * W]
            kw = k_s[pl.ds(k0, nwin), :]
            vw = v_s[pl.ds(k0, nwin), :]
            dr0 = rs - r + (WIN_R - 1)
            for h in range(2):
                s_nb = _dot_nt((q * head_mask[h]).astype(BF16), kw)
                bias = jnp.concatenate([bt_ref[0, h, dr0 + 2 * m] for m in range(kr // 2)], axis=-1)
                s_nb = jnp.where(col_valid, s_nb + bias, NEG_INF)
                chains.append([(s_nb, vw), (s_ctx[h][j * W:(j + 1) * W], v_s[0:L, :])])
        outs = _softmax_pv(chains)
        for j in range(rpb):
            o_ref[0, pl.ds(q0s[j], W), :] = jnp.where(first_head, outs[2 * j], outs[2 * j + 1]).astype(o_ref.dtype)
        return carry

    lax.fori_loop(0, rows // rpb, row_block, 0, unroll=4)

    tq = min(L, 256)
    lane_c = lax.broadcasted_iota(jnp.int32, (tq, LANES), 1)
    for i in range(L // tq):
        if need_ctx:
            q = q_ref[0, i * tq:(i + 1) * tq, :] * scale
            hm = [lane_c < C_HEAD_DIM, lane_c >= C_HEAD_DIM]
            s = [_dot_nt(jnp.where(hm[h], q, 0.0).astype(BF16), k_s[0:L, :]) for h in range(2)]
            outs = _softmax_pv([[(si, v_s[0:L, :])] for si in s])
            o_ref[0, i * tq:(i + 1) * tq, :] = jnp.where(lane_c < C_HEAD_DIM, outs[0], outs[1]).astype(o_ref.dtype)
        else:
            o_ref[0, i * tq:(i + 1) * tq, :] = jnp.zeros((tq, LANES), o_ref.dtype)


def _na_attention(qkv, bias_tab, L, need_ctx):
    B, T, D3 = qkv.shape
    D = D3 // 3
    HP = D // LANES
    rows = (T - L) // GRID_W
    kr = min(WIN_R, rows)
    assert kr % 2 == 0
    nd = bias_tab.shape[2]
    return pl.pallas_call(
        functools.partial(_na_kernel, L=L, rows=rows, kr=kr, need_ctx=need_ctx, scale=float(C_HEAD_DIM ** -0.5)),
        grid=(B, HP),
        in_specs=[pl.BlockSpec((1, T, LANES), lambda b, p: (b, 0, p)),
                  pl.BlockSpec((1, T, LANES), lambda b, p: (b, 0, HP + p)),
                  pl.BlockSpec((1, T, LANES), lambda b, p: (b, 0, 2 * HP + p)),
                  pl.BlockSpec((1, 2, nd, GRID_W, LANES), lambda b, p: (p, 0, 0, 0, 0))],
        out_specs=pl.BlockSpec((1, T, LANES), lambda b, p: (b, 0, p)),
        out_shape=jax.ShapeDtypeStruct((B, T, D), BF16),
        scratch_shapes=[pltpu.VMEM((T, LANES), BF16), pltpu.VMEM((T, LANES), BF16)],
        compiler_params=_params("arbitrary", "arbitrary"),
        name="na_attention",
    )(qkv, qkv, qkv, bias_tab)


def _na_bias_table(rpb):
    H = rpb.shape[0]
    qc = np.arange(GRID_W)[:, None]
    kc = np.arange(GRID_W)[None, :]
    dc = np.clip(kc - qc + (WIN_C - 1), 0, 2 * WIN_C - 2)
    pick = jnp.asarray(np.arange(2 * WIN_C - 1)[:, None, None] == dc[None], dtype=F32)
    t = jnp.einsum('hdc,cqk->hdqk', rpb, pick, precision=HIGHEST)
    t2 = jnp.concatenate([t[:, :-1], t[:, 1:]], axis=-1)
    return t2.reshape(H // 2, 2, 2 * WIN_R - 2, GRID_W, 2 * GRID_W)


def _bf(x):
    return x.astype(BF16)


def _seg_sum(x, ones_bd):
    hi = _bf(x)
    lo = _bf(x - hi.astype(F32))
    return _dot(hi, ones_bd) + _dot(lo, ones_bd)


def _head_ones(A):
    seg = np.arange(A) // A_HEAD_DIM
    return jnp.asarray(seg[:, None] == seg[None, :], dtype=BF16)


def _rwkv_prep_kernel(z_ref, zp_ref, zn_ref, mu_ref, w0_ref, w2_ref, a0_ref, a2_ref, g2_ref, kk_ref, ka_ref, rk_ref,
                      ones_ref, r_o, v_o, kkn_o, g_o, bonus_o, lw_o, beta_o, kd_o, *, nct, nt, A):
    for s in range(z_ref.shape[0]):
        _rwkv_prep_sample(s, z_ref, zp_ref, zn_ref, mu_ref, w0_ref, w2_ref, a0_ref, a2_ref, g2_ref, kk_ref, ka_ref,
                          rk_ref, ones_ref, r_o, v_o, kkn_o, g_o, bonus_o, lw_o, beta_o, kd_o, nct=nct, nt=nt, A=A)


def _rwkv_prep_sample(s, z_ref, zp_ref, zn_ref, mu_ref, w0_ref, w2_ref, a0_ref, a2_ref, g2_ref, kk_ref, ka_ref, rk_ref,
                      ones_ref, r_o, v_o, kkn_o, g_o, bonus_o, lw_o, beta_o, kd_o, *, nct, nt, A):
    i = pl.program_id(1)
    za = z_ref[s]
    tm = za.shape[0]
    row = lax.broadcasted_iota(jnp.int32, za.shape, 0)
    seg_first = (i == 0) | (i == nct)
    seg_last = (i == nct - 1) | (i == nt - 1)
    prev_row = jnp.where(seg_first, 0.0, zp_ref[s, 7:8, :])
    next_row = jnp.where(seg_last, 0.0, zn_ref[s, 0:1, :])
    prev = jnp.where(row == 0, prev_row, pltpu.roll(za, 1, 0))
    nxt = jnp.where(row == tm - 1, next_row, pltpu.roll(za, tm - 1, 0))
    zs = za + mu_ref[0:1, :] * (prev - za) + mu_ref[1:2, :] * (nxt - za)
    r = zs[:, 0:A]
    k = zs[:, A:2 * A]
    v = zs[:, 2 * A:3 * A]
    wd = _bf(jnp.tanh(zs[:, 3 * A:3 * A + LANES]))
    ad = _bf(zs[:, 3 * A + LANES:3 * A + 2 * LANES])
    gd = _bf(jax.nn.sigmoid(zs[:, 3 * A + 2 * LANES:3 * A + 3 * LANES]))
    ones = ones_ref[...]
    kk = k * kk_ref[...]
    kkn = kk / jnp.maximum(jnp.sqrt(_seg_sum(kk * kk, ones)), 1e-12)
    kd_sum = None
    for d in range(2):
        u = w0_ref[d:d + 1, :] + _dot(wd, w2_ref[d])
        lw_o[d, s] = -DECAY_FLOOR_SCALE * jax.nn.sigmoid(u)
        a = jax.nn.sigmoid(a0_ref[d:d + 1, :] + _dot(ad, a2_ref[d]))
        beta_o[d, s] = kkn * a
        kd = k * (1.0 + (a - 1.0) * ka_ref[...])
        kd_o[d, s] = kd
        kd_sum = kd if kd_sum is None else kd_sum + kd
    bonus_o[s] = _seg_sum(r * kd_sum * rk_ref[...], ones) * v
    r_o[s] = r
    v_o[s] = v
    kkn_o[s] = kkn
    g_o[s] = _dot(gd, g2_ref[...])


def _rwkv_prep(z, L, a_cols, mu, w0, w2, a0, a2, g2, k_k, k_a, r_k):
    B, T, _ = z.shape
    A = w0.shape[-1]
    assert 2 * LORA_W == LANES and 2 * LORA_A == LANES and LORA_G == LANES and a_cols == 3 * A + 3 * LANES
    tm = _row_tile(L, T - L)
    nt = T // tm
    hb = tm // 8

    def pad_lora(w):
        zero = jnp.zeros_like(w[0])
        return _bf(jnp.stack([jnp.concatenate([w[0], zero], 0), jnp.concatenate([zero, w[1]], 0)]))

    def const(shape):
        return pl.BlockSpec(shape, lambda b, i: (0,) * len(shape))

    nb = 2 if B % 2 == 0 else 1
    tile = pl.BlockSpec((nb, tm, A), lambda b, i: (b, i, 0))
    tile_d = pl.BlockSpec((2, nb, tm, A), lambda b, i: (0, b, i, 0))
    sd = jax.ShapeDtypeStruct((B, T, A), F32)
    sd_d = jax.ShapeDtypeStruct((2, B, T, A), F32)
    return pl.pallas_call(
        functools.partial(_rwkv_prep_kernel, nct=L // tm, nt=nt, A=A),
        grid=(B // nb, nt),
        in_specs=[pl.BlockSpec((nb, tm, a_cols), lambda b, i: (b, i, 0)),
                  pl.BlockSpec((nb, 8, a_cols), lambda b, i: (b, jnp.maximum(i * hb - 1, 0), 0)),
                  pl.BlockSpec((nb, 8, a_cols), lambda b, i: (b, jnp.minimum((i + 1) * hb, T // 8 - 1), 0)),
                  const((2, a_cols)), const((2, A)), const((2, 2 * LORA_W, A)), const((2, A)),
                  const((2, 2 * LORA_A, A)), const((LORA_G, A)), const((1, A)), const((1, A)), const((1, A)),
                  const((A, A))],
        out_specs=[tile, tile, tile, tile, tile, tile_d, tile_d, tile_d],
        out_shape=[sd, sd, sd, sd, sd, sd_d, sd_d, sd_d],
        compiler_params=_params("arbitrary", "arbitrary"),
        name="rwkv_prep",
    )(z, z, z, mu, w0, pad_lora(w2), a0, pad_lora(a2), _bf(g2), k_k.reshape(1, A), k_a.reshape(1, A),
      r_k.reshape(1, A), _head_ones(A))


def _tri_inverse(lms, eye, m16, m32):
    d0 = [_bf(jnp.where(m16, lm, 0.0)) for lm in lms]
    t = [eye + d.astype(F32) for d in d0]
    s = [_dot(d, d) for d in d0]
    for step in range(3):
        sb = [_bf(x) for x in s]
        t = [x + _dot(_bf(x), y) for x, y in zip(t, sb)]
        if step < 2:
            s = [_dot(y, y) for y in sb]
    for lvl in (m32 & (~m16), ~m32):
        tb = [_bf(x) for x in t]
        w = [_bf(_dot(_bf(jnp.where(lvl, lm, 0.0)), y)) for lm, y in zip(lms, tb)]
        t = [x + _dot(y, z) for x, y, z in zip(t, tb, w)]
    return t


def _wkv_kernel(*refs, NP, NB):
    C = SCAN_CHUNK
    P = 2 * C
    fwd_refs, bwd_refs, (yf_ref, yb_ref, h_s) = refs[0:6], refs[6:12], refs[12:]

    @pl.when(pl.program_id(1) == 0)
    def _():
        h_s[...] = jnp.zeros(h_s.shape, F32)

    ri = lax.broadcasted_iota(jnp.int32, (P, P), 0)
    ci = lax.broadcasted_iota(jnp.int32, (P, P), 1)
    same = (ri // C) == (ci // C)
    diff = (ri % C) - (ci % C)
    eye_b = ri == ci
    eye = eye_b.astype(F32)
    m16 = (ri // 16) == (ci // 16)
    m32 = (ri // 32) == (ci // 32)
    diff64 = lax.broadcasted_iota(jnp.int32, (C, C), 0) - lax.broadcasted_iota(jnp.int32, (C, C), 1)
    top = lax.broadcasted_iota(jnp.int32, (C, P), 1) < C
    zero_blk = jnp.zeros((P, P), BF16)
    sls = [slice(p * P, (p + 1) * P) for p in range(NP)]

    def bd(x):
        return [_bf(jnp.concatenate([jnp.where(top, x[:, sl], 0.0), jnp.where(top, 0.0, x[:, sl])], axis=0))
                for sl in sls]

    at, rt, bt, kt, bh, kh, vv, etots, before, before_eq = [], [], [], [], [], [], [], [], [], []
    for (r_ref, v_ref, kk_ref, lw_ref, beta_ref, kd_ref), sgn in ((fwd_refs, 1), (bwd_refs, -1)):
        order = diff * sgn
        tri = _bf(((diff64 * sgn) >= 0).astype(F32))
        for s in range(NB):
            before += [same & (order > 0)] * NP
            before_eq += [same & (order >= 0)] * NP
            lw = lw_ref[0, s]
            lw_hi = _bf(lw)
            lw_md = _bf(lw - lw_hi.astype(F32))
            lw_lo = _bf(lw - lw_hi.astype(F32) - lw_md.astype(F32))
            cum = _dot(tri, lw_hi) + _dot(tri, lw_md) + _dot(tri, lw_lo)
            tot = jnp.sum(lw, axis=0, keepdims=True)
            beta = beta_ref[0, s]
            kd = kd_ref[0, s]
            e_neg = jnp.exp(-cum)
            e_tail = jnp.exp(tot - cum)
            at += bd(-kk_ref[s] * jnp.exp(cum - lw))
            rt += bd(r_ref[s] * jnp.exp(cum))
            bt += bd(beta * e_neg)
            kt += bd(kd * e_neg)
            bh += bd(beta * e_tail)
            kh += bd(kd * e_tail)
            vv += bd(v_ref[s])
            etot = jnp.exp(tot)
            etots += [etot[:, sl] for sl in sls]

    ar = [jnp.concatenate([a, r], axis=0) for a, r in zip(at, rt)]
    arb = [_dot_nt(x, b) for x, b in zip(ar, bt)]
    ark = [_dot_nt(x, k) for x, k in zip(ar, kt)]
    lab = [jnp.where(m, x[:P], 0.0) for x, m in zip(arb, before)]
    tinv = _tri_inverse(lab, eye, m16, m32)
    u = [_dot(_bf(jnp.where(m, x[:P], 0.0)), v) for x, v, m in zip(ark, vv, before)]
    x = [_bf(_dot(_bf(t), jnp.concatenate([a, _bf(w)], axis=1))) for t, a, w in zip(tinv, at, u)]
    rhs = [jnp.concatenate([xi, jnp.concatenate([zero_blk, v], axis=1)], axis=0) for xi, v in zip(x, vv)]
    mn = [lax.dot_general(jnp.concatenate([b, k], axis=0), w, (((0,), (0,)), ((), ())), preferred_element_type=F32)
          for b, k, w in zip(bh, kh, rhs)]
    lr = [_bf(jnp.concatenate([jnp.where(m, xb[P:], 0.0), jnp.where(m, xk[P:], 0.0)], axis=1))
          for xb, xk, m in zip(arb, ark, before_eq)]
    qy = [_dot(l, w) for l, w in zip(lr, rhs)]
    qm = [_bf(jnp.concatenate([r.astype(F32) + q[:, :P], jnp.where(eye_b, e, 0.0) + m[:, :P]], axis=0))
          for r, q, m, e in zip(rt, qy, mn, etots)]
    nchain = 2 * NB * NP
    hin = [h_s[i] for i in range(nchain)]
    h_hi = [_bf(h) for h in hin]
    h_lo = [_bf(h - hh.astype(F32)) for h, hh in zip(hin, h_hi)]
    res = [_dot(w, hh) + _dot(w, hl) for w, hh, hl in zip(qm, h_hi, h_lo)]
    for i in range(nchain):
        ybd = res[i][:P] + qy[i][:, P:]
        y_ref = yf_ref if i < NB * NP else yb_ref
        y_ref[(i // NP) % NB, :, sls[i % NP]] = ybd[:C] + ybd[C:]
        h_s[i] = res[i][P:] + mn[i][:, P:]


def _wkv_scan(r, v, kk, lw, beta, kd, L):
    B, T, A = r.shape
    C = SCAN_CHUNK
    nC = T // C
    nct = L // C
    NP = A // (2 * C)

    def rev(c):
        return jnp.where(c < nct, nct - 1 - c, nC - 1 - (c - nct))

    NB = 2 if B % 2 == 0 else 1
    fwd = pl.BlockSpec((NB, C, A), lambda b, c: (b, c, 0))
    bwd = pl.BlockSpec((NB, C, A), lambda b, c: (b, rev(c), 0))
    fwd_d = pl.BlockSpec((1, NB, C, A), lambda b, c: (0, b, c, 0))
    bwd_d = pl.BlockSpec((1, NB, C, A), lambda b, c: (1, b, rev(c), 0))
    sd = jax.ShapeDtypeStruct((B, T, A), F32)
    return pl.pallas_call(
        functools.partial(_wkv_kernel, NP=NP, NB=NB),
        grid=(B // NB, nC),
        in_specs=[fwd, fwd, fwd, fwd_d, fwd_d, fwd_d, bwd, bwd, bwd, bwd_d, bwd_d, bwd_d],
        out_specs=[fwd, bwd],
        out_shape=[sd, sd],
        scratch_shapes=[pltpu.VMEM((2 * NB * NP, 2 * C, 2 * C), F32)],
        compiler_params=_params("arbitrary", "arbitrary"),
        name="wkv_scan",
    )(r, v, kk, lw, beta, kd, r, v, kk, lw, beta, kd)


def _rwkv_post_kernel(yf_ref, yb_ref, bonus_ref, g_ref, lnw_ref, lnb_ref, ones_ref, o_ref):
    y = yf_ref[0] + yb_ref[0]
    ones = ones_ref[...]
    inv_n = 1.0 / A_HEAD_DIM
    d = y - _seg_sum(y, ones) * inv_n
    var = _seg_sum(d * d, ones) * inv_n
    yn = d * lax.rsqrt(var + GN_EPS) * lnw_ref[...] + lnb_ref[...]
    o_ref[0] = ((yn + bonus_ref[0]) * g_ref[0]).astype(o_ref.dtype)


def _rwkv_post(yf, yb, bonus, g, ln_w, ln_b, L):
    B, T, A = bonus.shape
    tm = _wide_tile(T)
    tile = pl.BlockSpec((1, tm, A), lambda b, i: (b, i, 0))
    vec = pl.BlockSpec((1, A), lambda b, i: (0, 0))
    return pl.pallas_call(
        _rwkv_post_kernel,
        grid=(B, T // tm),
        in_specs=[tile, tile, tile, tile, vec, vec, pl.BlockSpec((A, A), lambda b, i: (0, 0))],
        out_specs=tile,
        out_shape=jax.ShapeDtypeStruct((B, T, A), BF16),
        compiler_params=_params("arbitrary", "arbitrary"),
        name="rwkv_post",
    )(yf, yb, bonus, g, ln_w.reshape(1, A), ln_b.reshape(1, A), _head_ones(A))


def _rwkv_mixer(z, L, a_cols, mu, w0, w2, a0, a2, g2, k_k, k_a, r_k, ln_w, ln_b):
    r, v, kk, g, bonus, lw, beta, kd = _rwkv_prep(z, L, a_cols, mu, w0, w2, a0, a2, g2, k_k, k_a, r_k)
    yf, yb = _wkv_scan(r, v, kk, lw, beta, kd, L)
    return _rwkv_post(yf, yb, bonus, g, ln_w, ln_b, L)


def _router_kernel(x_ref, nw_ref, ss_ref, wr_ref, h_ref, aff_ref, *, L):
    i = pl.program_id(1)
    tm = x_ref.shape[1]
    h = _norm_mod(x_ref[0], nw_ref[...], _per_row(ss_ref, i, tm, L, 0), _per_row(ss_ref, i, tm, L, 1))
    h_hi = h.astype(BF16)
    h_ref[0] = h_hi
    h_lo = _bf(h - h_hi.astype(F32))
    w = wr_ref[...]
    w_hi = _bf(w)
    w_lo = _bf(w - w_hi.astype(F32))
    logits = _dot_nt(w_hi, h_hi) + _dot_nt(w_hi, h_lo) + _dot_nt(w_lo, h_hi)
    m = jnp.max(logits, axis=0, keepdims=True)
    p = jnp.exp(logits - m)
    aff_ref[0] = p / jnp.sum(p, axis=0, keepdims=True)


def _norm_router(xa, nw, ss, w_router, L):
    B, T, D = xa.shape
    E = w_router.shape[1]
    tm = max(m for m in range(LANES, MAX_ROW_TILE + 1, LANES) if T % m == 0)
    return pl.pallas_call(
        functools.partial(_router_kernel, L=L),
        grid=(B, T // tm),
        in_specs=[pl.BlockSpec((1, tm, D), lambda b, i: (b, i, 0)),
                  pl.BlockSpec((1, D), lambda b, i: (0, 0)),
                  pl.BlockSpec((1, 2, 2, D), lambda b, i: (b, 0, 0, 0)),
                  pl.BlockSpec((E, D), lambda b, i: (0, 0))],
        out_specs=[pl.BlockSpec((1, tm, D), lambda b, i: (b, i, 0)),
                   pl.BlockSpec((1, E, tm), lambda b, i: (b, 0, i))],
        out_shape=[jax.ShapeDtypeStruct((B, T, D), BF16), jax.ShapeDtypeStruct((B, E, T), F32)],
        compiler_params=_params("arbitrary", "arbitrary"),
        name="norm_router",
    )(xa, nw.reshape(1, D), ss, w_router.T)


def _select_top(sets):
    keys = [pltpu.bitcast(aff, jnp.int32) for aff, _, _ in sets]
    E = keys[0].shape[0]

    def count_ge(key, thr):
        return jnp.sum(jnp.where(key >= thr, 1.0, 0.0), axis=1, keepdims=True)

    def narrow(_, carry):
        out = []
        for key, (_, cap, _), (lo, hi) in zip(keys, sets, carry):
            q = (hi - lo + 3) >> 2
            m1 = jnp.minimum(lo + q, hi)
            m2 = jnp.minimum(lo + 2 * q, hi)
            m3 = jnp.minimum(lo + 3 * q, hi)
            ok1, ok2, ok3 = count_ge(key, m1) >= cap, count_ge(key, m2) >= cap, count_ge(key, m3) >= cap
            new_lo = jnp.where(ok3, m3, jnp.where(ok2, m2, jnp.where(ok1, m1, lo)))
            new_hi = jnp.where(ok3, hi, jnp.where(ok2, m3 - 1, jnp.where(ok1, m2 - 1, m1 - 1)))
            out.append((new_lo, jnp.maximum(new_hi, new_lo)))
        return tuple(out)

    init = tuple((jnp.zeros((E, 1), jnp.int32), jnp.full((E, 1), 0x7F800000, jnp.int32)) for _ in sets)
    bounds = lax.fori_loop(0, 17, narrow, init)

    def prefix(x, tri):
        pb = tri.shape[0]
        parts, offset = [], jnp.zeros((E, 1), F32)
        for j in range(x.shape[1] // pb):
            xj = x[:, j * pb:(j + 1) * pb]
            inner = _dot(xj.astype(BF16), tri)
            parts.append(inner + offset)
            offset = offset + inner[:, pb - 1:pb] + xj[:, pb - 1:pb]
        return parts[0] if len(parts) == 1 else jnp.concatenate(parts, axis=1)

    res = []
    for key, (_, cap, tri), (thr, _) in zip(keys, sets, bounds):
        above = key > thr
        tie = key == thr
        need = cap - jnp.sum(jnp.where(above, 1.0, 0.0), axis=1, keepdims=True)
        tie_rank = prefix(jnp.where(tie, 1.0, 0.0), tri)
        sel = above | (tie & (tie_rank < need))
        res.append(jnp.where(sel, prefix(jnp.where(sel, 1.0, 0.0), tri), -1.0))
    return res


def _route_gather_kernel(aff_ref, h_ref, x_ref, pos_ref, gate_ref, slot_s, tri_s, *, L, cap_l, cap_c):
    b = pl.program_id(0)
    eg = pl.program_id(1)
    T = h_ref.shape[1]
    S = T - L

    @pl.when((b == 0) & (eg == 0))
    def _():
        n = tri_s.shape[0]
        tri_s[...] = jnp.where(lax.broadcasted_iota(jnp.int32, (n, n), 0) < lax.broadcasted_iota(jnp.int32, (n, n), 1),
                               1.0, 0.0).astype(BF16)

    @pl.when(eg == 0)
    def _():
        sets = [(aff_ref[0, :, L:], cap_l, tri_s[...])]
        if cap_c:
            sets.append((aff_ref[0, :, 0:L], cap_c, tri_s[...]))
        picked = _select_top(sets)
        slot_s[:, L:] = picked[0]
        if cap_c:
            slot_s[:, 0:L] = picked[1]

    def gather(k, lo, n, cap, row0):
        e = eg * x_ref.shape[0] + k
        slot = slot_s[pl.ds(e, 1), lo:lo + n]
        hit = slot == lax.broadcasted_iota(jnp.int32, (cap, n), 0).astype(F32)
        x_ref[k, 0, row0:row0 + cap, :] = _dot(jnp.where(hit, 1.0, 0.0).astype(BF16), h_ref[0, lo:lo + n, :]).astype(BF16)
        tok = lax.broadcasted_iota(jnp.int32, (cap, n), 1) + lo
        pos_ref[0, k, row0:row0 + cap, :] = jnp.sum(jnp.where(hit, tok, 0), axis=1, keepdims=True)
        aff = aff_ref[0, pl.ds(e, 1), lo:lo + n]
        gate_ref[0, k, row0:row0 + cap, :] = jnp.sum(jnp.where(hit, aff, 0.0), axis=1, keepdims=True)

    for k in range(x_ref.shape[0]):
        gather(k, L, S, cap_l, 0)
        if cap_c:
            gather(k, 0, L, cap_c, cap_l)


def _route_gather(aff, h2, L, cap_l, cap_c):
    B, T, D = h2.shape
    E = aff.shape[1]
    Ct = cap_l + cap_c
    ne = 2 if E % 2 == 0 else 1
    return pl.pallas_call(
        functools.partial(_route_gather_kernel, L=L, cap_l=cap_l, cap_c=cap_c),
        grid=(B, E // ne),
        in_specs=[pl.BlockSpec((1, E, T), lambda b, e: (b, 0, 0)),
                  pl.BlockSpec((1, T, D), lambda b, e: (b, 0, 0))],
        out_specs=[pl.BlockSpec((ne, 1, Ct, D), lambda b, e: (e, b, 0, 0)),
                   pl.BlockSpec((1, ne, Ct, 1), lambda b, e: (b, e, 0, 0)),
                   pl.BlockSpec((1, ne, Ct, 1), lambda b, e: (b, e, 0, 0))],
        out_shape=[jax.ShapeDtypeStruct((E, B, Ct, D), BF16), jax.ShapeDtypeStruct((B, E, Ct, 1), jnp.int32),
                   jax.ShapeDtypeStruct((B, E, Ct, 1), F32)],
        scratch_shapes=[pltpu.VMEM((E, T), F32), pltpu.VMEM((_row_tile(L, T - L),) * 2, BF16)],
        compiler_params=_params("arbitrary", "arbitrary"),
        name="moe_route_gather",
    )(aff, h2)


def _ffn_kernel(x_ref, w1_ref, w3_ref, w2_ref, o_ref, *, rm):
    j = pl.program_id(1)
    R = x_ref.shape[1]
    w1 = w1_ref[0, 0].astype(BF16)
    w3 = w3_ref[0, 0].astype(BF16)
    w2 = w2_ref[0, 0].astype(BF16)

    @pl.when(j == 0)
    def _():
        o_ref[...] = jnp.zeros(o_ref.shape, F32)

    def rows(i, carry):
        r0 = pl.multiple_of(i * rm, rm)
        x = x_ref[0, pl.ds(r0, rm), :]
        a = _dot(x, w1)
        b = _dot(x, w3)
        hid = (a * jax.nn.sigmoid(a) * b).astype(BF16)
        o_ref[0, pl.ds(r0, rm), :] += _dot(hid, w2)
        return carry

    lax.fori_loop(0, R // rm, rows, 0, unroll=True)


def _expert_ffn(xin, w1, w3, w2, layer):
    E, R, D = xin.shape
    F = w1.shape[-1]
    tf = min(512, F)
    rm = max(m for m in (MAX_ROW_TILE, 512, 256, 128, 64, 32, 16) if R % m == 0)
    return pl.pallas_call(
        functools.partial(_ffn_kernel, rm=rm),
        grid=(E, F // tf),
        in_specs=[pl.BlockSpec((1, R, D), lambda e, j: (e, 0, 0)),
                  pl.BlockSpec((1, 1, D, tf), lambda e, j: (layer, e, 0, j)),
                  pl.BlockSpec((1, 1, D, tf), lambda e, j: (layer, e, 0, j)),
                  pl.BlockSpec((1, 1, tf, D), lambda e, j: (layer, e, j, 0))],
        out_specs=pl.BlockSpec((1, R, D), lambda e, j: (e, 0, 0)),
        out_shape=jax.ShapeDtypeStruct((E, R, D), F32),
        compiler_params=_params("arbitrary", "arbitrary"),
        name="expert_ffn",
    )(xin, w1, w3, w2)


def _combine_kernel(y_ref, gate_ref, posl_ref, posc_ref, x_ref, g_ref, o_ref, yg_s, *, tq, nct, cap_l):
    E, _, Ct, td = y_ref.shape
    T = x_ref.shape[1]
    cap_c = Ct - cap_l
    nl = E * cap_l
    for e in range(E):
        yg_s[e * cap_l:(e + 1) * cap_l, :] = (y_ref[e, 0, :cap_l, :] * gate_ref[0, e, :cap_l, :]).astype(BF16)
        if cap_c:
            yg_s[nl + e * cap_c:nl + (e + 1) * cap_c, :] = (y_ref[e, 0, cap_l:, :] * gate_ref[0, e, cap_l:, :]).astype(BF16)

    def tile(i, gate_row, pos, lo, n):
        r0 = pl.multiple_of(i * tq, tq)
        tok = lax.broadcasted_iota(jnp.int32, (tq, n), 0) + r0
        onehot = jnp.where(tok == pos, 1.0, 0.0).astype(BF16)
        o_ref[0, pl.ds(r0, tq), :] = x_ref[0, pl.ds(r0, tq), :] + gate_row * _dot(onehot, yg_s[lo:lo + n, :])

    for i in range(nct):
        if cap_c:
            tile(i, g_ref[0, 0], posc_ref[0], nl, E * cap_c)
        else:
            o_ref[0, i * tq:(i + 1) * tq, :] = x_ref[0, i * tq:(i + 1) * tq, :]

    def body(i, carry):
        tile(i, g_ref[0, 1], posl_ref[0], 0, nl)
        return carry

    lax.fori_loop(nct, T // tq, body, 0, unroll=2)


def _moe_combine(y, gate, pos_l, pos_c, xa, g2, L, cap_l):
    E, B, Ct, D = y.shape
    T = xa.shape[1]
    td = min(512, D)
    tq = _row_tile(L, T - L)
    return pl.pallas_call(
        functools.partial(_combine_kernel, tq=tq, nct=L // tq, cap_l=cap_l),
        grid=(B, D // td),
        in_specs=[pl.BlockSpec((E, 1, Ct, td), lambda b, j: (0, b, 0, j)),
                  pl.BlockSpec((1, E, Ct, 1), lambda b, j: (b, 0, 0, 0)),
                  pl.BlockSpec((1, 1, pos_l.shape[-1]), lambda b, j: (b, 0, 0)),
                  pl.BlockSpec((1, 1, pos_c.shape[-1]), lambda b, j: (b, 0, 0)),
                  pl.BlockSpec((1, T, td), lambda b, j: (b, 0, j)),
                  pl.BlockSpec((1, 2, 1, td), lambda b, j: (b, 0, 0, j))],
        out_specs=pl.BlockSpec((1, T, td), lambda b, j: (b, 0, j)),
        out_shape=jax.ShapeDtypeStruct((B, T, D), F32),
        scratch_shapes=[pltpu.VMEM((E * Ct, td), BF16)],
        compiler_params=_params("arbitrary", "arbitrary"),
        name="moe_combine",
    )(y, gate, pos_l, pos_c, xa, g2)


def _moe(xa, nw, ss2, g2, w_router, w1, w3, w2, layer, L, need_ctx):
    B, T, D = xa.shape
    S = T - L
    E = w_router.shape[1]
    h2, aff = _norm_router(xa, nw, ss2, w_router, L)
    cap_l = CAPACITY_FACTOR * S // E
    cap_c = CAPACITY_FACTOR * L // E if need_ctx else 0
    Ct = cap_l + cap_c
    xin, pos, gate = _route_gather(aff, h2, L, cap_l, cap_c)
    pos_l = pos[:, :, :cap_l, 0].reshape(B, 1, E * cap_l)
    if cap_c:
        pos_c = pos[:, :, cap_l:, 0].reshape(B, 1, E * cap_c)
    else:
        pos_c = jnp.zeros((B, 1, LANES), jnp.int32)
    y = _expert_ffn(xin.reshape(E, B * Ct, D), w1, w3, w2, layer).reshape(E, B, Ct, D)
    return _moe_combine(y, gate, pos_l, pos_c, xa, g2, L, cap_l)


_PERM_EO = np.concatenate([np.arange(0, QK_ROPE, 2), np.arange(1, QK_ROPE, 2)])
_PERM_OE = np.concatenate([np.arange(1, QK_ROPE, 2), np.arange(0, QK_ROPE, 2)])


def _ab_input_weight(w_in, a_cols):
    D = w_in.shape[0]
    zr = w_in[:, a_cols + Q_RANK + KV_RANK:]
    zero = jnp.zeros((D, LANES - 2 * QK_ROPE), w_in.dtype)
    g1 = jnp.concatenate([zero, zr[:, _PERM_EO], zr[:, _PERM_EO]], axis=1)
    g2 = jnp.concatenate([zero, zr[:, _PERM_OE], zr[:, _PERM_OE]], axis=1)
    return jnp.concatenate([w_in[:, :a_cols + Q_RANK + KV_RANK], g1, g2], axis=1).astype(BF16)


def _mla_weights(w_qup, w_kvup):
    NH = w_qup.shape[1] // (QK_NOPE + QK_ROPE)
    wq = w_qup.reshape(Q_RANK, NH, QK_NOPE + QK_ROPE)
    rope = wq[:, :, QK_NOPE:]
    wq = jnp.concatenate([wq[:, :, :QK_NOPE], rope[:, :, _PERM_EO], rope[:, :, _PERM_OE]], axis=-1)
    wq = wq.reshape(Q_RANK, NH // 2, 2 * LANES).transpose(1, 0, 2)
    wkv = w_kvup.reshape(KV_RANK, NH, QK_NOPE + V_HEAD)
    wk = jnp.concatenate([wkv[:, :, :QK_NOPE], jnp.zeros((KV_RANK, NH, LANES - QK_NOPE), w_kvup.dtype)], axis=-1)
    wk = wk.reshape(KV_RANK, NH // 2, 2 * LANES).transpose(1, 0, 2)
    wv = wkv[:, :, QK_NOPE:].reshape(KV_RANK, NH // 2, 2 * V_HEAD).transpose(1, 0, 2)
    return wq.astype(BF16), wk.astype(BF16), wv.astype(BF16)


def _rope_tables(L, S):
    t = np.arange(S)
    row = (t // GRID_W).astype(np.float32)
    col = (t % GRID_W).astype(np.float32)
    n_freq = QK_ROPE // 4
    inv = (ROPE_BASE ** (-np.arange(n_freq, dtype=np.float32) / n_freq)).astype(np.float32)
    ang = jnp.concatenate([jnp.asarray(row[:, None] * inv), jnp.asarray(col[:, None] * inv)], axis=-1)
    cos = jnp.concatenate([jnp.ones((L, QK_ROPE // 2), F32), jnp.cos(ang)], axis=0)
    sin = jnp.concatenate([jnp.zeros((L, QK_ROPE // 2), F32), jnp.sin(ang)], axis=0)
    T = L + S
    cc = jnp.concatenate([cos, cos], axis=1)
    ss = jnp.concatenate([-sin, sin], axis=1)
    one = jnp.ones((T, LANES - 2 * QK_ROPE), F32)
    zero = jnp.zeros((T, LANES - 2 * QK_ROPE), F32)
    cq = jnp.concatenate([one, cc, ss], axis=1)
    ck = jnp.concatenate([zero, cc, cc], axis=1)
    sk = jnp.concatenate([zero, ss, ss], axis=1)
    return cq, ck, sk


def kernel(x, c, ctx, c_ctx, mod_w, mod_b, norm1_w, norm2_w, final_norm_w, ab_w_in, ab_w_out, rk_mu, rk_w0, rk_w2, rk_a0, rk_a2, rk_g2, rk_kk, rk_ka, rk_rk, rk_ln_w, rk_ln_b, mla_qn_w, mla_w_qup, mla_kvn_w, mla_w_kvup, na_w_qkv, na_rpb, na_w_out, moe_router, moe_w1, moe_w3, moe_w2):
    B, S, D = x.shape
    L = ctx.shape[1]
    depth = mod_w.shape[0]
    A = rk_w0.shape[-1]
    a_cols = rk_mu.shape[-1]

    rows_pad = -(B + 1) % 8
    cvec = jnp.concatenate([c, c_ctx[None], jnp.zeros((rows_pad, D), F32)], axis=0)
    mods = _mod_vectors(cvec, mod_w, mod_b)
    m_lat = mods[:, :B].reshape(depth, B, 6, D)
    m_ctx = jnp.broadcast_to(mods[:, B].reshape(depth, 1, 6, D), (depth, B, 6, D))
    mm = jnp.stack([m_ctx, m_lat], axis=2)

    cq, ck, sk = _rope_tables(L, S)
    xa = jnp.concatenate([ctx, x], axis=1)

    for layer in range(depth):
        need_ctx = layer < depth - 1
        i = layer // 2
        m = mm[layer]
        ss1, g1 = m[:, :, 0:2], m[:, :, 2:3]
        ss2, g2 = m[:, :, 3:5], m[:, :, 5:6]
        if layer % 2 == 0:
            w_in = _ab_input_weight(ab_w_in[i], a_cols)
            z = _norm_linear(xa, norm1_w[layer], ss1, w_in, L)
            o_a = _rwkv_mixer(z, L, a_cols, rk_mu[i], rk_w0[i], rk_w2[i], rk_a0[i], rk_a2[i], rk_g2[i],
                              rk_kk[i], rk_ka[i], rk_rk[i], rk_ln_w[i], rk_ln_b[i])
            wq, wk, wv = _mla_weights(mla_w_qup[i], mla_w_kvup[i])
            o_b = _mla_attention(z, mla_qn_w[i], mla_kvn_w[i], wq, wk, wv, cq, ck, sk, L,
                                 a_cols, a_cols + Q_RANK, a_cols + Q_RANK + KV_RANK)
            mixed, w_out = [o_a, o_b], ab_w_out[i]
        else:
            qkv = _norm_linear(xa, norm1_w[layer], ss1, na_w_qkv[i].astype(BF16), L, out_dtype=BF16)
            mixed, w_out = [_na_attention(qkv, _na_bias_table(na_rpb[i]), L, need_ctx)], na_w_out[i]
        xa = _linear_resid(mixed, w_out.astype(BF16), xa, g1, L)
        xa = _moe(xa, norm2_w[layer], ss2, g2, moe_router[layer], moe_w1, moe_w3, moe_w2, layer, L, need_ctx)
    return _final_norm(xa, final_norm_w, L)
```

```python
import functools

import jax
import jax.numpy as jnp
import numpy as np
from jax import lax
from jax.experimental import pallas as pl
from jax.experimental.pallas import tpu as pltpu

F32 = jnp.float32
BF16 = jnp.bfloat16
HIGHEST = lax.Precision.HIGHEST

GRID_W = 64
NORM_EPS = 1e-6
NEG_INF = -1e30
GN_EPS = 64e-5
A_HEAD_DIM = 64
LORA_W = 64
LORA_A = 64
LORA_G = 128
QK_NOPE = 64
QK_ROPE = 32
V_HEAD = 64
Q_RANK = 384
KV_RANK = 256
ROPE_BASE = 10000.0
C_HEAD_DIM = 64
WIN_R = 8
WIN_C = 16
N_EXPERTS = 16
CAPACITY_FACTOR = 2
SCAN_CHUNK = 64
DECAY_FLOOR_SCALE = float(np.exp(-0.5))
LANES = 128
MAX_ROW_TILE = 768

VMEM_LIMIT = 56 * 1024 * 1024


def _params(*sem):
    return pltpu.CompilerParams(dimension_semantics=sem, vmem_limit_bytes=VMEM_LIMIT)


def _dot(a, b, precision=None):
    return jnp.dot(a, b, preferred_element_type=F32, precision=precision)


def _dot_nt(a, b, precision=None):
    return lax.dot_general(a, b, (((1,), (1,)), ((), ())), preferred_element_type=F32, precision=precision)


def _row_tile(L, S):
    tm = 256
    while L % tm or S % tm:
        tm //= 2
    return tm


def _mod_kernel(c_ref, w_ref, b_ref, o_ref):
    c = c_ref[...]
    sc = c * jax.nn.sigmoid(c)
    o_ref[0] = _dot(sc.astype(BF16), w_ref[0].astype(BF16)) + b_ref[0]


def _mod_vectors(cvec, mod_w, mod_b):
    depth, D, N = mod_w.shape
    R = cvec.shape[0]
    tn = 1024
    return pl.pallas_call(
        _mod_kernel,
        grid=(depth, N // tn),
        in_specs=[pl.BlockSpec((R, D), lambda l, j: (0, 0)),
                  pl.BlockSpec((1, D, tn), lambda l, j: (l, 0, j)),
                  pl.BlockSpec((1, 1, tn), lambda l, j: (l, 0, j))],
        out_specs=pl.BlockSpec((1, R, tn), lambda l, j: (l, 0, j)),
        out_shape=jax.ShapeDtypeStruct((depth, R, N), F32),
        compiler_params=_params("arbitrary", "arbitrary"),
        name="mod_vectors",
    )(cvec, mod_w, mod_b.reshape(depth, 1, N))


def _wide_tile(T):
    return max(m for m in range(8, MAX_ROW_TILE + 1, 8) if T % m == 0)


def _per_row(mod_ref, i, tm, L, k):
    row = lax.broadcasted_iota(jnp.int32, (tm, 1), 0) + i * tm
    return jnp.where(row < L, mod_ref[0, 0, k:k + 1, :], mod_ref[0, 1, k:k + 1, :])


def _norm_mod(x, nw, shift, scale):
    y = x * lax.rsqrt(jnp.mean(x * x, axis=-1, keepdims=True) + NORM_EPS)
    y = y * nw
    return y * (1.0 + scale) + shift


def _norm_linear_kernel(x_ref, nw_ref, ss_ref, w_ref, o_ref, *, L):
    i = pl.program_id(1)
    tm = x_ref.shape[1]
    h = _norm_mod(x_ref[0], nw_ref[...], _per_row(ss_ref, i, tm, L, 0), _per_row(ss_ref, i, tm, L, 1))
    o_ref[0] = _dot(h.astype(BF16), w_ref[...]).astype(o_ref.dtype)


def _norm_linear(xa, nw, ss, w, L, out_dtype=F32):
    B, T, D = xa.shape
    N = w.shape[1]
    tm = _wide_tile(T)
    return pl.pallas_call(
        functools.partial(_norm_linear_kernel, L=L),
        grid=(B, T // tm),
        in_specs=[pl.BlockSpec((1, tm, D), lambda b, i: (b, i, 0)),
                  pl.BlockSpec((1, D), lambda b, i: (0, 0)),
                  pl.BlockSpec((1, 2, 2, D), lambda b, i: (b, 0, 0, 0)),
                  pl.BlockSpec((D, N), lambda b, i: (0, 0))],
        out_specs=pl.BlockSpec((1, tm, N), lambda b, i: (b, i, 0)),
        out_shape=jax.ShapeDtypeStruct((B, T, N), out_dtype),
        compiler_params=_params("arbitrary", "arbitrary"),
        name="norm_linear",
    )(xa, nw.reshape(1, D), ss, w)


def _linear_resid_kernel(*refs, ks, L):
    n = len(ks)
    a_refs, (w_ref, x_ref, g_ref, o_ref) = refs[:n], refs[n:]
    acc = None
    off = 0
    for a_ref, k in zip(a_refs, ks):
        part = _dot(a_ref[0].astype(BF16), w_ref[off:off + k, :])
        acc = part if acc is None else acc + part
        off += k
    o_ref[0] = x_ref[0] + _per_row(g_ref, pl.program_id(1), x_ref.shape[1], L, 0) * acc


def _linear_resid(a_list, w, xa, gate, L):
    B, T, D = xa.shape
    tm = _wide_tile(T)
    ks = tuple(a.shape[-1] for a in a_list)
    in_specs = [pl.BlockSpec((1, tm, k), lambda b, i: (b, i, 0)) for k in ks]
    in_specs += [pl.BlockSpec(w.shape, lambda b, i: (0, 0)),
                 pl.BlockSpec((1, tm, D), lambda b, i: (b, i, 0)),
                 pl.BlockSpec((1, 2, 1, D), lambda b, i: (b, 0, 0, 0))]
    return pl.pallas_call(
        functools.partial(_linear_resid_kernel, ks=ks, L=L),
        grid=(B, T // tm),
        in_specs=in_specs,
        out_specs=pl.BlockSpec((1, tm, D), lambda b, i: (b, i, 0)),
        out_shape=jax.ShapeDtypeStruct((B, T, D), F32),
        compiler_params=_params("arbitrary", "arbitrary"),
        name="linear_resid",
    )(*a_list, w, xa, gate)


def _rms_kernel(x_ref, w_ref, o_ref):
    x = x_ref[...]
    o_ref[...] = x * lax.rsqrt(jnp.mean(x * x, axis=-1, keepdims=True) + NORM_EPS) * w_ref[...]


def _final_norm(xa, w, L):
    B, T, D = xa.shape
    S = T - L
    tm = _row_tile(L, S)
    nct = L // tm
    nb = max(n for n in (4, 2, 1) if B % n == 0)
    return pl.pallas_call(
        _rms_kernel,
        grid=(B // nb, S // tm),
        in_specs=[pl.BlockSpec((nb, tm, D), lambda b, i: (b, i + nct, 0)),
                  pl.BlockSpec((1, D), lambda b, i: (0, 0))],
        out_specs=pl.BlockSpec((nb, tm, D), lambda b, i: (b, i, 0)),
        out_shape=jax.ShapeDtypeStruct((B, S, D), F32),
        compiler_params=_params("arbitrary", "arbitrary"),
        name="final_norm",
    )(xa, w.reshape(1, D))


def _rms(x, w):
    return x * lax.rsqrt(jnp.mean(x * x, axis=-1, keepdims=True) + NORM_EPS) * w


def _softmax_pv(chains):
    m = [functools.reduce(jnp.maximum, [jnp.max(s, axis=-1, keepdims=True) for s, _ in ch]) for ch in chains]
    p = [[jnp.exp(s - mi) for s, _ in ch] for ch, mi in zip(chains, m)]
    l = [functools.reduce(jnp.add, [jnp.sum(x, axis=-1, keepdims=True) for x in pc]) for pc in p]
    o = [functools.reduce(jnp.add, [_dot(x.astype(BF16), v) for x, (_, v) in zip(pc, ch)]) for pc, ch in zip(p, chains)]
    return [oi / li for oi, li in zip(o, l)]


def _mla_kernel(zq_ref, zkv_ref, zr_ref, qn_ref, kvn_ref, wq_ref, wk_ref, wv_ref, cq_ref, ck_ref, sk_ref,
                o_ref, q_s, k_s, v_s, *, L, tq, scale):
    T = zq_ref.shape[1]
    zqn = _rms(zq_ref[0], qn_ref[...]).astype(BF16)
    zkvn = _rms(zkv_ref[0], kvn_ref[...]).astype(BF16)
    qh = _dot(zqn, wq_ref[0])
    kn = _dot(zkvn, wk_ref[0])
    v_s[...] = _dot(zkvn, wv_ref[0]).astype(BF16)
    zr = zr_ref[0]
    kr = zr[:, :LANES] * ck_ref[...] + zr[:, LANES:] * sk_ref[...]
    cq = cq_ref[...] * scale
    for h in range(2):
        q_s[h] = (qh[:, h * LANES:(h + 1) * LANES] * cq).astype(BF16)
        k_s[h] = (kn[:, h * LANES:(h + 1) * LANES] + kr).astype(BF16)
    def tile(row0, nk, rows):
        first_head = lax.broadcasted_iota(jnp.int32, (rows, LANES), 1) < V_HEAD
        s = [_dot_nt(q_s[h, pl.ds(row0, rows), :], k_s[h, 0:nk, :]) for h in range(2)]
        outs = _softmax_pv([[(si, v_s[0:nk, :])] for si in s])
        o_ref[0, pl.ds(row0, rows), :] = jnp.where(first_head, outs[0], outs[1]).astype(o_ref.dtype)

    for i in range(L // tq):
        tile(i * tq, L, tq)
    tl = 2 * tq if (T - L) % (2 * tq) == 0 else tq
    for i in range((T - L) // tl):
        tile(L + i * tl, T, tl)


def _mla_attention(z, qn_w, kvn_w, wq, wk, wv, cq, ck, sk, L, col_q, col_kv, col_r):
    B, T, _ = z.shape
    HP = wq.shape[0]
    tq = _row_tile(L, T - L)
    scale = float((QK_NOPE + QK_ROPE) ** -0.5)
    return pl.pallas_call(
        functools.partial(_mla_kernel, L=L, tq=tq, scale=scale),
        grid=(B, HP),
        in_specs=[pl.BlockSpec((1, T, Q_RANK), lambda b, p: (b, 0, col_q // Q_RANK)),
                  pl.BlockSpec((1, T, KV_RANK), lambda b, p: (b, 0, col_kv // KV_RANK)),
                  pl.BlockSpec((1, T, 2 * LANES), lambda b, p: (b, 0, col_r // (2 * LANES))),
                  pl.BlockSpec((1, Q_RANK), lambda b, p: (0, 0)),
                  pl.BlockSpec((1, KV_RANK), lambda b, p: (0, 0)),
                  pl.BlockSpec((1, Q_RANK, 2 * LANES), lambda b, p: (p, 0, 0)),
                  pl.BlockSpec((1, KV_RANK, 2 * LANES), lambda b, p: (p, 0, 0)),
                  pl.BlockSpec((1, KV_RANK, LANES), lambda b, p: (p, 0, 0)),
                  pl.BlockSpec((T, LANES), lambda b, p: (0, 0)),
                  pl.BlockSpec((T, LANES), lambda b, p: (0, 0)),
                  pl.BlockSpec((T, LANES), lambda b, p: (0, 0))],
        out_specs=pl.BlockSpec((1, T, LANES), lambda b, p: (b, 0, p)),
        out_shape=jax.ShapeDtypeStruct((B, T, HP * LANES), BF16),
        scratch_shapes=[pltpu.VMEM((2, T, LANES), BF16), pltpu.VMEM((2, T, LANES), BF16),
                        pltpu.VMEM((T, LANES), BF16)],
        compiler_params=_params("arbitrary", "arbitrary"),
        name="mla_attention",
    )(z, z, z, qn_w.reshape(1, -1), kvn_w.reshape(1, -1), wq, wk, wv, cq, ck, sk)


def _na_kernel(q_ref, k_ref, v_ref, bt_ref, o_ref, k_s, v_s, *, L, rows, kr, need_ctx, scale):
    W = GRID_W
    rpb = 4 if rows % 4 == 0 else 1
    k_s[...] = k_ref[0].astype(BF16)
    v_s[...] = v_ref[0].astype(BF16)
    nwin = kr * W
    lane = lax.broadcasted_iota(jnp.int32, (W, LANES), 1)
    head_mask = [(lane < C_HEAD_DIM).astype(F32), (lane >= C_HEAD_DIM).astype(F32)]
    first_head = lane < C_HEAD_DIM
    qcol = lax.broadcasted_iota(jnp.int32, (W, nwin), 0)
    kcol = lax.broadcasted_iota(jnp.int32, (W, nwin), 1) % W
    cstart = jnp.clip(qcol - WIN_C // 2, 0, W - WIN_C)
    col_valid = (kcol >= cstart) & (kcol < cstart + WIN_C)

    def row_block(rb, carry):
        q_blk = q_ref[0, pl.ds(pl.multiple_of(L + rb * (rpb * W), W), rpb * W), :] * scale
        s_ctx = [_dot_nt((q_blk * jnp.concatenate([head_mask[h]] * rpb, axis=0)).astype(BF16), k_s[0:L, :])
                 for h in range(2)]
        chains, q0s = [], []
        for j in range(rpb):
            r = rb * rpb + j
            rs = jnp.clip(r - kr // 2, 0, rows - kr)
            k0 = pl.multiple_of(L + rs * W, W)
            q0s.append(pl.multiple_of(L + r * W, W))
            q = q_blk[j * W:(j + 1) * W]
            kw = k_s[pl.ds(k0, nwin), :]
            vw = v_s[pl.ds(k0, nwin), :]
            dr0 = rs - r + (WIN_R - 1)
            for h in range(2):
                s_nb = _dot_nt((q * head_mask[h]).astype(BF16), kw)
                bias = jnp.concatenate([bt_ref[0, h, dr0 + 2 * m] for m in range(kr // 2)], axis=-1)
                s_nb = jnp.where(col_valid, s_nb + bias, NEG_INF)
                chains.append([(s_nb, vw), (s_ctx[h][j * W:(j + 1) * W], v_s[0:L, :])])
        outs = _softmax_pv(chains)
        for j in range(rpb):
            o_ref[0, pl.ds(q0s[j], W), :] = jnp.where(first_head, outs[2 * j], outs[2 * j + 1]).astype(o_ref.dtype)
        return carry

    lax.fori_loop(0, rows // rpb, row_block, 0, unroll=4)

    tq = min(L, 256)
    lane_c = lax.broadcasted_iota(jnp.int32, (tq, LANES), 1)
    for i in range(L // tq):
        if need_ctx:
            q = q_ref[0, i * tq:(i + 1) * tq, :] * scale
            hm = [lane_c < C_HEAD_DIM, lane_c >= C_HEAD_DIM]
            s = [_dot_nt(jnp.where(hm[h], q, 0.0).astype(BF16), k_s[0:L, :]) for h in range(2)]
            outs = _softmax_pv([[(si, v_s[0:L, :])] for si in s])
            o_ref[0, i * tq:(i + 1) * tq, :] = jnp.where(lane_c < C_HEAD_DIM, outs[0], outs[1]).astype(o_ref.dtype)
        else:
            o_ref[0, i * tq:(i + 1) * tq, :] = jnp.zeros((tq, LANES), o_ref.dtype)


def _na_attention(qkv, bias_tab, L, need_ctx):
    B, T, D3 = qkv.shape
    D = D3 // 3
    HP = D // LANES
    rows = (T - L) // GRID_W
    kr = min(WIN_R, rows)
    assert kr % 2 == 0
    nd = bias_tab.shape[2]
    return pl.pallas_call(
        functools.partial(_na_kernel, L=L, rows=rows, kr=kr, need_ctx=need_ctx, scale=float(C_HEAD_DIM ** -0.5)),
        grid=(B, HP),
        in_specs=[pl.BlockSpec((1, T, LANES), lambda b, p: (b, 0, p)),
                  pl.BlockSpec((1, T, LANES), lambda b, p: (b, 0, HP + p)),
                  pl.BlockSpec((1, T, LANES), lambda b, p: (b, 0, 2 * HP + p)),
                  pl.BlockSpec((1, 2, nd, GRID_W, LANES), lambda b, p: (p, 0, 0, 0, 0))],
        out_specs=pl.BlockSpec((1, T, LANES), lambda b, p: (b, 0, p)),
        out_shape=jax.ShapeDtypeStruct((B, T, D), BF16),
        scratch_shapes=[pltpu.VMEM((T, LANES), BF16), pltpu.VMEM((T, LANES), BF16)],
        compiler_params=_params("arbitrary", "arbitrary"),
        name="na_attention",
    )(qkv, qkv, qkv, bias_tab)


def _na_bias_table(rpb):
    H = rpb.shape[0]
    qc = np.arange(GRID_W)[:, None]
    kc = np.arange(GRID_W)[None, :]
    dc = np.clip(kc - qc + (WIN_C - 1), 0, 2 * WIN_C - 2)
    pick = jnp.asarray(np.arange(2 * WIN_C - 1)[:, None, None] == dc[None], dtype=F32)
    t = jnp.einsum('hdc,cqk->hdqk', rpb, pick, precision=HIGHEST)
    t2 = jnp.concatenate([t[:, :-1], t[:, 1:]], axis=-1)
    return t2.reshape(H // 2, 2, 2 * WIN_R - 2, GRID_W, 2 * GRID_W)


def _bf(x):
    return x.astype(BF16)


def _seg_sum(x, ones_bd):
    hi = _bf(x)
    lo = _bf(x - hi.astype(F32))
    return _dot(hi, ones_bd) + _dot(lo, ones_bd)


def _head_ones(A):
    seg = np.arange(A) // A_HEAD_DIM
    return jnp.asarray(seg[:, None] == seg[None, :], dtype=BF16)


def _rwkv_prep_kernel(z_ref, zp_ref, zn_ref, mu_ref, w0_ref, w2_ref, a0_ref, a2_ref, g2_ref, kk_ref, ka_ref, rk_ref,
                      ones_ref, r_o, v_o, kkn_o, g_o, bonus_o, lw_o, beta_o, kd_o, *, nct, nt, A):
    for s in range(z_ref.shape[0]):
        _rwkv_prep_sample(s, z_ref, zp_ref, zn_ref, mu_ref, w0_ref, w2_ref, a0_ref, a2_ref, g2_ref, kk_ref, ka_ref,
                          rk_ref, ones_ref, r_o, v_o, kkn_o, g_o, bonus_o, lw_o, beta_o, kd_o, nct=nct, nt=nt, A=A)


def _rwkv_prep_sample(s, z_ref, zp_ref, zn_ref, mu_ref, w0_ref, w2_ref, a0_ref, a2_ref, g2_ref, kk_ref, ka_ref, rk_ref,
                      ones_ref, r_o, v_o, kkn_o, g_o, bonus_o, lw_o, beta_o, kd_o, *, nct, nt, A):
    i = pl.program_id(1)
    za = z_ref[s]
    tm = za.shape[0]
    row = lax.broadcasted_iota(jnp.int32, za.shape, 0)
    seg_first = (i == 0) | (i == nct)
    seg_last = (i == nct - 1) | (i == nt - 1)
    prev_row = jnp.where(seg_first, 0.0, zp_ref[s, 7:8, :])
    next_row = jnp.where(seg_last, 0.0, zn_ref[s, 0:1, :])
    prev = jnp.where(row == 0, prev_row, pltpu.roll(za, 1, 0))
    nxt = jnp.where(row == tm - 1, next_row, pltpu.roll(za, tm - 1, 0))
    zs = za + mu_ref[0:1, :] * (prev - za) + mu_ref[1:2, :] * (nxt - za)
    r = zs[:, 0:A]
    k = zs[:, A:2 * A]
    v = zs[:, 2 * A:3 * A]
    wd = _bf(jnp.tanh(zs[:, 3 * A:3 * A + LANES]))
    ad = _bf(zs[:, 3 * A + LANES:3 * A + 2 * LANES])
    gd = _bf(jax.nn.sigmoid(zs[:, 3 * A + 2 * LANES:3 * A + 3 * LANES]))
    ones = ones_ref[...]
    kk = k * kk_ref[...]
    kkn = kk / jnp.maximum(jnp.sqrt(_seg_sum(kk * kk, ones)), 1e-12)
    kd_sum = None
    for d in range(2):
        u = w0_ref[d:d + 1, :] + _dot(wd, w2_ref[d])
        lw_o[d, s] = -DECAY_FLOOR_SCALE * jax.nn.sigmoid(u)
        a = jax.nn.sigmoid(a0_ref[d:d + 1, :] + _dot(ad, a2_ref[d]))
        beta_o[d, s] = kkn * a
        kd = k * (1.0 + (a - 1.0) * ka_ref[...])
        kd_o[d, s] = kd
        kd_sum = kd if kd_sum is None else kd_sum + kd
    bonus_o[s] = _seg_sum(r * kd_sum * rk_ref[...], ones) * v
    r_o[s] = r
    v_o[s] = v
    kkn_o[s] = kkn
    g_o[s] = _dot(gd, g2_ref[...])


def _rwkv_prep(z, L, a_cols, mu, w0, w2, a0, a2, g2, k_k, k_a, r_k):
    B, T, _ = z.shape
    A = w0.shape[-1]
    assert 2 * LORA_W == LANES and 2 * LORA_A == LANES and LORA_G == LANES and a_cols == 3 * A + 3 * LANES
    tm = _row_tile(L, T - L)
    nt = T // tm
    hb = tm // 8

    def pad_lora(w):
        zero = jnp.zeros_like(w[0])
        return _bf(jnp.stack([jnp.concatenate([w[0], zero], 0), jnp.concatenate([zero, w[1]], 0)]))

    def const(shape):
        return pl.BlockSpec(shape, lambda b, i: (0,) * len(shape))

    nb = 2 if B % 2 == 0 else 1
    tile = pl.BlockSpec((nb, tm, A), lambda b, i: (b, i, 0))
    tile_d = pl.BlockSpec((2, nb, tm, A), lambda b, i: (0, b, i, 0))
    sd = jax.ShapeDtypeStruct((B, T, A), F32)
    sd_d = jax.ShapeDtypeStruct((2, B, T, A), F32)
    return pl.pallas_call(
        functools.partial(_rwkv_prep_kernel, nct=L // tm, nt=nt, A=A),
        grid=(B // nb, nt),
        in_specs=[pl.BlockSpec((nb, tm, a_cols), lambda b, i: (b, i, 0)),
                  pl.BlockSpec((nb, 8, a_cols), lambda b, i: (b, jnp.maximum(i * hb - 1, 0), 0)),
                  pl.BlockSpec((nb, 8, a_cols), lambda b, i: (b, jnp.minimum((i + 1) * hb, T // 8 - 1), 0)),
                  const((2, a_cols)), const((2, A)), const((2, 2 * LORA_W, A)), const((2, A)),
                  const((2, 2 * LORA_A, A)), const((LORA_G, A)), const((1, A)), const((1, A)), const((1, A)),
                  const((A, A))],
        out_specs=[tile, tile, tile, tile, tile, tile_d, tile_d, tile_d],
        out_shape=[sd, sd, sd, sd, sd, sd_d, sd_d, sd_d],
        compiler_params=_params("arbitrary", "arbitrary"),
        name="rwkv_prep",
    )(z, z, z, mu, w0, pad_lora(w2), a0, pad_lora(a2), _bf(g2), k_k.reshape(1, A), k_a.reshape(1, A),
      r_k.reshape(1, A), _head_ones(A))


def _tri_inverse(lms, eye, m16, m32):
    d0 = [_bf(jnp.where(m16, lm, 0.0)) for lm in lms]
    t = [eye + d.astype(F32) for d in d0]
    s = [_dot(d, d) for d in d0]
    for step in range(3):
        sb = [_bf(x) for x in s]
        t = [x + _dot(_bf(x), y) for x, y in zip(t, sb)]
        if step < 2:
            s = [_dot(y, y) for y in sb]
    for lvl in (m32 & (~m16), ~m32):
        tb = [_bf(x) for x in t]
        w = [_bf(_dot(_bf(jnp.where(lvl, lm, 0.0)), y)) for lm, y in zip(lms, tb)]
        t = [x + _dot(y, z) for x, y, z in zip(t, tb, w)]
    return t


def _wkv_kernel(*refs, NP, NB):
    C = SCAN_CHUNK
    P = 2 * C
    fwd_refs, bwd_refs, (yf_ref, yb_ref, h_s) = refs[0:6], refs[6:12], refs[12:]

    @pl.when(pl.program_id(1) == 0)
    def _():
        h_s[...] = jnp.zeros(h_s.shape, F32)

    ri = lax.broadcasted_iota(jnp.int32, (P, P), 0)
    ci = lax.broadcasted_iota(jnp.int32, (P, P), 1)
    same = (ri // C) == (ci // C)
    diff = (ri % C) - (ci % C)
    eye_b = ri == ci
    eye = eye_b.astype(F32)
    m16 = (ri // 16) == (ci // 16)
    m32 = (ri // 32) == (ci // 32)
    diff64 = lax.broadcasted_iota(jnp.int32, (C, C), 0) - lax.broadcasted_iota(jnp.int32, (C, C), 1)
    top = lax.broadcasted_iota(jnp.int32, (C, P), 1) < C
    zero_blk = jnp.zeros((P, P), BF16)
    sls = [slice(p * P, (p + 1) * P) for p in range(NP)]

    def bd(x):
        return [_bf(jnp.concatenate([jnp.where(top, x[:, sl], 0.0), jnp.where(top, 0.0, x[:, sl])], axis=0))
                for sl in sls]

    at, rt, bt, kt, bh, kh, vv, etots, before, before_eq = [], [], [], [], [], [], [], [], [], []
    for (r_ref, v_ref, kk_ref, lw_ref, beta_ref, kd_ref), sgn in ((fwd_refs, 1), (bwd_refs, -1)):
        order = diff * sgn
        tri = _bf(((diff64 * sgn) >= 0).astype(F32))
        for s in range(NB):
            before += [same & (order > 0)] * NP
            before_eq += [same & (order >= 0)] * NP
            lw = lw_ref[0, s]
            lw_hi = _bf(lw)
            lw_md = _bf(lw - lw_hi.astype(F32))
            lw_lo = _bf(lw - lw_hi.astype(F32) - lw_md.astype(F32))
            cum = _dot(tri, lw_hi) + _dot(tri, lw_md) + _dot(tri, lw_lo)
            tot = jnp.sum(lw, axis=0, keepdims=True)
            beta = beta_ref[0, s]
            kd = kd_ref[0, s]
            e_neg = jnp.exp(-cum)
            e_tail = jnp.exp(tot - cum)
            at += bd(-kk_ref[s] * jnp.exp(cum - lw))
            rt += bd(r_ref[s] * jnp.exp(cum))
            bt += bd(beta * e_neg)
            kt += bd(kd * e_neg)
            bh += bd(beta * e_tail)
            kh += bd(kd * e_tail)
            vv += bd(v_ref[s])
            etot = jnp.exp(tot)
            etots += [etot[:, sl] for sl in sls]

    ar = [jnp.concatenate([a, r], axis=0) for a, r in zip(at, rt)]
    arb = [_dot_nt(x, b) for x, b in zip(ar, bt)]
    ark = [_dot_nt(x, k) for x, k in zip(ar, kt)]
    lab = [jnp.where(m, x[:P], 0.0) for x, m in zip(arb, before)]
    tinv = _tri_inverse(lab, eye, m16, m32)
    u = [_dot(_bf(jnp.where(m, x[:P], 0.0)), v) for x, v, m in zip(ark, vv, before)]
    x = [_bf(_dot(_bf(t), jnp.concatenate([a, _bf(w)], axis=1))) for t, a, w in zip(tinv, at, u)]
    rhs = [jnp.concatenate([xi, jnp.concatenate([zero_blk, v], axis=1)], axis=0) for xi, v in zip(x, vv)]
    mn = [lax.dot_general(jnp.concatenate([b, k], axis=0), w, (((0,), (0,)), ((), ())), preferred_element_type=F32)
          for b, k, w in zip(bh, kh, rhs)]
    lr = [_bf(jnp.concatenate([jnp.where(m, xb[P:], 0.0), jnp.where(m, xk[P:], 0.0)], axis=1))
          for xb, xk, m in zip(arb, ark, before_eq)]
    qy = [_dot(l, w) for l, w in zip(lr, rhs)]
    qm = [_bf(jnp.concatenate([r.astype(F32) + q[:, :P], jnp.where(eye_b, e, 0.0) + m[:, :P]], axis=0))
          for r, q, m, e in zip(rt, qy, mn, etots)]
    nchain = 2 * NB * NP
    hin = [h_s[i] for i in range(nchain)]
    h_hi = [_bf(h) for h in hin]
    h_lo = [_bf(h - hh.astype(F32)) for h, hh in zip(hin, h_hi)]
    res = [_dot(w, hh) + _dot(w, hl) for w, hh, hl in zip(qm, h_hi, h_lo)]
    for i in range(nchain):
        ybd = res[i][:P] + qy[i][:, P:]
        y_ref = yf_ref if i < NB * NP else yb_ref
        y_ref[(i // NP) % NB, :, sls[i % NP]] = ybd[:C] + ybd[C:]
        h_s[i] = res[i][P:] + mn[i][:, P:]


def _wkv_scan(r, v, kk, lw, beta, kd, L):
    B, T, A = r.shape
    C = SCAN_CHUNK
    nC = T // C
    nct = L // C
    NP = A // (2 * C)

    def rev(c):
        return jnp.where(c < nct, nct - 1 - c, nC - 1 - (c - nct))

    NB = 2 if B % 2 == 0 else 1
    fwd = pl.BlockSpec((NB, C, A), lambda b, c: (b, c, 0))
    bwd = pl.BlockSpec((NB, C, A), lambda b, c: (b, rev(c), 0))
    fwd_d = pl.BlockSpec((1, NB, C, A), lambda b, c: (0, b, c, 0))
    bwd_d = pl.BlockSpec((1, NB, C, A), lambda b, c: (1, b, rev(c), 0))
    sd = jax.ShapeDtypeStruct((B, T, A), F32)
    return pl.pallas_call(
        functools.partial(_wkv_kernel, NP=NP, NB=NB),
        grid=(B // NB, nC),
        in_specs=[fwd, fwd, fwd, fwd_d, fwd_d, fwd_d, bwd, bwd, bwd, bwd_d, bwd_d, bwd_d],
        out_specs=[fwd, bwd],
        out_shape=[sd, sd],
        scratch_shapes=[pltpu.VMEM((2 * NB * NP, 2 * C, 2 * C), F32)],
        compiler_params=_params("arbitrary", "arbitrary"),
        name="wkv_scan",
    )(r, v, kk, lw, beta, kd, r, v, kk, lw, beta, kd)


def _rwkv_post_kernel(yf_ref, yb_ref, bonus_ref, g_ref, lnw_ref, lnb_ref, ones_ref, o_ref):
    y = yf_ref[0] + yb_ref[0]
    ones = ones_ref[...]
    inv_n = 1.0 / A_HEAD_DIM
    d = y - _seg_sum(y, ones) * inv_n
    var = _seg_sum(d * d, ones) * inv_n
    yn = d * lax.rsqrt(var + GN_EPS) * lnw_ref[...] + lnb_ref[...]
    o_ref[0] = ((yn + bonus_ref[0]) * g_ref[0]).astype(o_ref.dtype)


def _rwkv_post(yf, yb, bonus, g, ln_w, ln_b, L):
    B, T, A = bonus.shape
    tm = _wide_tile(T)
    tile = pl.BlockSpec((1, tm, A), lambda b, i: (b, i, 0))
    vec = pl.BlockSpec((1, A), lambda b, i: (0, 0))
    return pl.pallas_call(
        _rwkv_post_kernel,
        grid=(B, T // tm),
        in_specs=[tile, tile, tile, tile, vec, vec, pl.BlockSpec((A, A), lambda b, i: (0, 0))],
        out_specs=tile,
        out_shape=jax.ShapeDtypeStruct((B, T, A), BF16),
        compiler_params=_params("arbitrary", "arbitrary"),
        name="rwkv_post",
    )(yf, yb, bonus, g, ln_w.reshape(1, A), ln_b.reshape(1, A), _head_ones(A))


def _rwkv_mixer(z, L, a_cols, mu, w0, w2, a0, a2, g2, k_k, k_a, r_k, ln_w, ln_b):
    r, v, kk, g, bonus, lw, beta, kd = _rwkv_prep(z, L, a_cols, mu, w0, w2, a0, a2, g2, k_k, k_a, r_k)
    yf, yb = _wkv_scan(r, v, kk, lw, beta, kd, L)
    return _rwkv_post(yf, yb, bonus, g, ln_w, ln_b, L)


def _router_kernel(x_ref, nw_ref, ss_ref, wr_ref, h_ref, aff_ref, *, L):
    i = pl.program_id(1)
    tm = x_ref.shape[1]
    h = _norm_mod(x_ref[0], nw_ref[...], _per_row(ss_ref, i, tm, L, 0), _per_row(ss_ref, i, tm, L, 1))
    h_hi = h.astype(BF16)
    h_ref[0] = h_hi
    h_lo = _bf(h - h_hi.astype(F32))
    w = wr_ref[...]
    w_hi = _bf(w)
    w_lo = _bf(w - w_hi.astype(F32))
    logits = _dot_nt(w_hi, h_hi) + _dot_nt(w_hi, h_lo) + _dot_nt(w_lo, h_hi)
    m = jnp.max(logits, axis=0, keepdims=True)
    p = jnp.exp(logits - m)
    aff_ref[0] = p / jnp.sum(p, axis=0, keepdims=True)


def _norm_router(xa, nw, ss, w_router, L):
    B, T, D = xa.shape
    E = w_router.shape[1]
    tm = max(m for m in range(LANES, MAX_ROW_TILE + 1, LANES) if T % m == 0)
    return pl.pallas_call(
        functools.partial(_router_kernel, L=L),
        grid=(B, T // tm),
        in_specs=[pl.BlockSpec((1, tm, D), lambda b, i: (b, i, 0)),
                  pl.BlockSpec((1, D), lambda b, i: (0, 0)),
                  pl.BlockSpec((1, 2, 2, D), lambda b, i: (b, 0, 0, 0)),
                  pl.BlockSpec((E, D), lambda b, i: (0, 0))],
        out_specs=[pl.BlockSpec((1, tm, D), lambda b, i: (b, i, 0)),
                   pl.BlockSpec((1, E, tm), lambda b, i: (b, 0, i))],
        out_shape=[jax.ShapeDtypeStruct((B, T, D), BF16), jax.ShapeDtypeStruct((B, E, T), F32)],
        compiler_params=_params("arbitrary", "arbitrary"),
        name="norm_router",
    )(xa, nw.reshape(1, D), ss, w_router.T)


def _select_top(sets):
    keys = [pltpu.bitcast(aff, jnp.int32) for aff, _, _ in sets]
    E = keys[0].shape[0]

    def count_ge(key, thr):
        return jnp.sum(jnp.where(key >= thr, 1.0, 0.0), axis=1, keepdims=True)

    def narrow(_, carry):
        out = []
        for key, (_, cap, _), (lo, hi) in zip(keys, sets, carry):
            q = (hi - lo + 3) >> 2
            m1 = jnp.minimum(lo + q, hi)
            m2 = jnp.minimum(lo + 2 * q, hi)
            m3 = jnp.minimum(lo + 3 * q, hi)
            ok1, ok2, ok3 = count_ge(key, m1) >= cap, count_ge(key, m2) >= cap, count_ge(key, m3) >= cap
            new_lo = jnp.where(ok3, m3, jnp.where(ok2, m2, jnp.where(ok1, m1, lo)))
            new_hi = jnp.where(ok3, hi, jnp.where(ok2, m3 - 1, jnp.where(ok1, m2 - 1, m1 - 1)))
            out.append((new_lo, jnp.maximum(new_hi, new_lo)))
        return tuple(out)

    init = tuple((jnp.zeros((E, 1), jnp.int32), jnp.full((E, 1), 0x7F800000, jnp.int32)) for _ in sets)
    bounds = lax.fori_loop(0, 17, narrow, init)

    def prefix(x, tri):
        pb = tri.shape[0]
        parts, offset = [], jnp.zeros((E, 1), F32)
        for j in range(x.shape[1] // pb):
            xj = x[:, j * pb:(j + 1) * pb]
            inner = _dot(xj.astype(BF16), tri)
            parts.append(inner + offset)
            offset = offset + inner[:, pb - 1:pb] + xj[:, pb - 1:pb]
        return parts[0] if len(parts) == 1 else jnp.concatenate(parts, axis=1)

    res = []
    for key, (_, cap, tri), (thr, _) in zip(keys, sets, bounds):
        above = key > thr
        tie = key == thr
        need = cap - jnp.sum(jnp.where(above, 1.0, 0.0), axis=1, keepdims=True)
        tie_rank = prefix(jnp.where(tie, 1.0, 0.0), tri)
        sel = above | (tie & (tie_rank < need))
        res.append(jnp.where(sel, prefix(jnp.where(sel, 1.0, 0.0), tri), -1.0))
    return res


def _route_gather_kernel(aff_ref, h_ref, x_ref, pos_ref, gate_ref, slot_s, tri_s, *, L, cap_l, cap_c):
    b = pl.program_id(0)
    eg = pl.program_id(1)
    T = h_ref.shape[1]
    S = T - L

    @pl.when((b == 0) & (eg == 0))
    def _():
        n = tri_s.shape[0]
        tri_s[...] = jnp.where(lax.broadcasted_iota(jnp.int32, (n, n), 0) < lax.broadcasted_iota(jnp.int32, (n, n), 1),
                               1.0, 0.0).astype(BF16)

    @pl.when(eg == 0)
    def _():
        sets = [(aff_ref[0, :, L:], cap_l, tri_s[...])]
        if cap_c:
            sets.append((aff_ref[0, :, 0:L], cap_c, tri_s[...]))
        picked = _select_top(sets)
        slot_s[:, L:] = picked[0]
        if cap_c:
            slot_s[:, 0:L] = picked[1]

    def gather(k, lo, n, cap, row0):
        e = eg * x_ref.shape[0] + k
        slot = slot_s[pl.ds(e, 1), lo:lo + n]
        hit = slot == lax.broadcasted_iota(jnp.int32, (cap, n), 0).astype(F32)
        x_ref[k, 0, row0:row0 + cap, :] = _dot(jnp.where(hit, 1.0, 0.0).astype(BF16), h_ref[0, lo:lo + n, :]).astype(BF16)
        tok = lax.broadcasted_iota(jnp.int32, (cap, n), 1) + lo
        pos_ref[0, k, row0:row0 + cap, :] = jnp.sum(jnp.where(hit, tok, 0), axis=1, keepdims=True)
        aff = aff_ref[0, pl.ds(e, 1), lo:lo + n]
        gate_ref[0, k, row0:row0 + cap, :] = jnp.sum(jnp.where(hit, aff, 0.0), axis=1, keepdims=True)

    for k in range(x_ref.shape[0]):
        gather(k, L, S, cap_l, 0)
        if cap_c:
            gather(k, 0, L, cap_c, cap_l)


def _route_gather(aff, h2, L, cap_l, cap_c):
    B, T, D = h2.shape
    E = aff.shape[1]
    Ct = cap_l + cap_c
    ne = max(n for n in (4, 2, 1) if E % n == 0)
    return pl.pallas_call(
        functools.partial(_route_gather_kernel, L=L, cap_l=cap_l, cap_c=cap_c),
        grid=(B, E // ne),
        in_specs=[pl.BlockSpec((1, E, T), lambda b, e: (b, 0, 0)),
                  pl.BlockSpec((1, T, D), lambda b, e: (b, 0, 0))],
        out_specs=[pl.BlockSpec((ne, 1, Ct, D), lambda b, e: (e, b, 0, 0)),
                   pl.BlockSpec((1, ne, Ct, 1), lambda b, e: (b, e, 0, 0)),
                   pl.BlockSpec((1, ne, Ct, 1), lambda b, e: (b, e, 0, 0))],
        out_shape=[jax.ShapeDtypeStruct((E, B, Ct, D), BF16), jax.ShapeDtypeStruct((B, E, Ct, 1), jnp.int32),
                   jax.ShapeDtypeStruct((B, E, Ct, 1), F32)],
        scratch_shapes=[pltpu.VMEM((E, T), F32), pltpu.VMEM((_row_tile(L, T - L),) * 2, BF16)],
        compiler_params=_params("arbitrary", "arbitrary"),
        name="moe_route_gather",
    )(aff, h2)


def _ffn_kernel(x_ref, w1_ref, w3_ref, w2_ref, o_ref, *, rm):
    j = pl.program_id(1)
    R = x_ref.shape[1]
    w1 = w1_ref[0, 0].astype(BF16)
    w3 = w3_ref[0, 0].astype(BF16)
    w2 = w2_ref[0, 0].astype(BF16)

    @pl.when(j == 0)
    def _():
        o_ref[...] = jnp.zeros(o_ref.shape, F32)

    def rows(i, carry):
        r0 = pl.multiple_of(i * rm, rm)
        x = x_ref[0, pl.ds(r0, rm), :]
        a = _dot(x, w1)
        b = _dot(x, w3)
        hid = (a * jax.nn.sigmoid(a) * b).astype(BF16)
        o_ref[0, pl.ds(r0, rm), :] += _dot(hid, w2)
        return carry

    lax.fori_loop(0, R // rm, rows, 0, unroll=True)


def _expert_ffn(xin, w1, w3, w2, layer):
    E, R, D = xin.shape
    F = w1.shape[-1]
    tf = min(512, F)
    rm = max(m for m in (MAX_ROW_TILE, 512, 256, 128, 64, 32, 16) if R % m == 0)
    return pl.pallas_call(
        functools.partial(_ffn_kernel, rm=rm),
        grid=(E, F // tf),
        in_specs=[pl.BlockSpec((1, R, D), lambda e, j: (e, 0, 0)),
                  pl.BlockSpec((1, 1, D, tf), lambda e, j: (layer, e, 0, j)),
                  pl.BlockSpec((1, 1, D, tf), lambda e, j: (layer, e, 0, j)),
                  pl.BlockSpec((1, 1, tf, D), lambda e, j: (layer, e, j, 0))],
        out_specs=pl.BlockSpec((1, R, D), lambda e, j: (e, 0, 0)),
        out_shape=jax.ShapeDtypeStruct((E, R, D), F32),
        compiler_params=_params("arbitrary", "arbitrary"),
        name="expert_ffn",
    )(xin, w1, w3, w2)


def _combine_kernel(y_ref, gate_ref, posl_ref, posc_ref, x_ref, g_ref, o_ref, yg_s, *, tq, nct, cap_l):
    E, _, Ct, td = y_ref.shape
    T = x_ref.shape[1]
    cap_c = Ct - cap_l
    nl = E * cap_l
    for e in range(E):
        yg_s[e * cap_l:(e + 1) * cap_l, :] = (y_ref[e, 0, :cap_l, :] * gate_ref[0, e, :cap_l, :]).astype(BF16)
        if cap_c:
            yg_s[nl + e * cap_c:nl + (e + 1) * cap_c, :] = (y_ref[e, 0, cap_l:, :] * gate_ref[0, e, cap_l:, :]).astype(BF16)

    def tile(i, gate_row, pos, lo, n):
        r0 = pl.multiple_of(i * tq, tq)
        tok = lax.broadcasted_iota(jnp.int32, (tq, n), 0) + r0
        onehot = jnp.where(tok == pos, 1.0, 0.0).astype(BF16)
        o_ref[0, pl.ds(r0, tq), :] = x_ref[0, pl.ds(r0, tq), :] + gate_row * _dot(onehot, yg_s[lo:lo + n, :])

    for i in range(nct):
        if cap_c:
            tile(i, g_ref[0, 0], posc_ref[0], nl, E * cap_c)
        else:
            o_ref[0, i * tq:(i + 1) * tq, :] = x_ref[0, i * tq:(i + 1) * tq, :]

    def body(i, carry):
        tile(i, g_ref[0, 1], posl_ref[0], 0, nl)
        return carry

    lax.fori_loop(nct, T // tq, body, 0, unroll=2)


def _moe_combine(y, gate, pos_l, pos_c, xa, g2, L, cap_l):
    E, B, Ct, D = y.shape
    T = xa.shape[1]
    td = min(512, D)
    tq = _row_tile(L, T - L)
    return pl.pallas_call(
        functools.partial(_combine_kernel, tq=tq, nct=L // tq, cap_l=cap_l),
        grid=(B, D // td),
        in_specs=[pl.BlockSpec((E, 1, Ct, td), lambda b, j: (0, b, 0, j)),
                  pl.BlockSpec((1, E, Ct, 1), lambda b, j: (b, 0, 0, 0)),
                  pl.BlockSpec((1, 1, pos_l.shape[-1]), lambda b, j: (b, 0, 0)),
                  pl.BlockSpec((1, 1, pos_c.shape[-1]), lambda b, j: (b, 0, 0)),
                  pl.BlockSpec((1, T, td), lambda b, j: (b, 0, j)),
                  pl.BlockSpec((1, 2, 1, td), lambda b, j: (b, 0, 0, j))],
        out_specs=pl.BlockSpec((1, T, td), lambda b, j: (b, 0, j)),
        out_shape=jax.ShapeDtypeStruct((B, T, D), F32),
        scratch_shapes=[pltpu.VMEM((E * Ct, td), BF16)],
        compiler_params=_params("arbitrary", "arbitrary"),
        name="moe_combine",
    )(y, gate, pos_l, pos_c, xa, g2)


def _moe(xa, nw, ss2, g2, w_router, w1, w3, w2, layer, L, need_ctx):
    B, T, D = xa.shape
    S = T - L
    E = w_router.shape[1]
    h2, aff = _norm_router(xa, nw, ss2, w_router, L)
    cap_l = CAPACITY_FACTOR * S // E
    cap_c = CAPACITY_FACTOR * L // E if need_ctx else 0
    Ct = cap_l + cap_c
    xin, pos, gate = _route_gather(aff, h2, L, cap_l, cap_c)
    pos_l = pos[:, :, :cap_l, 0].reshape(B, 1, E * cap_l)
    if cap_c:
        pos_c = pos[:, :, cap_l:, 0].reshape(B, 1, E * cap_c)
    else:
        pos_c = jnp.zeros((B, 1, LANES), jnp.int32)
    y = _expert_ffn(xin.reshape(E, B * Ct, D), w1, w3, w2, layer).reshape(E, B, Ct, D)
    return _moe_combine(y, gate, pos_l, pos_c, xa, g2, L, cap_l)


_PERM_EO = np.concatenate([np.arange(0, QK_ROPE, 2), np.arange(1, QK_ROPE, 2)])
_PERM_OE = np.concatenate([np.arange(1, QK_ROPE, 2), np.arange(0, QK_ROPE, 2)])


def _ab_input_weight(w_in, a_cols):
    D = w_in.shape[0]
    zr = w_in[:, a_cols + Q_RANK + KV_RANK:]
    zero = jnp.zeros((D, LANES - 2 * QK_ROPE), w_in.dtype)
    g1 = jnp.concatenate([zero, zr[:, _PERM_EO], zr[:, _PERM_EO]], axis=1)
    g2 = jnp.concatenate([zero, zr[:, _PERM_OE], zr[:, _PERM_OE]], axis=1)
    return jnp.concatenate([w_in[:, :a_cols + Q_RANK + KV_RANK], g1, g2], axis=1).astype(BF16)


def _mla_weights(w_qup, w_kvup):
    NH = w_qup.shape[1] // (QK_NOPE + QK_ROPE)
    wq = w_qup.reshape(Q_RANK, NH, QK_NOPE + QK_ROPE)
    rope = wq[:, :, QK_NOPE:]
    wq = jnp.concatenate([wq[:, :, :QK_NOPE], rope[:, :, _PERM_EO], rope[:, :, _PERM_OE]], axis=-1)
    wq = wq.reshape(Q_RANK, NH // 2, 2 * LANES).transpose(1, 0, 2)
    wkv = w_kvup.reshape(KV_RANK, NH, QK_NOPE + V_HEAD)
    wk = jnp.concatenate([wkv[:, :, :QK_NOPE], jnp.zeros((KV_RANK, NH, LANES - QK_NOPE), w_kvup.dtype)], axis=-1)
    wk = wk.reshape(KV_RANK, NH // 2, 2 * LANES).transpose(1, 0, 2)
    wv = wkv[:, :, QK_NOPE:].reshape(KV_RANK, NH // 2, 2 * V_HEAD).transpose(1, 0, 2)
    return wq.astype(BF16), wk.astype(BF16), wv.astype(BF16)


def _rope_tables(L, S):
    t = np.arange(S)
    row = (t // GRID_W).astype(np.float32)
    col = (t % GRID_W).astype(np.float32)
    n_freq = QK_ROPE // 4
    inv = (ROPE_BASE ** (-np.arange(n_freq, dtype=np.float32) / n_freq)).astype(np.float32)
    ang = jnp.concatenate([jnp.asarray(row[:, None] * inv), jnp.asarray(col[:, None] * inv)], axis=-1)
    cos = jnp.concatenate([jnp.ones((L, QK_ROPE // 2), F32), jnp.cos(ang)], axis=0)
    sin = jnp.concatenate([jnp.zeros((L, QK_ROPE // 2), F32), jnp.sin(ang)], axis=0)
    T = L + S
    cc = jnp.concatenate([cos, cos], axis=1)
    ss = jnp.concatenate([-sin, sin], axis=1)
    one = jnp.ones((T, LANES - 2 * QK_ROPE), F32)
    zero = jnp.zeros((T, LANES - 2 * QK_ROPE), F32)
    cq = jnp.concatenate([one, cc, ss], axis=1)
    ck = jnp.concatenate([zero, cc, cc], axis=1)
    sk = jnp.concatenate([zero, ss, ss], axis=1)
    return cq, ck, sk


def kernel(x, c, ctx, c_ctx, mod_w, mod_b, norm1_w, norm2_w, final_norm_w, ab_w_in, ab_w_out, rk_mu, rk_w0, rk_w2, rk_a0, rk_a2, rk_g2, rk_kk, rk_ka, rk_rk, rk_ln_w, rk_ln_b, mla_qn_w, mla_w_qup, mla_kvn_w, mla_w_kvup, na_w_qkv, na_rpb, na_w_out, moe_router, moe_w1, moe_w3, moe_w2):
    B, S, D = x.shape
    L = ctx.shape[1]
    depth = mod_w.shape[0]
    A = rk_w0.shape[-1]
    a_cols = rk_mu.shape[-1]

    rows_pad = -(B + 1) % 8
    cvec = jnp.concatenate([c, c_ctx[None], jnp.zeros((rows_pad, D), F32)], axis=0)
    mods = _mod_vectors(cvec, mod_w, mod_b)
    m_lat = mods[:, :B].reshape(depth, B, 6, D)
    m_ctx = jnp.broadcast_to(mods[:, B].reshape(depth, 1, 6, D), (depth, B, 6, D))
    mm = jnp.stack([m_ctx, m_lat], axis=2)

    cq, ck, sk = _rope_tables(L, S)
    xa = jnp.concatenate([ctx, x], axis=1)

    for layer in range(depth):
        need_ctx = layer < depth - 1
        i = layer // 2
        m = mm[layer]
        ss1, g1 = m[:, :, 0:2], m[:, :, 2:3]
        ss2, g2 = m[:, :, 3:5], m[:, :, 5:6]
        if layer % 2 == 0:
            w_in = _ab_input_weight(ab_w_in[i], a_cols)
            z = _norm_linear(xa, norm1_w[layer], ss1, w_in, L)
            o_a = _rwkv_mixer(z, L, a_cols, rk_mu[i], rk_w0[i], rk_w2[i], rk_a0[i], rk_a2[i], rk_g2[i],
                              rk_kk[i], rk_ka[i], rk_rk[i], rk_ln_w[i], rk_ln_b[i])
            wq, wk, wv = _mla_weights(mla_w_qup[i], mla_w_kvup[i])
            o_b = _mla_attention(z, mla_qn_w[i], mla_kvn_w[i], wq, wk, wv, cq, ck, sk, L,
                                 a_cols, a_cols + Q_RANK, a_cols + Q_RANK + KV_RANK)
            mixed, w_out = [o_a, o_b], ab_w_out[i]
        else:
            qkv = _norm_linear(xa, norm1_w[layer], ss1, na_w_qkv[i].astype(BF16), L, out_dtype=BF16)
            mixed, w_out = [_na_attention(qkv, _na_bias_table(na_rpb[i]), L, need_ctx)], na_w_out[i]
        xa = _linear_resid(mixed, w_out.astype(BF16), xa, g1, L)
        xa = _moe(xa, norm2_w[layer], ss2, g2, moe_router[layer], moe_w1, moe_w3, moe_w2, layer, L, need_ctx)
    return _final_norm(xa, final_norm_w, L)
```

```python
import functools

import jax
import jax.numpy as jnp
import numpy as np
from jax import lax
from jax.experimental import pallas as pl
from jax.experimental.pallas import tpu as pltpu

F32 = jnp.float32
BF16 = jnp.bfloat16
HIGHEST = lax.Precision.HIGHEST

GRID_W = 64
NORM_EPS = 1e-6
NEG_INF = -1e30
GN_EPS = 64e-5
A_HEAD_DIM = 64
LORA_W = 64
LORA_A = 64
LORA_G = 128
QK_NOPE = 64
QK_ROPE = 32
V_HEAD = 64
Q_RANK = 384
KV_RANK = 256
ROPE_BASE = 10000.0
C_HEAD_DIM = 64
WIN_R = 8
WIN_C = 16
N_EXPERTS = 16
CAPACITY_FACTOR = 2
SCAN_CHUNK = 64
DECAY_FLOOR_SCALE = float(np.exp(-0.5))
LANES = 128
MAX_ROW_TILE = 768

VMEM_LIMIT = 56 * 1024 * 1024


def _params(*sem):
    return pltpu.CompilerParams(dimension_semantics=sem, vmem_limit_bytes=VMEM_LIMIT)


def _dot(a, b, precision=None):
    return jnp.dot(a, b, preferred_element_type=F32, precision=precision)


def _dot_nt(a, b, precision=None):
    return lax.dot_general(a, b, (((1,), (1,)), ((), ())), preferred_element_type=F32, precision=precision)


def _row_tile(L, S):
    tm = 256
    while L % tm or S % tm:
        tm //= 2
    return tm


def _mod_kernel(c_ref, w_ref, b_ref, o_ref):
    c = c_ref[...]
    sc = c * jax.nn.sigmoid(c)
    o_ref[0] = _dot(sc.astype(BF16), w_ref[0].astype(BF16)) + b_ref[0]


def _mod_vectors(cvec, mod_w, mod_b):
    depth, D, N = mod_w.shape
    R = cvec.shape[0]
    tn = 1024
    return pl.pallas_call(
        _mod_kernel,
        grid=(depth, N // tn),
        in_specs=[pl.BlockSpec((R, D), lambda l, j: (0, 0)),
                  pl.BlockSpec((1, D, tn), lambda l, j: (l, 0, j)),
                  pl.BlockSpec((1, 1, tn), lambda l, j: (l, 0, j))],
        out_specs=pl.BlockSpec((1, R, tn), lambda l, j: (l, 0, j)),
        out_shape=jax.ShapeDtypeStruct((depth, R, N), F32),
        compiler_params=_params("arbitrary", "arbitrary"),
        name="mod_vectors",
    )(cvec, mod_w, mod_b.reshape(depth, 1, N))


def _wide_tile(T):
    return max(m for m in range(8, MAX_ROW_TILE + 1, 8) if T % m == 0)


def _per_row(mod_ref, i, tm, L, k):
    row = lax.broadcasted_iota(jnp.int32, (tm, 1), 0) + i * tm
    return jnp.where(row < L, mod_ref[0, 0, k:k + 1, :], mod_ref[0, 1, k:k + 1, :])


def _norm_mod(x, nw, shift, scale):
    y = x * lax.rsqrt(jnp.mean(x * x, axis=-1, keepdims=True) + NORM_EPS)
    y = y * nw
    return y * (1.0 + scale) + shift


def _norm_linear_kernel(x_ref, nw_ref, ss_ref, w_ref, o_ref, *, L):
    i = pl.program_id(1)
    tm = x_ref.shape[1]
    h = _norm_mod(x_ref[0], nw_ref[...], _per_row(ss_ref, i, tm, L, 0), _per_row(ss_ref, i, tm, L, 1))
    o_ref[0] = _dot(h.astype(BF16), w_ref[...]).astype(o_ref.dtype)


def _norm_linear(xa, nw, ss, w, L, out_dtype=F32):
    B, T, D = xa.shape
    N = w.shape[1]
    tm = _wide_tile(T)
    return pl.pallas_call(
        functools.partial(_norm_linear_kernel, L=L),
        grid=(B, T // tm),
        in_specs=[pl.BlockSpec((1, tm, D), lambda b, i: (b, i, 0)),
                  pl.BlockSpec((1, D), lambda b, i: (0, 0)),
                  pl.BlockSpec((1, 2, 2, D), lambda b, i: (b, 0, 0, 0)),
                  pl.BlockSpec((D, N), lambda b, i: (0, 0))],
        out_specs=pl.BlockSpec((1, tm, N), lambda b, i: (b, i, 0)),
        out_shape=jax.ShapeDtypeStruct((B, T, N), out_dtype),
        compiler_params=_params("arbitrary", "arbitrary"),
        name="norm_linear",
    )(xa, nw.reshape(1, D), ss, w)


def _linear_resid_kernel(*refs, ks, L):
    n = len(ks)
    a_refs, (w_ref, x_ref, g_ref, o_ref) = refs[:n], refs[n:]
    acc = None
    off = 0
    for a_ref, k in zip(a_refs, ks):
        part = _dot(a_ref[0].astype(BF16), w_ref[off:off + k, :])
        acc = part if acc is None else acc + part
        off += k
    o_ref[0] = x_ref[0] + _per_row(g_ref, pl.program_id(1), x_ref.shape[1], L, 0) * acc


def _linear_resid(a_list, w, xa, gate, L):
    B, T, D = xa.shape
    tm = _wide_tile(T)
    ks = tuple(a.shape[-1] for a in a_list)
    in_specs = [pl.BlockSpec((1, tm, k), lambda b, i: (b, i, 0)) for k in ks]
    in_specs += [pl.BlockSpec(w.shape, lambda b, i: (0, 0)),
                 pl.BlockSpec((1, tm, D), lambda b, i: (b, i, 0)),
                 pl.BlockSpec((1, 2, 1, D), lambda b, i: (b, 0, 0, 0))]
    return pl.pallas_call(
        functools.partial(_linear_resid_kernel, ks=ks, L=L),
        grid=(B, T // tm),
        in_specs=in_specs,
        out_specs=pl.BlockSpec((1, tm, D), lambda b, i: (b, i, 0)),
        out_shape=jax.ShapeDtypeStruct((B, T, D), F32),
        compiler_params=_params("arbitrary", "arbitrary"),
        name="linear_resid",
    )(*a_list, w, xa, gate)


def _rms_kernel(x_ref, w_ref, o_ref):
    x = x_ref[...]
    o_ref[...] = x * lax.rsqrt(jnp.mean(x * x, axis=-1, keepdims=True) + NORM_EPS) * w_ref[...]


def _final_norm(xa, w, L):
    B, T, D = xa.shape
    S = T - L
    tm = _row_tile(L, S)
    nct = L // tm
    nb = max(n for n in (4, 2, 1) if B % n == 0)
    return pl.pallas_call(
        _rms_kernel,
        grid=(B // nb, S // tm),
        in_specs=[pl.BlockSpec((nb, tm, D), lambda b, i: (b, i + nct, 0)),
                  pl.BlockSpec((1, D), lambda b, i: (0, 0))],
        out_specs=pl.BlockSpec((nb, tm, D), lambda b, i: (b, i, 0)),
        out_shape=jax.ShapeDtypeStruct((B, S, D), F32),
        compiler_params=_params("arbitrary", "arbitrary"),
        name="final_norm",
    )(xa, w.reshape(1, D))


def _rms(x, w):
    return x * lax.rsqrt(jnp.mean(x * x, axis=-1, keepdims=True) + NORM_EPS) * w


def _softmax_pv(chains):
    m = [functools.reduce(jnp.maximum, [jnp.max(s, axis=-1, keepdims=True) for s, _ in ch]) for ch in chains]
    p = [[jnp.exp(s - mi) for s, _ in ch] for ch, mi in zip(chains, m)]
    l = [functools.reduce(jnp.add, [jnp.sum(x, axis=-1, keepdims=True) for x in pc]) for pc in p]
    o = [functools.reduce(jnp.add, [_dot(x.astype(BF16), v) for x, (_, v) in zip(pc, ch)]) for pc, ch in zip(p, chains)]
    return [oi / li for oi, li in zip(o, l)]


def _mla_kernel(zq_ref, zkv_ref, zr_ref, qn_ref, kvn_ref, wq_ref, wk_ref, wv_ref, cq_ref, ck_ref, sk_ref,
                o_ref, q_s, k_s, v_s, *, L, tq, scale):
    T = zq_ref.shape[1]
    zqn = _rms(zq_ref[0], qn_ref[...]).astype(BF16)
    zkvn = _rms(zkv_ref[0], kvn_ref[...]).astype(BF16)
    qh = _dot(zqn, wq_ref[0])
    kn = _dot(zkvn, wk_ref[0])
    v_s[...] = _dot(zkvn, wv_ref[0]).astype(BF16)
    zr = zr_ref[0]
    kr = zr[:, :LANES] * ck_ref[...] + zr[:, LANES:] * sk_ref[...]
    cq = cq_ref[...] * scale
    for h in range(2):
        q_s[h] = (qh[:, h * LANES:(h + 1) * LANES] * cq).astype(BF16)
        k_s[h] = (kn[:, h * LANES:(h + 1) * LANES] + kr).astype(BF16)
    def tile(row0, nk, rows):
        first_head = lax.broadcasted_iota(jnp.int32, (rows, LANES), 1) < V_HEAD
        s = [_dot_nt(q_s[h, pl.ds(row0, rows), :], k_s[h, 0:nk, :]) for h in range(2)]
        outs = _softmax_pv([[(si, v_s[0:nk, :])] for si in s])
        o_ref[0, pl.ds(row0, rows), :] = jnp.where(first_head, outs[0], outs[1]).astype(o_ref.dtype)

    for i in range(L // tq):
        tile(i * tq, L, tq)
    tl = 2 * tq if (T - L) % (2 * tq) == 0 else tq
    for i in range((T - L) // tl):
        tile(L + i * tl, T, tl)


def _mla_attention(z, qn_w, kvn_w, wq, wk, wv, cq, ck, sk, L, col_q, col_kv, col_r):
    B, T, _ = z.shape
    HP = wq.shape[0]
    tq = _row_tile(L, T - L)
    scale = float((QK_NOPE + QK_ROPE) ** -0.5)
    return pl.pallas_call(
        functools.partial(_mla_kernel, L=L, tq=tq, scale=scale),
        grid=(B, HP),
        in_specs=[pl.BlockSpec((1, T, Q_RANK), lambda b, p: (b, 0, col_q // Q_RANK)),
                  pl.BlockSpec((1, T, KV_RANK), lambda b, p: (b, 0, col_kv // KV_RANK)),
                  pl.BlockSpec((1, T, 2 * LANES), lambda b, p: (b, 0, col_r // (2 * LANES))),
                  pl.BlockSpec((1, Q_RANK), lambda b, p: (0, 0)),
                  pl.BlockSpec((1, KV_RANK), lambda b, p: (0, 0)),
                  pl.BlockSpec((1, Q_RANK, 2 * LANES), lambda b, p: (p, 0, 0)),
                  pl.BlockSpec((1, KV_RANK, 2 * LANES), lambda b, p: (p, 0, 0)),
                  pl.BlockSpec((1, KV_RANK, LANES), lambda b, p: (p, 0, 0)),
                  pl.BlockSpec((T, LANES), lambda b, p: (0, 0)),
                  pl.BlockSpec((T, LANES), lambda b, p: (0, 0)),
                  pl.BlockSpec((T, LANES), lambda b, p: (0, 0))],
        out_specs=pl.BlockSpec((1, T, LANES), lambda b, p: (b, 0, p)),
        out_shape=jax.ShapeDtypeStruct((B, T, HP * LANES), BF16),
        scratch_shapes=[pltpu.VMEM((2, T, LANES), BF16), pltpu.VMEM((2, T, LANES), BF16),
                        pltpu.VMEM((T, LANES), BF16)],
        compiler_params=_params("arbitrary", "arbitrary"),
        name="mla_attention",
    )(z, z, z, qn_w.reshape(1, -1), kvn_w.reshape(1, -1), wq, wk, wv, cq, ck, sk)


def _na_kernel(q_ref, k_ref, v_ref, bt_ref, o_ref, k_s, v_s, *, L, rows, kr, need_ctx, scale):
    W = GRID_W
    rpb = 4 if rows % 4 == 0 else 1
    k_s[...] = k_ref[0].astype(BF16)
    v_s[...] = v_ref[0].astype(BF16)
    nwin = kr * W
    lane = lax.broadcasted_iota(jnp.int32, (W, LANES), 1)
    head_mask = [(lane < C_HEAD_DIM).astype(F32), (lane >= C_HEAD_DIM).astype(F32)]
    first_head = lane < C_HEAD_DIM
    qcol = lax.broadcasted_iota(jnp.int32, (W, nwin), 0)
    kcol = lax.broadcasted_iota(jnp.int32, (W, nwin), 1) % W
    cstart = jnp.clip(qcol - WIN_C // 2, 0, W - WIN_C)
    col_valid = (kcol >= cstart) & (kcol < cstart + WIN_C)

    def row_block(rb, carry):
        q_blk = q_ref[0, pl.ds(pl.multiple_of(L + rb * (rpb * W), W), rpb * W), :] * scale
        s_ctx = [_dot_nt((q_blk * jnp.concatenate([head_mask[h]] * rpb, axis=0)).astype(BF16), k_s[0:L, :])
                 for h in range(2)]
        chains, q0s = [], []
        for j in range(rpb):
            r = rb * rpb + j
            rs = jnp.clip(r - kr // 2, 0, rows - kr)
            k0 = pl.multiple_of(L + rs * W, W)
            q0s.append(pl.multiple_of(L + r * W, W))
            q = q_blk[j * W:(j + 1) * W]
            kw = k_s[pl.ds(k0, nwin), :]
            vw = v_s[pl.ds(k0, nwin), :]
            dr0 = rs - r + (WIN_R - 1)
            for h in range(2):
                s_nb = _dot_nt((q * head_mask[h]).astype(BF16), kw)
                bias = jnp.concatenate([bt_ref[0, h, dr0 + 2 * m] for m in range(kr // 2)], axis=-1)
                s_nb = jnp.where(col_valid, s_nb + bias, NEG_INF)
                chains.append([(s_nb, vw), (s_ctx[h][j * W:(j + 1) * W], v_s[0:L, :])])
        outs = _softmax_pv(chains)
        for j in range(rpb):
            o_ref[0, pl.ds(q0s[j], W), :] = jnp.where(first_head, outs[2 * j], outs[2 * j + 1]).astype(o_ref.dtype)
        return carry

    lax.fori_loop(0, rows // rpb, row_block, 0, unroll=4)

    tq = min(L, 256)
    lane_c = lax.broadcasted_iota(jnp.int32, (tq, LANES), 1)
    for i in range(L // tq):
        if need_ctx:
            q = q_ref[0, i * tq:(i + 1) * tq, :] * scale
            hm = [lane_c < C_HEAD_DIM, lane_c >= C_HEAD_DIM]
            s = [_dot_nt(jnp.where(hm[h], q, 0.0).astype(BF16), k_s[0:L, :]) for h in range(2)]
            outs = _softmax_pv([[(si, v_s[0:L, :])] for si in s])
            o_ref[0, i * tq:(i + 1) * tq, :] = jnp.where(lane_c < C_HEAD_DIM, outs[0], outs[1]).astype(o_ref.dtype)
        else:
            o_ref[0, i * tq:(i + 1) * tq, :] = jnp.zeros((tq, LANES), o_ref.dtype)


def _na_attention(qkv, bias_tab, L, need_ctx):
    B, T, D3 = qkv.shape
    D = D3 // 3
    HP = D // LANES
    rows = (T - L) // GRID_W
    kr = min(WIN_R, rows)
    assert kr % 2 == 0
    nd = bias_tab.shape[2]
    return pl.pallas_call(
        functools.partial(_na_kernel, L=L, rows=rows, kr=kr, need_ctx=need_ctx, scale=float(C_HEAD_DIM ** -0.5)),
        grid=(B, HP),
        in_specs=[pl.BlockSpec((1, T, LANES), lambda b, p: (b, 0, p)),
                  pl.BlockSpec((1, T, LANES), lambda b, p: (b, 0, HP + p)),
                  pl.BlockSpec((1, T, LANES), lambda b, p: (b, 0, 2 * HP + p)),
                  pl.BlockSpec((1, 2, nd, GRID_W, LANES), lambda b, p: (p, 0, 0, 0, 0))],
        out_specs=pl.BlockSpec((1, T, LANES), lambda b, p: (b, 0, p)),
        out_shape=jax.ShapeDtypeStruct((B, T, D), BF16),
        scratch_shapes=[pltpu.VMEM((T, LANES), BF16), pltpu.VMEM((T, LANES), BF16)],
        compiler_params=_params("arbitrary", "arbitrary"),
        name="na_attention",
    )(qkv, qkv, qkv, bias_tab)


def _na_bias_table(rpb):
    H = rpb.shape[0]
    qc = np.arange(GRID_W)[:, None]
    kc = np.arange(GRID_W)[None, :]
    dc = np.clip(kc - qc + (WIN_C - 1), 0, 2 * WIN_C - 2)
    pick = jnp.asarray(np.arange(2 * WIN_C - 1)[:, None, None] == dc[None], dtype=F32)
    t = jnp.einsum('hdc,cqk->hdqk', rpb, pick, precision=HIGHEST)
    t2 = jnp.concatenate([t[:, :-1], t[:, 1:]], axis=-1)
    return t2.reshape(H // 2, 2, 2 * WIN_R - 2, GRID_W, 2 * GRID_W)


def _bf(x):
    return x.astype(BF16)


def _seg_sum(x, ones_bd):
    hi = _bf(x)
    lo = _bf(x - hi.astype(F32))
    return _dot(hi, ones_bd) + _dot(lo, ones_bd)


def _head_ones(A):
    seg = np.arange(A) // A_HEAD_DIM
    return jnp.asarray(seg[:, None] == seg[None, :], dtype=BF16)


def _rwkv_prep_kernel(z_ref, zp_ref, zn_ref, mu_ref, w0_ref, w2_ref, a0_ref, a2_ref, g2_ref, kk_ref, ka_ref, rk_ref,
                      ones_ref, r_o, v_o, kkn_o, g_o, bonus_o, lw_o, beta_o, kd_o, *, nct, nt, A):
    for s in range(z_ref.shape[0]):
        _rwkv_prep_sample(s, z_ref, zp_ref, zn_ref, mu_ref, w0_ref, w2_ref, a0_ref, a2_ref, g2_ref, kk_ref, ka_ref,
                          rk_ref, ones_ref, r_o, v_o, kkn_o, g_o, bonus_o, lw_o, beta_o, kd_o, nct=nct, nt=nt, A=A)


def _rwkv_prep_sample(s, z_ref, zp_ref, zn_ref, mu_ref, w0_ref, w2_ref, a0_ref, a2_ref, g2_ref, kk_ref, ka_ref, rk_ref,
                      ones_ref, r_o, v_o, kkn_o, g_o, bonus_o, lw_o, beta_o, kd_o, *, nct, nt, A):
    i = pl.program_id(1)
    za = z_ref[s]
    tm = za.shape[0]
    row = lax.broadcasted_iota(jnp.int32, za.shape, 0)
    seg_first = (i == 0) | (i == nct)
    seg_last = (i == nct - 1) | (i == nt - 1)
    prev_row = jnp.where(seg_first, 0.0, zp_ref[s, 7:8, :])
    next_row = jnp.where(seg_last, 0.0, zn_ref[s, 0:1, :])
    prev = jnp.where(row == 0, prev_row, pltpu.roll(za, 1, 0))
    nxt = jnp.where(row == tm - 1, next_row, pltpu.roll(za, tm - 1, 0))
    zs = za + mu_ref[0:1, :] * (prev - za) + mu_ref[1:2, :] * (nxt - za)
    r = zs[:, 0:A]
    k = zs[:, A:2 * A]
    v = zs[:, 2 * A:3 * A]
    wd = _bf(jnp.tanh(zs[:, 3 * A:3 * A + LANES]))
    ad = _bf(zs[:, 3 * A + LANES:3 * A + 2 * LANES])
    gd = _bf(jax.nn.sigmoid(zs[:, 3 * A + 2 * LANES:3 * A + 3 * LANES]))
    ones = ones_ref[...]
    kk = k * kk_ref[...]
    kkn = kk / jnp.maximum(jnp.sqrt(_seg_sum(kk * kk, ones)), 1e-12)
    kd_sum = None
    for d in range(2):
        u = w0_ref[d:d + 1, :] + _dot(wd, w2_ref[d])
        lw_o[d, s] = -DECAY_FLOOR_SCALE * jax.nn.sigmoid(u)
        a = jax.nn.sigmoid(a0_ref[d:d + 1, :] + _dot(ad, a2_ref[d]))
        beta_o[d, s] = a
        kd = k * (1.0 + (a - 1.0) * ka_ref[...])
        kd_sum = kd if kd_sum is None else kd_sum + kd
    kd_o[s] = k
    bonus_o[s] = _seg_sum(r * kd_sum * rk_ref[...], ones) * v
    r_o[s] = r
    v_o[s] = v
    kkn_o[s] = kkn
    g_o[s] = _dot(gd, g2_ref[...])


def _rwkv_prep(z, L, a_cols, mu, w0, w2, a0, a2, g2, k_k, k_a, r_k):
    B, T, _ = z.shape
    A = w0.shape[-1]
    assert 2 * LORA_W == LANES and 2 * LORA_A == LANES and LORA_G == LANES and a_cols == 3 * A + 3 * LANES
    tm = _row_tile(L, T - L)
    nt = T // tm
    hb = tm // 8

    def pad_lora(w):
        zero = jnp.zeros_like(w[0])
        return _bf(jnp.stack([jnp.concatenate([w[0], zero], 0), jnp.concatenate([zero, w[1]], 0)]))

    def const(shape):
        return pl.BlockSpec(shape, lambda b, i: (0,) * len(shape))

    nb = 2 if B % 2 == 0 else 1
    tile = pl.BlockSpec((nb, tm, A), lambda b, i: (b, i, 0))
    tile_d = pl.BlockSpec((2, nb, tm, A), lambda b, i: (0, b, i, 0))
    sd = jax.ShapeDtypeStruct((B, T, A), F32)
    sd_d = jax.ShapeDtypeStruct((2, B, T, A), F32)
    return pl.pallas_call(
        functools.partial(_rwkv_prep_kernel, nct=L // tm, nt=nt, A=A),
        grid=(B // nb, nt),
        in_specs=[pl.BlockSpec((nb, tm, a_cols), lambda b, i: (b, i, 0)),
                  pl.BlockSpec((nb, 8, a_cols), lambda b, i: (b, jnp.maximum(i * hb - 1, 0), 0)),
                  pl.BlockSpec((nb, 8, a_cols), lambda b, i: (b, jnp.minimum((i + 1) * hb, T // 8 - 1), 0)),
                  const((2, a_cols)), const((2, A)), const((2, 2 * LORA_W, A)), const((2, A)),
                  const((2, 2 * LORA_A, A)), const((LORA_G, A)), const((1, A)), const((1, A)), const((1, A)),
                  const((A, A))],
        out_specs=[tile, tile, tile, tile, tile, tile_d, tile_d, tile],
        out_shape=[sd, sd, sd, sd, sd, sd_d, sd_d, sd],
        compiler_params=_params("arbitrary", "arbitrary"),
        name="rwkv_prep",
    )(z, z, z, mu, w0, pad_lora(w2), a0, pad_lora(a2), _bf(g2), k_k.reshape(1, A), k_a.reshape(1, A),
      r_k.reshape(1, A), _head_ones(A))


def _tri_inverse(lms, eye, m16, m32):
    d0 = [_bf(jnp.where(m16, lm, 0.0)) for lm in lms]
    t = [eye + d.astype(F32) for d in d0]
    s = [_dot(d, d) for d in d0]
    for step in range(3):
        sb = [_bf(x) for x in s]
        t = [x + _dot(_bf(x), y) for x, y in zip(t, sb)]
        if step < 2:
            s = [_dot(y, y) for y in sb]
    for lvl in (m32 & (~m16), ~m32):
        tb = [_bf(x) for x in t]
        w = [_bf(_dot(_bf(jnp.where(lvl, lm, 0.0)), y)) for lm, y in zip(lms, tb)]
        t = [x + _dot(y, z) for x, y, z in zip(t, tb, w)]
    return t


def _wkv_kernel(*refs, NP, NB):
    C = SCAN_CHUNK
    P = 2 * C
    fwd_refs, bwd_refs, (ka_ref, yf_ref, yb_ref, h_s) = refs[0:6], refs[6:12], refs[12:]

    @pl.when(pl.program_id(1) == 0)
    def _():
        h_s[...] = jnp.zeros(h_s.shape, F32)

    ri = lax.broadcasted_iota(jnp.int32, (P, P), 0)
    ci = lax.broadcasted_iota(jnp.int32, (P, P), 1)
    same = (ri // C) == (ci // C)
    diff = (ri % C) - (ci % C)
    eye_b = ri == ci
    eye = eye_b.astype(F32)
    m16 = (ri // 16) == (ci // 16)
    m32 = (ri // 32) == (ci // 32)
    diff64 = lax.broadcasted_iota(jnp.int32, (C, C), 0) - lax.broadcasted_iota(jnp.int32, (C, C), 1)
    top = lax.broadcasted_iota(jnp.int32, (C, P), 1) < C
    zero_blk = jnp.zeros((P, P), BF16)
    sls = [slice(p * P, (p + 1) * P) for p in range(NP)]

    def bd(x):
        return [_bf(jnp.concatenate([jnp.where(top, x[:, sl], 0.0), jnp.where(top, 0.0, x[:, sl])], axis=0))
                for sl in sls]

    at, rt, bt, kt, bh, kh, vv, etots, before, before_eq = [], [], [], [], [], [], [], [], [], []
    for (r_ref, v_ref, kk_ref, lw_ref, beta_ref, kd_ref), sgn in ((fwd_refs, 1), (bwd_refs, -1)):
        order = diff * sgn
        tri = _bf(((diff64 * sgn) >= 0).astype(F32))
        for s in range(NB):
            before += [same & (order > 0)] * NP
            before_eq += [same & (order >= 0)] * NP
            lw = lw_ref[0, s]
            lw_hi = _bf(lw)
            lw_md = _bf(lw - lw_hi.astype(F32))
            lw_lo = _bf(lw - lw_hi.astype(F32) - lw_md.astype(F32))
            cum = _dot(tri, lw_hi) + _dot(tri, lw_md) + _dot(tri, lw_lo)
            tot = jnp.sum(lw, axis=0, keepdims=True)
            a = beta_ref[0, s]
            beta = kk_ref[s] * a
            kd = kd_ref[s] * (1.0 + (a - 1.0) * ka_ref[...])
            e_neg = jnp.exp(-cum)
            e_tail = jnp.exp(tot - cum)
            at += bd(-kk_ref[s] * jnp.exp(cum - lw))
            rt += bd(r_ref[s] * jnp.exp(cum))
            bt += bd(beta * e_neg)
            kt += bd(kd * e_neg)
            bh += bd(beta * e_tail)
            kh += bd(kd * e_tail)
            vv += bd(v_ref[s])
            etot = jnp.exp(tot)
            etots += [etot[:, sl] for sl in sls]

    ar = [jnp.concatenate([a, r], axis=0) for a, r in zip(at, rt)]
    arb = [_dot_nt(x, b) for x, b in zip(ar, bt)]
    ark = [_dot_nt(x, k) for x, k in zip(ar, kt)]
    lab = [jnp.where(m, x[:P], 0.0) for x, m in zip(arb, before)]
    tinv = _tri_inverse(lab, eye, m16, m32)
    u = [_dot(_bf(jnp.where(m, x[:P], 0.0)), v) for x, v, m in zip(ark, vv, before)]
    x = [_bf(_dot(_bf(t), jnp.concatenate([a, _bf(w)], axis=1))) for t, a, w in zip(tinv, at, u)]
    rhs = [jnp.concatenate([xi, jnp.concatenate([zero_blk, v], axis=1)], axis=0) for xi, v in zip(x, vv)]
    mn = [lax.dot_general(jnp.concatenate([b, k], axis=0), w, (((0,), (0,)), ((), ())), preferred_element_type=F32)
          for b, k, w in zip(bh, kh, rhs)]
    lr = [_bf(jnp.concatenate([jnp.where(m, xb[P:], 0.0), jnp.where(m, xk[P:], 0.0)], axis=1))
          for xb, xk, m in zip(arb, ark, before_eq)]
    qy = [_dot(l, w) for l, w in zip(lr, rhs)]
    qm = [_bf(jnp.concatenate([r.astype(F32) + q[:, :P], jnp.where(eye_b, e, 0.0) + m[:, :P]], axis=0))
          for r, q, m, e in zip(rt, qy, mn, etots)]
    nchain = 2 * NB * NP
    hin = [h_s[i] for i in range(nchain)]
    h_hi = [_bf(h) for h in hin]
    h_lo = [_bf(h - hh.astype(F32)) for h, hh in zip(hin, h_hi)]
    res = [_dot(w, hh) + _dot(w, hl) for w, hh, hl in zip(qm, h_hi, h_lo)]
    for i in range(nchain):
        ybd = res[i][:P] + qy[i][:, P:]
        y_ref = yf_ref if i < NB * NP else yb_ref
        y_ref[(i // NP) % NB, :, sls[i % NP]] = ybd[:C] + ybd[C:]
        h_s[i] = res[i][P:] + mn[i][:, P:]


def _wkv_scan(r, v, kk, lw, a, k, k_a, L):
    B, T, A = r.shape
    C = SCAN_CHUNK
    nC = T // C
    nct = L // C
    NP = A // (2 * C)

    def rev(c):
        return jnp.where(c < nct, nct - 1 - c, nC - 1 - (c - nct))

    NB = 2 if B % 2 == 0 else 1
    fwd = pl.BlockSpec((NB, C, A), lambda b, c: (b, c, 0))
    bwd = pl.BlockSpec((NB, C, A), lambda b, c: (b, rev(c), 0))
    fwd_d = pl.BlockSpec((1, NB, C, A), lambda b, c: (0, b, c, 0))
    bwd_d = pl.BlockSpec((1, NB, C, A), lambda b, c: (1, b, rev(c), 0))
    sd = jax.ShapeDtypeStruct((B, T, A), F32)
    return pl.pallas_call(
        functools.partial(_wkv_kernel, NP=NP, NB=NB),
        grid=(B // NB, nC),
        in_specs=[fwd, fwd, fwd, fwd_d, fwd_d, fwd, bwd, bwd, bwd, bwd_d, bwd_d, bwd,
                  pl.BlockSpec((1, A), lambda b, c: (0, 0))],
        out_specs=[fwd, bwd],
        out_shape=[sd, sd],
        scratch_shapes=[pltpu.VMEM((2 * NB * NP, 2 * C, 2 * C), F32)],
        compiler_params=_params("arbitrary", "arbitrary"),
        name="wkv_scan",
    )(r, v, kk, lw, a, k, r, v, kk, lw, a, k, k_a.reshape(1, A))


def _rwkv_post_kernel(yf_ref, yb_ref, bonus_ref, g_ref, lnw_ref, lnb_ref, ones_ref, o_ref):
    y = yf_ref[0] + yb_ref[0]
    ones = ones_ref[...]
    inv_n = 1.0 / A_HEAD_DIM
    d = y - _seg_sum(y, ones) * inv_n
    var = _seg_sum(d * d, ones) * inv_n
    yn = d * lax.rsqrt(var + GN_EPS) * lnw_ref[...] + lnb_ref[...]
    o_ref[0] = ((yn + bonus_ref[0]) * g_ref[0]).astype(o_ref.dtype)


def _rwkv_post(yf, yb, bonus, g, ln_w, ln_b, L):
    B, T, A = bonus.shape
    tm = _wide_tile(T)
    tile = pl.BlockSpec((1, tm, A), lambda b, i: (b, i, 0))
    vec = pl.BlockSpec((1, A), lambda b, i: (0, 0))
    return pl.pallas_call(
        _rwkv_post_kernel,
        grid=(B, T // tm),
        in_specs=[tile, tile, tile, tile, vec, vec, pl.BlockSpec((A, A), lambda b, i: (0, 0))],
        out_specs=tile,
        out_shape=jax.ShapeDtypeStruct((B, T, A), BF16),
        compiler_params=_params("arbitrary", "arbitrary"),
        name="rwkv_post",
    )(yf, yb, bonus, g, ln_w.reshape(1, A), ln_b.reshape(1, A), _head_ones(A))


def _rwkv_mixer(z, L, a_cols, mu, w0, w2, a0, a2, g2, k_k, k_a, r_k, ln_w, ln_b):
    r, v, kk, g, bonus, lw, a, k = _rwkv_prep(z, L, a_cols, mu, w0, w2, a0, a2, g2, k_k, k_a, r_k)
    yf, yb = _wkv_scan(r, v, kk, lw, a, k, k_a, L)
    return _rwkv_post(yf, yb, bonus, g, ln_w, ln_b, L)


def _router_kernel(x_ref, nw_ref, ss_ref, wr_ref, h_ref, aff_ref, *, L):
    i = pl.program_id(1)
    tm = x_ref.shape[1]
    h = _norm_mod(x_ref[0], nw_ref[...], _per_row(ss_ref, i, tm, L, 0), _per_row(ss_ref, i, tm, L, 1))
    h_hi = h.astype(BF16)
    h_ref[0] = h_hi
    h_lo = _bf(h - h_hi.astype(F32))
    w = wr_ref[...]
    w_hi = _bf(w)
    w_lo = _bf(w - w_hi.astype(F32))
    logits = _dot_nt(w_hi, h_hi) + _dot_nt(w_hi, h_lo) + _dot_nt(w_lo, h_hi)
    m = jnp.max(logits, axis=0, keepdims=True)
    p = jnp.exp(logits - m)
    aff_ref[0] = p / jnp.sum(p, axis=0, keepdims=True)


def _norm_router(xa, nw, ss, w_router, L):
    B, T, D = xa.shape
    E = w_router.shape[1]
    tm = max(m for m in range(LANES, MAX_ROW_TILE + 1, LANES) if T % m == 0)
    return pl.pallas_call(
        functools.partial(_router_kernel, L=L),
        grid=(B, T // tm),
        in_specs=[pl.BlockSpec((1, tm, D), lambda b, i: (b, i, 0)),
                  pl.BlockSpec((1, D), lambda b, i: (0, 0)),
                  pl.BlockSpec((1, 2, 2, D), lambda b, i: (b, 0, 0, 0)),
                  pl.BlockSpec((E, D), lambda b, i: (0, 0))],
        out_specs=[pl.BlockSpec((1, tm, D), lambda b, i: (b, i, 0)),
                   pl.BlockSpec((1, E, tm), lambda b, i: (b, 0, i))],
        out_shape=[jax.ShapeDtypeStruct((B, T, D), BF16), jax.ShapeDtypeStruct((B, E, T), F32)],
        compiler_params=_params("arbitrary", "arbitrary"),
        name="norm_router",
    )(xa, nw.reshape(1, D), ss, w_router.T)


def _select_top(sets):
    keys = [pltpu.bitcast(aff, jnp.int32) for aff, _, _ in sets]
    E = keys[0].shape[0]

    def count_ge(key, thr):
        return jnp.sum(jnp.where(key >= thr, 1.0, 0.0), axis=1, keepdims=True)

    def narrow(_, carry):
        out = []
        for key, (_, cap, _), (lo, hi) in zip(keys, sets, carry):
            q = (hi - lo + 3) >> 2
            m1 = jnp.minimum(lo + q, hi)
            m2 = jnp.minimum(lo + 2 * q, hi)
            m3 = jnp.minimum(lo + 3 * q, hi)
            ok1, ok2, ok3 = count_ge(key, m1) >= cap, count_ge(key, m2) >= cap, count_ge(key, m3) >= cap
            new_lo = jnp.where(ok3, m3, jnp.where(ok2, m2, jnp.where(ok1, m1, lo)))
            new_hi = jnp.where(ok3, hi, jnp.where(ok2, m3 - 1, jnp.where(ok1, m2 - 1, m1 - 1)))
            out.append((new_lo, jnp.maximum(new_hi, new_lo)))
        return tuple(out)

    init = tuple((jnp.zeros((E, 1), jnp.int32), jnp.full((E, 1), 0x7F800000, jnp.int32)) for _ in sets)
    bounds = lax.fori_loop(0, 17, narrow, init)

    def prefix(x, tri):
        pb = tri.shape[0]
        parts, offset = [], jnp.zeros((E, 1), F32)
        for j in range(x.shape[1] // pb):
            xj = x[:, j * pb:(j + 1) * pb]
            inner = _dot(xj.astype(BF16), tri)
            parts.append(inner + offset)
            offset = offset + inner[:, pb - 1:pb] + xj[:, pb - 1:pb]
        return parts[0] if len(parts) == 1 else jnp.concatenate(parts, axis=1)

    res = []
    for key, (_, cap, tri), (thr, _) in zip(keys, sets, bounds):
        above = key > thr
        tie = key == thr
        need = cap - jnp.sum(jnp.where(above, 1.0, 0.0), axis=1, keepdims=True)
        tie_rank = prefix(jnp.where(tie, 1.0, 0.0), tri)
        sel = above | (tie & (tie_rank < need))
        res.append(jnp.where(sel, prefix(jnp.where(sel, 1.0, 0.0), tri), -1.0))
    return res


def _route_gather_kernel(aff_ref, h_ref, x_ref, pos_ref, gate_ref, slot_s, tri_s, *, L, cap_l, cap_c):
    b = pl.program_id(0)
    eg = pl.program_id(1)
    T = h_ref.shape[1]
    S = T - L

    @pl.when((b == 0) & (eg == 0))
    def _():
        n = tri_s.shape[0]
        tri_s[...] = jnp.where(lax.broadcasted_iota(jnp.int32, (n, n), 0) < lax.broadcasted_iota(jnp.int32, (n, n), 1),
                               1.0, 0.0).astype(BF16)

    @pl.when(eg == 0)
    def _():
        sets = [(aff_ref[0, :, L:], cap_l, tri_s[...])]
        if cap_c:
            sets.append((aff_ref[0, :, 0:L], cap_c, tri_s[...]))
        picked = _select_top(sets)
        slot_s[:, L:] = picked[0]
        if cap_c:
            slot_s[:, 0:L] = picked[1]

    def gather(k, lo, n, cap, row0):
        e = eg * x_ref.shape[0] + k
        slot = slot_s[pl.ds(e, 1), lo:lo + n]
        hit = slot == lax.broadcasted_iota(jnp.int32, (cap, n), 0).astype(F32)
        x_ref[k, 0, row0:row0 + cap, :] = _dot(jnp.where(hit, 1.0, 0.0).astype(BF16), h_ref[0, lo:lo + n, :]).astype(BF16)
        tok = lax.broadcasted_iota(jnp.int32, (cap, n), 1) + lo
        pos_ref[0, k, row0:row0 + cap, :] = jnp.sum(jnp.where(hit, tok, 0), axis=1, keepdims=True)
        aff = aff_ref[0, pl.ds(e, 1), lo:lo + n]
        gate_ref[0, k, row0:row0 + cap, :] = jnp.sum(jnp.where(hit, aff, 0.0), axis=1, keepdims=True)

    for k in range(x_ref.shape[0]):
        gather(k, L, S, cap_l, 0)
        if cap_c:
            gather(k, 0, L, cap_c, cap_l)


def _route_gather(aff, h2, L, cap_l, cap_c):
    B, T, D = h2.shape
    E = aff.shape[1]
    Ct = cap_l + cap_c
    ne = max(n for n in (4, 2, 1) if E % n == 0)
    return pl.pallas_call(
        functools.partial(_route_gather_kernel, L=L, cap_l=cap_l, cap_c=cap_c),
        grid=(B, E // ne),
        in_specs=[pl.BlockSpec((1, E, T), lambda b, e: (b, 0, 0)),
                  pl.BlockSpec((1, T, D), lambda b, e: (b, 0, 0))],
        out_specs=[pl.BlockSpec((ne, 1, Ct, D), lambda b, e: (e, b, 0, 0)),
                   pl.BlockSpec((1, ne, Ct, 1), lambda b, e: (b, e, 0, 0)),
                   pl.BlockSpec((1, ne, Ct, 1), lambda b, e: (b, e, 0, 0))],
        out_shape=[jax.ShapeDtypeStruct((E, B, Ct, D), BF16), jax.ShapeDtypeStruct((B, E, Ct, 1), jnp.int32),
                   jax.ShapeDtypeStruct((B, E, Ct, 1), F32)],
        scratch_shapes=[pltpu.VMEM((E, T), F32), pltpu.VMEM((_row_tile(L, T - L),) * 2, BF16)],
        compiler_params=_params("arbitrary", "arbitrary"),
        name="moe_route_gather",
    )(aff, h2)


def _ffn_kernel(x_ref, w1_ref, w3_ref, w2_ref, o_ref, *, rm):
    j = pl.program_id(1)
    R = x_ref.shape[1]
    w1 = w1_ref[0, 0].astype(BF16)
    w3 = w3_ref[0, 0].astype(BF16)
    w2 = w2_ref[0, 0].astype(BF16)

    @pl.when(j == 0)
    def _():
        o_ref[...] = jnp.zeros(o_ref.shape, F32)

    def rows(i, carry):
        r0 = pl.multiple_of(i * rm, rm)
        x = x_ref[0, pl.ds(r0, rm), :]
        a = _dot(x, w1)
        b = _dot(x, w3)
        hid = (a * jax.nn.sigmoid(a) * b).astype(BF16)
        o_ref[0, pl.ds(r0, rm), :] += _dot(hid, w2)
        return carry

    lax.fori_loop(0, R // rm, rows, 0, unroll=True)


def _expert_ffn(xin, w1, w3, w2, layer):
    E, R, D = xin.shape
    F = w1.shape[-1]
    tf = min(512, F)
    rm = max(m for m in (MAX_ROW_TILE, 512, 256, 128, 64, 32, 16) if R % m == 0)
    return pl.pallas_call(
        functools.partial(_ffn_kernel, rm=rm),
        grid=(E, F // tf),
        in_specs=[pl.BlockSpec((1, R, D), lambda e, j: (e, 0, 0)),
                  pl.BlockSpec((1, 1, D, tf), lambda e, j: (layer, e, 0, j)),
                  pl.BlockSpec((1, 1, D, tf), lambda e, j: (layer, e, 0, j)),
                  pl.BlockSpec((1, 1, tf, D), lambda e, j: (layer, e, j, 0))],
        out_specs=pl.BlockSpec((1, R, D), lambda e, j: (e, 0, 0)),
        out_shape=jax.ShapeDtypeStruct((E, R, D), F32),
        compiler_params=_params("arbitrary", "arbitrary"),
        name="expert_ffn",
    )(xin, w1, w3, w2)


def _combine_kernel(y_ref, gate_ref, posl_ref, posc_ref, x_ref, g_ref, o_ref, yg_s, *, tq, nct, cap_l):
    E, _, Ct, td = y_ref.shape
    T = x_ref.shape[1]
    cap_c = Ct - cap_l
    nl = E * cap_l
    for e in range(E):
        yg_s[e * cap_l:(e + 1) * cap_l, :] = (y_ref[e, 0, :cap_l, :] * gate_ref[0, e, :cap_l, :]).astype(BF16)
        if cap_c:
            yg_s[nl + e * cap_c:nl + (e + 1) * cap_c, :] = (y_ref[e, 0, cap_l:, :] * gate_ref[0, e, cap_l:, :]).astype(BF16)

    def tile(i, gate_row, pos, lo, n):
        r0 = pl.multiple_of(i * tq, tq)
        tok = lax.broadcasted_iota(jnp.int32, (tq, n), 0) + r0
        onehot = jnp.where(tok == pos, 1.0, 0.0).astype(BF16)
        o_ref[0, pl.ds(r0, tq), :] = x_ref[0, pl.ds(r0, tq), :] + gate_row * _dot(onehot, yg_s[lo:lo + n, :])

    for i in range(nct):
        if cap_c:
            tile(i, g_ref[0, 0], posc_ref[0], nl, E * cap_c)
        else:
            o_ref[0, i * tq:(i + 1) * tq, :] = x_ref[0, i * tq:(i + 1) * tq, :]

    def body(i, carry):
        tile(i, g_ref[0, 1], posl_ref[0], 0, nl)
        return carry

    lax.fori_loop(nct, T // tq, body, 0, unroll=2)


def _moe_combine(y, gate, pos_l, pos_c, xa, g2, L, cap_l):
    E, B, Ct, D = y.shape
    T = xa.shape[1]
    td = min(512, D)
    tq = _row_tile(L, T - L)
    return pl.pallas_call(
        functools.partial(_combine_kernel, tq=tq, nct=L // tq, cap_l=cap_l),
        grid=(B, D // td),
        in_specs=[pl.BlockSpec((E, 1, Ct, td), lambda b, j: (0, b, 0, j)),
                  pl.BlockSpec((1, E, Ct, 1), lambda b, j: (b, 0, 0, 0)),
                  pl.BlockSpec((1, 1, pos_l.shape[-1]), lambda b, j: (b, 0, 0)),
                  pl.BlockSpec((1, 1, pos_c.shape[-1]), lambda b, j: (b, 0, 0)),
                  pl.BlockSpec((1, T, td), lambda b, j: (b, 0, j)),
                  pl.BlockSpec((1, 2, 1, td), lambda b, j: (b, 0, 0, j))],
        out_specs=pl.BlockSpec((1, T, td), lambda b, j: (b, 0, j)),
        out_shape=jax.ShapeDtypeStruct((B, T, D), F32),
        scratch_shapes=[pltpu.VMEM((E * Ct, td), BF16)],
        compiler_params=_params("arbitrary", "arbitrary"),
        name="moe_combine",
    )(y, gate, pos_l, pos_c, xa, g2)


def _moe(xa, nw, ss2, g2, w_router, w1, w3, w2, layer, L, need_ctx):
    B, T, D = xa.shape
    S = T - L
    E = w_router.shape[1]
    h2, aff = _norm_router(xa, nw, ss2, w_router, L)
    cap_l = CAPACITY_FACTOR * S // E
    cap_c = CAPACITY_FACTOR * L // E if need_ctx else 0
    Ct = cap_l + cap_c
    xin, pos, gate = _route_gather(aff, h2, L, cap_l, cap_c)
    pos_l = pos[:, :, :cap_l, 0].reshape(B, 1, E * cap_l)
    if cap_c:
        pos_c = pos[:, :, cap_l:, 0].reshape(B, 1, E * cap_c)
    else:
        pos_c = jnp.zeros((B, 1, LANES), jnp.int32)
    y = _expert_ffn(xin.reshape(E, B * Ct, D), w1, w3, w2, layer).reshape(E, B, Ct, D)
    return _moe_combine(y, gate, pos_l, pos_c, xa, g2, L, cap_l)


_PERM_EO = np.concatenate([np.arange(0, QK_ROPE, 2), np.arange(1, QK_ROPE, 2)])
_PERM_OE = np.concatenate([np.arange(1, QK_ROPE, 2), np.arange(0, QK_ROPE, 2)])


def _ab_input_weight(w_in, a_cols):
    D = w_in.shape[0]
    zr = w_in[:, a_cols + Q_RANK + KV_RANK:]
    zero = jnp.zeros((D, LANES - 2 * QK_ROPE), w_in.dtype)
    g1 = jnp.concatenate([zero, zr[:, _PERM_EO], zr[:, _PERM_EO]], axis=1)
    g2 = jnp.concatenate([zero, zr[:, _PERM_OE], zr[:, _PERM_OE]], axis=1)
    return jnp.concatenate([w_in[:, :a_cols + Q_RANK + KV_RANK], g1, g2], axis=1).astype(BF16)


def _mla_weights(w_qup, w_kvup):
    NH = w_qup.shape[1] // (QK_NOPE + QK_ROPE)
    wq = w_qup.reshape(Q_RANK, NH, QK_NOPE + QK_ROPE)
    rope = wq[:, :, QK_NOPE:]
    wq = jnp.concatenate([wq[:, :, :QK_NOPE], rope[:, :, _PERM_EO], rope[:, :, _PERM_OE]], axis=-1)
    wq = wq.reshape(Q_RANK, NH // 2, 2 * LANES).transpose(1, 0, 2)
    wkv = w_kvup.reshape(KV_RANK, NH, QK_NOPE + V_HEAD)
    wk = jnp.concatenate([wkv[:, :, :QK_NOPE], jnp.zeros((KV_RANK, NH, LANES - QK_NOPE), w_kvup.dtype)], axis=-1)
    wk = wk.reshape(KV_RANK, NH // 2, 2 * LANES).transpose(1, 0, 2)
    wv = wkv[:, :, QK_NOPE:].reshape(KV_RANK, NH // 2, 2 * V_HEAD).transpose(1, 0, 2)
    return wq.astype(BF16), wk.astype(BF16), wv.astype(BF16)


def _rope_tables(L, S):
    t = np.arange(S)
    row = (t // GRID_W).astype(np.float32)
    col = (t % GRID_W).astype(np.float32)
    n_freq = QK_ROPE // 4
    inv = (ROPE_BASE ** (-np.arange(n_freq, dtype=np.float32) / n_freq)).astype(np.float32)
    ang = jnp.concatenate([jnp.asarray(row[:, None] * inv), jnp.asarray(col[:, None] * inv)], axis=-1)
    cos = jnp.concatenate([jnp.ones((L, QK_ROPE // 2), F32), jnp.cos(ang)], axis=0)
    sin = jnp.concatenate([jnp.zeros((L, QK_ROPE // 2), F32), jnp.sin(ang)], axis=0)
    T = L + S
    cc = jnp.concatenate([cos, cos], axis=1)
    ss = jnp.concatenate([-sin, sin], axis=1)
    one = jnp.ones((T, LANES - 2 * QK_ROPE), F32)
    zero = jnp.zeros((T, LANES - 2 * QK_ROPE), F32)
    cq = jnp.concatenate([one, cc, ss], axis=1)
    ck = jnp.concatenate([zero, cc, cc], axis=1)
    sk = jnp.concatenate([zero, ss, ss], axis=1)
    return cq, ck, sk


def kernel(x, c, ctx, c_ctx, mod_w, mod_b, norm1_w, norm2_w, final_norm_w, ab_w_in, ab_w_out, rk_mu, rk_w0, rk_w2, rk_a0, rk_a2, rk_g2, rk_kk, rk_ka, rk_rk, rk_ln_w, rk_ln_b, mla_qn_w, mla_w_qup, mla_kvn_w, mla_w_kvup, na_w_qkv, na_rpb, na_w_out, moe_router, moe_w1, moe_w3, moe_w2):
    B, S, D = x.shape
    L = ctx.shape[1]
    depth = mod_w.shape[0]
    A = rk_w0.shape[-1]
    a_cols = rk_mu.shape[-1]

    rows_pad = -(B + 1) % 8
    cvec = jnp.concatenate([c, c_ctx[None], jnp.zeros((rows_pad, D), F32)], axis=0)
    mods = _mod_vectors(cvec, mod_w, mod_b)
    m_lat = mods[:, :B].reshape(depth, B, 6, D)
    m_ctx = jnp.broadcast_to(mods[:, B].reshape(depth, 1, 6, D), (depth, B, 6, D))
    mm = jnp.stack([m_ctx, m_lat], axis=2)

    cq, ck, sk = _rope_tables(L, S)
    xa = jnp.concatenate([ctx, x], axis=1)

    for layer in range(depth):
        need_ctx = layer < depth - 1
        i = layer // 2
        m = mm[layer]
        ss1, g1 = m[:, :, 0:2], m[:, :, 2:3]
        ss2, g2 = m[:, :, 3:5], m[:, :, 5:6]
        if layer % 2 == 0:
            w_in = _ab_input_weight(ab_w_in[i], a_cols)
            z = _norm_linear(xa, norm1_w[layer], ss1, w_in, L)
            o_a = _rwkv_mixer(z, L, a_cols, rk_mu[i], rk_w0[i], rk_w2[i], rk_a0[i], rk_a2[i], rk_g2[i],
                              rk_kk[i], rk_ka[i], rk_rk[i], rk_ln_w[i], rk_ln_b[i])
            wq, wk, wv = _mla_weights(mla_w_qup[i], mla_w_kvup[i])
            o_b = _mla_attention(z, mla_qn_w[i], mla_kvn_w[i], wq, wk, wv, cq, ck, sk, L,
                                 a_cols, a_cols + Q_RANK, a_cols + Q_RANK + KV_RANK)
            mixed, w_out = [o_a, o_b], ab_w_out[i]
        else:
            qkv = _norm_linear(xa, norm1_w[layer], ss1, na_w_qkv[i].astype(BF16), L, out_dtype=BF16)
            mixed, w_out = [_na_attention(qkv, _na_bias_table(na_rpb[i]), L, need_ctx)], na_w_out[i]
        xa = _linear_resid(mixed, w_out.astype(BF16), xa, g1, L)
        xa = _moe(xa, norm2_w[layer], ss2, g2, moe_router[layer], moe_w1, moe_w3, moe_w2, layer, L, need_ctx)
    return _final_norm(xa, final_norm_w, L)
```
